```python
import math, functools
import jax, jax.numpy as jnp
from jax import lax
import numpy as np

D_MODEL = 2048
BATCH = 2
SEQ = 4096
DEPTH = 4
DEC_BATCH = 16
DEC_SEQ = 32
PAST_LEN = 1024

CHUNK = 64
Q_BLOCK = 128
RET_HEADS = 8
RET_DK = D_MODEL // RET_HEADS
RET_DV = D_MODEL // RET_HEADS
MLA_HEADS = 16
MLA_Q_LORA = D_MODEL // 4
MLA_KV_LORA = D_MODEL // 4
MLA_NOPE = 128
MLA_ROPE = 64
MLA_V = D_MODEL // MLA_HEADS
FFN_HIDDEN = -(-8 * D_MODEL // (3 * 256)) * 256
IN_COLS = 2 * RET_HEADS * RET_DK + 2 * RET_HEADS * RET_DV + MLA_Q_LORA + MLA_KV_LORA + MLA_ROPE + 2 * D_MODEL
ROPE_BASE = 10000.0
RET_GAMMA_EXP0 = 5.0
RMS_EPS = 1e-6
GN_EPS = 1e-5
NEG_INF = -1e30

kernel_name = 'hybrid_retention_mla_streaming_step'


def rms_norm(x, g):
    xf = x.astype(jnp.float32)
    y = xf * lax.rsqrt(jnp.mean(jnp.square(xf), axis=-1, keepdims=True) + RMS_EPS)
    return (y * g.astype(jnp.float32)).astype(x.dtype)


def apply_rope(x, pos):
    dim = x.shape[-1]
    inv = jnp.exp(-math.log(ROPE_BASE) * jnp.arange(0, dim, 2, dtype=jnp.float32) / dim)
    ang = pos.astype(jnp.float32)[:, None] * inv[None, :]
    shape = (1, x.shape[1]) + (1,) * (x.ndim - 3) + (dim // 2,)
    cos, sin = jnp.cos(ang).reshape(shape), jnp.sin(ang).reshape(shape)
    xf = x.astype(jnp.float32)
    x1, x2 = xf[..., : dim // 2], xf[..., dim // 2:]
    return jnp.concatenate([x1 * cos - x2 * sin, x1 * sin + x2 * cos], axis=-1).astype(x.dtype)


def split_combined(proj):
    widths = (RET_HEADS * RET_DK, RET_HEADS * RET_DK, RET_HEADS * RET_DV, RET_HEADS * RET_DV,
              MLA_Q_LORA, MLA_KV_LORA, MLA_ROPE, D_MODEL, D_MODEL)
    offsets = np.cumsum(widths)[:-1].tolist()
    return jnp.split(proj, offsets, axis=-1)


def retention_decays(L):
    log_g = jnp.log1p(-jnp.exp2(-RET_GAMMA_EXP0 - jnp.arange(RET_HEADS, dtype=jnp.float32)))
    idx = jnp.arange(L, dtype=jnp.float32)
    diff = idx[:, None] - idx[None, :]
    dmat = jnp.where(diff >= 0, jnp.exp(jnp.maximum(diff, 0.0) * log_g[:, None, None]), 0.0)
    xi = jnp.exp((idx + 1.0) * log_g[:, None])
    zeta = jnp.exp((L - 1.0 - idx) * log_g[:, None])
    g_L = jnp.exp(L * log_g)
    return dmat, xi, zeta, g_L


def retention_chunk(state, q, k, v, decays):
    dmat, xi, zeta, g_L = decays
    scores = jnp.einsum('bhqd,bhkd->bhqk', q, k) * dmat
    out = (jnp.einsum('bhqk,bhkv->bhqv', scores, v)
           + jnp.einsum('bhqd,bhdv->bhqv', q, state) * xi[..., None])
    new_state = state * g_L[:, None, None] + jnp.einsum('bhkd,bhkv->bhdv', k * zeta[..., None], v)
    return new_state, out


def retention_qkv(rq, rk, rv, pos):
    B, S = rq.shape[:2]
    q = apply_rope(rq.reshape(B, S, RET_HEADS, RET_DK), pos).astype(jnp.float32)
    k = apply_rope(rk.reshape(B, S, RET_HEADS, RET_DK), pos).astype(jnp.float32) * (RET_DK ** -0.5)
    v = rv.reshape(B, S, RET_HEADS, RET_DV).astype(jnp.float32)
    return q, k, v


def retention_prompt(q, k, v):
    B, S = q.shape[:2]
    n = S // CHUNK

    def to_chunks(t):
        return t.reshape(B, n, CHUNK, RET_HEADS, t.shape[-1]).transpose(1, 0, 3, 2, 4)

    decays = retention_decays(CHUNK)
    s0 = jnp.zeros((B, RET_HEADS, RET_DK, RET_DV), jnp.float32)
    state, out = lax.scan(lambda s, t: retention_chunk(s, t[0], t[1], t[2], decays), s0,
                          (to_chunks(q), to_chunks(k), to_chunks(v)))
    out = out.transpose(1, 0, 3, 2, 4).reshape(B, S, RET_HEADS, RET_DV)
    return out, state


def retention_sample(q, k, v, state):
    L = q.shape[1]
    tr = lambda t: t.transpose(0, 2, 1, 3)
    new_state, out = retention_chunk(state.astype(jnp.float32), tr(q), tr(k), tr(v), retention_decays(L))
    return out.transpose(0, 2, 1, 3), new_state


def retention_output(o, g):
    B, S = o.shape[:2]
    mu = jnp.mean(o, axis=-1, keepdims=True)
    var = jnp.mean(jnp.square(o - mu), axis=-1, keepdims=True)
    o = (o - mu) * lax.rsqrt(var + GN_EPS)
    return jax.nn.silu(g) * o.reshape(B, S, RET_HEADS * RET_DV).astype(g.dtype)


def mla_latents(dq, dkv, kr, pos, q_norm_l, w_uq_l, kv_norm_l):
    B, S = dq.shape[:2]
    c_q = rms_norm(dq, q_norm_l)
    q = (c_q @ w_uq_l).reshape(B, S, MLA_HEADS, MLA_NOPE + MLA_ROPE)
    q_nope = q[..., :MLA_NOPE]
    q_rope = apply_rope(q[..., MLA_NOPE:], pos)
    c_kv = rms_norm(dkv, kv_norm_l)
    k_rope = apply_rope(kr, pos)
    return q_nope, q_rope, c_kv, k_rope


def mla_expand(c_kv, w_ukv_l):
    B, T = c_kv.shape[:2]
    kv = (c_kv @ w_ukv_l).reshape(B, T, MLA_HEADS, MLA_NOPE + MLA_V)
    return kv[..., :MLA_NOPE], kv[..., MLA_NOPE:]


def mla_attend(q_nope, q_rope, k_nope, k_rope, v, q_chunk, k_chunk):
    scale = (MLA_NOPE + MLA_ROPE) ** -0.5
    s = (jnp.einsum('bqhd,bkhd->bhqk', q_nope, k_nope)
         + jnp.einsum('bqhd,bkd->bhqk', q_rope, k_rope)).astype(jnp.float32) * scale
    s = jnp.where(k_chunk[None, :] <= q_chunk[:, None], s, NEG_INF)
    p = jax.nn.softmax(s, axis=-1).astype(v.dtype)
    return jnp.einsum('bhqk,bkhd->bqhd', p, v)


def merge_branches(o_a, o_b, ga, gb, w_o_l):
    return (jax.nn.sigmoid(ga) * o_a + jax.nn.sigmoid(gb) * o_b) @ w_o_l


def mixer_prompt(h, pos, w_in_l, q_norm_l, w_uq_l, kv_norm_l, w_ukv_l, w_o_l):
    B, S, _ = h.shape
    rq, rk, rv, rg, dq, dkv, kr, ga, gb = split_combined(h @ w_in_l)
    q, k, v = retention_qkv(rq, rk, rv, pos)
    o_ret, ret_state = retention_prompt(q, k, v)
    o_a = retention_output(o_ret, rg)
    q_nope, q_rope, c_kv, k_rope = mla_latents(dq, dkv, kr, pos, q_norm_l, w_uq_l, kv_norm_l)
    k_nope, v_m = mla_expand(c_kv, w_ukv_l)
    nb = S // Q_BLOCK
    qn = q_nope.reshape(B, nb, Q_BLOCK, MLA_HEADS, MLA_NOPE).swapaxes(0, 1)
    qr = q_rope.reshape(B, nb, Q_BLOCK, MLA_HEADS, MLA_ROPE).swapaxes(0, 1)
    k_chunk = pos // CHUNK
    q_chunk = k_chunk.reshape(nb, Q_BLOCK)
    o = lax.map(lambda t: mla_attend(t[0], t[1], k_nope, k_rope, v_m, t[2], k_chunk), (qn, qr, q_chunk))
    o_b = o.swapaxes(0, 1).reshape(B, S, MLA_HEADS * MLA_V)
    return merge_branches(o_a, o_b, ga, gb, w_o_l), (c_kv, k_rope, ret_state)


def mixer_sample(h, pos, cache_ckv_l, cache_kr_l, state_l, w_in_l, q_norm_l, w_uq_l, kv_norm_l, w_ukv_l, w_o_l):
    B, L, _ = h.shape
    rq, rk, rv, rg, dq, dkv, kr, ga, gb = split_combined(h @ w_in_l)
    q, k, v = retention_qkv(rq, rk, rv, pos)
    o_ret, ret_state = retention_sample(q, k, v, state_l)
    o_a = retention_output(o_ret, rg)
    q_nope, q_rope, c_kv, k_rope = mla_latents(dq, dkv, kr, pos, q_norm_l, w_uq_l, kv_norm_l)
    ckv_all = jnp.concatenate([cache_ckv_l.astype(c_kv.dtype), c_kv], axis=1)
    kr_all = jnp.concatenate([cache_kr_l.astype(k_rope.dtype), k_rope], axis=1)
    k_nope, v_m = mla_expand(ckv_all, w_ukv_l)
    k_chunk = jnp.arange(PAST_LEN + L) // CHUNK
    o = mla_attend(q_nope, q_rope, k_nope, kr_all, v_m, pos // CHUNK, k_chunk)
    o_b = o.reshape(B, L, MLA_HEADS * MLA_V)
    return merge_branches(o_a, o_b, ga, gb, w_o_l), (c_kv, k_rope, ret_state)


def swiglu(h, w_ffn_in_l, w_ffn_out_l):
    gate, up = jnp.split(h @ w_ffn_in_l, 2, axis=-1)
    return (jax.nn.silu(gate) * up) @ w_ffn_out_l


def block(x, c, norm_mix_l, norm_ffn_l, w_ada_l, b_ada_l, w_ffn_in_l, w_ffn_out_l, mixer_fn):
    mod = jax.nn.silu(c) @ w_ada_l + b_ada_l
    sh1, sc1, g1, sh2, sc2, g2 = jnp.split(mod[:, None, :], 6, axis=-1)
    mix, new_state = mixer_fn(rms_norm(x, norm_mix_l) * (1 + sc1) + sh1)
    x = x + g1 * mix
    x = x + g2 * swiglu(rms_norm(x, norm_ffn_l) * (1 + sc2) + sh2, w_ffn_in_l, w_ffn_out_l)
    return x, new_state


def setup_inputs(seed: int = 0) -> dict:
    key = jax.random.key(seed)
    ks = jax.random.split(key, 20)
    nrm = lambda k, shape, s: jax.random.normal(k, shape, jnp.float32) * s
    gain = lambda k, shape: 1.0 + 0.01 * jax.random.normal(k, shape, jnp.float32)
    return {
        'x_prompt': nrm(ks[0], (BATCH, SEQ, D_MODEL), 1.0),
        'x_sample': nrm(ks[1], (DEC_BATCH, DEC_SEQ, D_MODEL), 1.0),
        'c_prompt': nrm(ks[2], (BATCH, D_MODEL), 1.0),
        'c_sample': nrm(ks[3], (DEC_BATCH, D_MODEL), 1.0),
        'cache_mla_ckv': nrm(ks[4], (DEPTH, DEC_BATCH, PAST_LEN, MLA_KV_LORA), 1.0),
        'cache_mla_krope': nrm(ks[5], (DEPTH, DEC_BATCH, PAST_LEN, MLA_ROPE), 1.0),
        'state_ret': nrm(ks[6], (DEPTH, DEC_BATCH, RET_HEADS, RET_DK, RET_DV), 0.5),
        'w_ada': nrm(ks[7], (DEPTH, D_MODEL, 6 * D_MODEL), D_MODEL ** -0.5),
        'b_ada': nrm(ks[8], (DEPTH, 6 * D_MODEL), 0.01),
        'norm_mix': gain(ks[9], (DEPTH, D_MODEL)),
        'norm_ffn': gain(ks[10], (DEPTH, D_MODEL)),
        'w_in': nrm(ks[11], (DEPTH, D_MODEL, IN_COLS), D_MODEL ** -0.5),
        'mla_q_norm': gain(ks[12], (DEPTH, MLA_Q_LORA)),
        'w_uq': nrm(ks[13], (DEPTH, MLA_Q_LORA, MLA_HEADS * (MLA_NOPE + MLA_ROPE)), MLA_Q_LORA ** -0.5),
        'mla_kv_norm': gain(ks[14], (DEPTH, MLA_KV_LORA)),
        'w_ukv': nrm(ks[15], (DEPTH, MLA_KV_LORA, MLA_HEADS * (MLA_NOPE + MLA_V)), MLA_KV_LORA ** -0.5),
        'w_o': nrm(ks[16], (DEPTH, D_MODEL, D_MODEL), D_MODEL ** -0.5),
        'w_ffn_in': nrm(ks[17], (DEPTH, D_MODEL, 2 * FFN_HIDDEN), D_MODEL ** -0.5),
        'w_ffn_out': nrm(ks[18], (DEPTH, FFN_HIDDEN, D_MODEL), FFN_HIDDEN ** -0.5),
        'norm_final': gain(ks[19], (D_MODEL,)),
    }


def reference(x_prompt, x_sample, c_prompt, c_sample, cache_mla_ckv, cache_mla_krope, state_ret,
              w_ada, b_ada, norm_mix, norm_ffn, w_in, mla_q_norm, w_uq, mla_kv_norm, w_ukv, w_o,
              w_ffn_in, w_ffn_out, norm_final):
    pos_p = jnp.arange(x_prompt.shape[1])
    pos_s = PAST_LEN + jnp.arange(x_sample.shape[1])
    xp, xs = x_prompt, x_sample
    ckv_p, kr_p, ret_p, ckv_s, kr_s, ret_s = [], [], [], [], [], []
    for l in range(DEPTH):
        shared = dict(w_in_l=w_in[l], q_norm_l=mla_q_norm[l], w_uq_l=w_uq[l],
                      kv_norm_l=mla_kv_norm[l], w_ukv_l=w_ukv[l], w_o_l=w_o[l])
        mix_p = functools.partial(mixer_prompt, pos=pos_p, **shared)
        mix_s = functools.partial(mixer_sample, pos=pos_s, cache_ckv_l=cache_mla_ckv[l],
                                  cache_kr_l=cache_mla_krope[l], state_l=state_ret[l], **shared)
        xp, (a, b, c) = block(xp, c_prompt, norm_mix[l], norm_ffn[l], w_ada[l], b_ada[l],
                              w_ffn_in[l], w_ffn_out[l], mix_p)
        ckv_p.append(a); kr_p.append(b); ret_p.append(c)
        xs, (a, b, c) = block(xs, c_sample, norm_mix[l], norm_ffn[l], w_ada[l], b_ada[l],
                              w_ffn_in[l], w_ffn_out[l], mix_s)
        ckv_s.append(a); kr_s.append(b); ret_s.append(c)
    y_prompt = rms_norm(xp, norm_final)
    y_sample = rms_norm(xs, norm_final)
    return (y_prompt, y_sample, jnp.stack(ckv_p), jnp.stack(kr_p), jnp.stack(ret_p),
            jnp.stack(ckv_s), jnp.stack(kr_s), jnp.stack(ret_s))
```

```python
import functools
import math

import jax
import jax.numpy as jnp
from jax import lax
from jax.experimental import pallas as pl
from jax.experimental.pallas import tpu as pltpu

D_MODEL = 2048
DEPTH = 4
PAST_LEN = 1024
CHUNK = 64
RET_HEADS = 8
RET_DK = D_MODEL // RET_HEADS
RET_DV = D_MODEL // RET_HEADS
MLA_HEADS = 16
MLA_Q_LORA = D_MODEL // 4
MLA_KV_LORA = D_MODEL // 4
MLA_NOPE = 128
MLA_ROPE = 64
MLA_V = D_MODEL // MLA_HEADS
FFN_HIDDEN = -(-8 * D_MODEL // (3 * 256)) * 256
ROPE_BASE = 10000.0
RET_GAMMA_EXP0 = 5.0
RMS_EPS = 1e-6
GN_EPS = 1e-5
NEG_INF = -1e30

F32 = jnp.float32
BF16 = jnp.bfloat16

V7X_LANES = 128
V7X_VMEM_LIMIT_CAP = 56 * 1024 * 1024

MAIN_COLS = 6 * D_MODEL + MLA_Q_LORA + MLA_KV_LORA
HEAD_QK = 2 * V7X_LANES
HALF_ROPE = MLA_ROPE // 2
ADA_ROW_ALIGN = 16
TOKEN_TILE = 512
IN_TN = 1024
FFN_TH = 512
RET_L_PROMPT = 256
ATT_TQ = 512


def _cparams(sem, est_bytes):
    return pltpu.CompilerParams(dimension_semantics=sem,
                                vmem_limit_bytes=min(int(est_bytes), V7X_VMEM_LIMIT_CAP))


def _sigmoid(x):
    return jax.nn.sigmoid(x)


def _rms(x, g):
    return x * lax.rsqrt(jnp.mean(x * x, axis=-1, keepdims=True) + RMS_EPS) * g


def _norm_mod(x, g, sc, sh):
    tm, d = x.shape
    groups = sc.shape[0]
    y = _rms(x, g)
    if groups == 1:
        return y * (1.0 + sc) + sh
    y3 = y.reshape(groups, tm // groups, d)
    return (y3 * (1.0 + sc[:, None, :]) + sh[:, None, :]).reshape(tm, d)


def _gate_res(x, gate, upd):
    tm, n = x.shape
    groups = gate.shape[0]
    if groups == 1:
        return x + gate * upd
    return x + (gate[:, None, :] * upd.reshape(groups, tm // groups, n)).reshape(tm, n)


def _ada_kernel(c_ref, w_ref, b_ref, o_ref):
    c = c_ref[...]
    a = (c * _sigmoid(c)).astype(BF16)
    o_ref[0] = jnp.dot(a, w_ref[0], preferred_element_type=F32) + b_ref[0]


def _ada_call(c_all, w_ada_b, b_ada):
    nb = c_all.shape[0]
    n = w_ada_b.shape[-1]
    tn = D_MODEL
    est = 2 * (D_MODEL * tn * 2) + 4 * nb * (D_MODEL + 2 * tn) * 4 + (4 << 20)
    return pl.pallas_call(
        _ada_kernel,
        grid=(DEPTH, n // tn),
        in_specs=[pl.BlockSpec((nb, D_MODEL), lambda l, j: (0, 0)),
                  pl.BlockSpec((1, D_MODEL, tn), lambda l, j: (l, 0, j)),
                  pl.BlockSpec((1, 1, tn), lambda l, j: (l, 0, j))],
        out_specs=pl.BlockSpec((1, nb, tn), lambda l, j: (l, 0, j)),
        out_shape=jax.ShapeDtypeStruct((DEPTH, nb, n), F32),
        compiler_params=_cparams(("parallel", "parallel"), est),
        name="ada_mod",
    )(c_all, w_ada_b, b_ada.reshape(DEPTH, 1, n))


def _inproj_kernel(x_ref, sc_ref, sh_ref, g_ref, w_ref, wkr_ref, o_ref, kr_ref, h_scr):
    @pl.when(pl.program_id(1) == 0)
    def _():
        hb = _norm_mod(x_ref[...], g_ref[...], sc_ref[...], sh_ref[...]).astype(BF16)
        h_scr[...] = hb
        kr_ref[...] = jnp.dot(hb, wkr_ref[...], preferred_element_type=F32)

    o_ref[...] = jnp.dot(h_scr[...], w_ref[...], preferred_element_type=F32)


def _inproj_call(x, mod, norm_g, w_main, w_kr, l, tm, groups):
    m = x.shape[0]
    est = (2 * tm * D_MODEL * 4 + tm * D_MODEL * 2 + 2 * D_MODEL * IN_TN * 2 + 2 * tm * IN_TN * 4
           + 2 * tm * D_MODEL * 4 + (4 << 20))
    mod_spec = lambda k: pl.BlockSpec((None, None, None, groups, D_MODEL), lambda i, j: (l, k, i, 0, 0))
    return pl.pallas_call(
        _inproj_kernel,
        grid=(m // tm, MAIN_COLS // IN_TN),
        in_specs=[pl.BlockSpec((tm, D_MODEL), lambda i, j: (i, 0)),
                  mod_spec(1), mod_spec(0),
                  pl.BlockSpec((None, 1, D_MODEL), lambda i, j: (l, 0, 0)),
                  pl.BlockSpec((None, D_MODEL, IN_TN), lambda i, j: (l, 0, j)),
                  pl.BlockSpec((None, D_MODEL, V7X_LANES), lambda i, j: (l, 0, 0))],
        out_specs=[pl.BlockSpec((tm, IN_TN), lambda i, j: (i, j)),
                   pl.BlockSpec((tm, V7X_LANES), lambda i, j: (i, 0))],
        out_shape=[jax.ShapeDtypeStruct((m, MAIN_COLS), F32),
                   jax.ShapeDtypeStruct((m, V7X_LANES), F32)],
        scratch_shapes=[pltpu.VMEM((tm, D_MODEL), BF16)],
        compiler_params=_cparams(("parallel", "arbitrary"), est),
        name="in_proj",
    )(x, mod, mod, norm_g, w_main, w_kr)


def _ret_kernel(lg_ref, q_ref, k_ref, v_ref, rg_ref, ga_ref, cos_ref, sin_ref, s0_ref,
                a_ref, st_ref, dm_scr, *, chunk_len):
    h = pl.program_id(1)
    c = pl.program_id(2)
    lg = lg_ref[h]
    L = chunk_len

    @pl.when(c == 0)
    def _():
        st_ref[...] = s0_ref[...]
        ri = lax.broadcasted_iota(jnp.int32, (L, L), 0)
        ci = lax.broadcasted_iota(jnp.int32, (L, L), 1)
        diff = (ri - ci).astype(F32)
        dm_scr[...] = jnp.where(diff >= 0, jnp.exp(jnp.maximum(diff, 0.0) * lg), 0.0)

    cos = cos_ref[...]
    sin = sin_ref[...]
    half = RET_DK // 2

    def rope(x):
        x1, x2 = x[:, :half], x[:, half:]
        return jnp.concatenate([x1 * cos - x2 * sin, x1 * sin + x2 * cos], axis=-1)

    q = rope(q_ref[...])
    k = rope(k_ref[...]) * (RET_DK ** -0.5)
    vb = v_ref[...].astype(BF16)
    idx = lax.broadcasted_iota(jnp.int32, (L, 1), 0).astype(F32)
    xi = jnp.exp((idx + 1.0) * lg)
    zeta = jnp.exp((L - 1.0 - idx) * lg)
    g_l = jnp.exp(jnp.full((1, 1), float(L), F32) * lg)

    qb = q.astype(BF16)
    kb = k.astype(BF16)
    scores = lax.dot_general(qb, kb, (((1,), (1,)), ((), ())), preferred_element_type=F32) * dm_scr[...]
    st = st_ref[0, 0]
    o = (jnp.dot(scores.astype(BF16), vb, preferred_element_type=F32)
         + jnp.dot(qb, st.astype(BF16), preferred_element_type=F32) * xi)
    kz = (k * zeta).astype(BF16)
    st_ref[0, 0] = st * g_l + lax.dot_general(kz, vb, (((0,), (0,)), ((), ())),
                                              preferred_element_type=F32)

    mu = jnp.mean(o, axis=-1, keepdims=True)
    d = o - mu
    var = jnp.mean(d * d, axis=-1, keepdims=True)
    on = d * lax.rsqrt(var + GN_EPS)
    rg = rg_ref[...]
    a_ref[...] = _sigmoid(ga_ref[...]) * ((rg * _sigmoid(rg)) * on)


def _ret_call(proj, log_g, cos, sin, state0, batch, seq, chunk_len):
    L = chunk_len
    nc = seq // L
    col = lambda base: (lambda b, h, c: (b * nc + c, base + h))
    blk = lambda base: pl.BlockSpec((L, RET_DK), col(base))
    nh = RET_HEADS
    est = 2 * 6 * L * RET_DK * 4 + 4 * RET_DK * RET_DV * 4 + L * L * 4 + 8 * L * max(L, RET_DK) * 4 + (4 << 20)
    return pl.pallas_call(
        functools.partial(_ret_kernel, chunk_len=L),
        grid=(batch, nh, nc),
        in_specs=[pl.BlockSpec(memory_space=pltpu.SMEM),
                  blk(0), blk(nh), blk(2 * nh), blk(3 * nh), blk(4 * nh),
                  pl.BlockSpec((L, RET_DK // 2), lambda b, h, c: (c, 0)),
                  pl.BlockSpec((L, RET_DK // 2), lambda b, h, c: (c, 0)),
                  pl.BlockSpec((1, 1, RET_DK, RET_DV), lambda b, h, c: (b, h, 0, 0))],
        out_specs=[pl.BlockSpec((L, RET_DV), lambda b, h, c: (b * nc + c, h)),
                   pl.BlockSpec((1, 1, RET_DK, RET_DV), lambda b, h, c: (b, h, 0, 0))],
        out_shape=[jax.ShapeDtypeStruct((batch * seq, D_MODEL), F32),
                   jax.ShapeDtypeStruct((batch, nh, RET_DK, RET_DV), F32)],
        scratch_shapes=[pltpu.VMEM((L, L), F32)],
        compiler_params=_cparams(("parallel", "parallel", "arbitrary"), est),
        name="retention",
    )(log_g, proj, proj, proj, proj, proj, cos, sin, state0)


def _rope128(x, c, s):
    return x * c + pltpu.roll(x, V7X_LANES // 2, 1) * s


def _mla_q_kernel(dq_ref, dkv_ref, kr_ref, qn_ref, kvn_ref, c_ref, s_ref, wuq_ref,
                  q_out, ckv_out, kro_out):
    c = c_ref[...]
    s = s_ref[...]
    cq = _rms(dq_ref[...], qn_ref[...]).astype(BF16)
    for h in range(MLA_HEADS):
        lo = h * HEAD_QK
        qh = jnp.dot(cq, wuq_ref[:, lo:lo + HEAD_QK], preferred_element_type=F32)
        q_out[:, lo:lo + MLA_NOPE] = qh[:, :MLA_NOPE].astype(BF16)
        q_out[:, lo + MLA_NOPE:lo + HEAD_QK] = _rope128(qh[:, MLA_NOPE:], c, s).astype(BF16)
    ckv_out[...] = _rms(dkv_ref[...], kvn_ref[...])
    kro_out[...] = _rope128(kr_ref[...], c, s)


def _mla_q_call(proj, kr_raw, q_norm, kv_norm, rope_c, rope_s, w_uq_r, l, tm):
    m = proj.shape[0]
    dq_blk = (6 * D_MODEL) // MLA_Q_LORA
    qw = MLA_HEADS * HEAD_QK
    est = (2 * (2 * tm * MLA_Q_LORA * 4 + 3 * tm * V7X_LANES * 4) + 2 * MLA_Q_LORA * qw * 2
           + 2 * tm * qw * 2 + 2 * tm * MLA_KV_LORA * 4 + 2 * tm * V7X_LANES * 4 + 8 * tm * HEAD_QK * 4
           + (4 << 20))
    return pl.pallas_call(
        _mla_q_kernel,
        grid=(m // tm,),
        in_specs=[pl.BlockSpec((tm, MLA_Q_LORA), lambda i: (i, dq_blk)),
                  pl.BlockSpec((tm, MLA_KV_LORA), lambda i: (i, dq_blk + 1)),
                  pl.BlockSpec((tm, V7X_LANES), lambda i: (i, 0)),
                  pl.BlockSpec((None, 1, MLA_Q_LORA), lambda i: (l, 0, 0)),
                  pl.BlockSpec((None, 1, MLA_KV_LORA), lambda i: (l, 0, 0)),
                  pl.BlockSpec((tm, V7X_LANES), lambda i: (i, 0)),
                  pl.BlockSpec((tm, V7X_LANES), lambda i: (i, 0)),
                  pl.BlockSpec((None, MLA_Q_LORA, qw), lambda i: (l, 0, 0))],
        out_specs=[pl.BlockSpec((tm, qw), lambda i: (i, 0)),
                   pl.BlockSpec((tm, MLA_KV_LORA), lambda i: (i, 0)),
                   pl.BlockSpec((tm, V7X_LANES), lambda i: (i, 0))],
        out_shape=[jax.ShapeDtypeStruct((m, qw), BF16),
                   jax.ShapeDtypeStruct((m, MLA_KV_LORA), F32),
                   jax.ShapeDtypeStruct((m, V7X_LANES), F32)],
        compiler_params=_cparams(("parallel",), est),
        name="mla_latents",
    )(proj, proj, kr_raw, q_norm, kv_norm, rope_c, rope_s, w_uq_r)


def _mla_kv_kernel(ckv_ref, kr_ref, wukv_ref, kcat_out, v_out):
    cb = ckv_ref[...].astype(BF16)
    krb = kr_ref[...].astype(BF16)
    for hp in range(MLA_HEADS // 2):
        kk = jnp.dot(cb, wukv_ref[:, hp * HEAD_QK:(hp + 1) * HEAD_QK], preferred_element_type=F32)
        for t in range(2):
            lo = (2 * hp + t) * HEAD_QK
            kcat_out[:, lo:lo + MLA_NOPE] = kk[:, t * MLA_NOPE:(t + 1) * MLA_NOPE].astype(BF16)
            kcat_out[:, lo + MLA_NOPE:lo + HEAD_QK] = krb
    nv = MLA_HEADS * MLA_NOPE
    for hp in range(nv // HEAD_QK):
        lo = hp * HEAD_QK
        v_out[:, lo:lo + HEAD_QK] = jnp.dot(cb, wukv_ref[:, nv + lo:nv + lo + HEAD_QK],
                                            preferred_element_type=F32).astype(BF16)


def _mla_kv_call(ckv, kr128, w_ukv_r, l, rows, tm):
    kw = MLA_HEADS * HEAD_QK
    vw = MLA_HEADS * MLA_V
    ww = MLA_HEADS * (MLA_NOPE + MLA_V)
    est = (2 * tm * (MLA_KV_LORA + V7X_LANES) * 4 + 2 * MLA_KV_LORA * ww * 2 + 2 * tm * (kw + vw) * 2
           + 8 * tm * HEAD_QK * 4 + (4 << 20))
    return pl.pallas_call(
        _mla_kv_kernel,
        grid=(rows // tm,),
        in_specs=[pl.BlockSpec((tm, MLA_KV_LORA), lambda i: (i, 0)),
                  pl.BlockSpec((tm, V7X_LANES), lambda i: (i, 0)),
                  pl.BlockSpec((None, MLA_KV_LORA, ww), lambda i: (l, 0, 0))],
        out_specs=[pl.BlockSpec((tm, kw), lambda i: (i, 0)),
                   pl.BlockSpec((tm, vw), lambda i: (i, 0))],
        out_shape=[jax.ShapeDtypeStruct((rows, kw), BF16),
                   jax.ShapeDtypeStruct((rows, vw), BF16)],
        compiler_params=_cparams(("parallel",), est),
        name="mla_kv_expand",
    )(ckv, kr128, w_ukv_r)


def _flash_kernel(q_ref, k_ref, v_ref, gb_ref, o_ref, m_scr, l_scr, acc_scr, *, causal, nk):
    i = pl.program_id(2)
    j = pl.program_id(3)
    scale = (MLA_NOPE + MLA_ROPE) ** -0.5

    @pl.when(j == 0)
    def _():
        m_scr[...] = jnp.full(m_scr.shape, NEG_INF, F32)
        l_scr[...] = jnp.zeros(l_scr.shape, F32)
        acc_scr[...] = jnp.zeros(acc_scr.shape, F32)

    def step(masked):
        s = lax.dot_general(q_ref[...], k_ref[...], (((1,), (1,)), ((), ())),
                            preferred_element_type=F32) * scale
        if masked:
            rc = lax.broadcasted_iota(jnp.int32, s.shape, 0) // CHUNK
            kc = lax.broadcasted_iota(jnp.int32, s.shape, 1) // CHUNK
            s = jnp.where(kc <= rc, s, NEG_INF)
        m_prev = m_scr[...]
        m_new = jnp.maximum(m_prev, jnp.max(s, axis=-1, keepdims=True))
        alpha = jnp.exp(m_prev - m_new)
        p = jnp.exp(s - m_new)
        l_scr[...] = alpha * l_scr[...] + jnp.sum(p, axis=-1, keepdims=True)
        acc_scr[...] = alpha * acc_scr[...] + jnp.dot(p.astype(BF16), v_ref[...],
                                                      preferred_element_type=F32)
        m_scr[...] = m_new

    if causal:
        pl.when(j < i)(lambda: step(False))
        pl.when(j == i)(lambda: step(True))
    else:
        step(False)

    @pl.when(j == nk - 1)
    def _():
        o_ref[...] = _sigmoid(gb_ref[...]) * (acc_scr[...] / l_scr[...])


def _flash_call(q_cat, k_cat, v, proj, batch, sq, sk, tq, tk, causal):
    nq = sq // tq
    nk = sk // tk
    gb_blk = (5 * D_MODEL) // MLA_V
    if causal:
        kv_row = lambda b, h, i, j: b * nk + jnp.minimum(j, i)
    else:
        kv_row = lambda b, h, i, j: b * nk + j
    est = (2 * (tq * HEAD_QK * 2 + tk * HEAD_QK * 2 + tk * MLA_V * 2 + 2 * tq * MLA_V * 4)
           + 3 * tq * V7X_LANES * 4 + 6 * tq * tk * 4 + (4 << 20))
    return pl.pallas_call(
        functools.partial(_flash_kernel, causal=causal, nk=nk),
        grid=(batch, MLA_HEADS, nq, nk),
        in_specs=[pl.BlockSpec((tq, HEAD_QK), lambda b, h, i, j: (b * nq + i, h)),
                  pl.BlockSpec((tk, HEAD_QK), lambda b, h, i, j: (kv_row(b, h, i, j), h)),
                  pl.BlockSpec((tk, MLA_V), lambda b, h, i, j: (kv_row(b, h, i, j), h)),
                  pl.BlockSpec((tq, MLA_V), lambda b, h, i, j: (b * nq + i, gb_blk + h))],
        out_specs=pl.BlockSpec((tq, MLA_V), lambda b, h, i, j: (b * nq + i, h)),
        out_shape=jax.ShapeDtypeStruct((batch * sq, MLA_HEADS * MLA_V), F32),
        scratch_shapes=[pltpu.VMEM((tq, 1), F32), pltpu.VMEM((tq, 1), F32), pltpu.VMEM((tq, MLA_V), F32)],
        compiler_params=_cparams(("parallel", "parallel", "parallel", "arbitrary"), est),
        name="mla_attention",
    )(q_cat, k_cat, v, proj)


def _merge_kernel(a_ref, b_ref, x_ref, g1_ref, w_ref, o_ref, m_scr):
    @pl.when(pl.program_id(1) == 0)
    def _():
        m_scr[...] = (a_ref[...] + b_ref[...]).astype(BF16)

    mix = jnp.dot(m_scr[...], w_ref[...], preferred_element_type=F32)
    o_ref[...] = _gate_res(x_ref[...], g1_ref[...], mix)


def _merge_call(a_part, b_part, x, mod, w_o_b, l, tm, groups):
    m = x.shape[0]
    tn = IN_TN
    est = (2 * 2 * tm * D_MODEL * 4 + tm * D_MODEL * 2 + 2 * D_MODEL * tn * 2 + 4 * tm * tn * 4
           + 2 * tm * tn * 4 + (4 << 20))
    return pl.pallas_call(
        _merge_kernel,
        grid=(m // tm, D_MODEL // tn),
        in_specs=[pl.BlockSpec((tm, D_MODEL), lambda i, j: (i, 0)),
                  pl.BlockSpec((tm, D_MODEL), lambda i, j: (i, 0)),
                  pl.BlockSpec((tm, tn), lambda i, j: (i, j)),
                  pl.BlockSpec((None, None, None, groups, tn), lambda i, j: (l, 2, i, 0, j)),
                  pl.BlockSpec((None, D_MODEL, tn), lambda i, j: (l, 0, j))],
        out_specs=pl.BlockSpec((tm, tn), lambda i, j: (i, j)),
        out_shape=jax.ShapeDtypeStruct((m, D_MODEL), F32),
        scratch_shapes=[pltpu.VMEM((tm, D_MODEL), BF16)],
        compiler_params=_cparams(("parallel", "arbitrary"), est),
        name="merge_out_proj",
    )(a_part, b_part, x, mod, w_o_b)


def _ffn_kernel(x_ref, sc_ref, sh_ref, g2_ref, gn_ref, wg_ref, wu_ref, wo_ref, o_ref, h_scr, acc_scr,
                *, nh):
    j = pl.program_id(1)

    @pl.when(j == 0)
    def _():
        h_scr[...] = _norm_mod(x_ref[...], gn_ref[...], sc_ref[...], sh_ref[...]).astype(BF16)
        acc_scr[...] = jnp.zeros(acc_scr.shape, F32)

    hb = h_scr[...]
    gate = jnp.dot(hb, wg_ref[...], preferred_element_type=F32)
    up = jnp.dot(hb, wu_ref[...], preferred_element_type=F32)
    act = ((gate * _sigmoid(gate)) * up).astype(BF16)
    acc_scr[...] += jnp.dot(act, wo_ref[...], preferred_element_type=F32)

    @pl.when(j == nh - 1)
    def _():
        o_ref[...] = _gate_res(x_ref[...], g2_ref[...], acc_scr[...])


def _ffn_call(x, mod, norm_g, w_in_b, w_out_b, l, tm, groups):
    m = x.shape[0]
    th = FFN_TH
    nh = FFN_HIDDEN // th
    est = (4 * tm * D_MODEL * 4 + tm * D_MODEL * 2 + tm * D_MODEL * 4 + 2 * 3 * D_MODEL * th * 2
           + 4 * tm * th * 4 + tm * D_MODEL * 4 + (4 << 20))
    mod_spec = lambda k: pl.BlockSpec((None, None, None, groups, D_MODEL), lambda i, j: (l, k, i, 0, 0))
    return pl.pallas_call(
        functools.partial(_ffn_kernel, nh=nh),
        grid=(m // tm, nh),
        in_specs=[pl.BlockSpec((tm, D_MODEL), lambda i, j: (i, 0)),
                  mod_spec(4), mod_spec(3), mod_spec(5),
                  pl.BlockSpec((None, 1, D_MODEL), lambda i, j: (l, 0, 0)),
                  pl.BlockSpec((None, D_MODEL, th), lambda i, j: (l, 0, j)),
                  pl.BlockSpec((None, D_MODEL, th), lambda i, j: (l, 0, nh + j)),
                  pl.BlockSpec((None, th, D_MODEL), lambda i, j: (l, j, 0))],
        out_specs=pl.BlockSpec((tm, D_MODEL), lambda i, j: (i, 0)),
        out_shape=jax.ShapeDtypeStruct((m, D_MODEL), F32),
        scratch_shapes=[pltpu.VMEM((tm, D_MODEL), BF16), pltpu.VMEM((tm, D_MODEL), F32)],
        compiler_params=_cparams(("parallel", "arbitrary"), est),
        name="ffn_swiglu",
    )(x, mod, mod, mod, norm_g, w_in_b, w_in_b, w_out_b)


def _final_norm_kernel(x_ref, g_ref, o_ref):
    o_ref[...] = _rms(x_ref[...], g_ref[...])


def _final_norm_call(x, g, tm):
    m = x.shape[0]
    est = 4 * tm * D_MODEL * 4 + 2 * tm * D_MODEL * 4 + (4 << 20)
    return pl.pallas_call(
        _final_norm_kernel,
        grid=(m // tm,),
        in_specs=[pl.BlockSpec((tm, D_MODEL), lambda i: (i, 0)),
                  pl.BlockSpec((1, D_MODEL), lambda i: (0, 0))],
        out_specs=pl.BlockSpec((tm, D_MODEL), lambda i: (i, 0)),
        out_shape=jax.ShapeDtypeStruct((m, D_MODEL), F32),
        compiler_params=_cparams(("parallel",), est),
        name="final_norm",
    )(x, g.reshape(1, D_MODEL))


def _pack_rope_lanes(x):
    z = jnp.zeros(x.shape[:-1] + (HALF_ROPE,), x.dtype)
    return jnp.concatenate([x[..., :HALF_ROPE], z, x[..., HALF_ROPE:], z], axis=-1)


def _unpack_rope_lanes(x):
    return jnp.concatenate([x[..., :HALF_ROPE], x[..., 2 * HALF_ROPE:3 * HALF_ROPE]], axis=-1)


def _rope_tables(pos, dim):
    inv = jnp.exp(-math.log(ROPE_BASE) * jnp.arange(0, dim, 2, dtype=F32) / dim)
    ang = pos.astype(F32)[:, None] * inv[None, :]
    return jnp.cos(ang), jnp.sin(ang)


def _mla_rope_tables(pos, reps):
    cos, sin = _rope_tables(pos, MLA_ROPE)
    z = jnp.zeros_like(cos)
    c = jnp.concatenate([cos, z, cos, z], axis=-1)
    s = jnp.concatenate([-sin, z, sin, z], axis=-1)
    return jnp.tile(c, (reps, 1)), jnp.tile(s, (reps, 1))


def _prep_weights(w_in, w_uq, w_ukv):
    e = RET_HEADS * RET_DK
    o = [0, e, 2 * e, 3 * e, 4 * e, 4 * e + MLA_Q_LORA, 4 * e + MLA_Q_LORA + MLA_KV_LORA]
    o.append(o[-1] + MLA_ROPE)
    o.append(o[-1] + D_MODEL)
    o.append(o[-1] + D_MODEL)
    sl = lambda a, b: w_in[..., o[a]:o[b]]
    w_main = jnp.concatenate([sl(0, 4), sl(7, 9), sl(4, 6)], axis=-1).astype(BF16)
    w_kr = _pack_rope_lanes(sl(6, 7)).astype(BF16)
    uq = w_uq.reshape(DEPTH, MLA_Q_LORA, MLA_HEADS, MLA_NOPE + MLA_ROPE)
    uq = jnp.concatenate([uq[..., :MLA_NOPE], _pack_rope_lanes(uq[..., MLA_NOPE:])], axis=-1)
    w_uq_r = uq.reshape(DEPTH, MLA_Q_LORA, MLA_HEADS * HEAD_QK).astype(BF16)
    ukv = w_ukv.reshape(DEPTH, MLA_KV_LORA, MLA_HEADS, MLA_NOPE + MLA_V)
    w_ukv_r = jnp.concatenate([ukv[..., :MLA_NOPE].reshape(DEPTH, MLA_KV_LORA, -1),
                               ukv[..., MLA_NOPE:].reshape(DEPTH, MLA_KV_LORA, -1)], axis=-1).astype(BF16)
    return w_main, w_kr, w_uq_r, w_ukv_r


def _layer(l, x, mod, groups, tm, batch, seq, ret_len, ret_tabs, mla_tabs, state0, kv_source, att_cfg,
           log_g, W):
    proj, kr_raw = _inproj_call(x, mod, W["norm_mix"], W["w_main"], W["w_kr"], l, tm, groups)
    a_part, new_state = _ret_call(proj, log_g, ret_tabs[0], ret_tabs[1], state0, batch, seq, ret_len)
    q_cat, ckv, kr128 = _mla_q_call(proj, kr_raw, W["q_norm"], W["kv_norm"], mla_tabs[0], mla_tabs[1],
                                    W["w_uq_r"], l, tm)
    ckv_all, kr_all, sk = kv_source(ckv, kr128)
    k_cat, v = _mla_kv_call(ckv_all, kr_all, W["w_ukv_r"], l, ckv_all.shape[0], tm)
    tq, tk, causal = att_cfg
    b_part = _flash_call(q_cat, k_cat, v, proj, batch, seq, sk, tq, tk, causal)
    x = _merge_call(a_part, b_part, x, mod, W["w_o"], l, tm, groups)
    x = _ffn_call(x, mod, W["norm_ffn"], W["w_ffn_in"], W["w_ffn_out"], l, tm, groups)
    return x, ckv, kr128, new_state


def kernel(x_prompt, x_sample, c_prompt, c_sample, cache_mla_ckv, cache_mla_krope, state_ret, w_ada, b_ada,
           norm_mix, norm_ffn, w_in, mla_q_norm, w_uq, mla_kv_norm, w_ukv, w_o, w_ffn_in, w_ffn_out,
           norm_final):
    bp, sp, _ = x_prompt.shape
    bs, ss, _ = x_sample.shape
    tm = TOKEN_TILE
    assert sp % tm == 0 and (bs * ss) % tm == 0 and tm % ss == 0

    w_main, w_kr, w_uq_r, w_ukv_r = _prep_weights(w_in, w_uq, w_ukv)
    W = dict(w_main=w_main, w_kr=w_kr, w_uq_r=w_uq_r, w_ukv_r=w_ukv_r,
             w_o=w_o.astype(BF16), w_ffn_in=w_ffn_in.astype(BF16), w_ffn_out=w_ffn_out.astype(BF16),
             norm_mix=norm_mix.reshape(DEPTH, 1, D_MODEL), norm_ffn=norm_ffn.reshape(DEPTH, 1, D_MODEL),
             q_norm=mla_q_norm.reshape(DEPTH, 1, MLA_Q_LORA), kv_norm=mla_kv_norm.reshape(DEPTH, 1, MLA_KV_LORA))

    c_rows = -(-(bp + bs) // ADA_ROW_ALIGN) * ADA_ROW_ALIGN
    c_all = jnp.concatenate([c_prompt, c_sample, jnp.zeros((c_rows - bp - bs, D_MODEL), F32)], axis=0)
    mod_all = _ada_call(c_all, w_ada.astype(BF16), b_ada)[:, :bp + bs]
    mod_all = mod_all.reshape(DEPTH, bp + bs, 6, D_MODEL).transpose(0, 2, 1, 3)
    tiles_per_batch = sp // tm
    mod_p = jnp.repeat(mod_all[:, :, :bp], tiles_per_batch, axis=2)[:, :, :, None, :]
    groups_s = tm // ss
    mod_s = mod_all[:, :, bp:].reshape(DEPTH, 6, (bs * ss) // tm, groups_s, D_MODEL)

    log_g = jnp.log1p(-jnp.exp2(-RET_GAMMA_EXP0 - jnp.arange(RET_HEADS, dtype=F32)))
    pos_p = jnp.arange(sp)
    pos_s = PAST_LEN + jnp.arange(ss)
    ret_tabs_p = _rope_tables(pos_p, RET_DK)
    ret_tabs_s = _rope_tables(pos_s, RET_DK)
    mla_tabs_p = _mla_rope_tables(pos_p, bp)
    mla_tabs_s = _mla_rope_tables(pos_s, bs)
    zero_state = jnp.zeros((bp, RET_HEADS, RET_DK, RET_DV), F32)

    xp = x_prompt.reshape(bp * sp, D_MODEL)
    xs = x_sample.reshape(bs * ss, D_MODEL)
    outs = [[] for _ in range(6)]
    for l in range(DEPTH):
        xp, ckv, kr128, st = _layer(
            l, xp, mod_p, 1, tm, bp, sp, RET_L_PROMPT, ret_tabs_p, mla_tabs_p, zero_state,
            lambda ckv, kr: (ckv, kr, sp), (ATT_TQ, ATT_TQ, True), log_g, W)
        outs[0].append(ckv.reshape(bp, sp, MLA_KV_LORA))
        outs[1].append(_unpack_rope_lanes(kr128).reshape(bp, sp, MLA_ROPE))
        outs[2].append(st)

        def sample_kv(ckv, kr, l=l):
            ckv_all = jnp.concatenate([cache_mla_ckv[l], ckv.reshape(bs, ss, MLA_KV_LORA)], axis=1)
            kr_all = jnp.concatenate([_pack_rope_lanes(cache_mla_krope[l]),
                                      kr.reshape(bs, ss, V7X_LANES)], axis=1)
            sk = PAST_LEN + ss
            return ckv_all.reshape(bs * sk, MLA_KV_LORA), kr_all.reshape(bs * sk, V7X_LANES), sk

        xs, ckv, kr128, st = _layer(
            l, xs, mod_s, groups_s, tm, bs, ss, ss, ret_tabs_s, mla_tabs_s, state_ret[l],
            sample_kv, (ss, PAST_LEN + ss, False), log_g, W)
        outs[3].append(ckv.reshape(bs, ss, MLA_KV_LORA))
        outs[4].append(_unpack_rope_lanes(kr128).reshape(bs, ss, MLA_ROPE))
        outs[5].append(st)

    y_prompt = _final_norm_call(xp, norm_final, tm).reshape(bp, sp, D_MODEL)
    y_sample = _final_norm_call(xs, norm_final, tm).reshape(bs, ss, D_MODEL)
    return (y_prompt, y_sample, jnp.stack(outs[0]), jnp.stack(outs[1]), jnp.stack(outs[2]),
            jnp.stack(outs[3]), jnp.stack(outs[4]), jnp.stack(outs[5]))
```

```python
import functools
import math

import jax
import jax.numpy as jnp
from jax import lax
from jax.experimental import pallas as pl
from jax.experimental.pallas import tpu as pltpu

D_MODEL = 2048
DEPTH = 4
PAST_LEN = 1024
CHUNK = 64
RET_HEADS = 8
RET_DK = D_MODEL // RET_HEADS
RET_DV = D_MODEL // RET_HEADS
MLA_HEADS = 16
MLA_Q_LORA = D_MODEL // 4
MLA_KV_LORA = D_MODEL // 4
MLA_NOPE = 128
MLA_ROPE = 64
MLA_V = D_MODEL // MLA_HEADS
FFN_HIDDEN = -(-8 * D_MODEL // (3 * 256)) * 256
ROPE_BASE = 10000.0
RET_GAMMA_EXP0 = 5.0
RMS_EPS = 1e-6
GN_EPS = 1e-5
NEG_INF = -1e30

F32 = jnp.float32
BF16 = jnp.bfloat16

V7X_LANES = 128
V7X_VMEM_LIMIT_CAP = 56 * 1024 * 1024

MAIN_COLS = 6 * D_MODEL + MLA_Q_LORA + MLA_KV_LORA
HEAD_QK = 2 * V7X_LANES
HALF_ROPE = MLA_ROPE // 2
ADA_ROW_ALIGN = 16
TOKEN_TILE = 512
IN_TN = 1024
FFN_TH = 512
RET_L_PROMPT = 256
ATT_TQ = 1024
ATT_TK = 512
ATT_SPLIT = 4
QK_SCALE_LOG2 = (MLA_NOPE + MLA_ROPE) ** -0.5 * math.log2(math.e)


def _cparams(sem, est_bytes):
    return pltpu.CompilerParams(dimension_semantics=sem,
                                vmem_limit_bytes=min(int(est_bytes), V7X_VMEM_LIMIT_CAP))


def _sigmoid(x):
    return jax.nn.sigmoid(x)


def _rms(x, g):
    return x * lax.rsqrt(jnp.mean(x * x, axis=-1, keepdims=True) + RMS_EPS) * g


def _norm_mod(x, g, sc, sh):
    tm, d = x.shape
    groups = sc.shape[0]
    y = _rms(x, g)
    if groups == 1:
        return y * (1.0 + sc) + sh
    y3 = y.reshape(groups, tm // groups, d)
    return (y3 * (1.0 + sc[:, None, :]) + sh[:, None, :]).reshape(tm, d)


def _gate_res(x, gate, upd):
    tm, n = x.shape
    groups = gate.shape[0]
    if groups == 1:
        return x + gate * upd
    return x + (gate[:, None, :] * upd.reshape(groups, tm // groups, n)).reshape(tm, n)


def _ada_kernel(c_ref, w_ref, b_ref, o_ref):
    c = c_ref[...]
    a = (c * _sigmoid(c)).astype(BF16)
    o_ref[0] = jnp.dot(a, w_ref[0], preferred_element_type=F32) + b_ref[0]


def _ada_call(c_all, w_ada_b, b_ada):
    nb = c_all.shape[0]
    n = w_ada_b.shape[-1]
    tn = D_MODEL
    est = 2 * (D_MODEL * tn * 2) + 4 * nb * (D_MODEL + 2 * tn) * 4 + (4 << 20)
    return pl.pallas_call(
        _ada_kernel,
        grid=(DEPTH, n // tn),
        in_specs=[pl.BlockSpec((nb, D_MODEL), lambda l, j: (0, 0)),
                  pl.BlockSpec((1, D_MODEL, tn), lambda l, j: (l, 0, j)),
                  pl.BlockSpec((1, 1, tn), lambda l, j: (l, 0, j))],
        out_specs=pl.BlockSpec((1, nb, tn), lambda l, j: (l, 0, j)),
        out_shape=jax.ShapeDtypeStruct((DEPTH, nb, n), F32),
        compiler_params=_cparams(("parallel", "parallel"), est),
        name="ada_mod",
    )(c_all, w_ada_b, b_ada.reshape(DEPTH, 1, n))


def _inproj_kernel(x_ref, sc_ref, sh_ref, g_ref, w_ref, wkr_ref, o_ref, kr_ref, h_scr):
    @pl.when(pl.program_id(1) == 0)
    def _():
        hb = _norm_mod(x_ref[...], g_ref[...], sc_ref[...], sh_ref[...]).astype(BF16)
        h_scr[...] = hb
        kr_ref[...] = jnp.dot(hb, wkr_ref[...], preferred_element_type=F32)

    o_ref[...] = jnp.dot(h_scr[...], w_ref[...], preferred_element_type=F32)


def _inproj_call(x, mod, norm_g, w_main, w_kr, l, tm, groups):
    m = x.shape[0]
    est = (2 * tm * D_MODEL * 4 + tm * D_MODEL * 2 + 2 * D_MODEL * IN_TN * 2 + 2 * tm * IN_TN * 4
           + 2 * tm * D_MODEL * 4 + (4 << 20))
    mod_spec = lambda k: pl.BlockSpec((None, None, None, groups, D_MODEL), lambda i, j: (l, k, i, 0, 0))
    return pl.pallas_call(
        _inproj_kernel,
        grid=(m // tm, MAIN_COLS // IN_TN),
        in_specs=[pl.BlockSpec((tm, D_MODEL), lambda i, j: (i, 0)),
                  mod_spec(1), mod_spec(0),
                  pl.BlockSpec((None, 1, D_MODEL), lambda i, j: (l, 0, 0)),
                  pl.BlockSpec((None, D_MODEL, IN_TN), lambda i, j: (l, 0, j)),
                  pl.BlockSpec((None, D_MODEL, V7X_LANES), lambda i, j: (l, 0, 0))],
        out_specs=[pl.BlockSpec((tm, IN_TN), lambda i, j: (i, j)),
                   pl.BlockSpec((tm, V7X_LANES), lambda i, j: (i, 0))],
        out_shape=[jax.ShapeDtypeStruct((m, MAIN_COLS), F32),
                   jax.ShapeDtypeStruct((m, V7X_LANES), F32)],
        scratch_shapes=[pltpu.VMEM((tm, D_MODEL), BF16)],
        compiler_params=_cparams(("parallel", "arbitrary"), est),
        name="in_proj",
    )(x, mod, mod, norm_g, w_main, w_kr)


def _ret_kernel(lg_ref, q_ref, k_ref, v_ref, rg_ref, ga_ref, cos_ref, sin_ref, s0_ref,
                a_ref, st_ref, dm_scr, *, chunk_len):
    h = pl.program_id(1)
    c = pl.program_id(2)
    lg = lg_ref[h]
    L = chunk_len

    @pl.when(c == 0)
    def _():
        st_ref[...] = s0_ref[...]
        ri = lax.broadcasted_iota(jnp.int32, (L, L), 0)
        ci = lax.broadcasted_iota(jnp.int32, (L, L), 1)
        diff = (ri - ci).astype(F32)
        dm_scr[...] = jnp.where(diff >= 0, jnp.exp(jnp.maximum(diff, 0.0) * lg), 0.0)

    cos = cos_ref[...]
    sin = sin_ref[...]
    half = RET_DK // 2

    def rope(x):
        x1, x2 = x[:, :half], x[:, half:]
        return jnp.concatenate([x1 * cos - x2 * sin, x1 * sin + x2 * cos], axis=-1)

    q = rope(q_ref[...])
    k = rope(k_ref[...]) * (RET_DK ** -0.5)
    vb = v_ref[...].astype(BF16)
    idx = lax.broadcasted_iota(jnp.int32, (L, 1), 0).astype(F32)
    xi = jnp.exp((idx + 1.0) * lg)
    zeta = jnp.exp((L - 1.0 - idx) * lg)
    g_l = jnp.exp(jnp.full((1, 1), float(L), F32) * lg)

    qb = q.astype(BF16)
    kb = k.astype(BF16)
    scores = lax.dot_general(qb, kb, (((1,), (1,)), ((), ())), preferred_element_type=F32) * dm_scr[...]
    st = st_ref[0, 0]
    o = (jnp.dot(scores.astype(BF16), vb, preferred_element_type=F32)
         + jnp.dot(qb, st.astype(BF16), preferred_element_type=F32) * xi)
    kz = (k * zeta).astype(BF16)
    st_ref[0, 0] = st * g_l + lax.dot_general(kz, vb, (((0,), (0,)), ((), ())),
                                              preferred_element_type=F32)

    mu = jnp.mean(o, axis=-1, keepdims=True)
    d = o - mu
    var = jnp.mean(d * d, axis=-1, keepdims=True)
    on = d * lax.rsqrt(var + GN_EPS)
    rg = rg_ref[...]
    a_ref[...] = _sigmoid(ga_ref[...]) * ((rg * _sigmoid(rg)) * on)


def _ret_call(proj, log_g, cos, sin, state0, batch, seq, chunk_len):
    L = chunk_len
    nc = seq // L
    col = lambda base: (lambda b, h, c: (b * nc + c, base + h))
    blk = lambda base: pl.BlockSpec((L, RET_DK), col(base))
    nh = RET_HEADS
    est = 2 * 6 * L * RET_DK * 4 + 4 * RET_DK * RET_DV * 4 + L * L * 4 + 8 * L * max(L, RET_DK) * 4 + (4 << 20)
    return pl.pallas_call(
        functools.partial(_ret_kernel, chunk_len=L),
        grid=(batch, nh, nc),
        in_specs=[pl.BlockSpec(memory_space=pltpu.SMEM),
                  blk(0), blk(nh), blk(2 * nh), blk(3 * nh), blk(4 * nh),
                  pl.BlockSpec((L, RET_DK // 2), lambda b, h, c: (c, 0)),
                  pl.BlockSpec((L, RET_DK // 2), lambda b, h, c: (c, 0)),
                  pl.BlockSpec((1, 1, RET_DK, RET_DV), lambda b, h, c: (b, h, 0, 0))],
        out_specs=[pl.BlockSpec((L, RET_DV), lambda b, h, c: (b * nc + c, h)),
                   pl.BlockSpec((1, 1, RET_DK, RET_DV), lambda b, h, c: (b, h, 0, 0))],
        out_shape=[jax.ShapeDtypeStruct((batch * seq, D_MODEL), F32),
                   jax.ShapeDtypeStruct((batch, nh, RET_DK, RET_DV), F32)],
        scratch_shapes=[pltpu.VMEM((L, L), F32)],
        compiler_params=_cparams(("parallel", "parallel", "arbitrary"), est),
        name="retention",
    )(log_g, proj, proj, proj, proj, proj, cos, sin, state0)


def _rope128(x, c, s):
    return x * c + pltpu.roll(x, V7X_LANES // 2, 1) * s


def _mla_q_kernel(dq_ref, dkv_ref, kr_ref, qn_ref, kvn_ref, c_ref, s_ref, wuq_ref,
                  q_out, ckv_out, kro_out):
    c = c_ref[...]
    s = s_ref[...]
    cq = _rms(dq_ref[...], qn_ref[...]).astype(BF16)
    for h in range(MLA_HEADS):
        lo = h * HEAD_QK
        qh = jnp.dot(cq, wuq_ref[:, lo:lo + HEAD_QK], preferred_element_type=F32) * QK_SCALE_LOG2
        q_out[:, lo:lo + MLA_NOPE] = qh[:, :MLA_NOPE].astype(BF16)
        q_out[:, lo + MLA_NOPE:lo + HEAD_QK] = _rope128(qh[:, MLA_NOPE:], c, s).astype(BF16)
    ckv_out[...] = _rms(dkv_ref[...], kvn_ref[...])
    kro_out[...] = _rope128(kr_ref[...], c, s)


def _mla_q_call(proj, kr_raw, q_norm, kv_norm, rope_c, rope_s, w_uq_r, l, tm):
    m = proj.shape[0]
    dq_blk = (6 * D_MODEL) // MLA_Q_LORA
    qw = MLA_HEADS * HEAD_QK
    est = (2 * (2 * tm * MLA_Q_LORA * 4 + 3 * tm * V7X_LANES * 4) + 2 * MLA_Q_LORA * qw * 2
           + 2 * tm * qw * 2 + 2 * tm * MLA_KV_LORA * 4 + 2 * tm * V7X_LANES * 4 + 8 * tm * HEAD_QK * 4
           + (4 << 20))
    return pl.pallas_call(
        _mla_q_kernel,
        grid=(m // tm,),
        in_specs=[pl.BlockSpec((tm, MLA_Q_LORA), lambda i: (i, dq_blk)),
                  pl.BlockSpec((tm, MLA_KV_LORA), lambda i: (i, dq_blk + 1)),
                  pl.BlockSpec((tm, V7X_LANES), lambda i: (i, 0)),
                  pl.BlockSpec((None, 1, MLA_Q_LORA), lambda i: (l, 0, 0)),
                  pl.BlockSpec((None, 1, MLA_KV_LORA), lambda i: (l, 0, 0)),
                  pl.BlockSpec((tm, V7X_LANES), lambda i: (i, 0)),
                  pl.BlockSpec((tm, V7X_LANES), lambda i: (i, 0)),
                  pl.BlockSpec((None, MLA_Q_LORA, qw), lambda i: (l, 0, 0))],
        out_specs=[pl.BlockSpec((tm, qw), lambda i: (i, 0)),
                   pl.BlockSpec((tm, MLA_KV_LORA), lambda i: (i, 0)),
                   pl.BlockSpec((tm, V7X_LANES), lambda i: (i, 0))],
        out_shape=[jax.ShapeDtypeStruct((m, qw), BF16),
                   jax.ShapeDtypeStruct((m, MLA_KV_LORA), F32),
                   jax.ShapeDtypeStruct((m, V7X_LANES), F32)],
        compiler_params=_cparams(("parallel",), est),
        name="mla_latents",
    )(proj, proj, kr_raw, q_norm, kv_norm, rope_c, rope_s, w_uq_r)


def _mla_kv_kernel(ckv_ref, kr_ref, wukv_ref, kcat_out, v_out):
    cb = ckv_ref[...].astype(BF16)
    krb = kr_ref[...].astype(BF16)
    for hp in range(MLA_HEADS // 2):
        kk = jnp.dot(cb, wukv_ref[:, hp * HEAD_QK:(hp + 1) * HEAD_QK], preferred_element_type=F32)
        for t in range(2):
            lo = (2 * hp + t) * HEAD_QK
            kcat_out[:, lo:lo + MLA_NOPE] = kk[:, t * MLA_NOPE:(t + 1) * MLA_NOPE].astype(BF16)
            kcat_out[:, lo + MLA_NOPE:lo + HEAD_QK] = krb
    nv = MLA_HEADS * MLA_NOPE
    for hp in range(nv // HEAD_QK):
        lo = hp * HEAD_QK
        v_out[:, lo:lo + HEAD_QK] = jnp.dot(cb, wukv_ref[:, nv + lo:nv + lo + HEAD_QK],
                                            preferred_element_type=F32).astype(BF16)


def _mla_kv_call(ckv, kr128, w_ukv_r, l, rows, tm):
    kw = MLA_HEADS * HEAD_QK
    vw = MLA_HEADS * MLA_V
    ww = MLA_HEADS * (MLA_NOPE + MLA_V)
    est = (2 * tm * (MLA_KV_LORA + V7X_LANES) * 4 + 2 * MLA_KV_LORA * ww * 2 + 2 * tm * (kw + vw) * 2
           + 8 * tm * HEAD_QK * 4 + (4 << 20))
    return pl.pallas_call(
        _mla_kv_kernel,
        grid=(rows // tm,),
        in_specs=[pl.BlockSpec((tm, MLA_KV_LORA), lambda i: (i, 0)),
                  pl.BlockSpec((tm, V7X_LANES), lambda i: (i, 0)),
                  pl.BlockSpec((None, MLA_KV_LORA, ww), lambda i: (l, 0, 0))],
        out_specs=[pl.BlockSpec((tm, kw), lambda i: (i, 0)),
                   pl.BlockSpec((tm, vw), lambda i: (i, 0))],
        out_shape=[jax.ShapeDtypeStruct((rows, kw), BF16),
                   jax.ShapeDtypeStruct((rows, vw), BF16)],
        compiler_params=_cparams(("parallel",), est),
        name="mla_kv_expand",
    )(ckv, kr128, w_ukv_r)


def _scores(q, k_blk):
    return lax.dot_general(q, k_blk, (((1,), (1,)), ((), ())), preferred_element_type=F32)


def _softmax_tile(s, v_blk, m_prev, l_prev, acc_prev, mask):
    if mask is not None:
        s = jnp.where(mask, s, NEG_INF)
    n_groups = s.shape[1] // V7X_LANES
    groups = [s[:, g * V7X_LANES:(g + 1) * V7X_LANES] for g in range(n_groups)]
    m_new = jnp.maximum(m_prev, jnp.max(functools.reduce(jnp.maximum, groups), axis=-1, keepdims=True))
    alpha = jnp.exp2(m_prev - m_new)
    ps = [jnp.exp2(g - m_new) for g in groups]
    l_new = alpha * l_prev + functools.reduce(jnp.add, ps)
    p = jnp.concatenate([x.astype(BF16) for x in ps], axis=-1)
    acc_new = alpha * acc_prev + jnp.dot(p, v_blk, preferred_element_type=F32)
    return m_new, l_new, acc_new


def _attn_prompt_kernel(q_ref, k_ref, v_ref, gb_ref, o_ref, m_scr, l_scr, acc_scr, *, tq, tk, n_split):
    i = pl.program_id(2)
    rows = tq // n_split
    m_scr[...] = jnp.full(m_scr.shape, NEG_INF, F32)
    l_scr[...] = jnp.zeros(l_scr.shape, F32)
    acc_scr[...] = jnp.zeros(acc_scr.shape, F32)

    def scores(r, k_blk):
        return _scores(q_ref[pl.ds(r * rows, rows), :], k_blk)

    def update(r, s, v_blk, mask):
        rs = pl.ds(r * rows, rows)
        m_new, l_new, acc_new = _softmax_tile(s, v_blk, m_scr[rs, :], l_scr[rs, :], acc_scr[rs, :], mask)
        m_scr[rs, :] = m_new
        l_scr[rs, :] = l_new
        acc_scr[rs, :] = acc_new

    def body(j, carry):
        start = pl.multiple_of(j * tk, tk)
        k_blk = k_ref[pl.ds(start, tk), :]
        v_blk = v_ref[pl.ds(start, tk), :]
        ss = [scores(r, k_blk) for r in range(n_split)]
        for r in range(n_split):
            update(r, ss[r], v_blk, None)
        return carry

    blocks_per_tile = tq // tk
    lax.fori_loop(0, i * blocks_per_tile, body, 0)

    work = []
    for d in range(blocks_per_tile):
        k0 = d * tk
        for r in range(n_split):
            r0, r1 = r * rows, (r + 1) * rows
            width = min(k0 + tk, r1) - k0
            if width <= 0:
                continue
            start = pl.multiple_of(i * tq + k0, tk)
            mask = None
            if k0 + width > r0:
                rc = (lax.broadcasted_iota(jnp.int32, (rows, width), 0) + r0) // CHUNK
                kc = (lax.broadcasted_iota(jnp.int32, (rows, width), 1) + k0) // CHUNK
                mask = kc <= rc
            work.append((r, scores(r, k_ref[pl.ds(start, width), :]), v_ref[pl.ds(start, width), :], mask))
    for r, s, v_blk, mask in work:
        update(r, s, v_blk, mask)

    l_row = jnp.sum(l_scr[...], axis=-1, keepdims=True)
    o_ref[...] = _sigmoid(gb_ref[...]) * (acc_scr[...] / l_row)


def _attn_prompt_call(q_cat, k_cat, v, proj, batch, seq, tq):
    nq = seq // tq
    gb_blk = (5 * D_MODEL) // MLA_V
    est = (2 * (tq * HEAD_QK * 2 + seq * HEAD_QK * 2 + seq * MLA_V * 2 + 2 * tq * MLA_V * 4)
           + 3 * tq * V7X_LANES * 4 + 6 * tq * ATT_TK * 4 + (4 << 20))
    return pl.pallas_call(
        functools.partial(_attn_prompt_kernel, tq=tq, tk=ATT_TK, n_split=ATT_SPLIT),
        grid=(batch, MLA_HEADS, nq),
        in_specs=[pl.BlockSpec((tq, HEAD_QK), lambda b, h, i: (b * nq + i, h)),
                  pl.BlockSpec((seq, HEAD_QK), lambda b, h, i: (b, h)),
                  pl.BlockSpec((seq, MLA_V), lambda b, h, i: (b, h)),
                  pl.BlockSpec((tq, MLA_V), lambda b, h, i: (b * nq + i, gb_blk + h))],
        out_specs=pl.BlockSpec((tq, MLA_V), lambda b, h, i: (b * nq + i, h)),
        out_shape=jax.ShapeDtypeStruct((batch * seq, MLA_HEADS * MLA_V), F32),
        scratch_shapes=[pltpu.VMEM((tq, V7X_LANES), F32), pltpu.VMEM((tq, V7X_LANES), F32),
                        pltpu.VMEM((tq, MLA_V), F32)],
        compiler_params=_cparams(("parallel", "parallel", "arbitrary"), est),
        name="mla_attention_prompt",
    )(q_cat, k_cat, v, proj)


def _attn_sample_kernel(q_ref, k_ref, v_ref, gb_ref, o_ref):
    s = lax.dot_general(q_ref[...], k_ref[...], (((1,), (1,)), ((), ())), preferred_element_type=F32)
    p = jnp.exp2(s - jnp.max(s, axis=-1, keepdims=True))
    l_row = jnp.sum(p, axis=-1, keepdims=True)
    acc = jnp.dot(p.astype(BF16), v_ref[...], preferred_element_type=F32)
    o_ref[...] = _sigmoid(gb_ref[...]) * (acc / l_row)


def _attn_sample_call(q_cat, k_cat, v, proj, batch, sq, sk):
    assert (PAST_LEN + sq - 1) // CHUNK <= PAST_LEN // CHUNK
    gb_blk = (5 * D_MODEL) // MLA_V
    est = (2 * (sq * HEAD_QK * 2 + sk * HEAD_QK * 2 + sk * MLA_V * 2 + 2 * sq * MLA_V * 4) + 6 * sq * sk * 4
           + (4 << 20))
    return pl.pallas_call(
        _attn_sample_kernel,
        grid=(batch, MLA_HEADS),
        in_specs=[pl.BlockSpec((sq, HEAD_QK), lambda b, h: (b, h)),
                  pl.BlockSpec((sk, HEAD_QK), lambda b, h: (b, h)),
                  pl.BlockSpec((sk, MLA_V), lambda b, h: (b, h)),
                  pl.BlockSpec((sq, MLA_V), lambda b, h: (b, gb_blk + h))],
        out_specs=pl.BlockSpec((sq, MLA_V), lambda b, h: (b, h)),
        out_shape=jax.ShapeDtypeStruct((batch * sq, MLA_HEADS * MLA_V), F32),
        compiler_params=_cparams(("parallel", "parallel"), est),
        name="mla_attention_sample",
    )(q_cat, k_cat, v, proj)


def _merge_kernel(a_ref, b_ref, x_ref, g1_ref, w_ref, o_ref, m_scr):
    @pl.when(pl.program_id(1) == 0)
    def _():
        m_scr[...] = (a_ref[...] + b_ref[...]).astype(BF16)

    mix = jnp.dot(m_scr[...], w_ref[...], preferred_element_type=F32)
    o_ref[...] = _gate_res(x_ref[...], g1_ref[...], mix)


def _merge_call(a_part, b_part, x, mod, w_o_b, l, tm, groups):
    m = x.shape[0]
    tn = IN_TN
    est = (2 * 2 * tm * D_MODEL * 4 + tm * D_MODEL * 2 + 2 * D_MODEL * tn * 2 + 4 * tm * tn * 4
           + 2 * tm * tn * 4 + (4 << 20))
    return pl.pallas_call(
        _merge_kernel,
        grid=(m // tm, D_MODEL // tn),
        in_specs=[pl.BlockSpec((tm, D_MODEL), lambda i, j: (i, 0)),
                  pl.BlockSpec((tm, D_MODEL), lambda i, j: (i, 0)),
                  pl.BlockSpec((tm, tn), lambda i, j: (i, j)),
                  pl.BlockSpec((None, None, None, groups, tn), lambda i, j: (l, 2, i, 0, j)),
                  pl.BlockSpec((None, D_MODEL, tn), lambda i, j: (l, 0, j))],
        out_specs=pl.BlockSpec((tm, tn), lambda i, j: (i, j)),
        out_shape=jax.ShapeDtypeStruct((m, D_MODEL), F32),
        scratch_shapes=[pltpu.VMEM((tm, D_MODEL), BF16)],
        compiler_params=_cparams(("parallel", "arbitrary"), est),
        name="merge_out_proj",
    )(a_part, b_part, x, mod, w_o_b)


def _ffn_kernel(x_ref, sc_ref, sh_ref, g2_ref, gn_ref, wg_ref, wu_ref, wo_ref, o_ref, h_scr, acc_scr,
                *, nh):
    j = pl.program_id(1)

    @pl.when(j == 0)
    def _():
        h_scr[...] = _norm_mod(x_ref[...], gn_ref[...], sc_ref[...], sh_ref[...]).astype(BF16)
        acc_scr[...] = jnp.zeros(acc_scr.shape, F32)

    hb = h_scr[...]
    gate = jnp.dot(hb, wg_ref[...], preferred_element_type=F32)
    up = jnp.dot(hb, wu_ref[...], preferred_element_type=F32)
    act = ((gate * _sigmoid(gate)) * up).astype(BF16)
    acc_scr[...] += jnp.dot(act, wo_ref[...], preferred_element_type=F32)

    @pl.when(j == nh - 1)
    def _():
        o_ref[...] = _gate_res(x_ref[...], g2_ref[...], acc_scr[...])


def _ffn_call(x, mod, norm_g, w_in_b, w_out_b, l, tm, groups):
    m = x.shape[0]
    th = FFN_TH
    nh = FFN_HIDDEN // th
    est = (4 * tm * D_MODEL * 4 + tm * D_MODEL * 2 + tm * D_MODEL * 4 + 2 * 3 * D_MODEL * th * 2
           + 4 * tm * th * 4 + tm * D_MODEL * 4 + (4 << 20))
    mod_spec = lambda k: pl.BlockSpec((None, None, None, groups, D_MODEL), lambda i, j: (l, k, i, 0, 0))
    return pl.pallas_call(
        functools.partial(_ffn_kernel, nh=nh),
        grid=(m // tm, nh),
        in_specs=[pl.BlockSpec((tm, D_MODEL), lambda i, j: (i, 0)),
                  mod_spec(4), mod_spec(3), mod_spec(5),
                  pl.BlockSpec((None, 1, D_MODEL), lambda i, j: (l, 0, 0)),
                  pl.BlockSpec((None, D_MODEL, th), lambda i, j: (l, 0, j)),
                  pl.BlockSpec((None, D_MODEL, th), lambda i, j: (l, 0, nh + j)),
                  pl.BlockSpec((None, th, D_MODEL), lambda i, j: (l, j, 0))],
        out_specs=pl.BlockSpec((tm, D_MODEL), lambda i, j: (i, 0)),
        out_shape=jax.ShapeDtypeStruct((m, D_MODEL), F32),
        scratch_shapes=[pltpu.VMEM((tm, D_MODEL), BF16), pltpu.VMEM((tm, D_MODEL), F32)],
        compiler_params=_cparams(("parallel", "arbitrary"), est),
        name="ffn_swiglu",
    )(x, mod, mod, mod, norm_g, w_in_b, w_in_b, w_out_b)


def _final_norm_kernel(x_ref, g_ref, o_ref):
    o_ref[...] = _rms(x_ref[...], g_ref[...])


def _final_norm_call(x, g, tm):
    m = x.shape[0]
    est = 4 * tm * D_MODEL * 4 + 2 * tm * D_MODEL * 4 + (4 << 20)
    return pl.pallas_call(
        _final_norm_kernel,
        grid=(m // tm,),
        in_specs=[pl.BlockSpec((tm, D_MODEL), lambda i: (i, 0)),
                  pl.BlockSpec((1, D_MODEL), lambda i: (0, 0))],
        out_specs=pl.BlockSpec((tm, D_MODEL), lambda i: (i, 0)),
        out_shape=jax.ShapeDtypeStruct((m, D_MODEL), F32),
        compiler_params=_cparams(("parallel",), est),
        name="final_norm",
    )(x, g.reshape(1, D_MODEL))


def _pack_rope_lanes(x):
    z = jnp.zeros(x.shape[:-1] + (HALF_ROPE,), x.dtype)
    return jnp.concatenate([x[..., :HALF_ROPE], z, x[..., HALF_ROPE:], z], axis=-1)


def _unpack_rope_lanes(x):
    return jnp.concatenate([x[..., :HALF_ROPE], x[..., 2 * HALF_ROPE:3 * HALF_ROPE]], axis=-1)


def _rope_tables(pos, dim):
    inv = jnp.exp(-math.log(ROPE_BASE) * jnp.arange(0, dim, 2, dtype=F32) / dim)
    ang = pos.astype(F32)[:, None] * inv[None, :]
    return jnp.cos(ang), jnp.sin(ang)


def _mla_rope_tables(pos, reps):
    cos, sin = _rope_tables(pos, MLA_ROPE)
    z = jnp.zeros_like(cos)
    c = jnp.concatenate([cos, z, cos, z], axis=-1)
    s = jnp.concatenate([-sin, z, sin, z], axis=-1)
    return jnp.tile(c, (reps, 1)), jnp.tile(s, (reps, 1))


def _prep_weights(w_in, w_uq, w_ukv):
    e = RET_HEADS * RET_DK
    o = [0, e, 2 * e, 3 * e, 4 * e, 4 * e + MLA_Q_LORA, 4 * e + MLA_Q_LORA + MLA_KV_LORA]
    o.append(o[-1] + MLA_ROPE)
    o.append(o[-1] + D_MODEL)
    o.append(o[-1] + D_MODEL)
    sl = lambda a, b: w_in[..., o[a]:o[b]]
    w_main = jnp.concatenate([sl(0, 4), sl(7, 9), sl(4, 6)], axis=-1).astype(BF16)
    w_kr = _pack_rope_lanes(sl(6, 7)).astype(BF16)
    uq = w_uq.reshape(DEPTH, MLA_Q_LORA, MLA_HEADS, MLA_NOPE + MLA_ROPE)
    uq = jnp.concatenate([uq[..., :MLA_NOPE], _pack_rope_lanes(uq[..., MLA_NOPE:])], axis=-1)
    w_uq_r = uq.reshape(DEPTH, MLA_Q_LORA, MLA_HEADS * HEAD_QK).astype(BF16)
    ukv = w_ukv.reshape(DEPTH, MLA_KV_LORA, MLA_HEADS, MLA_NOPE + MLA_V)
    w_ukv_r = jnp.concatenate([ukv[..., :MLA_NOPE].reshape(DEPTH, MLA_KV_LORA, -1),
                               ukv[..., MLA_NOPE:].reshape(DEPTH, MLA_KV_LORA, -1)], axis=-1).astype(BF16)
    return w_main, w_kr, w_uq_r, w_ukv_r


def _layer(l, x, mod, groups, tm, batch, seq, ret_len, ret_tabs, mla_tabs, state0, kv_source, att_cfg,
           log_g, W):
    proj, kr_raw = _inproj_call(x, mod, W["norm_mix"], W["w_main"], W["w_kr"], l, tm, groups)
    a_part, new_state = _ret_call(proj, log_g, ret_tabs[0], ret_tabs[1], state0, batch, seq, ret_len)
    q_cat, ckv, kr128 = _mla_q_call(proj, kr_raw, W["q_norm"], W["kv_norm"], mla_tabs[0], mla_tabs[1],
                                    W["w_uq_r"], l, tm)
    ckv_all, kr_all, sk = kv_source(ckv, kr128)
    k_cat, v = _mla_kv_call(ckv_all, kr_all, W["w_ukv_r"], l, ckv_all.shape[0], tm)
    if att_cfg == "prompt":
        b_part = _attn_prompt_call(q_cat, k_cat, v, proj, batch, seq, ATT_TQ)
    else:
        b_part = _attn_sample_call(q_cat, k_cat, v, proj, batch, seq, sk)
    x = _merge_call(a_part, b_part, x, mod, W["w_o"], l, tm, groups)
    x = _ffn_call(x, mod, W["norm_ffn"], W["w_ffn_in"], W["w_ffn_out"], l, tm, groups)
    return x, ckv, kr128, new_state


def kernel(x_prompt, x_sample, c_prompt, c_sample, cache_mla_ckv, cache_mla_krope, state_ret, w_ada, b_ada,
           norm_mix, norm_ffn, w_in, mla_q_norm, w_uq, mla_kv_norm, w_ukv, w_o, w_ffn_in, w_ffn_out,
           norm_final):
    bp, sp, _ = x_prompt.shape
    bs, ss, _ = x_sample.shape
    tm = TOKEN_TILE
    assert sp % tm == 0 and (bs * ss) % tm == 0 and tm % ss == 0

    w_main, w_kr, w_uq_r, w_ukv_r = _prep_weights(w_in, w_uq, w_ukv)
    W = dict(w_main=w_main, w_kr=w_kr, w_uq_r=w_uq_r, w_ukv_r=w_ukv_r,
             w_o=w_o.astype(BF16), w_ffn_in=w_ffn_in.astype(BF16), w_ffn_out=w_ffn_out.astype(BF16),
             norm_mix=norm_mix.reshape(DEPTH, 1, D_MODEL), norm_ffn=norm_ffn.reshape(DEPTH, 1, D_MODEL),
             q_norm=mla_q_norm.reshape(DEPTH, 1, MLA_Q_LORA), kv_norm=mla_kv_norm.reshape(DEPTH, 1, MLA_KV_LORA))

    c_rows = -(-(bp + bs) // ADA_ROW_ALIGN) * ADA_ROW_ALIGN
    c_all = jnp.concatenate([c_prompt, c_sample, jnp.zeros((c_rows - bp - bs, D_MODEL), F32)], axis=0)
    mod_all = _ada_call(c_all, w_ada.astype(BF16), b_ada)[:, :bp + bs]
    mod_all = mod_all.reshape(DEPTH, bp + bs, 6, D_MODEL).transpose(0, 2, 1, 3)
    tiles_per_batch = sp // tm
    mod_p = jnp.repeat(mod_all[:, :, :bp], tiles_per_batch, axis=2)[:, :, :, None, :]
    groups_s = tm // ss
    mod_s = mod_all[:, :, bp:].reshape(DEPTH, 6, (bs * ss) // tm, groups_s, D_MODEL)

    log_g = jnp.log1p(-jnp.exp2(-RET_GAMMA_EXP0 - jnp.arange(RET_HEADS, dtype=F32)))
    pos_p = jnp.arange(sp)
    pos_s = PAST_LEN + jnp.arange(ss)
    ret_tabs_p = _rope_tables(pos_p, RET_DK)
    ret_tabs_s = _rope_tables(pos_s, RET_DK)
    mla_tabs_p = _mla_rope_tables(pos_p, bp)
    mla_tabs_s = _mla_rope_tables(pos_s, bs)
    zero_state = jnp.zeros((bp, RET_HEADS, RET_DK, RET_DV), F32)

    xp = x_prompt.reshape(bp * sp, D_MODEL)
    xs = x_sample.reshape(bs * ss, D_MODEL)
    outs = [[] for _ in range(6)]
    for l in range(DEPTH):
        xp, ckv, kr128, st = _layer(
            l, xp, mod_p, 1, tm, bp, sp, RET_L_PROMPT, ret_tabs_p, mla_tabs_p, zero_state,
            lambda ckv, kr: (ckv, kr, sp), "prompt", log_g, W)
        outs[0].append(ckv.reshape(bp, sp, MLA_KV_LORA))
        outs[1].append(_unpack_rope_lanes(kr128).reshape(bp, sp, MLA_ROPE))
        outs[2].append(st)

        def sample_kv(ckv, kr, l=l):
            ckv_all = jnp.concatenate([cache_mla_ckv[l], ckv.reshape(bs, ss, MLA_KV_LORA)], axis=1)
            kr_all = jnp.concatenate([_pack_rope_lanes(cache_mla_krope[l]),
                                      kr.reshape(bs, ss, V7X_LANES)], axis=1)
            sk = PAST_LEN + ss
            return ckv_all.reshape(bs * sk, MLA_KV_LORA), kr_all.reshape(bs * sk, V7X_LANES), sk

        xs, ckv, kr128, st = _layer(
            l, xs, mod_s, groups_s, tm, bs, ss, ss, ret_tabs_s, mla_tabs_s, state_ret[l],
            sample_kv, "sample", log_g, W)
        outs[3].append(ckv.reshape(bs, ss, MLA_KV_LORA))
        outs[4].append(_unpack_rope_lanes(kr128).reshape(bs, ss, MLA_ROPE))
        outs[5].append(st)

    y_prompt = _final_norm_call(xp, norm_final, tm).reshape(bp, sp, D_MODEL)
    y_sample = _final_norm_call(xs, norm_final, tm).reshape(bs, ss, D_MODEL)
    return (y_prompt, y_sample, jnp.stack(outs[0]), jnp.stack(outs[1]), jnp.stack(outs[2]),
            jnp.stack(outs[3]), jnp.stack(outs[4]), jnp.stack(outs[5]))
```

```python
import functools
import math

import jax
import jax.numpy as jnp
from jax import lax
from jax.experimental import pallas as pl
from jax.experimental.pallas import tpu as pltpu

D_MODEL = 2048
DEPTH = 4
PAST_LEN = 1024
CHUNK = 64
RET_HEADS = 8
RET_DK = D_MODEL // RET_HEADS
RET_DV = D_MODEL // RET_HEADS
MLA_HEADS = 16
MLA_Q_LORA = D_MODEL // 4
MLA_KV_LORA = D_MODEL // 4
MLA_NOPE = 128
MLA_ROPE = 64
MLA_V = D_MODEL // MLA_HEADS
FFN_HIDDEN = -(-8 * D_MODEL // (3 * 256)) * 256
ROPE_BASE = 10000.0
RET_GAMMA_EXP0 = 5.0
RMS_EPS = 1e-6
GN_EPS = 1e-5
NEG_INF = -1e30

F32 = jnp.float32
BF16 = jnp.bfloat16

V7X_LANES = 128
V7X_VMEM_LIMIT_CAP = 56 * 1024 * 1024

COL_RQ, COL_RK, COL_RV, COL_RG = 0, D_MODEL, 2 * D_MODEL, 3 * D_MODEL
COL_DQ = 4 * D_MODEL
COL_DKV = COL_DQ + MLA_Q_LORA
COL_GA = COL_DKV + MLA_KV_LORA
COL_GB = COL_GA + D_MODEL
MAIN_COLS = COL_GB + D_MODEL
HEAD_QK = 2 * V7X_LANES
HALF_ROPE = MLA_ROPE // 2
ADA_ROW_ALIGN = 16
TOKEN_TILE = 512
IN_TM_PROMPT = 1024
IN_TN = 512
MERGE_TN = 1024
ADA_TN = 1024
FFN_TH = 512
RET_L_PROMPT = 256
ATT_TQ = 1024
ATT_TK = 512
ATT_SPLIT = 4
QK_SCALE_LOG2 = (MLA_NOPE + MLA_ROPE) ** -0.5 * math.log2(math.e)


def _cparams(sem, est_bytes):
    return pltpu.CompilerParams(dimension_semantics=sem,
                                vmem_limit_bytes=min(int(est_bytes), V7X_VMEM_LIMIT_CAP))


def _sigmoid(x):
    return jax.nn.sigmoid(x)


def _rms(x, g):
    return x * lax.rsqrt(jnp.mean(x * x, axis=-1, keepdims=True) + RMS_EPS) * g


def _norm_mod(x, g, sc, sh):
    tm, d = x.shape
    groups = sc.shape[0]
    y = _rms(x, g)
    if groups == 1:
        return y * (1.0 + sc) + sh
    y3 = y.reshape(groups, tm // groups, d)
    return (y3 * (1.0 + sc[:, None, :]) + sh[:, None, :]).reshape(tm, d)


def _gate_res(x, gate, upd):
    tm, n = x.shape
    groups = gate.shape[0]
    if groups == 1:
        return x + gate * upd
    return x + (gate[:, None, :] * upd.reshape(groups, tm // groups, n)).reshape(tm, n)


def _pack_rope_lanes(x):
    z = jnp.zeros(x.shape[:-1] + (HALF_ROPE,), x.dtype)
    return jnp.concatenate([x[..., :HALF_ROPE], z, x[..., HALF_ROPE:], z], axis=-1)


def _unpack_rope_lanes(x):
    return jnp.concatenate([x[..., :HALF_ROPE], x[..., 2 * HALF_ROPE:3 * HALF_ROPE]], axis=-1)


def _ada_kernel(c_ref, w_ref, b_ref, o_ref):
    c = c_ref[...]
    a = (c * _sigmoid(c)).astype(BF16)
    o_ref[0] = jnp.dot(a, w_ref[0].astype(BF16), preferred_element_type=F32) + b_ref[0]


def _ada_call(c_all, w_ada, b_ada):
    nb = c_all.shape[0]
    n = w_ada.shape[-1]
    tn = ADA_TN
    est = 2 * (D_MODEL * tn * 4) + D_MODEL * tn * 2 + 4 * nb * (D_MODEL + 2 * tn) * 4 + (4 << 20)
    return pl.pallas_call(
        _ada_kernel,
        grid=(DEPTH, n // tn),
        in_specs=[pl.BlockSpec((nb, D_MODEL), lambda l, j: (0, 0)),
                  pl.BlockSpec((1, D_MODEL, tn), lambda l, j: (l, 0, j)),
                  pl.BlockSpec((1, 1, tn), lambda l, j: (l, 0, j))],
        out_specs=pl.BlockSpec((1, nb, tn), lambda l, j: (l, 0, j)),
        out_shape=jax.ShapeDtypeStruct((DEPTH, nb, n), F32),
        compiler_params=_cparams(("parallel", "parallel"), est),
        name="ada_mod",
    )(c_all, w_ada, b_ada.reshape(DEPTH, 1, n))


def _inproj_kernel(x_ref, sc_ref, sh_ref, g_ref, w_ref, wg_ref, wkr_ref, o_ref, kr_ref, h_scr, *, n_main):
    j = pl.program_id(1)

    @pl.when(j == 0)
    def _():
        hb = _norm_mod(x_ref[...], g_ref[...], sc_ref[...], sh_ref[...]).astype(BF16)
        h_scr[...] = hb
        kr_ref[...] = jnp.dot(hb, wkr_ref[...], preferred_element_type=F32)

    @pl.when(j < n_main)
    def _():
        o_ref[...] = jnp.dot(h_scr[...], w_ref[...].astype(BF16), preferred_element_type=F32)

    @pl.when(j >= n_main)
    def _():
        o_ref[...] = jnp.dot(h_scr[...], wg_ref[...], preferred_element_type=F32)


def _inproj_call(x, mod, norm_g, w_in, w_gates, w_kr, l, tm, groups, mod_stride):
    m = x.shape[0]
    n_main = COL_GA // IN_TN
    est = (2 * tm * D_MODEL * 4 + tm * D_MODEL * 2 + 2 * D_MODEL * IN_TN * 4 + 2 * D_MODEL * IN_TN * 2
           + D_MODEL * IN_TN * 2 + 4 * tm * IN_TN * 4 + tm * D_MODEL * 4 + (4 << 20))
    mod_spec = lambda k: pl.BlockSpec((None, None, None, groups, D_MODEL),
                                      lambda i, j: (l, k, i * mod_stride, 0, 0))
    return pl.pallas_call(
        functools.partial(_inproj_kernel, n_main=n_main),
        grid=(m // tm, MAIN_COLS // IN_TN),
        in_specs=[pl.BlockSpec((tm, D_MODEL), lambda i, j: (i, 0)),
                  mod_spec(1), mod_spec(0),
                  pl.BlockSpec((None, 1, D_MODEL), lambda i, j: (l, 0, 0)),
                  pl.BlockSpec((None, D_MODEL, IN_TN), lambda i, j: (l, 0, jnp.minimum(j, n_main - 1))),
                  pl.BlockSpec((None, D_MODEL, IN_TN), lambda i, j: (l, 0, jnp.maximum(j - n_main, 0))),
                  pl.BlockSpec((None, D_MODEL, V7X_LANES), lambda i, j: (l, 0, 0))],
        out_specs=[pl.BlockSpec((tm, IN_TN), lambda i, j: (i, j)),
                   pl.BlockSpec((tm, V7X_LANES), lambda i, j: (i, 0))],
        out_shape=[jax.ShapeDtypeStruct((m, MAIN_COLS), F32),
                   jax.ShapeDtypeStruct((m, V7X_LANES), F32)],
        scratch_shapes=[pltpu.VMEM((tm, D_MODEL), BF16)],
        compiler_params=_cparams(("parallel", "arbitrary"), est),
        name="in_proj",
    )(x, mod, mod, norm_g, w_in, w_gates, w_kr)


def _ret_kernel(lg_ref, q_ref, k_ref, v_ref, rg_ref, ga_ref, cos_ref, sin_ref, s0_ref,
                a_ref, st_ref, dm_scr, *, chunk_len):
    h = pl.program_id(1)
    c = pl.program_id(2)
    lg = lg_ref[h]
    L = chunk_len

    @pl.when(c == 0)
    def _():
        st_ref[...] = s0_ref[...]
        ri = lax.broadcasted_iota(jnp.int32, (L, L), 0)
        ci = lax.broadcasted_iota(jnp.int32, (L, L), 1)
        diff = (ri - ci).astype(F32)
        dm_scr[...] = jnp.where(diff >= 0, jnp.exp(jnp.maximum(diff, 0.0) * lg), 0.0)

    cos = cos_ref[...]
    sin = sin_ref[...]
    half = RET_DK // 2

    def rope(x):
        x1, x2 = x[:, :half], x[:, half:]
        return jnp.concatenate([x1 * cos - x2 * sin, x1 * sin + x2 * cos], axis=-1)

    q = rope(q_ref[...])
    k = rope(k_ref[...]) * (RET_DK ** -0.5)
    vb = v_ref[...].astype(BF16)
    idx = lax.broadcasted_iota(jnp.int32, (L, 1), 0).astype(F32)
    xi = jnp.exp((idx + 1.0) * lg)
    zeta = jnp.exp((L - 1.0 - idx) * lg)
    g_l = jnp.exp(jnp.full((1, 1), float(L), F32) * lg)

    qb = q.astype(BF16)
    kb = k.astype(BF16)
    scores = lax.dot_general(qb, kb, (((1,), (1,)), ((), ())), preferred_element_type=F32) * dm_scr[...]
    st = st_ref[0, 0]
    o = (jnp.dot(scores.astype(BF16), vb, preferred_element_type=F32)
         + jnp.dot(qb, st.astype(BF16), preferred_element_type=F32) * xi)
    kz = (k * zeta).astype(BF16)
    st_ref[0, 0] = st * g_l + lax.dot_general(kz, vb, (((0,), (0,)), ((), ())),
                                              preferred_element_type=F32)

    mu = jnp.mean(o, axis=-1, keepdims=True)
    d = o - mu
    var = jnp.mean(d * d, axis=-1, keepdims=True)
    on = d * lax.rsqrt(var + GN_EPS)
    rg = rg_ref[...]
    a_ref[...] = _sigmoid(ga_ref[...]) * ((rg * _sigmoid(rg)) * on)


def _ret_call(proj, log_g, cos, sin, state0, state_layer, batch, seq, chunk_len):
    L = chunk_len
    nc = seq // L
    col = lambda base: (lambda b, h, c: (b * nc + c, base // RET_DK + h))
    blk = lambda base: pl.BlockSpec((L, RET_DK), col(base))
    nh = RET_HEADS
    est = 2 * 6 * L * RET_DK * 4 + 4 * RET_DK * RET_DV * 4 + L * L * 4 + 8 * L * max(L, RET_DK) * 4 + (4 << 20)
    return pl.pallas_call(
        functools.partial(_ret_kernel, chunk_len=L),
        grid=(batch, nh, nc),
        in_specs=[pl.BlockSpec(memory_space=pltpu.SMEM),
                  blk(COL_RQ), blk(COL_RK), blk(COL_RV), blk(COL_RG), blk(COL_GA),
                  pl.BlockSpec((L, RET_DK // 2), lambda b, h, c: (c, 0)),
                  pl.BlockSpec((L, RET_DK // 2), lambda b, h, c: (c, 0)),
                  pl.BlockSpec((None, 1, 1, RET_DK, RET_DV), lambda b, h, c: (state_layer, b, h, 0, 0))],
        out_specs=[pl.BlockSpec((L, RET_DV), lambda b, h, c: (b * nc + c, h)),
                   pl.BlockSpec((1, 1, RET_DK, RET_DV), lambda b, h, c: (b, h, 0, 0))],
        out_shape=[jax.ShapeDtypeStruct((batch * seq, D_MODEL), F32),
                   jax.ShapeDtypeStruct((batch, nh, RET_DK, RET_DV), F32)],
        scratch_shapes=[pltpu.VMEM((L, L), F32)],
        compiler_params=_cparams(("parallel", "parallel", "arbitrary"), est),
        name="retention",
    )(log_g, proj, proj, proj, proj, proj, cos, sin, state0)


def _rope128(x, c, s):
    return x * c + pltpu.roll(x, V7X_LANES // 2, 1) * s


def _mla_q_kernel(dq_ref, dkv_ref, kr_ref, qn_ref, kvn_ref, c_ref, s_ref, wuq_ref,
                  q_out, ckv_out, kro_out):
    c = c_ref[...]
    s = s_ref[...]
    cq = _rms(dq_ref[...], qn_ref[...]).astype(BF16)
    for h in range(MLA_HEADS):
        lo = h * HEAD_QK
        qh = jnp.dot(cq, wuq_ref[:, lo:lo + HEAD_QK], preferred_element_type=F32) * QK_SCALE_LOG2
        q_out[:, lo:lo + MLA_NOPE] = qh[:, :MLA_NOPE].astype(BF16)
        q_out[:, lo + MLA_NOPE:lo + HEAD_QK] = _rope128(qh[:, MLA_NOPE:], c, s).astype(BF16)
    ckv_out[...] = _rms(dkv_ref[...], kvn_ref[...])
    kro_out[...] = _rope128(kr_ref[...], c, s)


def _mla_q_call(proj, kr_raw, q_norm, kv_norm, rope_c, rope_s, w_uq_r, l, tm):
    m = proj.shape[0]
    dq_blk = COL_DQ // MLA_Q_LORA
    qw = MLA_HEADS * HEAD_QK
    est = (2 * (2 * tm * MLA_Q_LORA * 4 + 3 * tm * V7X_LANES * 4) + 2 * MLA_Q_LORA * qw * 2
           + 2 * tm * qw * 2 + 2 * tm * MLA_KV_LORA * 4 + 2 * tm * V7X_LANES * 4 + 8 * tm * HEAD_QK * 4
           + (4 << 20))
    return pl.pallas_call(
        _mla_q_kernel,
        grid=(m // tm,),
        in_specs=[pl.BlockSpec((tm, MLA_Q_LORA), lambda i: (i, dq_blk)),
                  pl.BlockSpec((tm, MLA_KV_LORA), lambda i: (i, dq_blk + 1)),
                  pl.BlockSpec((tm, V7X_LANES), lambda i: (i, 0)),
                  pl.BlockSpec((None, 1, MLA_Q_LORA), lambda i: (l, 0, 0)),
                  pl.BlockSpec((None, 1, MLA_KV_LORA), lambda i: (l, 0, 0)),
                  pl.BlockSpec((tm, V7X_LANES), lambda i: (i, 0)),
                  pl.BlockSpec((tm, V7X_LANES), lambda i: (i, 0)),
                  pl.BlockSpec((None, MLA_Q_LORA, qw), lambda i: (l, 0, 0))],
        out_specs=[pl.BlockSpec((tm, qw), lambda i: (i, 0)),
                   pl.BlockSpec((tm, MLA_KV_LORA), lambda i: (i, 0)),
                   pl.BlockSpec((tm, V7X_LANES), lambda i: (i, 0))],
        out_shape=[jax.ShapeDtypeStruct((m, qw), BF16),
                   jax.ShapeDtypeStruct((m, MLA_KV_LORA), F32),
                   jax.ShapeDtypeStruct((m, V7X_LANES), F32)],
        compiler_params=_cparams(("parallel",), est),
        name="mla_latents",
    )(proj, proj, kr_raw, q_norm, kv_norm, rope_c, rope_s, w_uq_r)


def _mla_q_sample_kernel(dq_ref, dkv_ref, kr_ref, qn_ref, kvn_ref, c_ref, s_ref, wuq_ref, wukv_ref,
                         qlat_out, qr_out, ckv_out, kro_out):
    c = c_ref[...]
    s = s_ref[...]
    cq = _rms(dq_ref[...], qn_ref[...]).astype(BF16)
    for h in range(MLA_HEADS):
        lo = h * HEAD_QK
        qh = jnp.dot(cq, wuq_ref[:, lo:lo + HEAD_QK], preferred_element_type=F32) * QK_SCALE_LOG2
        w_uk = wukv_ref[:, h * MLA_NOPE:(h + 1) * MLA_NOPE]
        q_lat = lax.dot_general(qh[:, :MLA_NOPE].astype(BF16), w_uk, (((1,), (1,)), ((), ())),
                                preferred_element_type=F32)
        qlat_out[:, h * MLA_KV_LORA:(h + 1) * MLA_KV_LORA] = q_lat.astype(BF16)
        qr_out[:, h * MLA_ROPE:(h + 1) * MLA_ROPE] = _unpack_rope_lanes(
            _rope128(qh[:, MLA_NOPE:], c, s)).astype(BF16)
    ckv_out[...] = _rms(dkv_ref[...], kvn_ref[...])
    kro_out[...] = _unpack_rope_lanes(_rope128(kr_ref[...], c, s))


def _mla_q_sample_call(proj, kr_raw, q_norm, kv_norm, rope_c, rope_s, w_uq_r, w_ukv_r, l):
    m = proj.shape[0]
    dq_blk = COL_DQ // MLA_Q_LORA
    qw = MLA_HEADS * HEAD_QK
    ww = MLA_HEADS * (MLA_NOPE + MLA_V)
    lat_w = MLA_HEADS * MLA_KV_LORA
    rope_w = MLA_HEADS * MLA_ROPE
    est = (2 * (2 * m * MLA_Q_LORA * 4 + 3 * m * V7X_LANES * 4) + 2 * MLA_Q_LORA * (qw + ww) * 2
           + 2 * m * (lat_w + rope_w) * 2 + 2 * m * (MLA_KV_LORA + MLA_ROPE) * 4 + 8 * m * MLA_KV_LORA * 4
           + (4 << 20))
    full = lambda shape: pl.BlockSpec(shape, lambda i: (0,) * len(shape))
    return pl.pallas_call(
        _mla_q_sample_kernel,
        grid=(1,),
        in_specs=[pl.BlockSpec((m, MLA_Q_LORA), lambda i: (0, dq_blk)),
                  pl.BlockSpec((m, MLA_KV_LORA), lambda i: (0, dq_blk + 1)),
                  full((m, V7X_LANES)),
                  pl.BlockSpec((None, 1, MLA_Q_LORA), lambda i: (l, 0, 0)),
                  pl.BlockSpec((None, 1, MLA_KV_LORA), lambda i: (l, 0, 0)),
                  full((m, V7X_LANES)), full((m, V7X_LANES)),
                  pl.BlockSpec((None, MLA_Q_LORA, qw), lambda i: (l, 0, 0)),
                  pl.BlockSpec((None, MLA_KV_LORA, ww), lambda i: (l, 0, 0))],
        out_specs=[full((m, lat_w)), full((m, rope_w)), full((m, MLA_KV_LORA)), full((m, MLA_ROPE))],
        out_shape=[jax.ShapeDtypeStruct((m, lat_w), BF16),
                   jax.ShapeDtypeStruct((m, rope_w), BF16),
                   jax.ShapeDtypeStruct((m, MLA_KV_LORA), F32),
                   jax.ShapeDtypeStruct((m, MLA_ROPE), F32)],
        compiler_params=_cparams(("arbitrary",), est),
        name="mla_latents_sample",
    )(proj, proj, kr_raw, q_norm, kv_norm, rope_c, rope_s, w_uq_r, w_ukv_r)


def _mla_kv_kernel(ckv_ref, kr_ref, wukv_ref, kcat_out, v_out):
    cb = ckv_ref[...].astype(BF16)
    krb = kr_ref[...].astype(BF16)
    for hp in range(MLA_HEADS // 2):
        kk = jnp.dot(cb, wukv_ref[:, hp * HEAD_QK:(hp + 1) * HEAD_QK], preferred_element_type=F32)
        for t in range(2):
            lo = (2 * hp + t) * HEAD_QK
            kcat_out[:, lo:lo + MLA_NOPE] = kk[:, t * MLA_NOPE:(t + 1) * MLA_NOPE].astype(BF16)
            kcat_out[:, lo + MLA_NOPE:lo + HEAD_QK] = krb
    nv = MLA_HEADS * MLA_NOPE
    for hp in range(nv // HEAD_QK):
        lo = hp * HEAD_QK
        v_out[:, lo:lo + HEAD_QK] = jnp.dot(cb, wukv_ref[:, nv + lo:nv + lo + HEAD_QK],
                                            preferred_element_type=F32).astype(BF16)


def _mla_kv_call(ckv, kr128, w_ukv_r, l, rows, tm):
    kw = MLA_HEADS * HEAD_QK
    vw = MLA_HEADS * MLA_V
    ww = MLA_HEADS * (MLA_NOPE + MLA_V)
    est = (2 * tm * (MLA_KV_LORA + V7X_LANES) * 4 + 2 * MLA_KV_LORA * ww * 2 + 2 * tm * (kw + vw) * 2
           + 8 * tm * HEAD_QK * 4 + (4 << 20))
    return pl.pallas_call(
        _mla_kv_kernel,
        grid=(rows // tm,),
        in_specs=[pl.BlockSpec((tm, MLA_KV_LORA), lambda i: (i, 0)),
                  pl.BlockSpec((tm, V7X_LANES), lambda i: (i, 0)),
                  pl.BlockSpec((None, MLA_KV_LORA, ww), lambda i: (l, 0, 0))],
        out_specs=[pl.BlockSpec((tm, kw), lambda i: (i, 0)),
                   pl.BlockSpec((tm, vw), lambda i: (i, 0))],
        out_shape=[jax.ShapeDtypeStruct((rows, kw), BF16),
                   jax.ShapeDtypeStruct((rows, vw), BF16)],
        compiler_params=_cparams(("parallel",), est),
        name="mla_kv_expand",
    )(ckv, kr128, w_ukv_r)


def _scores(q, k_blk):
    return lax.dot_general(q, k_blk, (((1,), (1,)), ((), ())), preferred_element_type=F32)


def _softmax_tile(s, v_blk, m_prev, l_prev, acc_prev, mask):
    if mask is not None:
        s = jnp.where(mask, s, NEG_INF)
    n_groups = s.shape[1] // V7X_LANES
    groups = [s[:, g * V7X_LANES:(g + 1) * V7X_LANES] for g in range(n_groups)]
    m_new = jnp.maximum(m_prev, jnp.max(functools.reduce(jnp.maximum, groups), axis=-1, keepdims=True))
    alpha = jnp.exp2(m_prev - m_new)
    ps = [jnp.exp2(g - m_new) for g in groups]
    l_new = alpha * l_prev + functools.reduce(jnp.add, ps)
    p = jnp.concatenate([x.astype(BF16) for x in ps], axis=-1)
    acc_new = alpha * acc_prev + jnp.dot(p, v_blk, preferred_element_type=F32)
    return m_new, l_new, acc_new


def _attn_prompt_kernel(q_ref, k_ref, v_ref, gb_ref, o_ref, m_scr, l_scr, acc_scr, *, tq, tk, n_split):
    i = pl.program_id(2)
    rows = tq // n_split
    m_scr[...] = jnp.full(m_scr.shape, NEG_INF, F32)
    l_scr[...] = jnp.zeros(l_scr.shape, F32)
    acc_scr[...] = jnp.zeros(acc_scr.shape, F32)

    def scores(r, k_blk):
        return _scores(q_ref[pl.ds(r * rows, rows), :], k_blk)

    def update(r, s, v_blk, mask):
        rs = pl.ds(r * rows, rows)
        m_new, l_new, acc_new = _softmax_tile(s, v_blk, m_scr[rs, :], l_scr[rs, :], acc_scr[rs, :], mask)
        m_scr[rs, :] = m_new
        l_scr[rs, :] = l_new
        acc_scr[rs, :] = acc_new

    def body(j, carry):
        start = pl.multiple_of(j * tk, tk)
        k_blk = k_ref[pl.ds(start, tk), :]
        v_blk = v_ref[pl.ds(start, tk), :]
        ss = [scores(r, k_blk) for r in range(n_split)]
        for r in range(n_split):
            update(r, ss[r], v_blk, None)
        return carry

    blocks_per_tile = tq // tk
    lax.fori_loop(0, i * blocks_per_tile, body, 0)

    work = []
    for d in range(blocks_per_tile):
        k0 = d * tk
        for r in range(n_split):
            r0, r1 = r * rows, (r + 1) * rows
            width = min(k0 + tk, r1) - k0
            if width <= 0:
                continue
            start = pl.multiple_of(i * tq + k0, tk)
            mask = None
            if k0 + width > r0:
                rc = (lax.broadcasted_iota(jnp.int32, (rows, width), 0) + r0) // CHUNK
                kc = (lax.broadcasted_iota(jnp.int32, (rows, width), 1) + k0) // CHUNK
                mask = kc <= rc
            work.append((r, scores(r, k_ref[pl.ds(start, width), :]), v_ref[pl.ds(start, width), :], mask))
    for r, s, v_blk, mask in work:
        update(r, s, v_blk, mask)

    l_row = jnp.sum(l_scr[...], axis=-1, keepdims=True)
    o_ref[...] = _sigmoid(gb_ref[...]) * (acc_scr[...] / l_row)


def _attn_prompt_call(q_cat, k_cat, v, proj, batch, seq, tq):
    nq = seq // tq
    gb_blk = COL_GB // MLA_V
    est = (2 * (tq * HEAD_QK * 2 + seq * HEAD_QK * 2 + seq * MLA_V * 2 + 2 * tq * MLA_V * 4)
           + 3 * tq * V7X_LANES * 4 + 6 * tq * ATT_TK * 4 + (4 << 20))
    return pl.pallas_call(
        functools.partial(_attn_prompt_kernel, tq=tq, tk=ATT_TK, n_split=ATT_SPLIT),
        grid=(batch, MLA_HEADS, nq),
        in_specs=[pl.BlockSpec((tq, HEAD_QK), lambda b, h, i: (b * nq + i, h)),
                  pl.BlockSpec((seq, HEAD_QK), lambda b, h, i: (b, h)),
                  pl.BlockSpec((seq, MLA_V), lambda b, h, i: (b, h)),
                  pl.BlockSpec((tq, MLA_V), lambda b, h, i: (b * nq + i, gb_blk + h))],
        out_specs=pl.BlockSpec((tq, MLA_V), lambda b, h, i: (b * nq + i, h)),
        out_shape=jax.ShapeDtypeStruct((batch * seq, MLA_HEADS * MLA_V), F32),
        scratch_shapes=[pltpu.VMEM((tq, V7X_LANES), F32), pltpu.VMEM((tq, V7X_LANES), F32),
                        pltpu.VMEM((tq, MLA_V), F32)],
        compiler_params=_cparams(("parallel", "parallel", "arbitrary"), est),
        name="mla_attention_prompt",
    )(q_cat, k_cat, v, proj)


def _attn_sample_kernel(qlat_ref, qr_ref, cc_ref, ckr_ref, nc_ref, nkr_ref, gb0_ref, gb1_ref, wukv_ref,
                        o_ref, *, sq):
    nt = (((1,), (1,)), ((), ()))
    q_lat = jnp.concatenate([qlat_ref[:, h * MLA_KV_LORA:(h + 1) * MLA_KV_LORA] for h in range(MLA_HEADS)],
                            axis=0)
    q_r = jnp.concatenate([qr_ref[:, h * MLA_ROPE:(h + 1) * MLA_ROPE] for h in range(MLA_HEADS)], axis=0)
    kc = cc_ref[...].astype(BF16)
    kn = nc_ref[...].astype(BF16)
    s_c = (lax.dot_general(q_lat, kc, nt, preferred_element_type=F32)
           + lax.dot_general(q_r, ckr_ref[...].astype(BF16), nt, preferred_element_type=F32))
    s_n = (lax.dot_general(q_lat, kn, nt, preferred_element_type=F32)
           + lax.dot_general(q_r, nkr_ref[...].astype(BF16), nt, preferred_element_type=F32))
    m = jnp.maximum(jnp.max(s_c, axis=-1, keepdims=True), jnp.max(s_n, axis=-1, keepdims=True))
    p_c = jnp.exp2(s_c - m)
    p_n = jnp.exp2(s_n - m)
    l_row = jnp.sum(p_c, axis=-1, keepdims=True) + jnp.sum(p_n, axis=-1, keepdims=True)
    o_lat = (jnp.dot(p_c.astype(BF16), kc, preferred_element_type=F32)
             + jnp.dot(p_n.astype(BF16), kn, preferred_element_type=F32)) / l_row
    o_lat = o_lat.astype(BF16)
    nv = MLA_HEADS * MLA_NOPE
    half = MLA_HEADS // 2
    for h in range(MLA_HEADS):
        w_uv = wukv_ref[:, nv + h * MLA_V:nv + (h + 1) * MLA_V]
        o_h = jnp.dot(o_lat[h * sq:(h + 1) * sq, :], w_uv, preferred_element_type=F32)
        gb_ref = gb0_ref if h < half else gb1_ref
        gb = gb_ref[:, (h % half) * MLA_V:(h % half + 1) * MLA_V]
        o_ref[:, h * MLA_V:(h + 1) * MLA_V] = _sigmoid(gb) * o_h


def _attn_sample_call(q_lat, q_rope, ckv_new, kr_new, cache_ckv, cache_kr, proj, w_ukv_r, l, batch, sq):
    assert (PAST_LEN + sq - 1) // CHUNK <= PAST_LEN // CHUNK
    past = cache_ckv.shape[2]
    lat_w = MLA_HEADS * MLA_KV_LORA
    rope_w = MLA_HEADS * MLA_ROPE
    ww = MLA_HEADS * (MLA_NOPE + MLA_V)
    gw = D_MODEL // 2
    gb_blk = COL_GB // gw
    rows = MLA_HEADS * sq
    est = (2 * (sq * (lat_w + rope_w) * 2 + past * (MLA_KV_LORA + V7X_LANES) * 4 + MLA_KV_LORA * ww * 2
                + 4 * sq * D_MODEL * 4) + past * (MLA_KV_LORA + V7X_LANES) * 2 + 6 * rows * past * 4
           + 4 * rows * MLA_KV_LORA * 4 + (4 << 20))
    return pl.pallas_call(
        functools.partial(_attn_sample_kernel, sq=sq),
        grid=(batch,),
        in_specs=[pl.BlockSpec((sq, lat_w), lambda b: (b, 0)),
                  pl.BlockSpec((sq, rope_w), lambda b: (b, 0)),
                  pl.BlockSpec((None, None, past, MLA_KV_LORA), lambda b: (l, b, 0, 0)),
                  pl.BlockSpec((None, None, past, MLA_ROPE), lambda b: (l, b, 0, 0)),
                  pl.BlockSpec((sq, MLA_KV_LORA), lambda b: (b, 0)),
                  pl.BlockSpec((sq, MLA_ROPE), lambda b: (b, 0)),
                  pl.BlockSpec((sq, gw), lambda b: (b, gb_blk)),
                  pl.BlockSpec((sq, gw), lambda b: (b, gb_blk + 1)),
                  pl.BlockSpec((None, MLA_KV_LORA, ww), lambda b: (l, 0, 0))],
        out_specs=pl.BlockSpec((sq, D_MODEL), lambda b: (b, 0)),
        out_shape=jax.ShapeDtypeStruct((batch * sq, D_MODEL), F32),
        compiler_params=_cparams(("parallel",), est),
        name="mla_attention_sample",
    )(q_lat, q_rope, cache_ckv, cache_kr, ckv_new, kr_new, proj, proj, w_ukv_r)


def _merge_kernel(a_ref, b_ref, x_ref, g1_ref, w_ref, o_ref, m_scr):
    @pl.when(pl.program_id(1) == 0)
    def _():
        m_scr[...] = (a_ref[...] + b_ref[...]).astype(BF16)

    mix = jnp.dot(m_scr[...], w_ref[...], preferred_element_type=F32)
    o_ref[...] = _gate_res(x_ref[...], g1_ref[...], mix)


def _merge_call(a_part, b_part, x, mod, w_o_b, l, tm, groups):
    m = x.shape[0]
    tn = MERGE_TN
    est = (2 * 2 * tm * D_MODEL * 4 + tm * D_MODEL * 2 + 2 * D_MODEL * tn * 2 + 4 * tm * tn * 4
           + 2 * tm * tn * 4 + (4 << 20))
    return pl.pallas_call(
        _merge_kernel,
        grid=(m // tm, D_MODEL // tn),
        in_specs=[pl.BlockSpec((tm, D_MODEL), lambda i, j: (i, 0)),
                  pl.BlockSpec((tm, D_MODEL), lambda i, j: (i, 0)),
                  pl.BlockSpec((tm, tn), lambda i, j: (i, j)),
                  pl.BlockSpec((None, None, None, groups, tn), lambda i, j: (l, 2, i, 0, j)),
                  pl.BlockSpec((None, D_MODEL, tn), lambda i, j: (l, 0, j))],
        out_specs=pl.BlockSpec((tm, tn), lambda i, j: (i, j)),
        out_shape=jax.ShapeDtypeStruct((m, D_MODEL), F32),
        scratch_shapes=[pltpu.VMEM((tm, D_MODEL), BF16)],
        compiler_params=_cparams(("parallel", "arbitrary"), est),
        name="merge_out_proj",
    )(a_part, b_part, x, mod, w_o_b)


def _ffn_kernel(x_ref, sc_ref, sh_ref, g2_ref, gn_ref, wg_ref, wu_ref, wo_ref, o_ref, h_scr, acc_scr,
                *, nh):
    j = pl.program_id(1)

    @pl.when(j == 0)
    def _():
        h_scr[...] = _norm_mod(x_ref[...], gn_ref[...], sc_ref[...], sh_ref[...]).astype(BF16)
        acc_scr[...] = jnp.zeros(acc_scr.shape, F32)

    hb = h_scr[...]
    gate = jnp.dot(hb, wg_ref[...], preferred_element_type=F32)
    up = jnp.dot(hb, wu_ref[...], preferred_element_type=F32)
    act = ((gate * _sigmoid(gate)) * up).astype(BF16)
    acc_scr[...] += jnp.dot(act, wo_ref[...], preferred_element_type=F32)

    @pl.when(j == nh - 1)
    def _():
        o_ref[...] = _gate_res(x_ref[...], g2_ref[...], acc_scr[...])


def _ffn_call(x, mod, norm_g, w_in_b, w_out_b, l, tm, groups):
    m = x.shape[0]
    th = FFN_TH
    nh = FFN_HIDDEN // th
    est = (4 * tm * D_MODEL * 4 + tm * D_MODEL * 2 + tm * D_MODEL * 4 + 2 * 3 * D_MODEL * th * 2
           + 4 * tm * th * 4 + tm * D_MODEL * 4 + (4 << 20))
    mod_spec = lambda k: pl.BlockSpec((None, None, None, groups, D_MODEL), lambda i, j: (l, k, i, 0, 0))
    return pl.pallas_call(
        functools.partial(_ffn_kernel, nh=nh),
        grid=(m // tm, nh),
        in_specs=[pl.BlockSpec((tm, D_MODEL), lambda i, j: (i, 0)),
                  mod_spec(4), mod_spec(3), mod_spec(5),
                  pl.BlockSpec((None, 1, D_MODEL), lambda i, j: (l, 0, 0)),
                  pl.BlockSpec((None, D_MODEL, th), lambda i, j: (l, 0, j)),
                  pl.BlockSpec((None, D_MODEL, th), lambda i, j: (l, 0, nh + j)),
                  pl.BlockSpec((None, th, D_MODEL), lambda i, j: (l, j, 0))],
        out_specs=pl.BlockSpec((tm, D_MODEL), lambda i, j: (i, 0)),
        out_shape=jax.ShapeDtypeStruct((m, D_MODEL), F32),
        scratch_shapes=[pltpu.VMEM((tm, D_MODEL), BF16), pltpu.VMEM((tm, D_MODEL), F32)],
        compiler_params=_cparams(("parallel", "arbitrary"), est),
        name="ffn_swiglu",
    )(x, mod, mod, mod, norm_g, w_in_b, w_in_b, w_out_b)


def _final_norm_kernel(x_ref, g_ref, o_ref):
    o_ref[...] = _rms(x_ref[...], g_ref[...])


def _final_norm_call(x, g, tm):
    m = x.shape[0]
    est = 4 * tm * D_MODEL * 4 + 2 * tm * D_MODEL * 4 + (4 << 20)
    return pl.pallas_call(
        _final_norm_kernel,
        grid=(m // tm,),
        in_specs=[pl.BlockSpec((tm, D_MODEL), lambda i: (i, 0)),
                  pl.BlockSpec((1, D_MODEL), lambda i: (0, 0))],
        out_specs=pl.BlockSpec((tm, D_MODEL), lambda i: (i, 0)),
        out_shape=jax.ShapeDtypeStruct((m, D_MODEL), F32),
        compiler_params=_cparams(("parallel",), est),
        name="final_norm",
    )(x, g.reshape(1, D_MODEL))


def _rope_tables(pos, dim):
    inv = jnp.exp(-math.log(ROPE_BASE) * jnp.arange(0, dim, 2, dtype=F32) / dim)
    ang = pos.astype(F32)[:, None] * inv[None, :]
    return jnp.cos(ang), jnp.sin(ang)


def _mla_rope_tables(pos, reps):
    cos, sin = _rope_tables(pos, MLA_ROPE)
    z = jnp.zeros_like(cos)
    c = jnp.concatenate([cos, z, cos, z], axis=-1)
    s = jnp.concatenate([-sin, z, sin, z], axis=-1)
    return jnp.tile(c, (reps, 1)), jnp.tile(s, (reps, 1))


def _prep_weights(w_in, w_uq, w_ukv):
    kr0 = COL_GA
    ga0 = kr0 + MLA_ROPE
    w_gates = w_in[..., ga0:ga0 + 2 * D_MODEL].astype(BF16)
    w_kr = _pack_rope_lanes(w_in[..., kr0:ga0]).astype(BF16)
    uq = w_uq.reshape(DEPTH, MLA_Q_LORA, MLA_HEADS, MLA_NOPE + MLA_ROPE)
    uq = jnp.concatenate([uq[..., :MLA_NOPE], _pack_rope_lanes(uq[..., MLA_NOPE:])], axis=-1)
    w_uq_r = uq.reshape(DEPTH, MLA_Q_LORA, MLA_HEADS * HEAD_QK).astype(BF16)
    ukv = w_ukv.reshape(DEPTH, MLA_KV_LORA, MLA_HEADS, MLA_NOPE + MLA_V)
    w_ukv_r = jnp.concatenate([ukv[..., :MLA_NOPE].reshape(DEPTH, MLA_KV_LORA, -1),
                               ukv[..., MLA_NOPE:].reshape(DEPTH, MLA_KV_LORA, -1)], axis=-1).astype(BF16)
    return w_gates, w_kr, w_uq_r, w_ukv_r


def _layer_prompt(l, x, mod, tm, batch, seq, ret_tabs, mla_tabs, zero_state, log_g, W):
    proj, kr_raw = _inproj_call(x, mod, W["norm_mix"], W["w_in"], W["w_gates"], W["w_kr"], l, IN_TM_PROMPT, 1,
                                IN_TM_PROMPT // tm)
    a_part, new_state = _ret_call(proj, log_g, ret_tabs[0], ret_tabs[1], zero_state, 0, batch, seq,
                                  RET_L_PROMPT)
    q_cat, ckv, kr128 = _mla_q_call(proj, kr_raw, W["q_norm"], W["kv_norm"], mla_tabs[0], mla_tabs[1],
                                    W["w_uq_r"], l, tm)
    k_cat, v = _mla_kv_call(ckv, kr128, W["w_ukv_r"], l, ckv.shape[0], tm)
    b_part = _attn_prompt_call(q_cat, k_cat, v, proj, batch, seq, ATT_TQ)
    x = _merge_call(a_part, b_part, x, mod, W["w_o"], l, tm, 1)
    x = _ffn_call(x, mod, W["norm_ffn"], W["w_ffn_in"], W["w_ffn_out"], l, tm, 1)
    return x, ckv, _unpack_rope_lanes(kr128), new_state


def _layer_sample(l, x, mod, groups, tm, batch, seq, ret_tabs, mla_tabs, state_ret, cache_ckv, cache_kr,
                  log_g, W):
    proj, kr_raw = _inproj_call(x, mod, W["norm_mix"], W["w_in"], W["w_gates"], W["w_kr"], l, tm, groups, 1)
    a_part, new_state = _ret_call(proj, log_g, ret_tabs[0], ret_tabs[1], state_ret, l, batch, seq, seq)
    q_lat, q_rope, ckv, kr = _mla_q_sample_call(proj, kr_raw, W["q_norm"], W["kv_norm"], mla_tabs[0],
                                                mla_tabs[1], W["w_uq_r"], W["w_ukv_r"], l)
    b_part = _attn_sample_call(q_lat, q_rope, ckv, kr, cache_ckv, cache_kr, proj, W["w_ukv_r"], l, batch, seq)
    x = _merge_call(a_part, b_part, x, mod, W["w_o"], l, tm, groups)
    x = _ffn_call(x, mod, W["norm_ffn"], W["w_ffn_in"], W["w_ffn_out"], l, tm, groups)
    return x, ckv, kr, new_state


def kernel(x_prompt, x_sample, c_prompt, c_sample, cache_mla_ckv, cache_mla_krope, state_ret, w_ada, b_ada,
           norm_mix, norm_ffn, w_in, mla_q_norm, w_uq, mla_kv_norm, w_ukv, w_o, w_ffn_in, w_ffn_out,
           norm_final):
    bp, sp, _ = x_prompt.shape
    bs, ss, _ = x_sample.shape
    tm = TOKEN_TILE
    assert sp % IN_TM_PROMPT == 0 and (bs * ss) % tm == 0 and tm % ss == 0

    w_gates, w_kr, w_uq_r, w_ukv_r = _prep_weights(w_in, w_uq, w_ukv)
    W = dict(w_in=w_in, w_gates=w_gates, w_kr=w_kr, w_uq_r=w_uq_r, w_ukv_r=w_ukv_r,
             w_o=w_o.astype(BF16), w_ffn_in=w_ffn_in.astype(BF16), w_ffn_out=w_ffn_out.astype(BF16),
             norm_mix=norm_mix.reshape(DEPTH, 1, D_MODEL), norm_ffn=norm_ffn.reshape(DEPTH, 1, D_MODEL),
             q_norm=mla_q_norm.reshape(DEPTH, 1, MLA_Q_LORA), kv_norm=mla_kv_norm.reshape(DEPTH, 1, MLA_KV_LORA))

    c_rows = -(-(bp + bs) // ADA_ROW_ALIGN) * ADA_ROW_ALIGN
    c_all = jnp.concatenate([c_prompt, c_sample, jnp.zeros((c_rows - bp - bs, D_MODEL), F32)], axis=0)
    mod_all = _ada_call(c_all, w_ada, b_ada)[:, :bp + bs]
    mod_all = mod_all.reshape(DEPTH, bp + bs, 6, D_MODEL).transpose(0, 2, 1, 3)
    tiles_per_batch = sp // tm
    mod_p = jnp.repeat(mod_all[:, :, :bp], tiles_per_batch, axis=2)[:, :, :, None, :]
    groups_s = tm // ss
    mod_s = mod_all[:, :, bp:].reshape(DEPTH, 6, (bs * ss) // tm, groups_s, D_MODEL)

    log_g = jnp.log1p(-jnp.exp2(-RET_GAMMA_EXP0 - jnp.arange(RET_HEADS, dtype=F32)))
    pos_p = jnp.arange(sp)
    pos_s = PAST_LEN + jnp.arange(ss)
    ret_tabs_p = _rope_tables(pos_p, RET_DK)
    ret_tabs_s = _rope_tables(pos_s, RET_DK)
    mla_tabs_p = _mla_rope_tables(pos_p, bp)
    mla_tabs_s = _mla_rope_tables(pos_s, bs)
    zero_state = jnp.zeros((1, bp, RET_HEADS, RET_DK, RET_DV), F32)

    xp = x_prompt.reshape(bp * sp, D_MODEL)
    xs = x_sample.reshape(bs * ss, D_MODEL)
    outs = [[] for _ in range(6)]
    for l in range(DEPTH):
        xp, ckv, kr, st = _layer_prompt(l, xp, mod_p, tm, bp, sp, ret_tabs_p, mla_tabs_p, zero_state, log_g, W)
        outs[0].append(ckv.reshape(bp, sp, MLA_KV_LORA))
        outs[1].append(kr.reshape(bp, sp, MLA_ROPE))
        outs[2].append(st)
        xs, ckv, kr, st = _layer_sample(l, xs, mod_s, groups_s, tm, bs, ss, ret_tabs_s, mla_tabs_s, state_ret,
                                        cache_mla_ckv, cache_mla_krope, log_g, W)
        outs[3].append(ckv.reshape(bs, ss, MLA_KV_LORA))
        outs[4].append(kr.reshape(bs, ss, MLA_ROPE))
        outs[5].append(st)

    y_prompt = _final_norm_call(xp, norm_final, tm).reshape(bp, sp, D_MODEL)
    y_sample = _final_norm_call(xs, norm_final, tm).reshape(bs, ss, D_MODEL)
    return (y_prompt, y_sample, jnp.stack(outs[0]), jnp.stack(outs[1]), jnp.stack(outs[2]),
            jnp.stack(outs[3]), jnp.stack(outs[4]), jnp.stack(outs[5]))
```

```python
import functools
import math

import jax
import jax.numpy as jnp
from jax import lax
from jax.experimental import pallas as pl
from jax.experimental.pallas import tpu as pltpu

D_MODEL = 2048
DEPTH = 4
PAST_LEN = 1024
CHUNK = 64
RET_HEADS = 8
RET_DK = D_MODEL // RET_HEADS
RET_DV = D_MODEL // RET_HEADS
MLA_HEADS = 16
MLA_Q_LORA = D_MODEL // 4
MLA_KV_LORA = D_MODEL // 4
MLA_NOPE = 128
MLA_ROPE = 64
MLA_V = D_MODEL // MLA_HEADS
FFN_HIDDEN = -(-8 * D_MODEL // (3 * 256)) * 256
ROPE_BASE = 10000.0
RET_GAMMA_EXP0 = 5.0
RMS_EPS = 1e-6
GN_EPS = 1e-5
NEG_INF = -1e30

F32 = jnp.float32
BF16 = jnp.bfloat16

V7X_LANES = 128
V7X_VMEM_LIMIT_CAP = 56 * 1024 * 1024

COL_RQ, COL_RK, COL_RV, COL_RG = 0, D_MODEL, 2 * D_MODEL, 3 * D_MODEL
COL_DQ = 4 * D_MODEL
COL_DKV = COL_DQ + MLA_Q_LORA
COL_GA = COL_DKV + MLA_KV_LORA
COL_GB = COL_GA + D_MODEL
MAIN_COLS = COL_GB + D_MODEL
HEAD_QK = 2 * V7X_LANES
HALF_ROPE = MLA_ROPE // 2
ADA_ROW_ALIGN = 16
TOKEN_TILE = 512
IN_TM_PROMPT = 1024
IN_TN = 512
MERGE_TN = 1024
ADA_TN = 1024
FFN_TH = 512
RET_L_PROMPT = 256
ATT_TQ = 1024
ATT_TK = 512
ATT_SPLIT = 4
QK_SCALE_LOG2 = (MLA_NOPE + MLA_ROPE) ** -0.5 * math.log2(math.e)


def _cparams(sem, est_bytes):
    return pltpu.CompilerParams(dimension_semantics=sem,
                                vmem_limit_bytes=min(int(est_bytes), V7X_VMEM_LIMIT_CAP))


def _sigmoid(x):
    return jax.nn.sigmoid(x)


def _rms(x, g):
    return x * lax.rsqrt(jnp.mean(x * x, axis=-1, keepdims=True) + RMS_EPS) * g


def _norm_mod(x, g, sc, sh):
    tm, d = x.shape
    groups = sc.shape[0]
    y = _rms(x, g)
    if groups == 1:
        return y * (1.0 + sc) + sh
    y3 = y.reshape(groups, tm // groups, d)
    return (y3 * (1.0 + sc[:, None, :]) + sh[:, None, :]).reshape(tm, d)


def _gate_res(x, gate, upd):
    tm, n = x.shape
    groups = gate.shape[0]
    if groups == 1:
        return x + gate * upd
    return x + (gate[:, None, :] * upd.reshape(groups, tm // groups, n)).reshape(tm, n)


def _pack_rope_lanes(x):
    z = jnp.zeros(x.shape[:-1] + (HALF_ROPE,), x.dtype)
    return jnp.concatenate([x[..., :HALF_ROPE], z, x[..., HALF_ROPE:], z], axis=-1)


def _unpack_rope_lanes(x):
    return jnp.concatenate([x[..., :HALF_ROPE], x[..., 2 * HALF_ROPE:3 * HALF_ROPE]], axis=-1)


def _ada_kernel(c_ref, w_ref, b_ref, o_ref):
    c = c_ref[...]
    a = (c * _sigmoid(c)).astype(BF16)
    o_ref[0] = jnp.dot(a, w_ref[0].astype(BF16), preferred_element_type=F32) + b_ref[0]


def _ada_call(c_all, w_ada, b_ada):
    nb = c_all.shape[0]
    n = w_ada.shape[-1]
    tn = ADA_TN
    est = 2 * (D_MODEL * tn * 4) + D_MODEL * tn * 2 + 4 * nb * (D_MODEL + 2 * tn) * 4 + (4 << 20)
    return pl.pallas_call(
        _ada_kernel,
        grid=(DEPTH, n // tn),
        in_specs=[pl.BlockSpec((nb, D_MODEL), lambda l, j: (0, 0)),
                  pl.BlockSpec((1, D_MODEL, tn), lambda l, j: (l, 0, j)),
                  pl.BlockSpec((1, 1, tn), lambda l, j: (l, 0, j))],
        out_specs=pl.BlockSpec((1, nb, tn), lambda l, j: (l, 0, j)),
        out_shape=jax.ShapeDtypeStruct((DEPTH, nb, n), F32),
        compiler_params=_cparams(("parallel", "parallel"), est),
        name="ada_mod",
    )(c_all, w_ada, b_ada.reshape(DEPTH, 1, n))


def _inproj_kernel(x_ref, sc_ref, sh_ref, g_ref, w_ref, wn_ref, wkr_ref, o_ref, kr_ref, h_scr, *, n_main):
    j = pl.program_id(1)
    nt = (((1,), (1,)), ((), ()))

    @pl.when(j == 0)
    def _():
        hb = _norm_mod(x_ref[...], g_ref[...], sc_ref[...], sh_ref[...]).astype(BF16)
        h_scr[...] = hb
        wkr = wkr_ref[...]
        z = jnp.zeros((HALF_ROPE, wkr.shape[1]), wkr.dtype)
        wkr = jnp.concatenate([wkr[:HALF_ROPE], z, wkr[HALF_ROPE:], z], axis=0).astype(BF16)
        kr_ref[...] = lax.dot_general(hb, wkr, nt, preferred_element_type=F32)

    @pl.when(j < n_main)
    def _():
        o_ref[...] = lax.dot_general(h_scr[...], w_ref[...].astype(BF16), nt, preferred_element_type=F32)

    @pl.when(j >= n_main)
    def _():
        w = jnp.concatenate([w_ref[MLA_ROPE:, :], wn_ref[...]], axis=0).astype(BF16)
        o_ref[...] = lax.dot_general(h_scr[...], w, nt, preferred_element_type=F32)


def _inproj_call(x, mod, norm_g, w_in_t, l, tm, groups, mod_stride):
    m = x.shape[0]
    n_main = COL_GA // IN_TN
    sub = IN_TN // MLA_ROPE
    est = (2 * tm * D_MODEL * 4 + tm * D_MODEL * 2 + 2 * D_MODEL * (IN_TN + MLA_ROPE) * 4
           + 2 * D_MODEL * IN_TN * 2 + 4 * tm * IN_TN * 4 + tm * D_MODEL * 4 + (4 << 20))
    mod_spec = lambda k: pl.BlockSpec((None, None, None, groups, D_MODEL),
                                      lambda i, j: (l, k, i * mod_stride, 0, 0))
    return pl.pallas_call(
        functools.partial(_inproj_kernel, n_main=n_main),
        grid=(m // tm, MAIN_COLS // IN_TN),
        in_specs=[pl.BlockSpec((tm, D_MODEL), lambda i, j: (i, 0)),
                  mod_spec(1), mod_spec(0),
                  pl.BlockSpec((None, 1, D_MODEL), lambda i, j: (l, 0, 0)),
                  pl.BlockSpec((None, IN_TN, D_MODEL), lambda i, j: (l, j, 0)),
                  pl.BlockSpec((None, MLA_ROPE, D_MODEL),
                               lambda i, j: (l, (jnp.maximum(j, n_main) + 1) * sub, 0)),
                  pl.BlockSpec((None, MLA_ROPE, D_MODEL), lambda i, j: (l, COL_GA // MLA_ROPE, 0))],
        out_specs=[pl.BlockSpec((tm, IN_TN), lambda i, j: (i, j)),
                   pl.BlockSpec((tm, V7X_LANES), lambda i, j: (i, 0))],
        out_shape=[jax.ShapeDtypeStruct((m, MAIN_COLS), F32),
                   jax.ShapeDtypeStruct((m, V7X_LANES), F32)],
        scratch_shapes=[pltpu.VMEM((tm, D_MODEL), BF16)],
        compiler_params=_cparams(("parallel", "arbitrary"), est),
        name="in_proj",
    )(x, mod, mod, norm_g, w_in_t, w_in_t, w_in_t)


def _ret_kernel(lg_ref, q_ref, k_ref, v_ref, rg_ref, ga_ref, cos_ref, sin_ref, s0_ref,
                a_ref, st_ref, dm_scr, *, chunk_len):
    h = pl.program_id(1)
    c = pl.program_id(2)
    lg = lg_ref[h]
    L = chunk_len

    @pl.when(c == 0)
    def _():
        st_ref[...] = s0_ref[...]
        ri = lax.broadcasted_iota(jnp.int32, (L, L), 0)
        ci = lax.broadcasted_iota(jnp.int32, (L, L), 1)
        diff = (ri - ci).astype(F32)
        dm_scr[...] = jnp.where(diff >= 0, jnp.exp(jnp.maximum(diff, 0.0) * lg), 0.0)

    cos = cos_ref[...]
    sin = sin_ref[...]
    half = RET_DK // 2

    def rope(x):
        x1, x2 = x[:, :half], x[:, half:]
        return jnp.concatenate([x1 * cos - x2 * sin, x1 * sin + x2 * cos], axis=-1)

    q = rope(q_ref[...])
    k = rope(k_ref[...]) * (RET_DK ** -0.5)
    vb = v_ref[...].astype(BF16)
    idx = lax.broadcasted_iota(jnp.int32, (L, 1), 0).astype(F32)
    xi = jnp.exp((idx + 1.0) * lg)
    zeta = jnp.exp((L - 1.0 - idx) * lg)
    g_l = jnp.exp(jnp.full((1, 1), float(L), F32) * lg)

    qb = q.astype(BF16)
    kb = k.astype(BF16)
    scores = lax.dot_general(qb, kb, (((1,), (1,)), ((), ())), preferred_element_type=F32) * dm_scr[...]
    st = st_ref[0, 0]
    o = (jnp.dot(scores.astype(BF16), vb, preferred_element_type=F32)
         + jnp.dot(qb, st.astype(BF16), preferred_element_type=F32) * xi)
    kz = (k * zeta).astype(BF16)
    st_ref[0, 0] = st * g_l + lax.dot_general(kz, vb, (((0,), (0,)), ((), ())),
                                              preferred_element_type=F32)

    mu = jnp.mean(o, axis=-1, keepdims=True)
    d = o - mu
    var = jnp.mean(d * d, axis=-1, keepdims=True)
    on = d * lax.rsqrt(var + GN_EPS)
    rg = rg_ref[...]
    a_ref[...] = _sigmoid(ga_ref[...]) * ((rg * _sigmoid(rg)) * on)


def _ret_call(proj, log_g, cos, sin, state0, state_layer, batch, seq, chunk_len):
    L = chunk_len
    nc = seq // L
    col = lambda base: (lambda b, h, c: (b * nc + c, base // RET_DK + h))
    blk = lambda base: pl.BlockSpec((L, RET_DK), col(base))
    nh = RET_HEADS
    est = 2 * 6 * L * RET_DK * 4 + 4 * RET_DK * RET_DV * 4 + L * L * 4 + 8 * L * max(L, RET_DK) * 4 + (4 << 20)
    return pl.pallas_call(
        functools.partial(_ret_kernel, chunk_len=L),
        grid=(batch, nh, nc),
        in_specs=[pl.BlockSpec(memory_space=pltpu.SMEM),
                  blk(COL_RQ), blk(COL_RK), blk(COL_RV), blk(COL_RG), blk(COL_GA),
                  pl.BlockSpec((L, RET_DK // 2), lambda b, h, c: (c, 0)),
                  pl.BlockSpec((L, RET_DK // 2), lambda b, h, c: (c, 0)),
                  pl.BlockSpec((None, 1, 1, RET_DK, RET_DV), lambda b, h, c: (state_layer, b, h, 0, 0))],
        out_specs=[pl.BlockSpec((L, RET_DV), lambda b, h, c: (b * nc + c, h)),
                   pl.BlockSpec((1, 1, RET_DK, RET_DV), lambda b, h, c: (b, h, 0, 0))],
        out_shape=[jax.ShapeDtypeStruct((batch * seq, D_MODEL), F32),
                   jax.ShapeDtypeStruct((batch, nh, RET_DK, RET_DV), F32)],
        scratch_shapes=[pltpu.VMEM((L, L), F32)],
        compiler_params=_cparams(("parallel", "parallel", "arbitrary"), est),
        name="retention",
    )(log_g, proj, proj, proj, proj, proj, cos, sin, state0)


def _rope128(x, c, s):
    return x * c + pltpu.roll(x, V7X_LANES // 2, 1) * s


def _mla_q_kernel(dq_ref, dkv_ref, kr_ref, qn_ref, kvn_ref, c_ref, s_ref, wuq_ref,
                  q_out, ckv_out, kro_out):
    c = c_ref[...]
    s = s_ref[...]
    cq = _rms(dq_ref[...], qn_ref[...]).astype(BF16)
    for h in range(MLA_HEADS):
        lo = h * HEAD_QK
        qh = jnp.dot(cq, wuq_ref[:, lo:lo + HEAD_QK], preferred_element_type=F32) * QK_SCALE_LOG2
        q_out[:, lo:lo + MLA_NOPE] = qh[:, :MLA_NOPE].astype(BF16)
        q_out[:, lo + MLA_NOPE:lo + HEAD_QK] = _rope128(qh[:, MLA_NOPE:], c, s).astype(BF16)
    ckv_out[...] = _rms(dkv_ref[...], kvn_ref[...])
    kro_out[...] = _rope128(kr_ref[...], c, s)


def _mla_q_call(proj, kr_raw, q_norm, kv_norm, rope_c, rope_s, w_uq_r, l, tm):
    m = proj.shape[0]
    dq_blk = COL_DQ // MLA_Q_LORA
    qw = MLA_HEADS * HEAD_QK
    est = (2 * (2 * tm * MLA_Q_LORA * 4 + 3 * tm * V7X_LANES * 4) + 2 * MLA_Q_LORA * qw * 2
           + 2 * tm * qw * 2 + 2 * tm * MLA_KV_LORA * 4 + 2 * tm * V7X_LANES * 4 + 8 * tm * HEAD_QK * 4
           + (4 << 20))
    return pl.pallas_call(
        _mla_q_kernel,
        grid=(m // tm,),
        in_specs=[pl.BlockSpec((tm, MLA_Q_LORA), lambda i: (i, dq_blk)),
                  pl.BlockSpec((tm, MLA_KV_LORA), lambda i: (i, dq_blk + 1)),
                  pl.BlockSpec((tm, V7X_LANES), lambda i: (i, 0)),
                  pl.BlockSpec((None, 1, MLA_Q_LORA), lambda i: (l, 0, 0)),
                  pl.BlockSpec((None, 1, MLA_KV_LORA), lambda i: (l, 0, 0)),
                  pl.BlockSpec((tm, V7X_LANES), lambda i: (i, 0)),
                  pl.BlockSpec((tm, V7X_LANES), lambda i: (i, 0)),
                  pl.BlockSpec((None, MLA_Q_LORA, qw), lambda i: (l, 0, 0))],
        out_specs=[pl.BlockSpec((tm, qw), lambda i: (i, 0)),
                   pl.BlockSpec((tm, MLA_KV_LORA), lambda i: (i, 0)),
                   pl.BlockSpec((tm, V7X_LANES), lambda i: (i, 0))],
        out_shape=[jax.ShapeDtypeStruct((m, qw), BF16),
                   jax.ShapeDtypeStruct((m, MLA_KV_LORA), F32),
                   jax.ShapeDtypeStruct((m, V7X_LANES), F32)],
        compiler_params=_cparams(("parallel",), est),
        name="mla_latents",
    )(proj, proj, kr_raw, q_norm, kv_norm, rope_c, rope_s, w_uq_r)


def _mla_q_sample_kernel(dq_ref, dkv_ref, kr_ref, qn_ref, kvn_ref, c_ref, s_ref, wuq_ref, wukv_ref,
                         qlat_out, qr_out, ckv_out, kro_out):
    c = c_ref[...]
    s = s_ref[...]
    cq = _rms(dq_ref[...], qn_ref[...]).astype(BF16)
    for h in range(MLA_HEADS):
        lo = h * HEAD_QK
        qh = jnp.dot(cq, wuq_ref[:, lo:lo + HEAD_QK], preferred_element_type=F32) * QK_SCALE_LOG2
        w_uk = wukv_ref[:, h * MLA_NOPE:(h + 1) * MLA_NOPE]
        q_lat = lax.dot_general(qh[:, :MLA_NOPE].astype(BF16), w_uk, (((1,), (1,)), ((), ())),
                                preferred_element_type=F32)
        qlat_out[:, h * MLA_KV_LORA:(h + 1) * MLA_KV_LORA] = q_lat.astype(BF16)
        qr_out[:, h * MLA_ROPE:(h + 1) * MLA_ROPE] = _unpack_rope_lanes(
            _rope128(qh[:, MLA_NOPE:], c, s)).astype(BF16)
    ckv_out[...] = _rms(dkv_ref[...], kvn_ref[...])
    kro_out[...] = _unpack_rope_lanes(_rope128(kr_ref[...], c, s))


def _mla_q_sample_call(proj, kr_raw, q_norm, kv_norm, rope_c, rope_s, w_uq_r, w_ukv_r, l):
    m = proj.shape[0]
    dq_blk = COL_DQ // MLA_Q_LORA
    qw = MLA_HEADS * HEAD_QK
    ww = MLA_HEADS * (MLA_NOPE + MLA_V)
    lat_w = MLA_HEADS * MLA_KV_LORA
    rope_w = MLA_HEADS * MLA_ROPE
    est = (2 * (2 * m * MLA_Q_LORA * 4 + 3 * m * V7X_LANES * 4) + 2 * MLA_Q_LORA * (qw + ww) * 2
           + 2 * m * (lat_w + rope_w) * 2 + 2 * m * (MLA_KV_LORA + MLA_ROPE) * 4 + 8 * m * MLA_KV_LORA * 4
           + (4 << 20))
    full = lambda shape: pl.BlockSpec(shape, lambda i: (0,) * len(shape))
    return pl.pallas_call(
        _mla_q_sample_kernel,
        grid=(1,),
        in_specs=[pl.BlockSpec((m, MLA_Q_LORA), lambda i: (0, dq_blk)),
                  pl.BlockSpec((m, MLA_KV_LORA), lambda i: (0, dq_blk + 1)),
                  full((m, V7X_LANES)),
                  pl.BlockSpec((None, 1, MLA_Q_LORA), lambda i: (l, 0, 0)),
                  pl.BlockSpec((None, 1, MLA_KV_LORA), lambda i: (l, 0, 0)),
                  full((m, V7X_LANES)), full((m, V7X_LANES)),
                  pl.BlockSpec((None, MLA_Q_LORA, qw), lambda i: (l, 0, 0)),
                  pl.BlockSpec((None, MLA_KV_LORA, ww), lambda i: (l, 0, 0))],
        out_specs=[full((m, lat_w)), full((m, rope_w)), full((m, MLA_KV_LORA)), full((m, MLA_ROPE))],
        out_shape=[jax.ShapeDtypeStruct((m, lat_w), BF16),
                   jax.ShapeDtypeStruct((m, rope_w), BF16),
                   jax.ShapeDtypeStruct((m, MLA_KV_LORA), F32),
                   jax.ShapeDtypeStruct((m, MLA_ROPE), F32)],
        compiler_params=_cparams(("arbitrary",), est),
        name="mla_latents_sample",
    )(proj, proj, kr_raw, q_norm, kv_norm, rope_c, rope_s, w_uq_r, w_ukv_r)


def _mla_kv_kernel(ckv_ref, kr_ref, wukv_ref, kcat_out, v_out):
    cb = ckv_ref[...].astype(BF16)
    krb = kr_ref[...].astype(BF16)
    for hp in range(MLA_HEADS // 2):
        kk = jnp.dot(cb, wukv_ref[:, hp * HEAD_QK:(hp + 1) * HEAD_QK], preferred_element_type=F32)
        for t in range(2):
            lo = (2 * hp + t) * HEAD_QK
            kcat_out[:, lo:lo + MLA_NOPE] = kk[:, t * MLA_NOPE:(t + 1) * MLA_NOPE].astype(BF16)
            kcat_out[:, lo + MLA_NOPE:lo + HEAD_QK] = krb
    nv = MLA_HEADS * MLA_NOPE
    for hp in range(nv // HEAD_QK):
        lo = hp * HEAD_QK
        v_out[:, lo:lo + HEAD_QK] = jnp.dot(cb, wukv_ref[:, nv + lo:nv + lo + HEAD_QK],
                                            preferred_element_type=F32).astype(BF16)


def _mla_kv_call(ckv, kr128, w_ukv_r, l, rows, tm):
    kw = MLA_HEADS * HEAD_QK
    vw = MLA_HEADS * MLA_V
    ww = MLA_HEADS * (MLA_NOPE + MLA_V)
    est = (2 * tm * (MLA_KV_LORA + V7X_LANES) * 4 + 2 * MLA_KV_LORA * ww * 2 + 2 * tm * (kw + vw) * 2
           + 8 * tm * HEAD_QK * 4 + (4 << 20))
    return pl.pallas_call(
        _mla_kv_kernel,
        grid=(rows // tm,),
        in_specs=[pl.BlockSpec((tm, MLA_KV_LORA), lambda i: (i, 0)),
                  pl.BlockSpec((tm, V7X_LANES), lambda i: (i, 0)),
                  pl.BlockSpec((None, MLA_KV_LORA, ww), lambda i: (l, 0, 0))],
        out_specs=[pl.BlockSpec((tm, kw), lambda i: (i, 0)),
                   pl.BlockSpec((tm, vw), lambda i: (i, 0))],
        out_shape=[jax.ShapeDtypeStruct((rows, kw), BF16),
                   jax.ShapeDtypeStruct((rows, vw), BF16)],
        compiler_params=_cparams(("parallel",), est),
        name="mla_kv_expand",
    )(ckv, kr128, w_ukv_r)


def _scores(q, k_blk):
    return lax.dot_general(q, k_blk, (((1,), (1,)), ((), ())), preferred_element_type=F32)


def _softmax_tile(s, v_blk, m_prev, l_prev, acc_prev, mask):
    if mask is not None:
        s = jnp.where(mask, s, NEG_INF)
    n_groups = s.shape[1] // V7X_LANES
    groups = [s[:, g * V7X_LANES:(g + 1) * V7X_LANES] for g in range(n_groups)]
    m_new = jnp.maximum(m_prev, jnp.max(functools.reduce(jnp.maximum, groups), axis=-1, keepdims=True))
    alpha = jnp.exp2(m_prev - m_new)
    ps = [jnp.exp2(g - m_new) for g in groups]
    l_new = alpha * l_prev + functools.reduce(jnp.add, ps)
    p = jnp.concatenate([x.astype(BF16) for x in ps], axis=-1)
    acc_new = alpha * acc_prev + jnp.dot(p, v_blk, preferred_element_type=F32)
    return m_new, l_new, acc_new


def _attn_prompt_kernel(q_ref, k_ref, v_ref, gb_ref, o_ref, m_scr, l_scr, acc_scr, *, tq, tk, n_split):
    i = pl.program_id(2)
    rows = tq // n_split
    m_scr[...] = jnp.full(m_scr.shape, NEG_INF, F32)
    l_scr[...] = jnp.zeros(l_scr.shape, F32)
    acc_scr[...] = jnp.zeros(acc_scr.shape, F32)

    def scores(r, k_blk):
        return _scores(q_ref[pl.ds(r * rows, rows), :], k_blk)

    def update(r, s, v_blk, mask):
        rs = pl.ds(r * rows, rows)
        m_new, l_new, acc_new = _softmax_tile(s, v_blk, m_scr[rs, :], l_scr[rs, :], acc_scr[rs, :], mask)
        m_scr[rs, :] = m_new
        l_scr[rs, :] = l_new
        acc_scr[rs, :] = acc_new

    def body(j, carry):
        start = pl.multiple_of(j * tk, tk)
        k_blk = k_ref[pl.ds(start, tk), :]
        v_blk = v_ref[pl.ds(start, tk), :]
        ss = [scores(r, k_blk) for r in range(n_split)]
        for r in range(n_split):
            update(r, ss[r], v_blk, None)
        return carry

    blocks_per_tile = tq // tk
    lax.fori_loop(0, i * blocks_per_tile, body, 0)

    work = []
    for d in range(blocks_per_tile):
        k0 = d * tk
        for r in range(n_split):
            r0, r1 = r * rows, (r + 1) * rows
            width = min(k0 + tk, r1) - k0
            if width <= 0:
                continue
            start = pl.multiple_of(i * tq + k0, tk)
            mask = None
            if k0 + width > r0:
                rc = (lax.broadcasted_iota(jnp.int32, (rows, width), 0) + r0) // CHUNK
                kc = (lax.broadcasted_iota(jnp.int32, (rows, width), 1) + k0) // CHUNK
                mask = kc <= rc
            work.append((r, scores(r, k_ref[pl.ds(start, width), :]), v_ref[pl.ds(start, width), :], mask))
    for r, s, v_blk, mask in work:
        update(r, s, v_blk, mask)

    l_row = jnp.sum(l_scr[...], axis=-1, keepdims=True)
    o_ref[...] = _sigmoid(gb_ref[...]) * (acc_scr[...] / l_row)


def _attn_prompt_call(q_cat, k_cat, v, proj, batch, seq, tq):
    nq = seq // tq
    gb_blk = COL_GB // MLA_V
    est = (2 * (tq * HEAD_QK * 2 + seq * HEAD_QK * 2 + seq * MLA_V * 2 + 2 * tq * MLA_V * 4)
           + 3 * tq * V7X_LANES * 4 + 6 * tq * ATT_TK * 4 + (4 << 20))
    return pl.pallas_call(
        functools.partial(_attn_prompt_kernel, tq=tq, tk=ATT_TK, n_split=ATT_SPLIT),
        grid=(batch, MLA_HEADS, nq),
        in_specs=[pl.BlockSpec((tq, HEAD_QK), lambda b, h, i: (b * nq + i, h)),
                  pl.BlockSpec((seq, HEAD_QK), lambda b, h, i: (b, h)),
                  pl.BlockSpec((seq, MLA_V), lambda b, h, i: (b, h)),
                  pl.BlockSpec((tq, MLA_V), lambda b, h, i: (b * nq + i, gb_blk + h))],
        out_specs=pl.BlockSpec((tq, MLA_V), lambda b, h, i: (b * nq + i, h)),
        out_shape=jax.ShapeDtypeStruct((batch * seq, MLA_HEADS * MLA_V), F32),
        scratch_shapes=[pltpu.VMEM((tq, V7X_LANES), F32), pltpu.VMEM((tq, V7X_LANES), F32),
                        pltpu.VMEM((tq, MLA_V), F32)],
        compiler_params=_cparams(("parallel", "parallel", "arbitrary"), est),
        name="mla_attention_prompt",
    )(q_cat, k_cat, v, proj)


def _attn_sample_kernel(qlat_ref, qr_ref, cc_ref, ckr_ref, nc_ref, nkr_ref, gb0_ref, gb1_ref, wukv_ref,
                        o_ref, *, sq):
    nt = (((1,), (1,)), ((), ()))
    q_lat = jnp.concatenate([qlat_ref[:, h * MLA_KV_LORA:(h + 1) * MLA_KV_LORA] for h in range(MLA_HEADS)],
                            axis=0)
    q_r = jnp.concatenate([qr_ref[:, h * MLA_ROPE:(h + 1) * MLA_ROPE] for h in range(MLA_HEADS)], axis=0)
    kc = cc_ref[...].astype(BF16)
    kn = nc_ref[...].astype(BF16)
    s_c = (lax.dot_general(q_lat, kc, nt, preferred_element_type=F32)
           + lax.dot_general(q_r, ckr_ref[...].astype(BF16), nt, preferred_element_type=F32))
    s_n = (lax.dot_general(q_lat, kn, nt, preferred_element_type=F32)
           + lax.dot_general(q_r, nkr_ref[...].astype(BF16), nt, preferred_element_type=F32))
    m = jnp.maximum(jnp.max(s_c, axis=-1, keepdims=True), jnp.max(s_n, axis=-1, keepdims=True))
    p_c = jnp.exp2(s_c - m)
    p_n = jnp.exp2(s_n - m)
    l_row = jnp.sum(p_c, axis=-1, keepdims=True) + jnp.sum(p_n, axis=-1, keepdims=True)
    o_lat = (jnp.dot(p_c.astype(BF16), kc, preferred_element_type=F32)
             + jnp.dot(p_n.astype(BF16), kn, preferred_element_type=F32)) / l_row
    o_lat = o_lat.astype(BF16)
    nv = MLA_HEADS * MLA_NOPE
    half = MLA_HEADS // 2
    for h in range(MLA_HEADS):
        w_uv = wukv_ref[:, nv + h * MLA_V:nv + (h + 1) * MLA_V]
        o_h = jnp.dot(o_lat[h * sq:(h + 1) * sq, :], w_uv, preferred_element_type=F32)
        gb_ref = gb0_ref if h < half else gb1_ref
        gb = gb_ref[:, (h % half) * MLA_V:(h % half + 1) * MLA_V]
        o_ref[:, h * MLA_V:(h + 1) * MLA_V] = _sigmoid(gb) * o_h


def _attn_sample_call(q_lat, q_rope, ckv_new, kr_new, cache_ckv, cache_kr, proj, w_ukv_r, l, batch, sq):
    assert (PAST_LEN + sq - 1) // CHUNK <= PAST_LEN // CHUNK
    past = cache_ckv.shape[2]
    lat_w = MLA_HEADS * MLA_KV_LORA
    rope_w = MLA_HEADS * MLA_ROPE
    ww = MLA_HEADS * (MLA_NOPE + MLA_V)
    gw = D_MODEL // 2
    gb_blk = COL_GB // gw
    rows = MLA_HEADS * sq
    est = (2 * (sq * (lat_w + rope_w) * 2 + past * (MLA_KV_LORA + V7X_LANES) * 4 + MLA_KV_LORA * ww * 2
                + 4 * sq * D_MODEL * 4) + past * (MLA_KV_LORA + V7X_LANES) * 2 + 6 * rows * past * 4
           + 4 * rows * MLA_KV_LORA * 4 + (4 << 20))
    return pl.pallas_call(
        functools.partial(_attn_sample_kernel, sq=sq),
        grid=(batch,),
        in_specs=[pl.BlockSpec((sq, lat_w), lambda b: (b, 0)),
                  pl.BlockSpec((sq, rope_w), lambda b: (b, 0)),
                  pl.BlockSpec((None, None, past, MLA_KV_LORA), lambda b: (l, b, 0, 0)),
                  pl.BlockSpec((None, None, past, MLA_ROPE), lambda b: (l, b, 0, 0)),
                  pl.BlockSpec((sq, MLA_KV_LORA), lambda b: (b, 0)),
                  pl.BlockSpec((sq, MLA_ROPE), lambda b: (b, 0)),
                  pl.BlockSpec((sq, gw), lambda b: (b, gb_blk)),
                  pl.BlockSpec((sq, gw), lambda b: (b, gb_blk + 1)),
                  pl.BlockSpec((None, MLA_KV_LORA, ww), lambda b: (l, 0, 0))],
        out_specs=pl.BlockSpec((sq, D_MODEL), lambda b: (b, 0)),
        out_shape=jax.ShapeDtypeStruct((batch * sq, D_MODEL), F32),
        compiler_params=_cparams(("parallel",), est),
        name="mla_attention_sample",
    )(q_lat, q_rope, cache_ckv, cache_kr, ckv_new, kr_new, proj, proj, w_ukv_r)


def _merge_kernel(a_ref, b_ref, x_ref, g1_ref, w_ref, o_ref, m_scr):
    @pl.when(pl.program_id(1) == 0)
    def _():
        m_scr[...] = (a_ref[...] + b_ref[...]).astype(BF16)

    mix = jnp.dot(m_scr[...], w_ref[...], preferred_element_type=F32)
    o_ref[...] = _gate_res(x_ref[...], g1_ref[...], mix)


def _merge_call(a_part, b_part, x, mod, w_o_b, l, tm, groups):
    m = x.shape[0]
    tn = MERGE_TN
    est = (2 * 2 * tm * D_MODEL * 4 + tm * D_MODEL * 2 + 2 * D_MODEL * tn * 2 + 4 * tm * tn * 4
           + 2 * tm * tn * 4 + (4 << 20))
    return pl.pallas_call(
        _merge_kernel,
        grid=(m // tm, D_MODEL // tn),
        in_specs=[pl.BlockSpec((tm, D_MODEL), lambda i, j: (i, 0)),
                  pl.BlockSpec((tm, D_MODEL), lambda i, j: (i, 0)),
                  pl.BlockSpec((tm, tn), lambda i, j: (i, j)),
                  pl.BlockSpec((None, None, None, groups, tn), lambda i, j: (l, 2, i, 0, j)),
                  pl.BlockSpec((None, D_MODEL, tn), lambda i, j: (l, 0, j))],
        out_specs=pl.BlockSpec((tm, tn), lambda i, j: (i, j)),
        out_shape=jax.ShapeDtypeStruct((m, D_MODEL), F32),
        scratch_shapes=[pltpu.VMEM((tm, D_MODEL), BF16)],
        compiler_params=_cparams(("parallel", "arbitrary"), est),
        name="merge_out_proj",
    )(a_part, b_part, x, mod, w_o_b)


def _ffn_kernel(x_ref, sc_ref, sh_ref, g2_ref, gn_ref, wg_ref, wu_ref, wo_ref, o_ref, h_scr, acc_scr,
                *, nh):
    j = pl.program_id(1)

    @pl.when(j == 0)
    def _():
        h_scr[...] = _norm_mod(x_ref[...], gn_ref[...], sc_ref[...], sh_ref[...]).astype(BF16)
        acc_scr[...] = jnp.zeros(acc_scr.shape, F32)

    hb = h_scr[...]
    gate = jnp.dot(hb, wg_ref[...], preferred_element_type=F32)
    up = jnp.dot(hb, wu_ref[...], preferred_element_type=F32)
    act = ((gate * _sigmoid(gate)) * up).astype(BF16)
    acc_scr[...] += jnp.dot(act, wo_ref[...], preferred_element_type=F32)

    @pl.when(j == nh - 1)
    def _():
        o_ref[...] = _gate_res(x_ref[...], g2_ref[...], acc_scr[...])


def _ffn_call(x, mod, norm_g, w_in_b, w_out_b, l, tm, groups):
    m = x.shape[0]
    th = FFN_TH
    nh = FFN_HIDDEN // th
    est = (4 * tm * D_MODEL * 4 + tm * D_MODEL * 2 + tm * D_MODEL * 4 + 2 * 3 * D_MODEL * th * 2
           + 4 * tm * th * 4 + tm * D_MODEL * 4 + (4 << 20))
    mod_spec = lambda k: pl.BlockSpec((None, None, None, groups, D_MODEL), lambda i, j: (l, k, i, 0, 0))
    return pl.pallas_call(
        functools.partial(_ffn_kernel, nh=nh),
        grid=(m // tm, nh),
        in_specs=[pl.BlockSpec((tm, D_MODEL), lambda i, j: (i, 0)),
                  mod_spec(4), mod_spec(3), mod_spec(5),
                  pl.BlockSpec((None, 1, D_MODEL), lambda i, j: (l, 0, 0)),
                  pl.BlockSpec((None, D_MODEL, th), lambda i, j: (l, 0, j)),
                  pl.BlockSpec((None, D_MODEL, th), lambda i, j: (l, 0, nh + j)),
                  pl.BlockSpec((None, th, D_MODEL), lambda i, j: (l, j, 0))],
        out_specs=pl.BlockSpec((tm, D_MODEL), lambda i, j: (i, 0)),
        out_shape=jax.ShapeDtypeStruct((m, D_MODEL), F32),
        scratch_shapes=[pltpu.VMEM((tm, D_MODEL), BF16), pltpu.VMEM((tm, D_MODEL), F32)],
        compiler_params=_cparams(("parallel", "arbitrary"), est),
        name="ffn_swiglu",
    )(x, mod, mod, mod, norm_g, w_in_b, w_in_b, w_out_b)


def _final_norm_kernel(x_ref, g_ref, o_ref):
    o_ref[...] = _rms(x_ref[...], g_ref[...])


def _final_norm_call(x, g, tm):
    m = x.shape[0]
    est = 4 * tm * D_MODEL * 4 + 2 * tm * D_MODEL * 4 + (4 << 20)
    return pl.pallas_call(
        _final_norm_kernel,
        grid=(m // tm,),
        in_specs=[pl.BlockSpec((tm, D_MODEL), lambda i: (i, 0)),
                  pl.BlockSpec((1, D_MODEL), lambda i: (0, 0))],
        out_specs=pl.BlockSpec((tm, D_MODEL), lambda i: (i, 0)),
        out_shape=jax.ShapeDtypeStruct((m, D_MODEL), F32),
        compiler_params=_cparams(("parallel",), est),
        name="final_norm",
    )(x, g.reshape(1, D_MODEL))


def _rope_tables(pos, dim):
    inv = jnp.exp(-math.log(ROPE_BASE) * jnp.arange(0, dim, 2, dtype=F32) / dim)
    ang = pos.astype(F32)[:, None] * inv[None, :]
    return jnp.cos(ang), jnp.sin(ang)


def _mla_rope_tables(pos, reps):
    cos, sin = _rope_tables(pos, MLA_ROPE)
    z = jnp.zeros_like(cos)
    c = jnp.concatenate([cos, z, cos, z], axis=-1)
    s = jnp.concatenate([-sin, z, sin, z], axis=-1)
    return jnp.tile(c, (reps, 1)), jnp.tile(s, (reps, 1))


def _prep_weights(w_uq, w_ukv):
    uq = w_uq.reshape(DEPTH, MLA_Q_LORA, MLA_HEADS, MLA_NOPE + MLA_ROPE)
    uq = jnp.concatenate([uq[..., :MLA_NOPE], _pack_rope_lanes(uq[..., MLA_NOPE:])], axis=-1)
    w_uq_r = uq.reshape(DEPTH, MLA_Q_LORA, MLA_HEADS * HEAD_QK).astype(BF16)
    ukv = w_ukv.reshape(DEPTH, MLA_KV_LORA, MLA_HEADS, MLA_NOPE + MLA_V)
    w_ukv_r = jnp.concatenate([ukv[..., :MLA_NOPE].reshape(DEPTH, MLA_KV_LORA, -1),
                               ukv[..., MLA_NOPE:].reshape(DEPTH, MLA_KV_LORA, -1)], axis=-1).astype(BF16)
    return w_uq_r, w_ukv_r


def _layer_prompt(l, x, mod, tm, batch, seq, ret_tabs, mla_tabs, zero_state, log_g, W):
    proj, kr_raw = _inproj_call(x, mod, W["norm_mix"], W["w_in_t"], l, IN_TM_PROMPT, 1, IN_TM_PROMPT // tm)
    a_part, new_state = _ret_call(proj, log_g, ret_tabs[0], ret_tabs[1], zero_state, 0, batch, seq,
                                  RET_L_PROMPT)
    q_cat, ckv, kr128 = _mla_q_call(proj, kr_raw, W["q_norm"], W["kv_norm"], mla_tabs[0], mla_tabs[1],
                                    W["w_uq_r"], l, tm)
    k_cat, v = _mla_kv_call(ckv, kr128, W["w_ukv_r"], l, ckv.shape[0], tm)
    b_part = _attn_prompt_call(q_cat, k_cat, v, proj, batch, seq, ATT_TQ)
    x = _merge_call(a_part, b_part, x, mod, W["w_o"], l, tm, 1)
    x = _ffn_call(x, mod, W["norm_ffn"], W["w_ffn_in"], W["w_ffn_out"], l, tm, 1)
    return x, ckv, _unpack_rope_lanes(kr128), new_state


def _layer_sample(l, x, mod, groups, tm, batch, seq, ret_tabs, mla_tabs, state_ret, cache_ckv, cache_kr,
                  log_g, W):
    proj, kr_raw = _inproj_call(x, mod, W["norm_mix"], W["w_in_t"], l, tm, groups, 1)
    a_part, new_state = _ret_call(proj, log_g, ret_tabs[0], ret_tabs[1], state_ret, l, batch, seq, seq)
    q_lat, q_rope, ckv, kr = _mla_q_sample_call(proj, kr_raw, W["q_norm"], W["kv_norm"], mla_tabs[0],
                                                mla_tabs[1], W["w_uq_r"], W["w_ukv_r"], l)
    b_part = _attn_sample_call(q_lat, q_rope, ckv, kr, cache_ckv, cache_kr, proj, W["w_ukv_r"], l, batch, seq)
    x = _merge_call(a_part, b_part, x, mod, W["w_o"], l, tm, groups)
    x = _ffn_call(x, mod, W["norm_ffn"], W["w_ffn_in"], W["w_ffn_out"], l, tm, groups)
    return x, ckv, kr, new_state


def kernel(x_prompt, x_sample, c_prompt, c_sample, cache_mla_ckv, cache_mla_krope, state_ret, w_ada, b_ada,
           norm_mix, norm_ffn, w_in, mla_q_norm, w_uq, mla_kv_norm, w_ukv, w_o, w_ffn_in, w_ffn_out,
           norm_final):
    bp, sp, _ = x_prompt.shape
    bs, ss, _ = x_sample.shape
    tm = TOKEN_TILE
    assert sp % IN_TM_PROMPT == 0 and (bs * ss) % tm == 0 and tm % ss == 0

    w_uq_r, w_ukv_r = _prep_weights(w_uq, w_ukv)
    W = dict(w_in_t=jnp.swapaxes(w_in, 1, 2), w_uq_r=w_uq_r, w_ukv_r=w_ukv_r,
             w_o=w_o.astype(BF16), w_ffn_in=w_ffn_in.astype(BF16), w_ffn_out=w_ffn_out.astype(BF16),
             norm_mix=norm_mix.reshape(DEPTH, 1, D_MODEL), norm_ffn=norm_ffn.reshape(DEPTH, 1, D_MODEL),
             q_norm=mla_q_norm.reshape(DEPTH, 1, MLA_Q_LORA), kv_norm=mla_kv_norm.reshape(DEPTH, 1, MLA_KV_LORA))

    c_rows = -(-(bp + bs) // ADA_ROW_ALIGN) * ADA_ROW_ALIGN
    c_all = jnp.concatenate([c_prompt, c_sample, jnp.zeros((c_rows - bp - bs, D_MODEL), F32)], axis=0)
    mod_all = _ada_call(c_all, w_ada, b_ada)[:, :bp + bs]
    mod_all = mod_all.reshape(DEPTH, bp + bs, 6, D_MODEL).transpose(0, 2, 1, 3)
    tiles_per_batch = sp // tm
    mod_p = jnp.repeat(mod_all[:, :, :bp], tiles_per_batch, axis=2)[:, :, :, None, :]
    groups_s = tm // ss
    mod_s = mod_all[:, :, bp:].reshape(DEPTH, 6, (bs * ss) // tm, groups_s, D_MODEL)

    log_g = jnp.log1p(-jnp.exp2(-RET_GAMMA_EXP0 - jnp.arange(RET_HEADS, dtype=F32)))
    pos_p = jnp.arange(sp)
    pos_s = PAST_LEN + jnp.arange(ss)
    ret_tabs_p = _rope_tables(pos_p, RET_DK)
    ret_tabs_s = _rope_tables(pos_s, RET_DK)
    mla_tabs_p = _mla_rope_tables(pos_p, bp)
    mla_tabs_s = _mla_rope_tables(pos_s, bs)
    zero_state = jnp.zeros((1, bp, RET_HEADS, RET_DK, RET_DV), F32)

    xp = x_prompt.reshape(bp * sp, D_MODEL)
    xs = x_sample.reshape(bs * ss, D_MODEL)
    outs = [[] for _ in range(6)]
    for l in range(DEPTH):
        xp, ckv, kr, st = _layer_prompt(l, xp, mod_p, tm, bp, sp, ret_tabs_p, mla_tabs_p, zero_state, log_g, W)
        outs[0].append(ckv.reshape(bp, sp, MLA_KV_LORA))
        outs[1].append(kr.reshape(bp, sp, MLA_ROPE))
        outs[2].append(st)
        xs, ckv, kr, st = _layer_sample(l, xs, mod_s, groups_s, tm, bs, ss, ret_tabs_s, mla_tabs_s, state_ret,
                                        cache_mla_ckv, cache_mla_krope, log_g, W)
        outs[3].append(ckv.reshape(bs, ss, MLA_KV_LORA))
        outs[4].append(kr.reshape(bs, ss, MLA_ROPE))
        outs[5].append(st)

    y_prompt = _final_norm_call(xp, norm_final, tm).reshape(bp, sp, D_MODEL)
    y_sample = _final_norm_call(xs, norm_final, tm).reshape(bs, ss, D_MODEL)
    return (y_prompt, y_sample, jnp.stack(outs[0]), jnp.stack(outs[1]), jnp.stack(outs[2]),
            jnp.stack(outs[3]), jnp.stack(outs[4]), jnp.stack(outs[5]))
```

```python
import functools
import math

import jax
import jax.numpy as jnp
from jax import lax
from jax.experimental import pallas as pl
from jax.experimental.pallas import tpu as pltpu

D_MODEL = 2048
DEPTH = 4
PAST_LEN = 1024
CHUNK = 64
RET_HEADS = 8
RET_DK = D_MODEL // RET_HEADS
RET_DV = D_MODEL // RET_HEADS
MLA_HEADS = 16
MLA_Q_LORA = D_MODEL // 4
MLA_KV_LORA = D_MODEL // 4
MLA_NOPE = 128
MLA_ROPE = 64
MLA_V = D_MODEL // MLA_HEADS
FFN_HIDDEN = -(-8 * D_MODEL // (3 * 256)) * 256
ROPE_BASE = 10000.0
RET_GAMMA_EXP0 = 5.0
RMS_EPS = 1e-6
GN_EPS = 1e-5
NEG_INF = -1e30

F32 = jnp.float32
BF16 = jnp.bfloat16

V7X_LANES = 128
V7X_VMEM_LIMIT_CAP = 56 * 1024 * 1024

COL_RQ, COL_RK, COL_RV, COL_RG = 0, D_MODEL, 2 * D_MODEL, 3 * D_MODEL
COL_DQ = 4 * D_MODEL
COL_DKV = COL_DQ + MLA_Q_LORA
COL_GA = COL_DKV + MLA_KV_LORA
COL_GB = COL_GA + D_MODEL
MAIN_COLS = COL_GB + D_MODEL
HEAD_QK = 2 * V7X_LANES
HALF_ROPE = MLA_ROPE // 2
ADA_ROW_ALIGN = 16
TOKEN_TILE = 512
IN_TM_PROMPT = 2048
IN_TN = 512
MERGE_TN = 1024
ADA_TN = 1024
FFN_TH = 512
RET_L_PROMPT = 256
RET_HEADS_PROMPT = 2
RET_HEADS_SAMPLE = 4
ATT_TQ = 1024
ATT_TK = 512
ATT_SPLIT = 4
QK_SCALE_LOG2 = (MLA_NOPE + MLA_ROPE) ** -0.5 * math.log2(math.e)


def _cparams(sem, est_bytes):
    return pltpu.CompilerParams(dimension_semantics=sem,
                                vmem_limit_bytes=min(int(est_bytes), V7X_VMEM_LIMIT_CAP))


def _sigmoid(x):
    return jax.nn.sigmoid(x)


def _rms(x, g):
    return x * lax.rsqrt(jnp.mean(x * x, axis=-1, keepdims=True) + RMS_EPS) * g


def _norm_mod(x, g, sc, sh):
    tm, d = x.shape
    groups = sc.shape[0]
    y = _rms(x, g)
    if groups == 1:
        return y * (1.0 + sc) + sh
    y3 = y.reshape(groups, tm // groups, d)
    return (y3 * (1.0 + sc[:, None, :]) + sh[:, None, :]).reshape(tm, d)


def _gate_res(x, gate, upd):
    tm, n = x.shape
    groups = gate.shape[0]
    if groups == 1:
        return x + gate * upd
    return x + (gate[:, None, :] * upd.reshape(groups, tm // groups, n)).reshape(tm, n)


def _pack_rope_lanes(x):
    z = jnp.zeros(x.shape[:-1] + (HALF_ROPE,), x.dtype)
    return jnp.concatenate([x[..., :HALF_ROPE], z, x[..., HALF_ROPE:], z], axis=-1)


def _unpack_rope_lanes(x):
    return jnp.concatenate([x[..., :HALF_ROPE], x[..., 2 * HALF_ROPE:3 * HALF_ROPE]], axis=-1)


def _ada_kernel(c_ref, w_ref, b_ref, o_ref):
    c = c_ref[...]
    a = (c * _sigmoid(c)).astype(BF16)
    o_ref[0] = jnp.dot(a, w_ref[0].astype(BF16), preferred_element_type=F32) + b_ref[0]


def _ada_call(c_all, w_ada, b_ada):
    nb = c_all.shape[0]
    n = w_ada.shape[-1]
    tn = ADA_TN
    est = 2 * (D_MODEL * tn * 4) + D_MODEL * tn * 2 + 4 * nb * (D_MODEL + 2 * tn) * 4 + (4 << 20)
    return pl.pallas_call(
        _ada_kernel,
        grid=(DEPTH, n // tn),
        in_specs=[pl.BlockSpec((nb, D_MODEL), lambda l, j: (0, 0)),
                  pl.BlockSpec((1, D_MODEL, tn), lambda l, j: (l, 0, j)),
                  pl.BlockSpec((1, 1, tn), lambda l, j: (l, 0, j))],
        out_specs=pl.BlockSpec((1, nb, tn), lambda l, j: (l, 0, j)),
        out_shape=jax.ShapeDtypeStruct((DEPTH, nb, n), F32),
        compiler_params=_cparams(("parallel", "parallel"), est),
        name="ada_mod",
    )(c_all, w_ada, b_ada.reshape(DEPTH, 1, n))


def _inproj_kernel(x_ref, sc_ref, sh_ref, g_ref, w_ref, wn_ref, wkr_ref, o_ref, kr_ref, h_scr, *, n_main):
    j = pl.program_id(1)
    nt = (((1,), (1,)), ((), ()))

    @pl.when(j == 0)
    def _():
        wkr = wkr_ref[...]
        z = jnp.zeros((HALF_ROPE, wkr.shape[1]), wkr.dtype)
        wkr = jnp.concatenate([wkr[:HALF_ROPE], z, wkr[HALF_ROPE:], z], axis=0).astype(BF16)
        tm = x_ref.shape[0]
        for r0 in range(0, tm, TOKEN_TILE):
            rs = pl.ds(r0, TOKEN_TILE)
            hb = _norm_mod(x_ref[rs, :], g_ref[...], sc_ref[...], sh_ref[...]).astype(BF16)
            h_scr[rs, :] = hb
            kr_ref[rs, :] = lax.dot_general(hb, wkr, nt, preferred_element_type=F32)

    @pl.when(j < n_main)
    def _():
        o_ref[...] = lax.dot_general(h_scr[...], w_ref[...].astype(BF16), nt, preferred_element_type=F32)

    @pl.when(j >= n_main)
    def _():
        w = jnp.concatenate([w_ref[MLA_ROPE:, :], wn_ref[...]], axis=0).astype(BF16)
        o_ref[...] = lax.dot_general(h_scr[...], w, nt, preferred_element_type=F32)


def _inproj_call(x, mod, norm_g, w_in_t, l, tm, groups, mod_stride):
    m = x.shape[0]
    n_main = COL_GA // IN_TN
    sub = IN_TN // MLA_ROPE
    assert groups == 1 or tm == TOKEN_TILE
    est = (tm * D_MODEL * 4 + tm * D_MODEL * 2 + 2 * D_MODEL * (IN_TN + 2 * MLA_ROPE) * 4
           + D_MODEL * IN_TN * 2 + 3 * tm * IN_TN * 4 + 6 * TOKEN_TILE * D_MODEL * 4 + (4 << 20))
    mod_spec = lambda k: pl.BlockSpec((None, None, None, groups, D_MODEL),
                                      lambda i, j: (l, k, i * mod_stride, 0, 0))
    return pl.pallas_call(
        functools.partial(_inproj_kernel, n_main=n_main),
        grid=(m // tm, MAIN_COLS // IN_TN),
        in_specs=[pl.BlockSpec((tm, D_MODEL), lambda i, j: (i, 0), pipeline_mode=pl.Buffered(1)),
                  mod_spec(1), mod_spec(0),
                  pl.BlockSpec((None, 1, D_MODEL), lambda i, j: (l, 0, 0)),
                  pl.BlockSpec((None, IN_TN, D_MODEL), lambda i, j: (l, j, 0)),
                  pl.BlockSpec((None, MLA_ROPE, D_MODEL),
                               lambda i, j: (l, (jnp.maximum(j, n_main) + 1) * sub, 0)),
                  pl.BlockSpec((None, MLA_ROPE, D_MODEL), lambda i, j: (l, COL_GA // MLA_ROPE, 0))],
        out_specs=[pl.BlockSpec((tm, IN_TN), lambda i, j: (i, j)),
                   pl.BlockSpec((tm, V7X_LANES), lambda i, j: (i, 0))],
        out_shape=[jax.ShapeDtypeStruct((m, MAIN_COLS), F32),
                   jax.ShapeDtypeStruct((m, V7X_LANES), F32)],
        scratch_shapes=[pltpu.VMEM((tm, D_MODEL), BF16)],
        compiler_params=_cparams(("parallel", "arbitrary"), est),
        name="in_proj",
    )(x, mod, mod, norm_g, w_in_t, w_in_t, w_in_t)


def _ret_kernel(lg_ref, q_ref, k_ref, v_ref, rg_ref, ga_ref, cos_ref, sin_ref, s0_ref,
                a_ref, st_ref, dm_scr, *, chunk_len, heads):
    hg = pl.program_id(1)
    c = pl.program_id(2)
    L = chunk_len
    lgs = [lg_ref[hg * heads + t] for t in range(heads)]

    @pl.when(c == 0)
    def _():
        st_ref[...] = s0_ref[...]
        ri = lax.broadcasted_iota(jnp.int32, (L, L), 0)
        ci = lax.broadcasted_iota(jnp.int32, (L, L), 1)
        diff = (ri - ci).astype(F32)
        for t in range(heads):
            dm_scr[t] = jnp.where(diff >= 0, jnp.exp(jnp.maximum(diff, 0.0) * lgs[t]), 0.0)

    cos = cos_ref[...]
    sin = sin_ref[...]
    half = RET_DK // 2
    idx = lax.broadcasted_iota(jnp.int32, (L, 1), 0).astype(F32)
    nt = (((1,), (1,)), ((), ()))
    tn = (((0,), (0,)), ((), ()))

    def rope(x):
        x1, x2 = x[:, :half], x[:, half:]
        return jnp.concatenate([x1 * cos - x2 * sin, x1 * sin + x2 * cos], axis=-1)

    stage = []
    for t in range(heads):
        cs = pl.ds(t * RET_DK, RET_DK)
        q = rope(q_ref[:, cs])
        k = rope(k_ref[:, cs]) * (RET_DK ** -0.5)
        vb = v_ref[:, cs].astype(BF16)
        qb = q.astype(BF16)
        zeta = jnp.exp((L - 1.0 - idx) * lgs[t])
        g_l = jnp.exp(jnp.full((1, 1), float(L), F32) * lgs[t])
        st = st_ref[0, t]
        scores = lax.dot_general(qb, k.astype(BF16), nt, preferred_element_type=F32)
        cross = jnp.dot(qb, st.astype(BF16), preferred_element_type=F32)
        st_ref[0, t] = st * g_l + lax.dot_general((k * zeta).astype(BF16), vb, tn, preferred_element_type=F32)
        stage.append((scores, cross, vb))

    outs = []
    for t in range(heads):
        scores, cross, vb = stage[t]
        xi = jnp.exp((idx + 1.0) * lgs[t])
        outs.append(jnp.dot((scores * dm_scr[t]).astype(BF16), vb, preferred_element_type=F32) + cross * xi)

    for t in range(heads):
        cs = pl.ds(t * RET_DK, RET_DK)
        o = outs[t]
        mu = jnp.mean(o, axis=-1, keepdims=True)
        d = o - mu
        var = jnp.mean(d * d, axis=-1, keepdims=True)
        on = d * lax.rsqrt(var + GN_EPS)
        rg = rg_ref[:, cs]
        a_ref[:, cs] = _sigmoid(ga_ref[:, cs]) * ((rg * _sigmoid(rg)) * on)


def _ret_call(proj, log_g, cos, sin, state0, state_layer, batch, seq, chunk_len, heads):
    L = chunk_len
    nc = seq // L
    w = heads * RET_DK
    assert COL_GA % w == 0 and RET_HEADS % heads == 0
    col = lambda base: (lambda b, h, c: (b * nc + c, base // w + h))
    blk = lambda base: pl.BlockSpec((L, w), col(base))
    nh = RET_HEADS
    est = (2 * 6 * L * w * 4 + 4 * heads * RET_DK * RET_DV * 4 + heads * L * L * 4
           + 8 * heads * L * max(L, RET_DK) * 4 + (4 << 20))
    return pl.pallas_call(
        functools.partial(_ret_kernel, chunk_len=L, heads=heads),
        grid=(batch, nh // heads, nc),
        in_specs=[pl.BlockSpec(memory_space=pltpu.SMEM),
                  blk(COL_RQ), blk(COL_RK), blk(COL_RV), blk(COL_RG), blk(COL_GA),
                  pl.BlockSpec((L, RET_DK // 2), lambda b, h, c: (c, 0)),
                  pl.BlockSpec((L, RET_DK // 2), lambda b, h, c: (c, 0)),
                  pl.BlockSpec((None, 1, heads, RET_DK, RET_DV), lambda b, h, c: (state_layer, b, h, 0, 0))],
        out_specs=[pl.BlockSpec((L, w), lambda b, h, c: (b * nc + c, h)),
                   pl.BlockSpec((1, heads, RET_DK, RET_DV), lambda b, h, c: (b, h, 0, 0))],
        out_shape=[jax.ShapeDtypeStruct((batch * seq, D_MODEL), F32),
                   jax.ShapeDtypeStruct((batch, nh, RET_DK, RET_DV), F32)],
        scratch_shapes=[pltpu.VMEM((heads, L, L), F32)],
        compiler_params=_cparams(("parallel", "parallel", "arbitrary"), est),
        name="retention",
    )(log_g, proj, proj, proj, proj, proj, cos, sin, state0)


def _rope128(x, c, s):
    return x * c + pltpu.roll(x, V7X_LANES // 2, 1) * s


def _mla_q_kernel(dq_ref, dkv_ref, kr_ref, qn_ref, kvn_ref, c_ref, s_ref, wuq_ref,
                  q_out, ckv_out, kro_out):
    c = c_ref[...]
    s = s_ref[...]
    cq = _rms(dq_ref[...], qn_ref[...]).astype(BF16)
    for h in range(MLA_HEADS):
        lo = h * HEAD_QK
        qh = jnp.dot(cq, wuq_ref[:, lo:lo + HEAD_QK], preferred_element_type=F32) * QK_SCALE_LOG2
        q_out[:, lo:lo + MLA_NOPE] = qh[:, :MLA_NOPE].astype(BF16)
        q_out[:, lo + MLA_NOPE:lo + HEAD_QK] = _rope128(qh[:, MLA_NOPE:], c, s).astype(BF16)
    ckv_out[...] = _rms(dkv_ref[...], kvn_ref[...])
    kro_out[...] = _rope128(kr_ref[...], c, s)


def _mla_q_call(proj, kr_raw, q_norm, kv_norm, rope_c, rope_s, w_uq_r, l, tm):
    m = proj.shape[0]
    dq_blk = COL_DQ // MLA_Q_LORA
    qw = MLA_HEADS * HEAD_QK
    est = (2 * (2 * tm * MLA_Q_LORA * 4 + 3 * tm * V7X_LANES * 4) + 2 * MLA_Q_LORA * qw * 2
           + 2 * tm * qw * 2 + 2 * tm * MLA_KV_LORA * 4 + 2 * tm * V7X_LANES * 4 + 8 * tm * HEAD_QK * 4
           + (4 << 20))
    return pl.pallas_call(
        _mla_q_kernel,
        grid=(m // tm,),
        in_specs=[pl.BlockSpec((tm, MLA_Q_LORA), lambda i: (i, dq_blk)),
                  pl.BlockSpec((tm, MLA_KV_LORA), lambda i: (i, dq_blk + 1)),
                  pl.BlockSpec((tm, V7X_LANES), lambda i: (i, 0)),
                  pl.BlockSpec((None, 1, MLA_Q_LORA), lambda i: (l, 0, 0)),
                  pl.BlockSpec((None, 1, MLA_KV_LORA), lambda i: (l, 0, 0)),
                  pl.BlockSpec((tm, V7X_LANES), lambda i: (i, 0)),
                  pl.BlockSpec((tm, V7X_LANES), lambda i: (i, 0)),
                  pl.BlockSpec((None, MLA_Q_LORA, qw), lambda i: (l, 0, 0))],
        out_specs=[pl.BlockSpec((tm, qw), lambda i: (i, 0)),
                   pl.BlockSpec((tm, MLA_KV_LORA), lambda i: (i, 0)),
                   pl.BlockSpec((tm, V7X_LANES), lambda i: (i, 0))],
        out_shape=[jax.ShapeDtypeStruct((m, qw), BF16),
                   jax.ShapeDtypeStruct((m, MLA_KV_LORA), F32),
                   jax.ShapeDtypeStruct((m, V7X_LANES), F32)],
        compiler_params=_cparams(("parallel",), est),
        name="mla_latents",
    )(proj, proj, kr_raw, q_norm, kv_norm, rope_c, rope_s, w_uq_r)


def _mla_q_sample_kernel(dq_ref, dkv_ref, kr_ref, qn_ref, kvn_ref, c_ref, s_ref, wuq_ref, wukv_ref,
                         qlat_out, qr_out, ckv_out, kro_out):
    c = c_ref[...]
    s = s_ref[...]
    cq = _rms(dq_ref[...], qn_ref[...]).astype(BF16)
    for h in range(MLA_HEADS):
        lo = h * HEAD_QK
        qh = jnp.dot(cq, wuq_ref[:, lo:lo + HEAD_QK], preferred_element_type=F32) * QK_SCALE_LOG2
        w_uk = wukv_ref[:, h * MLA_NOPE:(h + 1) * MLA_NOPE]
        q_lat = lax.dot_general(qh[:, :MLA_NOPE].astype(BF16), w_uk, (((1,), (1,)), ((), ())),
                                preferred_element_type=F32)
        qlat_out[:, h * MLA_KV_LORA:(h + 1) * MLA_KV_LORA] = q_lat.astype(BF16)
        qr_out[:, h * MLA_ROPE:(h + 1) * MLA_ROPE] = _unpack_rope_lanes(
            _rope128(qh[:, MLA_NOPE:], c, s)).astype(BF16)
    ckv_out[...] = _rms(dkv_ref[...], kvn_ref[...])
    kro_out[...] = _unpack_rope_lanes(_rope128(kr_ref[...], c, s))


def _mla_q_sample_call(proj, kr_raw, q_norm, kv_norm, rope_c, rope_s, w_uq_r, w_ukv_r, l):
    m = proj.shape[0]
    dq_blk = COL_DQ // MLA_Q_LORA
    qw = MLA_HEADS * HEAD_QK
    ww = MLA_HEADS * (MLA_NOPE + MLA_V)
    lat_w = MLA_HEADS * MLA_KV_LORA
    rope_w = MLA_HEADS * MLA_ROPE
    est = (2 * (2 * m * MLA_Q_LORA * 4 + 3 * m * V7X_LANES * 4) + 2 * MLA_Q_LORA * (qw + ww) * 2
           + 2 * m * (lat_w + rope_w) * 2 + 2 * m * (MLA_KV_LORA + MLA_ROPE) * 4 + 8 * m * MLA_KV_LORA * 4
           + (4 << 20))
    full = lambda shape: pl.BlockSpec(shape, lambda i: (0,) * len(shape))
    return pl.pallas_call(
        _mla_q_sample_kernel,
        grid=(1,),
        in_specs=[pl.BlockSpec((m, MLA_Q_LORA), lambda i: (0, dq_blk)),
                  pl.BlockSpec((m, MLA_KV_LORA), lambda i: (0, dq_blk + 1)),
                  full((m, V7X_LANES)),
                  pl.BlockSpec((None, 1, MLA_Q_LORA), lambda i: (l, 0, 0)),
                  pl.BlockSpec((None, 1, MLA_KV_LORA), lambda i: (l, 0, 0)),
                  full((m, V7X_LANES)), full((m, V7X_LANES)),
                  pl.BlockSpec((None, MLA_Q_LORA, qw), lambda i: (l, 0, 0)),
                  pl.BlockSpec((None, MLA_KV_LORA, ww), lambda i: (l, 0, 0))],
        out_specs=[full((m, lat_w)), full((m, rope_w)), full((m, MLA_KV_LORA)), full((m, MLA_ROPE))],
        out_shape=[jax.ShapeDtypeStruct((m, lat_w), BF16),
                   jax.ShapeDtypeStruct((m, rope_w), BF16),
                   jax.ShapeDtypeStruct((m, MLA_KV_LORA), F32),
                   jax.ShapeDtypeStruct((m, MLA_ROPE), F32)],
        compiler_params=_cparams(("arbitrary",), est),
        name="mla_latents_sample",
    )(proj, proj, kr_raw, q_norm, kv_norm, rope_c, rope_s, w_uq_r, w_ukv_r)


def _mla_kv_kernel(ckv_ref, kr_ref, wukv_ref, kcat_out, v_out):
    cb = ckv_ref[...].astype(BF16)
    krb = kr_ref[...].astype(BF16)
    for hp in range(MLA_HEADS // 2):
        kk = jnp.dot(cb, wukv_ref[:, hp * HEAD_QK:(hp + 1) * HEAD_QK], preferred_element_type=F32)
        for t in range(2):
            lo = (2 * hp + t) * HEAD_QK
            kcat_out[:, lo:lo + MLA_NOPE] = kk[:, t * MLA_NOPE:(t + 1) * MLA_NOPE].astype(BF16)
            kcat_out[:, lo + MLA_NOPE:lo + HEAD_QK] = krb
    nv = MLA_HEADS * MLA_NOPE
    for hp in range(nv // HEAD_QK):
        lo = hp * HEAD_QK
        v_out[:, lo:lo + HEAD_QK] = jnp.dot(cb, wukv_ref[:, nv + lo:nv + lo + HEAD_QK],
                                            preferred_element_type=F32).astype(BF16)


def _mla_kv_call(ckv, kr128, w_ukv_r, l, rows, tm):
    kw = MLA_HEADS * HEAD_QK
    vw = MLA_HEADS * MLA_V
    ww = MLA_HEADS * (MLA_NOPE + MLA_V)
    est = (2 * tm * (MLA_KV_LORA + V7X_LANES) * 4 + 2 * MLA_KV_LORA * ww * 2 + 2 * tm * (kw + vw) * 2
           + 8 * tm * HEAD_QK * 4 + (4 << 20))
    return pl.pallas_call(
        _mla_kv_kernel,
        grid=(rows // tm,),
        in_specs=[pl.BlockSpec((tm, MLA_KV_LORA), lambda i: (i, 0)),
                  pl.BlockSpec((tm, V7X_LANES), lambda i: (i, 0)),
                  pl.BlockSpec((None, MLA_KV_LORA, ww), lambda i: (l, 0, 0))],
        out_specs=[pl.BlockSpec((tm, kw), lambda i: (i, 0)),
                   pl.BlockSpec((tm, vw), lambda i: (i, 0))],
        out_shape=[jax.ShapeDtypeStruct((rows, kw), BF16),
                   jax.ShapeDtypeStruct((rows, vw), BF16)],
        compiler_params=_cparams(("parallel",), est),
        name="mla_kv_expand",
    )(ckv, kr128, w_ukv_r)


def _scores(q, k_blk):
    return lax.dot_general(q, k_blk, (((1,), (1,)), ((), ())), preferred_element_type=F32)


def _softmax_tile(s, v_blk, m_prev, l_prev, acc_prev, tri):
    n_groups = s.shape[1] // V7X_LANES
    groups = [s[:, g * V7X_LANES:(g + 1) * V7X_LANES] for g in range(n_groups)]
    if tri is not None:
        n_tri = tri.shape[1] // V7X_LANES
        for t in range(n_tri):
            g = n_groups - n_tri + t
            groups[g] = jnp.where(tri[:, t * V7X_LANES:(t + 1) * V7X_LANES], groups[g], NEG_INF)
    m_new = jnp.maximum(m_prev, jnp.max(functools.reduce(jnp.maximum, groups), axis=-1, keepdims=True))
    alpha = jnp.exp2(m_prev - m_new)
    ps = [jnp.exp2(g - m_new) for g in groups]
    l_new = alpha * l_prev + functools.reduce(jnp.add, ps)
    p = jnp.concatenate([x.astype(BF16) for x in ps], axis=-1)
    acc_new = alpha * acc_prev + jnp.dot(p, v_blk, preferred_element_type=F32)
    return m_new, l_new, acc_new


def _attn_prompt_kernel(q_ref, k_ref, v_ref, gb_ref, o_ref, m_scr, l_scr, acc_scr, *, tq, tk, n_split):
    i = pl.program_id(2)
    rows = tq // n_split
    m_scr[...] = jnp.full(m_scr.shape, NEG_INF, F32)
    l_scr[...] = jnp.zeros(l_scr.shape, F32)
    acc_scr[...] = jnp.zeros(acc_scr.shape, F32)

    def scores(r, k_blk):
        return _scores(q_ref[pl.ds(r * rows, rows), :], k_blk)

    def update(r, s, v_blk, mask):
        rs = pl.ds(r * rows, rows)
        m_new, l_new, acc_new = _softmax_tile(s, v_blk, m_scr[rs, :], l_scr[rs, :], acc_scr[rs, :], mask)
        m_scr[rs, :] = m_new
        l_scr[rs, :] = l_new
        acc_scr[rs, :] = acc_new

    blocks_per_tile = tq // tk

    def body(j, carry):
        pending = []
        for d in range(blocks_per_tile):
            start = pl.multiple_of(j * tq + d * tk, tk)
            k_blk = k_ref[pl.ds(start, tk), :]
            v_blk = v_ref[pl.ds(start, tk), :]
            pending.append(([scores(r, k_blk) for r in range(n_split)], v_blk))
        for ss, v_blk in pending:
            for r in range(n_split):
                update(r, ss[r], v_blk, None)
        return carry

    lax.fori_loop(0, i, body, 0)

    tri = (lax.broadcasted_iota(jnp.int32, (rows, rows), 1) // CHUNK
           <= lax.broadcasted_iota(jnp.int32, (rows, rows), 0) // CHUNK)
    for d in range(blocks_per_tile):
        k0 = d * tk
        work = []
        for r in range(n_split):
            r0, r1 = r * rows, (r + 1) * rows
            width = min(k0 + tk, r1) - k0
            if width <= 0:
                continue
            on_diagonal = k0 + width > r0
            assert not on_diagonal or (k0 + width == r1 and width >= rows)
            start = pl.multiple_of(i * tq + k0, tk)
            work.append((r, scores(r, k_ref[pl.ds(start, width), :]), v_ref[pl.ds(start, width), :],
                         tri if on_diagonal else None))
        for r, s, v_blk, mask in work:
            update(r, s, v_blk, mask)

    l_row = jnp.sum(l_scr[...], axis=-1, keepdims=True)
    o_ref[...] = _sigmoid(gb_ref[...]) * (acc_scr[...] / l_row)


def _attn_prompt_call(q_cat, k_cat, v, proj, batch, seq, tq):
    nq = seq // tq
    gb_blk = COL_GB // MLA_V
    est = (2 * (tq * HEAD_QK * 2 + seq * HEAD_QK * 2 + seq * MLA_V * 2 + 2 * tq * MLA_V * 4)
           + 3 * tq * V7X_LANES * 4 + 6 * tq * ATT_TK * 4 + (4 << 20))
    return pl.pallas_call(
        functools.partial(_attn_prompt_kernel, tq=tq, tk=ATT_TK, n_split=ATT_SPLIT),
        grid=(batch, MLA_HEADS, nq),
        in_specs=[pl.BlockSpec((tq, HEAD_QK), lambda b, h, i: (b * nq + i, h)),
                  pl.BlockSpec((seq, HEAD_QK), lambda b, h, i: (b, h)),
                  pl.BlockSpec((seq, MLA_V), lambda b, h, i: (b, h)),
                  pl.BlockSpec((tq, MLA_V), lambda b, h, i: (b * nq + i, gb_blk + h))],
        out_specs=pl.BlockSpec((tq, MLA_V), lambda b, h, i: (b * nq + i, h)),
        out_shape=jax.ShapeDtypeStruct((batch * seq, MLA_HEADS * MLA_V), F32),
        scratch_shapes=[pltpu.VMEM((tq, V7X_LANES), F32), pltpu.VMEM((tq, V7X_LANES), F32),
                        pltpu.VMEM((tq, MLA_V), F32)],
        compiler_params=_cparams(("parallel", "parallel", "arbitrary"), est),
        name="mla_attention_prompt",
    )(q_cat, k_cat, v, proj)


def _attn_sample_kernel(qlat_ref, qr_ref, cc_ref, ckr_ref, nc_ref, nkr_ref, gb0_ref, gb1_ref, wukv_ref,
                        o_ref, *, sq):
    nt = (((1,), (1,)), ((), ()))
    q_lat = jnp.concatenate([qlat_ref[:, h * MLA_KV_LORA:(h + 1) * MLA_KV_LORA] for h in range(MLA_HEADS)],
                            axis=0)
    q_r = jnp.concatenate([qr_ref[:, h * MLA_ROPE:(h + 1) * MLA_ROPE] for h in range(MLA_HEADS)], axis=0)
    kc = cc_ref[...].astype(BF16)
    kn = nc_ref[...].astype(BF16)
    s_c = (lax.dot_general(q_lat, kc, nt, preferred_element_type=F32)
           + lax.dot_general(q_r, ckr_ref[...].astype(BF16), nt, preferred_element_type=F32))
    s_n = (lax.dot_general(q_lat, kn, nt, preferred_element_type=F32)
           + lax.dot_general(q_r, nkr_ref[...].astype(BF16), nt, preferred_element_type=F32))
    m = jnp.maximum(jnp.max(s_c, axis=-1, keepdims=True), jnp.max(s_n, axis=-1, keepdims=True))
    p_c = jnp.exp2(s_c - m)
    p_n = jnp.exp2(s_n - m)
    l_row = jnp.sum(p_c, axis=-1, keepdims=True) + jnp.sum(p_n, axis=-1, keepdims=True)
    o_lat = (jnp.dot(p_c.astype(BF16), kc, preferred_element_type=F32)
             + jnp.dot(p_n.astype(BF16), kn, preferred_element_type=F32)) / l_row
    o_lat = o_lat.astype(BF16)
    nv = MLA_HEADS * MLA_NOPE
    half = MLA_HEADS // 2
    for h in range(MLA_HEADS):
        w_uv = wukv_ref[:, nv + h * MLA_V:nv + (h + 1) * MLA_V]
        o_h = jnp.dot(o_lat[h * sq:(h + 1) * sq, :], w_uv, preferred_element_type=F32)
        gb_ref = gb0_ref if h < half else gb1_ref
        gb = gb_ref[:, (h % half) * MLA_V:(h % half + 1) * MLA_V]
        o_ref[:, h * MLA_V:(h + 1) * MLA_V] = _sigmoid(gb) * o_h


def _attn_sample_call(q_lat, q_rope, ckv_new, kr_new, cache_ckv, cache_kr, proj, w_ukv_r, l, batch, sq):
    assert (PAST_LEN + sq - 1) // CHUNK <= PAST_LEN // CHUNK
    past = cache_ckv.shape[2]
    lat_w = MLA_HEADS * MLA_KV_LORA
    rope_w = MLA_HEADS * MLA_ROPE
    ww = MLA_HEADS * (MLA_NOPE + MLA_V)
    gw = D_MODEL // 2
    gb_blk = COL_GB // gw
    rows = MLA_HEADS * sq
    est = (2 * (sq * (lat_w + rope_w) * 2 + past * (MLA_KV_LORA + V7X_LANES) * 4 + MLA_KV_LORA * ww * 2
                + 4 * sq * D_MODEL * 4) + past * (MLA_KV_LORA + V7X_LANES) * 2 + 6 * rows * past * 4
           + 4 * rows * MLA_KV_LORA * 4 + (4 << 20))
    return pl.pallas_call(
        functools.partial(_attn_sample_kernel, sq=sq),
        grid=(batch,),
        in_specs=[pl.BlockSpec((sq, lat_w), lambda b: (b, 0)),
                  pl.BlockSpec((sq, rope_w), lambda b: (b, 0)),
                  pl.BlockSpec((None, None, past, MLA_KV_LORA), lambda b: (l, b, 0, 0)),
                  pl.BlockSpec((None, None, past, MLA_ROPE), lambda b: (l, b, 0, 0)),
                  pl.BlockSpec((sq, MLA_KV_LORA), lambda b: (b, 0)),
                  pl.BlockSpec((sq, MLA_ROPE), lambda b: (b, 0)),
                  pl.BlockSpec((sq, gw), lambda b: (b, gb_blk)),
                  pl.BlockSpec((sq, gw), lambda b: (b, gb_blk + 1)),
                  pl.BlockSpec((None, MLA_KV_LORA, ww), lambda b: (l, 0, 0))],
        out_specs=pl.BlockSpec((sq, D_MODEL), lambda b: (b, 0)),
        out_shape=jax.ShapeDtypeStruct((batch * sq, D_MODEL), F32),
        compiler_params=_cparams(("parallel",), est),
        name="mla_attention_sample",
    )(q_lat, q_rope, cache_ckv, cache_kr, ckv_new, kr_new, proj, proj, w_ukv_r)


def _merge_kernel(a_ref, b_ref, x_ref, g1_ref, w_ref, o_ref, m_scr):
    @pl.when(pl.program_id(1) == 0)
    def _():
        m_scr[...] = (a_ref[...] + b_ref[...]).astype(BF16)

    mix = jnp.dot(m_scr[...], w_ref[...], preferred_element_type=F32)
    o_ref[...] = _gate_res(x_ref[...], g1_ref[...], mix)


def _merge_call(a_part, b_part, x, mod, w_o_b, l, tm, groups):
    m = x.shape[0]
    tn = MERGE_TN
    est = (2 * 2 * tm * D_MODEL * 4 + tm * D_MODEL * 2 + 2 * D_MODEL * tn * 2 + 4 * tm * tn * 4
           + 2 * tm * tn * 4 + (4 << 20))
    return pl.pallas_call(
        _merge_kernel,
        grid=(m // tm, D_MODEL // tn),
        in_specs=[pl.BlockSpec((tm, D_MODEL), lambda i, j: (i, 0)),
                  pl.BlockSpec((tm, D_MODEL), lambda i, j: (i, 0)),
                  pl.BlockSpec((tm, tn), lambda i, j: (i, j)),
                  pl.BlockSpec((None, None, None, groups, tn), lambda i, j: (l, 2, i, 0, j)),
                  pl.BlockSpec((None, D_MODEL, tn), lambda i, j: (l, 0, j))],
        out_specs=pl.BlockSpec((tm, tn), lambda i, j: (i, j)),
        out_shape=jax.ShapeDtypeStruct((m, D_MODEL), F32),
        scratch_shapes=[pltpu.VMEM((tm, D_MODEL), BF16)],
        compiler_params=_cparams(("parallel", "arbitrary"), est),
        name="merge_out_proj",
    )(a_part, b_part, x, mod, w_o_b)


def _ffn_kernel(x_ref, sc_ref, sh_ref, g2_ref, gn_ref, wg_ref, wu_ref, wo_ref, o_ref, h_scr, acc_scr,
                *, nh):
    j = pl.program_id(1)

    @pl.when(j == 0)
    def _():
        h_scr[...] = _norm_mod(x_ref[...], gn_ref[...], sc_ref[...], sh_ref[...]).astype(BF16)
        acc_scr[...] = jnp.zeros(acc_scr.shape, F32)

    hb = h_scr[...]
    gate = jnp.dot(hb, wg_ref[...], preferred_element_type=F32)
    up = jnp.dot(hb, wu_ref[...], preferred_element_type=F32)
    act = ((gate * _sigmoid(gate)) * up).astype(BF16)
    acc_scr[...] += jnp.dot(act, wo_ref[...], preferred_element_type=F32)

    @pl.when(j == nh - 1)
    def _():
        o_ref[...] = _gate_res(x_ref[...], g2_ref[...], acc_scr[...])


def _ffn_call(x, mod, norm_g, w_in_b, w_out_b, l, tm, groups):
    m = x.shape[0]
    th = FFN_TH
    nh = FFN_HIDDEN // th
    est = (4 * tm * D_MODEL * 4 + tm * D_MODEL * 2 + tm * D_MODEL * 4 + 2 * 3 * D_MODEL * th * 2
           + 4 * tm * th * 4 + tm * D_MODEL * 4 + (4 << 20))
    mod_spec = lambda k: pl.BlockSpec((None, None, None, groups, D_MODEL), lambda i, j: (l, k, i, 0, 0))
    return pl.pallas_call(
        functools.partial(_ffn_kernel, nh=nh),
        grid=(m // tm, nh),
        in_specs=[pl.BlockSpec((tm, D_MODEL), lambda i, j: (i, 0)),
                  mod_spec(4), mod_spec(3), mod_spec(5),
                  pl.BlockSpec((None, 1, D_MODEL), lambda i, j: (l, 0, 0)),
                  pl.BlockSpec((None, D_MODEL, th), lambda i, j: (l, 0, j)),
                  pl.BlockSpec((None, D_MODEL, th), lambda i, j: (l, 0, nh + j)),
                  pl.BlockSpec((None, th, D_MODEL), lambda i, j: (l, j, 0))],
        out_specs=pl.BlockSpec((tm, D_MODEL), lambda i, j: (i, 0)),
        out_shape=jax.ShapeDtypeStruct((m, D_MODEL), F32),
        scratch_shapes=[pltpu.VMEM((tm, D_MODEL), BF16), pltpu.VMEM((tm, D_MODEL), F32)],
        compiler_params=_cparams(("parallel", "arbitrary"), est),
        name="ffn_swiglu",
    )(x, mod, mod, mod, norm_g, w_in_b, w_in_b, w_out_b)


def _final_norm_kernel(x_ref, g_ref, o_ref):
    o_ref[...] = _rms(x_ref[...], g_ref[...])


def _final_norm_call(x, g, tm):
    m = x.shape[0]
    est = 4 * tm * D_MODEL * 4 + 2 * tm * D_MODEL * 4 + (4 << 20)
    return pl.pallas_call(
        _final_norm_kernel,
        grid=(m // tm,),
        in_specs=[pl.BlockSpec((tm, D_MODEL), lambda i: (i, 0)),
                  pl.BlockSpec((1, D_MODEL), lambda i: (0, 0))],
        out_specs=pl.BlockSpec((tm, D_MODEL), lambda i: (i, 0)),
        out_shape=jax.ShapeDtypeStruct((m, D_MODEL), F32),
        compiler_params=_cparams(("parallel",), est),
        name="final_norm",
    )(x, g.reshape(1, D_MODEL))


def _rope_tables(pos, dim):
    inv = jnp.exp(-math.log(ROPE_BASE) * jnp.arange(0, dim, 2, dtype=F32) / dim)
    ang = pos.astype(F32)[:, None] * inv[None, :]
    return jnp.cos(ang), jnp.sin(ang)


def _mla_rope_tables(pos, reps):
    cos, sin = _rope_tables(pos, MLA_ROPE)
    z = jnp.zeros_like(cos)
    c = jnp.concatenate([cos, z, cos, z], axis=-1)
    s = jnp.concatenate([-sin, z, sin, z], axis=-1)
    return jnp.tile(c, (reps, 1)), jnp.tile(s, (reps, 1))


def _prep_weights(w_uq, w_ukv):
    uq = w_uq.reshape(DEPTH, MLA_Q_LORA, MLA_HEADS, MLA_NOPE + MLA_ROPE)
    uq = jnp.concatenate([uq[..., :MLA_NOPE], _pack_rope_lanes(uq[..., MLA_NOPE:])], axis=-1)
    w_uq_r = uq.reshape(DEPTH, MLA_Q_LORA, MLA_HEADS * HEAD_QK).astype(BF16)
    ukv = w_ukv.reshape(DEPTH, MLA_KV_LORA, MLA_HEADS, MLA_NOPE + MLA_V)
    w_ukv_r = jnp.concatenate([ukv[..., :MLA_NOPE].reshape(DEPTH, MLA_KV_LORA, -1),
                               ukv[..., MLA_NOPE:].reshape(DEPTH, MLA_KV_LORA, -1)], axis=-1).astype(BF16)
    return w_uq_r, w_ukv_r


def _layer_prompt(l, x, mod, tm, batch, seq, ret_tabs, mla_tabs, zero_state, log_g, W):
    proj, kr_raw = _inproj_call(x, mod, W["norm_mix"], W["w_in_t"], l, IN_TM_PROMPT, 1, IN_TM_PROMPT // tm)
    a_part, new_state = _ret_call(proj, log_g, ret_tabs[0], ret_tabs[1], zero_state, 0, batch, seq,
                                  RET_L_PROMPT, RET_HEADS_PROMPT)
    q_cat, ckv, kr128 = _mla_q_call(proj, kr_raw, W["q_norm"], W["kv_norm"], mla_tabs[0], mla_tabs[1],
                                    W["w_uq_r"], l, tm)
    k_cat, v = _mla_kv_call(ckv, kr128, W["w_ukv_r"], l, ckv.shape[0], tm)
    b_part = _attn_prompt_call(q_cat, k_cat, v, proj, batch, seq, ATT_TQ)
    x = _merge_call(a_part, b_part, x, mod, W["w_o"], l, tm, 1)
    x = _ffn_call(x, mod, W["norm_ffn"], W["w_ffn_in"], W["w_ffn_out"], l, tm, 1)
    return x, ckv, _unpack_rope_lanes(kr128), new_state


def _layer_sample(l, x, mod, groups, tm, batch, seq, ret_tabs, mla_tabs, state_ret, cache_ckv, cache_kr,
                  log_g, W):
    proj, kr_raw = _inproj_call(x, mod, W["norm_mix"], W["w_in_t"], l, tm, groups, 1)
    a_part, new_state = _ret_call(proj, log_g, ret_tabs[0], ret_tabs[1], state_ret, l, batch, seq, seq,
                                  RET_HEADS_SAMPLE)
    q_lat, q_rope, ckv, kr = _mla_q_sample_call(proj, kr_raw, W["q_norm"], W["kv_norm"], mla_tabs[0],
                                                mla_tabs[1], W["w_uq_r"], W["w_ukv_r"], l)
    b_part = _attn_sample_call(q_lat, q_rope, ckv, kr, cache_ckv, cache_kr, proj, W["w_ukv_r"], l, batch, seq)
    x = _merge_call(a_part, b_part, x, mod, W["w_o"], l, tm, groups)
    x = _ffn_call(x, mod, W["norm_ffn"], W["w_ffn_in"], W["w_ffn_out"], l, tm, groups)
    return x, ckv, kr, new_state


def kernel(x_prompt, x_sample, c_prompt, c_sample, cache_mla_ckv, cache_mla_krope, state_ret, w_ada, b_ada,
           norm_mix, norm_ffn, w_in, mla_q_norm, w_uq, mla_kv_norm, w_ukv, w_o, w_ffn_in, w_ffn_out,
           norm_final):
    bp, sp, _ = x_prompt.shape
    bs, ss, _ = x_sample.shape
    tm = TOKEN_TILE
    assert sp % IN_TM_PROMPT == 0 and (bs * ss) % tm == 0 and tm % ss == 0

    w_uq_r, w_ukv_r = _prep_weights(w_uq, w_ukv)
    W = dict(w_in_t=jnp.swapaxes(w_in, 1, 2), w_uq_r=w_uq_r, w_ukv_r=w_ukv_r,
             w_o=w_o.astype(BF16), w_ffn_in=w_ffn_in.astype(BF16), w_ffn_out=w_ffn_out.astype(BF16),
             norm_mix=norm_mix.reshape(DEPTH, 1, D_MODEL), norm_ffn=norm_ffn.reshape(DEPTH, 1, D_MODEL),
             q_norm=mla_q_norm.reshape(DEPTH, 1, MLA_Q_LORA), kv_norm=mla_kv_norm.reshape(DEPTH, 1, MLA_KV_LORA))

    c_rows = -(-(bp + bs) // ADA_ROW_ALIGN) * ADA_ROW_ALIGN
    c_all = jnp.concatenate([c_prompt, c_sample, jnp.zeros((c_rows - bp - bs, D_MODEL), F32)], axis=0)
    mod_all = _ada_call(c_all, w_ada, b_ada)[:, :bp + bs]
    mod_all = mod_all.reshape(DEPTH, bp + bs, 6, D_MODEL).transpose(0, 2, 1, 3)
    tiles_per_batch = sp // tm
    mod_p = jnp.repeat(mod_all[:, :, :bp], tiles_per_batch, axis=2)[:, :, :, None, :]
    groups_s = tm // ss
    mod_s = mod_all[:, :, bp:].reshape(DEPTH, 6, (bs * ss) // tm, groups_s, D_MODEL)

    log_g = jnp.log1p(-jnp.exp2(-RET_GAMMA_EXP0 - jnp.arange(RET_HEADS, dtype=F32)))
    pos_p = jnp.arange(sp)
    pos_s = PAST_LEN + jnp.arange(ss)
    ret_tabs_p = _rope_tables(pos_p, RET_DK)
    ret_tabs_s = _rope_tables(pos_s, RET_DK)
    mla_tabs_p = _mla_rope_tables(pos_p, bp)
    mla_tabs_s = _mla_rope_tables(pos_s, bs)
    zero_state = jnp.zeros((1, bp, RET_HEADS, RET_DK, RET_DV), F32)

    xp = x_prompt.reshape(bp * sp, D_MODEL)
    xs = x_sample.reshape(bs * ss, D_MODEL)
    outs = [[] for _ in range(6)]
    for l in range(DEPTH):
        xp, ckv, kr, st = _layer_prompt(l, xp, mod_p, tm, bp, sp, ret_tabs_p, mla_tabs_p, zero_state, log_g, W)
        outs[0].append(ckv.reshape(bp, sp, MLA_KV_LORA))
        outs[1].append(kr.reshape(bp, sp, MLA_ROPE))
        outs[2].append(st)
        xs, ckv, kr, st = _layer_sample(l, xs, mod_s, groups_s, tm, bs, ss, ret_tabs_s, mla_tabs_s, state_ret,
                                        cache_mla_ckv, cache_mla_krope, log_g, W)
        outs[3].append(ckv.reshape(bs, ss, MLA_KV_LORA))
        outs[4].append(kr.reshape(bs, ss, MLA_ROPE))
        outs[5].append(st)

    y_prompt = _final_norm_call(xp, norm_final, tm).reshape(bp, sp, D_MODEL)
    y_sample = _final_norm_call(xs, norm_final, tm).reshape(bs, ss, D_MODEL)
    return (y_prompt, y_sample, jnp.stack(outs[0]), jnp.stack(outs[1]), jnp.stack(outs[2]),
            jnp.stack(outs[3]), jnp.stack(outs[4]), jnp.stack(outs[5]))
```

```python
import functools
import math

import jax
import jax.numpy as jnp
from jax import lax
from jax.experimental import pallas as pl
from jax.experimental.pallas import tpu as pltpu

D_MODEL = 2048
DEPTH = 4
PAST_LEN = 1024
CHUNK = 64
RET_HEADS = 8
RET_DK = D_MODEL // RET_HEADS
RET_DV = D_MODEL // RET_HEADS
MLA_HEADS = 16
MLA_Q_LORA = D_MODEL // 4
MLA_KV_LORA = D_MODEL // 4
MLA_NOPE = 128
MLA_ROPE = 64
MLA_V = D_MODEL // MLA_HEADS
FFN_HIDDEN = -(-8 * D_MODEL // (3 * 256)) * 256
ROPE_BASE = 10000.0
RET_GAMMA_EXP0 = 5.0
RMS_EPS = 1e-6
GN_EPS = 1e-5
NEG_INF = -1e30

F32 = jnp.float32
BF16 = jnp.bfloat16

V7X_LANES = 128
V7X_VMEM_LIMIT_CAP = 56 * 1024 * 1024

COL_RQ, COL_RK, COL_RV, COL_RG = 0, D_MODEL, 2 * D_MODEL, 3 * D_MODEL
COL_DQ = 4 * D_MODEL
COL_DKV = COL_DQ + MLA_Q_LORA
COL_GA = COL_DKV + MLA_KV_LORA
COL_GB = COL_GA + D_MODEL
MAIN_COLS = COL_GB + D_MODEL
HEAD_QK = 2 * V7X_LANES
HALF_ROPE = MLA_ROPE // 2
ADA_ROW_ALIGN = 16
TOKEN_TILE = 512
IN_TM_PROMPT = 2048
IN_TN = 512
ADA_TN = 1024
FFN_TH = 512
RET_L_PROMPT = 256
RET_HEADS_PROMPT = 2
RET_HEADS_SAMPLE = 4
ATT_TQ = 1024
ATT_TK = 512
ATT_SPLIT = 4
QK_SCALE_LOG2 = (MLA_NOPE + MLA_ROPE) ** -0.5 * math.log2(math.e)


def _cparams(sem, est_bytes):
    return pltpu.CompilerParams(dimension_semantics=sem,
                                vmem_limit_bytes=min(int(est_bytes), V7X_VMEM_LIMIT_CAP))


def _sigmoid(x):
    return jax.nn.sigmoid(x)


def _rms(x, g):
    return x * lax.rsqrt(jnp.mean(x * x, axis=-1, keepdims=True) + RMS_EPS) * g


def _norm_mod(x, g, sc, sh):
    tm, d = x.shape
    groups = sc.shape[0]
    y = _rms(x, g)
    if groups == 1:
        return y * (1.0 + sc) + sh
    y3 = y.reshape(groups, tm // groups, d)
    return (y3 * (1.0 + sc[:, None, :]) + sh[:, None, :]).reshape(tm, d)


def _gate_res(x, gate, upd):
    tm, n = x.shape
    groups = gate.shape[0]
    if groups == 1:
        return x + gate * upd
    return x + (gate[:, None, :] * upd.reshape(groups, tm // groups, n)).reshape(tm, n)


def _pack_rope_lanes(x):
    z = jnp.zeros(x.shape[:-1] + (HALF_ROPE,), x.dtype)
    return jnp.concatenate([x[..., :HALF_ROPE], z, x[..., HALF_ROPE:], z], axis=-1)


def _unpack_rope_lanes(x):
    return jnp.concatenate([x[..., :HALF_ROPE], x[..., 2 * HALF_ROPE:3 * HALF_ROPE]], axis=-1)


def _ada_kernel(c_ref, w_ref, b_ref, o_ref):
    c = c_ref[...]
    a = (c * _sigmoid(c)).astype(BF16)
    o_ref[0] = jnp.dot(a, w_ref[0].astype(BF16), preferred_element_type=F32) + b_ref[0]


def _ada_call(c_all, w_ada, b_ada):
    nb = c_all.shape[0]
    n = w_ada.shape[-1]
    tn = ADA_TN
    est = 2 * (D_MODEL * tn * 4) + D_MODEL * tn * 2 + 4 * nb * (D_MODEL + 2 * tn) * 4 + (4 << 20)
    return pl.pallas_call(
        _ada_kernel,
        grid=(DEPTH, n // tn),
        in_specs=[pl.BlockSpec((nb, D_MODEL), lambda l, j: (0, 0)),
                  pl.BlockSpec((1, D_MODEL, tn), lambda l, j: (l, 0, j)),
                  pl.BlockSpec((1, 1, tn), lambda l, j: (l, 0, j))],
        out_specs=pl.BlockSpec((1, nb, tn), lambda l, j: (l, 0, j)),
        out_shape=jax.ShapeDtypeStruct((DEPTH, nb, n), F32),
        compiler_params=_cparams(("parallel", "parallel"), est),
        name="ada_mod",
    )(c_all, w_ada, b_ada.reshape(DEPTH, 1, n))


def _inproj_kernel(x_ref, sc_ref, sh_ref, g_ref, w_ref, wn_ref, wkr_ref, o_ref, kr_ref, h_scr, *, n_main):
    j = pl.program_id(1)
    nt = (((1,), (1,)), ((), ()))

    @pl.when(j == 0)
    def _():
        wkr = wkr_ref[...]
        z = jnp.zeros((HALF_ROPE, wkr.shape[1]), wkr.dtype)
        wkr = jnp.concatenate([wkr[:HALF_ROPE], z, wkr[HALF_ROPE:], z], axis=0).astype(BF16)
        tm = x_ref.shape[0]
        for r0 in range(0, tm, TOKEN_TILE):
            rs = pl.ds(r0, TOKEN_TILE)
            hb = _norm_mod(x_ref[rs, :], g_ref[...], sc_ref[...], sh_ref[...]).astype(BF16)
            h_scr[rs, :] = hb
            kr_ref[rs, :] = lax.dot_general(hb, wkr, nt, preferred_element_type=F32)

    @pl.when(j < n_main)
    def _():
        o_ref[...] = lax.dot_general(h_scr[...], w_ref[...].astype(BF16), nt,
                                     preferred_element_type=F32).astype(BF16)

    @pl.when(j >= n_main)
    def _():
        w = jnp.concatenate([w_ref[MLA_ROPE:, :], wn_ref[...]], axis=0).astype(BF16)
        o_ref[...] = lax.dot_general(h_scr[...], w, nt, preferred_element_type=F32).astype(BF16)


def _inproj_call(x, mod, norm_g, w_in_t, l, tm, groups, mod_stride):
    m = x.shape[0]
    n_main = COL_GA // IN_TN
    sub = IN_TN // MLA_ROPE
    assert groups == 1 or tm == TOKEN_TILE
    est = (tm * D_MODEL * 4 + tm * D_MODEL * 2 + 2 * D_MODEL * (IN_TN + 2 * MLA_ROPE) * 4
           + D_MODEL * IN_TN * 2 + 3 * tm * IN_TN * 4 + 6 * TOKEN_TILE * D_MODEL * 4 + (4 << 20))
    mod_spec = lambda k: pl.BlockSpec((None, None, None, groups, D_MODEL),
                                      lambda i, j: (l, k, i * mod_stride, 0, 0))
    return pl.pallas_call(
        functools.partial(_inproj_kernel, n_main=n_main),
        grid=(m // tm, MAIN_COLS // IN_TN),
        in_specs=[pl.BlockSpec((tm, D_MODEL), lambda i, j: (i, 0), pipeline_mode=pl.Buffered(1)),
                  mod_spec(1), mod_spec(0),
                  pl.BlockSpec((None, 1, D_MODEL), lambda i, j: (l, 0, 0)),
                  pl.BlockSpec((None, IN_TN, D_MODEL), lambda i, j: (l, j, 0)),
                  pl.BlockSpec((None, MLA_ROPE, D_MODEL),
                               lambda i, j: (l, (jnp.maximum(j, n_main) + 1) * sub, 0)),
                  pl.BlockSpec((None, MLA_ROPE, D_MODEL), lambda i, j: (l, COL_GA // MLA_ROPE, 0))],
        out_specs=[pl.BlockSpec((tm, IN_TN), lambda i, j: (i, j)),
                   pl.BlockSpec((tm, V7X_LANES), lambda i, j: (i, 0))],
        out_shape=[jax.ShapeDtypeStruct((m, MAIN_COLS), BF16),
                   jax.ShapeDtypeStruct((m, V7X_LANES), F32)],
        scratch_shapes=[pltpu.VMEM((tm, D_MODEL), BF16)],
        compiler_params=_cparams(("parallel", "arbitrary"), est),
        name="in_proj",
    )(x, mod, mod, norm_g, w_in_t, w_in_t, w_in_t)


def _ret_kernel(lg_ref, q_ref, k_ref, v_ref, rg_ref, ga_ref, cos_ref, sin_ref, s0_ref,
                a_ref, st_ref, dm_scr, *, chunk_len, heads):
    hg = pl.program_id(1)
    c = pl.program_id(2)
    L = chunk_len
    lgs = [lg_ref[hg * heads + t] for t in range(heads)]

    @pl.when(c == 0)
    def _():
        st_ref[...] = s0_ref[...]
        ri = lax.broadcasted_iota(jnp.int32, (L, L), 0)
        ci = lax.broadcasted_iota(jnp.int32, (L, L), 1)
        diff = (ri - ci).astype(F32)
        for t in range(heads):
            dm_scr[t] = jnp.where(diff >= 0, jnp.exp(jnp.maximum(diff, 0.0) * lgs[t]), 0.0)

    cos = cos_ref[...]
    sin = sin_ref[...]
    half = RET_DK // 2
    idx = lax.broadcasted_iota(jnp.int32, (L, 1), 0).astype(F32)
    nt = (((1,), (1,)), ((), ()))
    tn = (((0,), (0,)), ((), ()))

    def rope(x):
        x1, x2 = x[:, :half], x[:, half:]
        return jnp.concatenate([x1 * cos - x2 * sin, x1 * sin + x2 * cos], axis=-1)

    stage = []
    for t in range(heads):
        cs = pl.ds(t * RET_DK, RET_DK)
        q = rope(q_ref[:, cs].astype(F32))
        k = rope(k_ref[:, cs].astype(F32)) * (RET_DK ** -0.5)
        vb = v_ref[:, cs]
        qb = q.astype(BF16)
        zeta = jnp.exp((L - 1.0 - idx) * lgs[t])
        g_l = jnp.exp(jnp.full((1, 1), float(L), F32) * lgs[t])
        st = st_ref[0, t]
        scores = lax.dot_general(qb, k.astype(BF16), nt, preferred_element_type=F32)
        cross = jnp.dot(qb, st.astype(BF16), preferred_element_type=F32)
        st_ref[0, t] = st * g_l + lax.dot_general((k * zeta).astype(BF16), vb, tn, preferred_element_type=F32)
        stage.append((scores, cross, vb))

    outs = []
    for t in range(heads):
        scores, cross, vb = stage[t]
        xi = jnp.exp((idx + 1.0) * lgs[t])
        outs.append(jnp.dot((scores * dm_scr[t]).astype(BF16), vb, preferred_element_type=F32) + cross * xi)

    for t in range(heads):
        cs = pl.ds(t * RET_DK, RET_DK)
        o = outs[t]
        mu = jnp.mean(o, axis=-1, keepdims=True)
        d = o - mu
        var = jnp.mean(d * d, axis=-1, keepdims=True)
        on = d * lax.rsqrt(var + GN_EPS)
        rg = rg_ref[:, cs].astype(F32)
        a_ref[:, cs] = (_sigmoid(ga_ref[:, cs].astype(F32)) * ((rg * _sigmoid(rg)) * on)).astype(BF16)


def _ret_call(proj, log_g, cos, sin, state0, state_layer, batch, seq, chunk_len, heads):
    L = chunk_len
    nc = seq // L
    w = heads * RET_DK
    assert COL_GA % w == 0 and RET_HEADS % heads == 0
    col = lambda base: (lambda b, h, c: (b * nc + c, base // w + h))
    blk = lambda base: pl.BlockSpec((L, w), col(base))
    nh = RET_HEADS
    est = (2 * 6 * L * w * 4 + 4 * heads * RET_DK * RET_DV * 4 + heads * L * L * 4
           + 8 * heads * L * max(L, RET_DK) * 4 + (4 << 20))
    return pl.pallas_call(
        functools.partial(_ret_kernel, chunk_len=L, heads=heads),
        grid=(batch, nh // heads, nc),
        in_specs=[pl.BlockSpec(memory_space=pltpu.SMEM),
                  blk(COL_RQ), blk(COL_RK), blk(COL_RV), blk(COL_RG), blk(COL_GA),
                  pl.BlockSpec((L, RET_DK // 2), lambda b, h, c: (c, 0)),
                  pl.BlockSpec((L, RET_DK // 2), lambda b, h, c: (c, 0)),
                  pl.BlockSpec((None, 1, heads, RET_DK, RET_DV), lambda b, h, c: (state_layer, b, h, 0, 0))],
        out_specs=[pl.BlockSpec((L, w), lambda b, h, c: (b * nc + c, h)),
                   pl.BlockSpec((1, heads, RET_DK, RET_DV), lambda b, h, c: (b, h, 0, 0))],
        out_shape=[jax.ShapeDtypeStruct((batch * seq, D_MODEL), BF16),
                   jax.ShapeDtypeStruct((batch, nh, RET_DK, RET_DV), F32)],
        scratch_shapes=[pltpu.VMEM((heads, L, L), F32)],
        compiler_params=_cparams(("parallel", "parallel", "arbitrary"), est),
        name="retention",
    )(log_g, proj, proj, proj, proj, proj, cos, sin, state0)


def _rope128(x, c, s):
    return x * c + pltpu.roll(x, V7X_LANES // 2, 1) * s


def _mla_q_kernel(dq_ref, dkv_ref, kr_ref, qn_ref, kvn_ref, c_ref, s_ref, wuq_ref,
                  q_out, ckv_out, kro_out):
    c = c_ref[...]
    s = s_ref[...]
    cq = _rms(dq_ref[...].astype(F32), qn_ref[...]).astype(BF16)
    for h in range(MLA_HEADS):
        lo = h * HEAD_QK
        qh = jnp.dot(cq, wuq_ref[:, lo:lo + HEAD_QK], preferred_element_type=F32) * QK_SCALE_LOG2
        q_out[:, lo:lo + MLA_NOPE] = qh[:, :MLA_NOPE].astype(BF16)
        q_out[:, lo + MLA_NOPE:lo + HEAD_QK] = _rope128(qh[:, MLA_NOPE:], c, s).astype(BF16)
    ckv_out[...] = _rms(dkv_ref[...].astype(F32), kvn_ref[...])
    kro_out[...] = _rope128(kr_ref[...], c, s)


def _mla_q_call(proj, kr_raw, q_norm, kv_norm, rope_c, rope_s, w_uq_r, l, tm):
    m = proj.shape[0]
    dq_blk = COL_DQ // MLA_Q_LORA
    qw = MLA_HEADS * HEAD_QK
    est = (2 * (2 * tm * MLA_Q_LORA * 4 + 3 * tm * V7X_LANES * 4) + 2 * MLA_Q_LORA * qw * 2
           + 2 * tm * qw * 2 + 2 * tm * MLA_KV_LORA * 4 + 2 * tm * V7X_LANES * 4 + 8 * tm * HEAD_QK * 4
           + (4 << 20))
    return pl.pallas_call(
        _mla_q_kernel,
        grid=(m // tm,),
        in_specs=[pl.BlockSpec((tm, MLA_Q_LORA), lambda i: (i, dq_blk)),
                  pl.BlockSpec((tm, MLA_KV_LORA), lambda i: (i, dq_blk + 1)),
                  pl.BlockSpec((tm, V7X_LANES), lambda i: (i, 0)),
                  pl.BlockSpec((None, 1, MLA_Q_LORA), lambda i: (l, 0, 0)),
                  pl.BlockSpec((None, 1, MLA_KV_LORA), lambda i: (l, 0, 0)),
                  pl.BlockSpec((tm, V7X_LANES), lambda i: (i, 0)),
                  pl.BlockSpec((tm, V7X_LANES), lambda i: (i, 0)),
                  pl.BlockSpec((None, MLA_Q_LORA, qw), lambda i: (l, 0, 0))],
        out_specs=[pl.BlockSpec((tm, qw), lambda i: (i, 0)),
                   pl.BlockSpec((tm, MLA_KV_LORA), lambda i: (i, 0)),
                   pl.BlockSpec((tm, V7X_LANES), lambda i: (i, 0))],
        out_shape=[jax.ShapeDtypeStruct((m, qw), BF16),
                   jax.ShapeDtypeStruct((m, MLA_KV_LORA), F32),
                   jax.ShapeDtypeStruct((m, V7X_LANES), F32)],
        compiler_params=_cparams(("parallel",), est),
        name="mla_latents",
    )(proj, proj, kr_raw, q_norm, kv_norm, rope_c, rope_s, w_uq_r)


def _mla_q_sample_kernel(dq_ref, dkv_ref, kr_ref, qn_ref, kvn_ref, c_ref, s_ref, wuq_ref, wukv_ref,
                         qlat_out, qr_out, ckv_out, kro_out):
    c = c_ref[...]
    s = s_ref[...]
    cq = _rms(dq_ref[...].astype(F32), qn_ref[...]).astype(BF16)
    for h in range(MLA_HEADS):
        lo = h * HEAD_QK
        qh = jnp.dot(cq, wuq_ref[:, lo:lo + HEAD_QK], preferred_element_type=F32) * QK_SCALE_LOG2
        w_uk = wukv_ref[:, h * MLA_NOPE:(h + 1) * MLA_NOPE]
        q_lat = lax.dot_general(qh[:, :MLA_NOPE].astype(BF16), w_uk, (((1,), (1,)), ((), ())),
                                preferred_element_type=F32)
        qlat_out[:, h * MLA_KV_LORA:(h + 1) * MLA_KV_LORA] = q_lat.astype(BF16)
        qr_out[:, h * MLA_ROPE:(h + 1) * MLA_ROPE] = _unpack_rope_lanes(
            _rope128(qh[:, MLA_NOPE:], c, s)).astype(BF16)
    ckv_out[...] = _rms(dkv_ref[...].astype(F32), kvn_ref[...])
    kro_out[...] = _unpack_rope_lanes(_rope128(kr_ref[...], c, s))


def _mla_q_sample_call(proj, kr_raw, q_norm, kv_norm, rope_c, rope_s, w_uq_r, w_ukv_r, l):
    m = proj.shape[0]
    dq_blk = COL_DQ // MLA_Q_LORA
    qw = MLA_HEADS * HEAD_QK
    ww = MLA_HEADS * (MLA_NOPE + MLA_V)
    lat_w = MLA_HEADS * MLA_KV_LORA
    rope_w = MLA_HEADS * MLA_ROPE
    est = (2 * (2 * m * MLA_Q_LORA * 4 + 3 * m * V7X_LANES * 4) + 2 * MLA_Q_LORA * (qw + ww) * 2
           + 2 * m * (lat_w + rope_w) * 2 + 2 * m * (MLA_KV_LORA + MLA_ROPE) * 4 + 8 * m * MLA_KV_LORA * 4
           + (4 << 20))
    full = lambda shape: pl.BlockSpec(shape, lambda i: (0,) * len(shape))
    return pl.pallas_call(
        _mla_q_sample_kernel,
        grid=(1,),
        in_specs=[pl.BlockSpec((m, MLA_Q_LORA), lambda i: (0, dq_blk)),
                  pl.BlockSpec((m, MLA_KV_LORA), lambda i: (0, dq_blk + 1)),
                  full((m, V7X_LANES)),
                  pl.BlockSpec((None, 1, MLA_Q_LORA), lambda i: (l, 0, 0)),
                  pl.BlockSpec((None, 1, MLA_KV_LORA), lambda i: (l, 0, 0)),
                  full((m, V7X_LANES)), full((m, V7X_LANES)),
                  pl.BlockSpec((None, MLA_Q_LORA, qw), lambda i: (l, 0, 0)),
                  pl.BlockSpec((None, MLA_KV_LORA, ww), lambda i: (l, 0, 0))],
        out_specs=[full((m, lat_w)), full((m, rope_w)), full((m, MLA_KV_LORA)), full((m, MLA_ROPE))],
        out_shape=[jax.ShapeDtypeStruct((m, lat_w), BF16),
                   jax.ShapeDtypeStruct((m, rope_w), BF16),
                   jax.ShapeDtypeStruct((m, MLA_KV_LORA), F32),
                   jax.ShapeDtypeStruct((m, MLA_ROPE), F32)],
        compiler_params=_cparams(("arbitrary",), est),
        name="mla_latents_sample",
    )(proj, proj, kr_raw, q_norm, kv_norm, rope_c, rope_s, w_uq_r, w_ukv_r)


def _mla_kv_kernel(ckv_ref, kr_ref, wukv_ref, kcat_out, v_out):
    cb = ckv_ref[...].astype(BF16)
    krb = kr_ref[...].astype(BF16)
    for hp in range(MLA_HEADS // 2):
        kk = jnp.dot(cb, wukv_ref[:, hp * HEAD_QK:(hp + 1) * HEAD_QK], preferred_element_type=F32)
        for t in range(2):
            lo = (2 * hp + t) * HEAD_QK
            kcat_out[:, lo:lo + MLA_NOPE] = kk[:, t * MLA_NOPE:(t + 1) * MLA_NOPE].astype(BF16)
            kcat_out[:, lo + MLA_NOPE:lo + HEAD_QK] = krb
    nv = MLA_HEADS * MLA_NOPE
    for hp in range(nv // HEAD_QK):
        lo = hp * HEAD_QK
        v_out[:, lo:lo + HEAD_QK] = jnp.dot(cb, wukv_ref[:, nv + lo:nv + lo + HEAD_QK],
                                            preferred_element_type=F32).astype(BF16)


def _mla_kv_call(ckv, kr128, w_ukv_r, l, rows, tm):
    kw = MLA_HEADS * HEAD_QK
    vw = MLA_HEADS * MLA_V
    ww = MLA_HEADS * (MLA_NOPE + MLA_V)
    est = (2 * tm * (MLA_KV_LORA + V7X_LANES) * 4 + 2 * MLA_KV_LORA * ww * 2 + 2 * tm * (kw + vw) * 2
           + 8 * tm * HEAD_QK * 4 + (4 << 20))
    return pl.pallas_call(
        _mla_kv_kernel,
        grid=(rows // tm,),
        in_specs=[pl.BlockSpec((tm, MLA_KV_LORA), lambda i: (i, 0)),
                  pl.BlockSpec((tm, V7X_LANES), lambda i: (i, 0)),
                  pl.BlockSpec((None, MLA_KV_LORA, ww), lambda i: (l, 0, 0))],
        out_specs=[pl.BlockSpec((tm, kw), lambda i: (i, 0)),
                   pl.BlockSpec((tm, vw), lambda i: (i, 0))],
        out_shape=[jax.ShapeDtypeStruct((rows, kw), BF16),
                   jax.ShapeDtypeStruct((rows, vw), BF16)],
        compiler_params=_cparams(("parallel",), est),
        name="mla_kv_expand",
    )(ckv, kr128, w_ukv_r)


def _scores(q, k_blk):
    return lax.dot_general(q, k_blk, (((1,), (1,)), ((), ())), preferred_element_type=F32)


def _softmax_tile(s, v_blk, m_prev, l_prev, acc_prev, tri):
    n_groups = s.shape[1] // V7X_LANES
    groups = [s[:, g * V7X_LANES:(g + 1) * V7X_LANES] for g in range(n_groups)]
    if tri is not None:
        n_tri = tri.shape[1] // V7X_LANES
        for t in range(n_tri):
            g = n_groups - n_tri + t
            groups[g] = jnp.where(tri[:, t * V7X_LANES:(t + 1) * V7X_LANES], groups[g], NEG_INF)
    m_new = jnp.maximum(m_prev, jnp.max(functools.reduce(jnp.maximum, groups), axis=-1, keepdims=True))
    alpha = jnp.exp2(m_prev - m_new)
    ps = [jnp.exp2(g - m_new) for g in groups]
    l_new = alpha * l_prev + functools.reduce(jnp.add, ps)
    p = jnp.concatenate([x.astype(BF16) for x in ps], axis=-1)
    acc_new = alpha * acc_prev + jnp.dot(p, v_blk, preferred_element_type=F32)
    return m_new, l_new, acc_new


def _attn_prompt_kernel(q_ref, k_ref, v_ref, gb_ref, o_ref, m_scr, l_scr, acc_scr, *, tq, tk, n_split):
    i = pl.program_id(2)
    rows = tq // n_split
    m_scr[...] = jnp.full(m_scr.shape, NEG_INF, F32)
    l_scr[...] = jnp.zeros(l_scr.shape, F32)
    acc_scr[...] = jnp.zeros(acc_scr.shape, F32)

    def scores(r, k_blk):
        return _scores(q_ref[pl.ds(r * rows, rows), :], k_blk)

    def update(r, s, v_blk, mask):
        rs = pl.ds(r * rows, rows)
        m_new, l_new, acc_new = _softmax_tile(s, v_blk, m_scr[rs, :], l_scr[rs, :], acc_scr[rs, :], mask)
        m_scr[rs, :] = m_new
        l_scr[rs, :] = l_new
        acc_scr[rs, :] = acc_new

    blocks_per_tile = tq // tk

    def body(j, carry):
        pending = []
        for d in range(blocks_per_tile):
            start = pl.multiple_of(j * tq + d * tk, tk)
            k_blk = k_ref[pl.ds(start, tk), :]
            v_blk = v_ref[pl.ds(start, tk), :]
            pending.append(([scores(r, k_blk) for r in range(n_split)], v_blk))
        for ss, v_blk in pending:
            for r in range(n_split):
                update(r, ss[r], v_blk, None)
        return carry

    lax.fori_loop(0, i, body, 0)

    tri = (lax.broadcasted_iota(jnp.int32, (rows, rows), 1) // CHUNK
           <= lax.broadcasted_iota(jnp.int32, (rows, rows), 0) // CHUNK)
    for d in range(blocks_per_tile):
        k0 = d * tk
        work = []
        for r in range(n_split):
            r0, r1 = r * rows, (r + 1) * rows
            width = min(k0 + tk, r1) - k0
            if width <= 0:
                continue
            on_diagonal = k0 + width > r0
            assert not on_diagonal or (k0 + width == r1 and width >= rows)
            start = pl.multiple_of(i * tq + k0, tk)
            work.append((r, scores(r, k_ref[pl.ds(start, width), :]), v_ref[pl.ds(start, width), :],
                         tri if on_diagonal else None))
        for r, s, v_blk, mask in work:
            update(r, s, v_blk, mask)

    l_row = jnp.sum(l_scr[...], axis=-1, keepdims=True)
    o_ref[...] = (_sigmoid(gb_ref[...].astype(F32)) * (acc_scr[...] / l_row)).astype(BF16)


def _attn_prompt_call(q_cat, k_cat, v, proj, batch, seq, tq):
    nq = seq // tq
    gb_blk = COL_GB // MLA_V
    est = (2 * (tq * HEAD_QK * 2 + seq * HEAD_QK * 2 + seq * MLA_V * 2 + 2 * tq * MLA_V * 4)
           + 3 * tq * V7X_LANES * 4 + 6 * tq * ATT_TK * 4 + (4 << 20))
    return pl.pallas_call(
        functools.partial(_attn_prompt_kernel, tq=tq, tk=ATT_TK, n_split=ATT_SPLIT),
        grid=(batch, MLA_HEADS, nq),
        in_specs=[pl.BlockSpec((tq, HEAD_QK), lambda b, h, i: (b * nq + i, h)),
                  pl.BlockSpec((seq, HEAD_QK), lambda b, h, i: (b, h)),
                  pl.BlockSpec((seq, MLA_V), lambda b, h, i: (b, h)),
                  pl.BlockSpec((tq, MLA_V), lambda b, h, i: (b * nq + i, gb_blk + h))],
        out_specs=pl.BlockSpec((tq, MLA_V), lambda b, h, i: (b * nq + i, h)),
        out_shape=jax.ShapeDtypeStruct((batch * seq, MLA_HEADS * MLA_V), BF16),
        scratch_shapes=[pltpu.VMEM((tq, V7X_LANES), F32), pltpu.VMEM((tq, V7X_LANES), F32),
                        pltpu.VMEM((tq, MLA_V), F32)],
        compiler_params=_cparams(("parallel", "parallel", "arbitrary"), est),
        name="mla_attention_prompt",
    )(q_cat, k_cat, v, proj)


def _attn_sample_kernel(qlat_ref, qr_ref, cc_ref, ckr_ref, nc_ref, nkr_ref, gb0_ref, gb1_ref, wukv_ref,
                        o_ref, *, sq):
    nt = (((1,), (1,)), ((), ()))
    q_lat = jnp.concatenate([qlat_ref[:, h * MLA_KV_LORA:(h + 1) * MLA_KV_LORA] for h in range(MLA_HEADS)],
                            axis=0)
    q_r = jnp.concatenate([qr_ref[:, h * MLA_ROPE:(h + 1) * MLA_ROPE] for h in range(MLA_HEADS)], axis=0)
    kc = cc_ref[...].astype(BF16)
    kn = nc_ref[...].astype(BF16)
    s_c = (lax.dot_general(q_lat, kc, nt, preferred_element_type=F32)
           + lax.dot_general(q_r, ckr_ref[...].astype(BF16), nt, preferred_element_type=F32))
    s_n = (lax.dot_general(q_lat, kn, nt, preferred_element_type=F32)
           + lax.dot_general(q_r, nkr_ref[...].astype(BF16), nt, preferred_element_type=F32))
    m = jnp.maximum(jnp.max(s_c, axis=-1, keepdims=True), jnp.max(s_n, axis=-1, keepdims=True))
    p_c = jnp.exp2(s_c - m)
    p_n = jnp.exp2(s_n - m)
    l_row = jnp.sum(p_c, axis=-1, keepdims=True) + jnp.sum(p_n, axis=-1, keepdims=True)
    o_lat = (jnp.dot(p_c.astype(BF16), kc, preferred_element_type=F32)
             + jnp.dot(p_n.astype(BF16), kn, preferred_element_type=F32)) / l_row
    o_lat = o_lat.astype(BF16)
    nv = MLA_HEADS * MLA_NOPE
    half = MLA_HEADS // 2
    for h in range(MLA_HEADS):
        w_uv = wukv_ref[:, nv + h * MLA_V:nv + (h + 1) * MLA_V]
        o_h = jnp.dot(o_lat[h * sq:(h + 1) * sq, :], w_uv, preferred_element_type=F32)
        gb_ref = gb0_ref if h < half else gb1_ref
        gb = gb_ref[:, (h % half) * MLA_V:(h % half + 1) * MLA_V].astype(F32)
        o_ref[:, h * MLA_V:(h + 1) * MLA_V] = (_sigmoid(gb) * o_h).astype(BF16)


def _attn_sample_call(q_lat, q_rope, ckv_new, kr_new, cache_ckv, cache_kr, proj, w_ukv_r, l, batch, sq):
    assert (PAST_LEN + sq - 1) // CHUNK <= PAST_LEN // CHUNK
    past = cache_ckv.shape[2]
    lat_w = MLA_HEADS * MLA_KV_LORA
    rope_w = MLA_HEADS * MLA_ROPE
    ww = MLA_HEADS * (MLA_NOPE + MLA_V)
    gw = D_MODEL // 2
    gb_blk = COL_GB // gw
    rows = MLA_HEADS * sq
    est = (2 * (sq * (lat_w + rope_w) * 2 + past * (MLA_KV_LORA + V7X_LANES) * 4 + MLA_KV_LORA * ww * 2
                + 4 * sq * D_MODEL * 4) + past * (MLA_KV_LORA + V7X_LANES) * 2 + 6 * rows * past * 4
           + 4 * rows * MLA_KV_LORA * 4 + (4 << 20))
    return pl.pallas_call(
        functools.partial(_attn_sample_kernel, sq=sq),
        grid=(batch,),
        in_specs=[pl.BlockSpec((sq, lat_w), lambda b: (b, 0)),
                  pl.BlockSpec((sq, rope_w), lambda b: (b, 0)),
                  pl.BlockSpec((None, None, past, MLA_KV_LORA), lambda b: (l, b, 0, 0)),
                  pl.BlockSpec((None, None, past, MLA_ROPE), lambda b: (l, b, 0, 0)),
                  pl.BlockSpec((sq, MLA_KV_LORA), lambda b: (b, 0)),
                  pl.BlockSpec((sq, MLA_ROPE), lambda b: (b, 0)),
                  pl.BlockSpec((sq, gw), lambda b: (b, gb_blk)),
                  pl.BlockSpec((sq, gw), lambda b: (b, gb_blk + 1)),
                  pl.BlockSpec((None, MLA_KV_LORA, ww), lambda b: (l, 0, 0))],
        out_specs=pl.BlockSpec((sq, D_MODEL), lambda b: (b, 0)),
        out_shape=jax.ShapeDtypeStruct((batch * sq, D_MODEL), BF16),
        compiler_params=_cparams(("parallel",), est),
        name="mla_attention_sample",
    )(q_lat, q_rope, cache_ckv, cache_kr, ckv_new, kr_new, proj, proj, w_ukv_r)


def _merge_kernel(a_ref, b_ref, x_ref, g1_ref, w_ref, o_ref):
    m = (a_ref[...].astype(F32) + b_ref[...].astype(F32)).astype(BF16)
    mix = jnp.dot(m, w_ref[...], preferred_element_type=F32)
    o_ref[...] = _gate_res(x_ref[...], g1_ref[...], mix)


def _merge_call(a_part, b_part, x, mod, w_o_b, l, tm, groups):
    m = x.shape[0]
    est = (2 * 2 * tm * D_MODEL * 2 + 4 * tm * D_MODEL * 4 + D_MODEL * D_MODEL * 2 + 3 * tm * D_MODEL * 4
           + (4 << 20))
    return pl.pallas_call(
        _merge_kernel,
        grid=(m // tm,),
        in_specs=[pl.BlockSpec((tm, D_MODEL), lambda i: (i, 0)),
                  pl.BlockSpec((tm, D_MODEL), lambda i: (i, 0)),
                  pl.BlockSpec((tm, D_MODEL), lambda i: (i, 0)),
                  pl.BlockSpec((None, None, None, groups, D_MODEL), lambda i: (l, 2, i, 0, 0)),
                  pl.BlockSpec((None, D_MODEL, D_MODEL), lambda i: (l, 0, 0), pipeline_mode=pl.Buffered(1))],
        out_specs=pl.BlockSpec((tm, D_MODEL), lambda i: (i, 0)),
        out_shape=jax.ShapeDtypeStruct((m, D_MODEL), F32),
        compiler_params=_cparams(("parallel",), est),
        name="merge_out_proj",
    )(a_part, b_part, x, mod, w_o_b)


def _ffn_kernel(x_ref, sc_ref, sh_ref, g2_ref, gn_ref, wg_ref, wu_ref, wo_ref, o_ref, h_scr, acc_scr,
                *, nh):
    j = pl.program_id(1)

    @pl.when(j == 0)
    def _():
        h_scr[...] = _norm_mod(x_ref[...], gn_ref[...], sc_ref[...], sh_ref[...]).astype(BF16)
        acc_scr[...] = jnp.zeros(acc_scr.shape, F32)

    hb = h_scr[...]
    gate = jnp.dot(hb, wg_ref[...], preferred_element_type=F32)
    up = jnp.dot(hb, wu_ref[...], preferred_element_type=F32)
    act = ((gate * _sigmoid(gate)) * up).astype(BF16)
    acc_scr[...] += jnp.dot(act, wo_ref[...], preferred_element_type=F32)

    @pl.when(j == nh - 1)
    def _():
        o_ref[...] = _gate_res(x_ref[...], g2_ref[...], acc_scr[...])


def _ffn_call(x, mod, norm_g, w_in_b, w_out_b, l, tm, groups):
    m = x.shape[0]
    th = FFN_TH
    nh = FFN_HIDDEN // th
    est = (4 * tm * D_MODEL * 4 + tm * D_MODEL * 2 + tm * D_MODEL * 4 + 2 * 3 * D_MODEL * th * 2
           + 4 * tm * th * 4 + tm * D_MODEL * 4 + (4 << 20))
    mod_spec = lambda k: pl.BlockSpec((None, None, None, groups, D_MODEL), lambda i, j: (l, k, i, 0, 0))
    return pl.pallas_call(
        functools.partial(_ffn_kernel, nh=nh),
        grid=(m // tm, nh),
        in_specs=[pl.BlockSpec((tm, D_MODEL), lambda i, j: (i, 0)),
                  mod_spec(4), mod_spec(3), mod_spec(5),
                  pl.BlockSpec((None, 1, D_MODEL), lambda i, j: (l, 0, 0)),
                  pl.BlockSpec((None, D_MODEL, th), lambda i, j: (l, 0, j)),
                  pl.BlockSpec((None, D_MODEL, th), lambda i, j: (l, 0, nh + j)),
                  pl.BlockSpec((None, th, D_MODEL), lambda i, j: (l, j, 0))],
        out_specs=pl.BlockSpec((tm, D_MODEL), lambda i, j: (i, 0)),
        out_shape=jax.ShapeDtypeStruct((m, D_MODEL), F32),
        scratch_shapes=[pltpu.VMEM((tm, D_MODEL), BF16), pltpu.VMEM((tm, D_MODEL), F32)],
        compiler_params=_cparams(("parallel", "arbitrary"), est),
        name="ffn_swiglu",
    )(x, mod, mod, mod, norm_g, w_in_b, w_in_b, w_out_b)


def _final_norm_kernel(x_ref, g_ref, o_ref):
    o_ref[...] = _rms(x_ref[...], g_ref[...])


def _final_norm_call(x, g, tm):
    m = x.shape[0]
    est = 4 * tm * D_MODEL * 4 + 2 * tm * D_MODEL * 4 + (4 << 20)
    return pl.pallas_call(
        _final_norm_kernel,
        grid=(m // tm,),
        in_specs=[pl.BlockSpec((tm, D_MODEL), lambda i: (i, 0)),
                  pl.BlockSpec((1, D_MODEL), lambda i: (0, 0))],
        out_specs=pl.BlockSpec((tm, D_MODEL), lambda i: (i, 0)),
        out_shape=jax.ShapeDtypeStruct((m, D_MODEL), F32),
        compiler_params=_cparams(("parallel",), est),
        name="final_norm",
    )(x, g.reshape(1, D_MODEL))


def _rope_tables(pos, dim):
    inv = jnp.exp(-math.log(ROPE_BASE) * jnp.arange(0, dim, 2, dtype=F32) / dim)
    ang = pos.astype(F32)[:, None] * inv[None, :]
    return jnp.cos(ang), jnp.sin(ang)


def _mla_rope_tables(pos, reps):
    cos, sin = _rope_tables(pos, MLA_ROPE)
    z = jnp.zeros_like(cos)
    c = jnp.concatenate([cos, z, cos, z], axis=-1)
    s = jnp.concatenate([-sin, z, sin, z], axis=-1)
    return jnp.tile(c, (reps, 1)), jnp.tile(s, (reps, 1))


def _prep_weights(w_uq, w_ukv):
    uq = w_uq.reshape(DEPTH, MLA_Q_LORA, MLA_HEADS, MLA_NOPE + MLA_ROPE)
    uq = jnp.concatenate([uq[..., :MLA_NOPE], _pack_rope_lanes(uq[..., MLA_NOPE:])], axis=-1)
    w_uq_r = uq.reshape(DEPTH, MLA_Q_LORA, MLA_HEADS * HEAD_QK).astype(BF16)
    ukv = w_ukv.reshape(DEPTH, MLA_KV_LORA, MLA_HEADS, MLA_NOPE + MLA_V)
    w_ukv_r = jnp.concatenate([ukv[..., :MLA_NOPE].reshape(DEPTH, MLA_KV_LORA, -1),
                               ukv[..., MLA_NOPE:].reshape(DEPTH, MLA_KV_LORA, -1)], axis=-1).astype(BF16)
    return w_uq_r, w_ukv_r


def _layer_prompt(l, x, mod, tm, batch, seq, ret_tabs, mla_tabs, zero_state, log_g, W):
    proj, kr_raw = _inproj_call(x, mod, W["norm_mix"], W["w_in_t"], l, IN_TM_PROMPT, 1, IN_TM_PROMPT // tm)
    a_part, new_state = _ret_call(proj, log_g, ret_tabs[0], ret_tabs[1], zero_state, 0, batch, seq,
                                  RET_L_PROMPT, RET_HEADS_PROMPT)
    q_cat, ckv, kr128 = _mla_q_call(proj, kr_raw, W["q_norm"], W["kv_norm"], mla_tabs[0], mla_tabs[1],
                                    W["w_uq_r"], l, tm)
    k_cat, v = _mla_kv_call(ckv, kr128, W["w_ukv_r"], l, ckv.shape[0], tm)
    b_part = _attn_prompt_call(q_cat, k_cat, v, proj, batch, seq, ATT_TQ)
    x = _merge_call(a_part, b_part, x, mod, W["w_o"], l, tm, 1)
    x = _ffn_call(x, mod, W["norm_ffn"], W["w_ffn_in"], W["w_ffn_out"], l, tm, 1)
    return x, ckv, _unpack_rope_lanes(kr128), new_state


def _layer_sample(l, x, mod, groups, tm, batch, seq, ret_tabs, mla_tabs, state_ret, cache_ckv, cache_kr,
                  log_g, W):
    proj, kr_raw = _inproj_call(x, mod, W["norm_mix"], W["w_in_t"], l, tm, groups, 1)
    a_part, new_state = _ret_call(proj, log_g, ret_tabs[0], ret_tabs[1], state_ret, l, batch, seq, seq,
                                  RET_HEADS_SAMPLE)
    q_lat, q_rope, ckv, kr = _mla_q_sample_call(proj, kr_raw, W["q_norm"], W["kv_norm"], mla_tabs[0],
                                                mla_tabs[1], W["w_uq_r"], W["w_ukv_r"], l)
    b_part = _attn_sample_call(q_lat, q_rope, ckv, kr, cache_ckv, cache_kr, proj, W["w_ukv_r"], l, batch, seq)
    x = _merge_call(a_part, b_part, x, mod, W["w_o"], l, tm, groups)
    x = _ffn_call(x, mod, W["norm_ffn"], W["w_ffn_in"], W["w_ffn_out"], l, tm, groups)
    return x, ckv, kr, new_state


def kernel(x_prompt, x_sample, c_prompt, c_sample, cache_mla_ckv, cache_mla_krope, state_ret, w_ada, b_ada,
           norm_mix, norm_ffn, w_in, mla_q_norm, w_uq, mla_kv_norm, w_ukv, w_o, w_ffn_in, w_ffn_out,
           norm_final):
    bp, sp, _ = x_prompt.shape
    bs, ss, _ = x_sample.shape
    tm = TOKEN_TILE
    assert sp % IN_TM_PROMPT == 0 and (bs * ss) % tm == 0 and tm % ss == 0

    w_uq_r, w_ukv_r = _prep_weights(w_uq, w_ukv)
    W = dict(w_in_t=jnp.swapaxes(w_in, 1, 2), w_uq_r=w_uq_r, w_ukv_r=w_ukv_r,
             w_o=w_o.astype(BF16), w_ffn_in=w_ffn_in.astype(BF16), w_ffn_out=w_ffn_out.astype(BF16),
             norm_mix=norm_mix.reshape(DEPTH, 1, D_MODEL), norm_ffn=norm_ffn.reshape(DEPTH, 1, D_MODEL),
             q_norm=mla_q_norm.reshape(DEPTH, 1, MLA_Q_LORA), kv_norm=mla_kv_norm.reshape(DEPTH, 1, MLA_KV_LORA))

    c_rows = -(-(bp + bs) // ADA_ROW_ALIGN) * ADA_ROW_ALIGN
    c_all = jnp.concatenate([c_prompt, c_sample, jnp.zeros((c_rows - bp - bs, D_MODEL), F32)], axis=0)
    mod_all = _ada_call(c_all, w_ada, b_ada)[:, :bp + bs]
    mod_all = mod_all.reshape(DEPTH, bp + bs, 6, D_MODEL).transpose(0, 2, 1, 3)
    tiles_per_batch = sp // tm
    mod_p = jnp.repeat(mod_all[:, :, :bp], tiles_per_batch, axis=2)[:, :, :, None, :]
    groups_s = tm // ss
    mod_s = mod_all[:, :, bp:].reshape(DEPTH, 6, (bs * ss) // tm, groups_s, D_MODEL)

    log_g = jnp.log1p(-jnp.exp2(-RET_GAMMA_EXP0 - jnp.arange(RET_HEADS, dtype=F32)))
    pos_p = jnp.arange(sp)
    pos_s = PAST_LEN + jnp.arange(ss)
    ret_tabs_p = _rope_tables(pos_p, RET_DK)
    ret_tabs_s = _rope_tables(pos_s, RET_DK)
    mla_tabs_p = _mla_rope_tables(pos_p, bp)
    mla_tabs_s = _mla_rope_tables(pos_s, bs)
    zero_state = jnp.zeros((1, bp, RET_HEADS, RET_DK, RET_DV), F32)

    xp = x_prompt.reshape(bp * sp, D_MODEL)
    xs = x_sample.reshape(bs * ss, D_MODEL)
    outs = [[] for _ in range(6)]
    for l in range(DEPTH):
        xp, ckv, kr, st = _layer_prompt(l, xp, mod_p, tm, bp, sp, ret_tabs_p, mla_tabs_p, zero_state, log_g, W)
        outs[0].append(ckv.reshape(bp, sp, MLA_KV_LORA))
        outs[1].append(kr.reshape(bp, sp, MLA_ROPE))
        outs[2].append(st)
        xs, ckv, kr, st = _layer_sample(l, xs, mod_s, groups_s, tm, bs, ss, ret_tabs_s, mla_tabs_s, state_ret,
                                        cache_mla_ckv, cache_mla_krope, log_g, W)
        outs[3].append(ckv.reshape(bs, ss, MLA_KV_LORA))
        outs[4].append(kr.reshape(bs, ss, MLA_ROPE))
        outs[5].append(st)

    y_prompt = _final_norm_call(xp, norm_final, tm).reshape(bp, sp, D_MODEL)
    y_sample = _final_norm_call(xs, norm_final, tm).reshape(bs, ss, D_MODEL)
    return (y_prompt, y_sample, jnp.stack(outs[0]), jnp.stack(outs[1]), jnp.stack(outs[2]),
            jnp.stack(outs[3]), jnp.stack(outs[4]), jnp.stack(outs[5]))
```

```python
import functools
import math

import jax
import jax.numpy as jnp
from jax import lax
from jax.experimental import pallas as pl
from jax.experimental.pallas import tpu as pltpu

D_MODEL = 2048
DEPTH = 4
PAST_LEN = 1024
CHUNK = 64
RET_HEADS = 8
RET_DK = D_MODEL // RET_HEADS
RET_DV = D_MODEL // RET_HEADS
MLA_HEADS = 16
MLA_Q_LORA = D_MODEL // 4
MLA_KV_LORA = D_MODEL // 4
MLA_NOPE = 128
MLA_ROPE = 64
MLA_V = D_MODEL // MLA_HEADS
FFN_HIDDEN = -(-8 * D_MODEL // (3 * 256)) * 256
ROPE_BASE = 10000.0
RET_GAMMA_EXP0 = 5.0
RMS_EPS = 1e-6
GN_EPS = 1e-5
NEG_INF = -1e30

F32 = jnp.float32
BF16 = jnp.bfloat16

V7X_LANES = 128
V7X_VMEM_LIMIT_CAP = 56 * 1024 * 1024

COL_RQ, COL_RK, COL_RV, COL_RG = 0, D_MODEL, 2 * D_MODEL, 3 * D_MODEL
COL_DQ = 4 * D_MODEL
COL_DKV = COL_DQ + MLA_Q_LORA
COL_GA = COL_DKV + MLA_KV_LORA
COL_GB = COL_GA + D_MODEL
MAIN_COLS = COL_GB + D_MODEL
HEAD_QK = 2 * V7X_LANES
HALF_ROPE = MLA_ROPE // 2
ADA_ROW_ALIGN = 16
TOKEN_TILE = 512
IN_TM_PROMPT = 2048
IN_TN = 512
ADA_TN = 1024
FFN_TH = 512
RET_L_PROMPT = 256
RET_HEADS_PROMPT = 2
RET_HEADS_SAMPLE = 4
ATT_TQ = 1024
ATT_TK = 512
ATT_SPLIT = 4
QK_SCALE_LOG2 = (MLA_NOPE + MLA_ROPE) ** -0.5 * math.log2(math.e)


def _cparams(sem, est_bytes):
    return pltpu.CompilerParams(dimension_semantics=sem,
                                vmem_limit_bytes=min(int(est_bytes), V7X_VMEM_LIMIT_CAP))


def _sigmoid(x):
    return jax.nn.sigmoid(x)


def _rms(x, g):
    return x * lax.rsqrt(jnp.mean(x * x, axis=-1, keepdims=True) + RMS_EPS) * g


def _norm_mod(x, g, sc, sh):
    tm, d = x.shape
    groups = sc.shape[0]
    y = _rms(x, g)
    if groups == 1:
        return y * (1.0 + sc) + sh
    y3 = y.reshape(groups, tm // groups, d)
    return (y3 * (1.0 + sc[:, None, :]) + sh[:, None, :]).reshape(tm, d)


def _gate_res(x, gate, upd):
    tm, n = x.shape
    groups = gate.shape[0]
    if groups == 1:
        return x + gate * upd
    return x + (gate[:, None, :] * upd.reshape(groups, tm // groups, n)).reshape(tm, n)


def _pack_rope_lanes(x):
    z = jnp.zeros(x.shape[:-1] + (HALF_ROPE,), x.dtype)
    return jnp.concatenate([x[..., :HALF_ROPE], z, x[..., HALF_ROPE:], z], axis=-1)


def _unpack_rope_lanes(x):
    return jnp.concatenate([x[..., :HALF_ROPE], x[..., 2 * HALF_ROPE:3 * HALF_ROPE]], axis=-1)


def _ada_kernel(c_ref, w_ref, b_ref, o_ref):
    c = c_ref[...]
    a = (c * _sigmoid(c)).astype(BF16)
    o_ref[0] = jnp.dot(a, w_ref[0].astype(BF16), preferred_element_type=F32) + b_ref[0]


def _ada_call(c_all, w_ada, b_ada):
    nb = c_all.shape[0]
    n = w_ada.shape[-1]
    tn = ADA_TN
    est = 2 * (D_MODEL * tn * 4) + D_MODEL * tn * 2 + 4 * nb * (D_MODEL + 2 * tn) * 4 + (4 << 20)
    return pl.pallas_call(
        _ada_kernel,
        grid=(DEPTH, n // tn),
        in_specs=[pl.BlockSpec((nb, D_MODEL), lambda l, j: (0, 0)),
                  pl.BlockSpec((1, D_MODEL, tn), lambda l, j: (l, 0, j)),
                  pl.BlockSpec((1, 1, tn), lambda l, j: (l, 0, j))],
        out_specs=pl.BlockSpec((1, nb, tn), lambda l, j: (l, 0, j)),
        out_shape=jax.ShapeDtypeStruct((DEPTH, nb, n), F32),
        compiler_params=_cparams(("parallel", "parallel"), est),
        name="ada_mod",
    )(c_all, w_ada, b_ada.reshape(DEPTH, 1, n))


def _inproj_kernel(x_ref, sc_ref, sh_ref, g_ref, w_ref, wn_ref, wkr_ref, o_ref, kr_ref, h_scr, *, n_main):
    j = pl.program_id(1)
    nt = (((1,), (1,)), ((), ()))

    @pl.when(j == 0)
    def _():
        wkr = wkr_ref[...]
        z = jnp.zeros((HALF_ROPE, wkr.shape[1]), wkr.dtype)
        wkr = jnp.concatenate([wkr[:HALF_ROPE], z, wkr[HALF_ROPE:], z], axis=0).astype(BF16)
        tm = x_ref.shape[0]
        for r0 in range(0, tm, TOKEN_TILE):
            rs = pl.ds(r0, TOKEN_TILE)
            hb = _norm_mod(x_ref[rs, :], g_ref[...], sc_ref[...], sh_ref[...]).astype(BF16)
            h_scr[rs, :] = hb
            kr_ref[rs, :] = lax.dot_general(hb, wkr, nt, preferred_element_type=F32)

    @pl.when(j < n_main)
    def _():
        o_ref[...] = lax.dot_general(h_scr[...], w_ref[...].astype(BF16), nt,
                                     preferred_element_type=F32).astype(BF16)

    @pl.when(j >= n_main)
    def _():
        w = jnp.concatenate([w_ref[MLA_ROPE:, :], wn_ref[...]], axis=0).astype(BF16)
        o_ref[...] = lax.dot_general(h_scr[...], w, nt, preferred_element_type=F32).astype(BF16)


def _inproj_call(x, mod, norm_g, w_in_t, l, tm, groups, mod_stride):
    m = x.shape[0]
    n_main = COL_GA // IN_TN
    sub = IN_TN // MLA_ROPE
    assert groups == 1 or tm == TOKEN_TILE
    est = (tm * D_MODEL * 4 + tm * D_MODEL * 2 + 2 * D_MODEL * (IN_TN + 2 * MLA_ROPE) * 4
           + D_MODEL * IN_TN * 2 + 3 * tm * IN_TN * 4 + 6 * TOKEN_TILE * D_MODEL * 4 + (4 << 20))
    mod_spec = lambda k: pl.BlockSpec((None, None, None, groups, D_MODEL),
                                      lambda i, j: (l, k, i * mod_stride, 0, 0))
    return pl.pallas_call(
        functools.partial(_inproj_kernel, n_main=n_main),
        grid=(m // tm, MAIN_COLS // IN_TN),
        in_specs=[pl.BlockSpec((tm, D_MODEL), lambda i, j: (i, 0), pipeline_mode=pl.Buffered(1)),
                  mod_spec(1), mod_spec(0),
                  pl.BlockSpec((None, 1, D_MODEL), lambda i, j: (l, 0, 0)),
                  pl.BlockSpec((None, IN_TN, D_MODEL), lambda i, j: (l, j, 0)),
                  pl.BlockSpec((None, MLA_ROPE, D_MODEL),
                               lambda i, j: (l, (jnp.maximum(j, n_main) + 1) * sub, 0)),
                  pl.BlockSpec((None, MLA_ROPE, D_MODEL), lambda i, j: (l, COL_GA // MLA_ROPE, 0))],
        out_specs=[pl.BlockSpec((tm, IN_TN), lambda i, j: (i, j)),
                   pl.BlockSpec((tm, V7X_LANES), lambda i, j: (i, 0))],
        out_shape=[jax.ShapeDtypeStruct((m, MAIN_COLS), BF16),
                   jax.ShapeDtypeStruct((m, V7X_LANES), F32)],
        scratch_shapes=[pltpu.VMEM((tm, D_MODEL), BF16)],
        compiler_params=_cparams(("parallel", "arbitrary"), est),
        name="in_proj",
    )(x, mod, mod, norm_g, w_in_t, w_in_t, w_in_t)


def _ret_kernel(lg_ref, q_ref, k_ref, v_ref, rg_ref, ga_ref, cos_ref, sin_ref, s0_ref, *rest,
                chunk_len, heads):
    a_ref, st_ref, dm_scr = rest[-3:]
    hg = pl.program_id(1)
    c = pl.program_id(2)
    L = chunk_len
    lgs = [lg_ref[hg * heads + t] for t in range(heads)]

    @pl.when(c == 0)
    def _():
        st_ref[...] = s0_ref[...]
        ri = lax.broadcasted_iota(jnp.int32, (L, L), 0)
        ci = lax.broadcasted_iota(jnp.int32, (L, L), 1)
        diff = (ri - ci).astype(F32)
        for t in range(heads):
            dm_scr[t] = jnp.where(diff >= 0, jnp.exp(jnp.maximum(diff, 0.0) * lgs[t]), 0.0)

    cos = cos_ref[...]
    sin = sin_ref[...]
    half = RET_DK // 2
    idx = lax.broadcasted_iota(jnp.int32, (L, 1), 0).astype(F32)
    nt = (((1,), (1,)), ((), ()))
    tn = (((0,), (0,)), ((), ()))

    def rope(x):
        x1, x2 = x[:, :half], x[:, half:]
        return jnp.concatenate([x1 * cos - x2 * sin, x1 * sin + x2 * cos], axis=-1)

    stage = []
    for t in range(heads):
        cs = pl.ds(t * RET_DK, RET_DK)
        q = rope(q_ref[:, cs].astype(F32))
        k = rope(k_ref[:, cs].astype(F32)) * (RET_DK ** -0.5)
        vb = v_ref[:, cs]
        qb = q.astype(BF16)
        zeta = jnp.exp((L - 1.0 - idx) * lgs[t])
        g_l = jnp.exp(jnp.full((1, 1), float(L), F32) * lgs[t])
        st = st_ref[0, t]
        scores = lax.dot_general(qb, k.astype(BF16), nt, preferred_element_type=F32)
        cross = jnp.dot(qb, st.astype(BF16), preferred_element_type=F32)
        st_ref[0, t] = st * g_l + lax.dot_general((k * zeta).astype(BF16), vb, tn, preferred_element_type=F32)
        stage.append((scores, cross, vb))

    outs = []
    for t in range(heads):
        scores, cross, vb = stage[t]
        xi = jnp.exp((idx + 1.0) * lgs[t])
        outs.append(jnp.dot((scores * dm_scr[t]).astype(BF16), vb, preferred_element_type=F32) + cross * xi)

    for t in range(heads):
        cs = pl.ds(t * RET_DK, RET_DK)
        o = outs[t]
        mu = jnp.mean(o, axis=-1, keepdims=True)
        d = o - mu
        var = jnp.mean(d * d, axis=-1, keepdims=True)
        on = d * lax.rsqrt(var + GN_EPS)
        rg = rg_ref[:, cs].astype(F32)
        a_ref[:, cs] = (_sigmoid(ga_ref[:, cs].astype(F32)) * ((rg * _sigmoid(rg)) * on)).astype(BF16)


def _stacked_out(stack_prev, in_specs, args, out_index):
    if stack_prev is None:
        return {}
    in_specs.append(pl.BlockSpec(memory_space=pl.ANY))
    args.append(stack_prev)
    return {len(args) - 1: out_index}


def _ret_call(proj, log_g, cos, sin, state0, state_layer, batch, seq, chunk_len, heads, l, stack_prev):
    L = chunk_len
    nc = seq // L
    w = heads * RET_DK
    assert COL_GA % w == 0 and RET_HEADS % heads == 0
    col = lambda base: (lambda b, h, c: (b * nc + c, base // w + h))
    blk = lambda base: pl.BlockSpec((L, w), col(base))
    nh = RET_HEADS
    est = (2 * 6 * L * w * 4 + 4 * heads * RET_DK * RET_DV * 4 + heads * L * L * 4
           + 8 * heads * L * max(L, RET_DK) * 4 + (4 << 20))
    in_specs = [pl.BlockSpec(memory_space=pltpu.SMEM),
                blk(COL_RQ), blk(COL_RK), blk(COL_RV), blk(COL_RG), blk(COL_GA),
                pl.BlockSpec((L, RET_DK // 2), lambda b, h, c: (c, 0)),
                pl.BlockSpec((L, RET_DK // 2), lambda b, h, c: (c, 0)),
                pl.BlockSpec((None, 1, heads, RET_DK, RET_DV), lambda b, h, c: (state_layer, b, h, 0, 0))]
    args = [log_g, proj, proj, proj, proj, proj, cos, sin, state0]
    aliases = _stacked_out(stack_prev, in_specs, args, 1)
    return pl.pallas_call(
        functools.partial(_ret_kernel, chunk_len=L, heads=heads),
        grid=(batch, nh // heads, nc),
        in_specs=in_specs,
        out_specs=[pl.BlockSpec((L, w), lambda b, h, c: (b * nc + c, h)),
                   pl.BlockSpec((None, 1, heads, RET_DK, RET_DV), lambda b, h, c: (l, b, h, 0, 0))],
        out_shape=[jax.ShapeDtypeStruct((batch * seq, D_MODEL), BF16),
                   jax.ShapeDtypeStruct((DEPTH, batch, nh, RET_DK, RET_DV), F32)],
        scratch_shapes=[pltpu.VMEM((heads, L, L), F32)],
        input_output_aliases=aliases,
        compiler_params=_cparams(("parallel", "parallel", "arbitrary"), est),
        name="retention",
    )(*args)


def _rope128(x, c, s):
    return x * c + pltpu.roll(x, V7X_LANES // 2, 1) * s


def _mla_q_kernel(dq_ref, dkv_ref, kr_ref, qn_ref, kvn_ref, c_ref, s_ref, wuq_ref, *rest):
    q_out, ckv_out, kro_out, kr64_out = rest[-4:]
    c = c_ref[...]
    s = s_ref[...]
    cq = _rms(dq_ref[...].astype(F32), qn_ref[...]).astype(BF16)
    for h in range(MLA_HEADS):
        lo = h * HEAD_QK
        qh = jnp.dot(cq, wuq_ref[:, lo:lo + HEAD_QK], preferred_element_type=F32) * QK_SCALE_LOG2
        q_out[:, lo:lo + MLA_NOPE] = qh[:, :MLA_NOPE].astype(BF16)
        q_out[:, lo + MLA_NOPE:lo + HEAD_QK] = _rope128(qh[:, MLA_NOPE:], c, s).astype(BF16)
    ckv_out[...] = _rms(dkv_ref[...].astype(F32), kvn_ref[...])
    kr = _rope128(kr_ref[...], c, s)
    kro_out[...] = kr
    kr64_out[...] = _unpack_rope_lanes(kr)


def _mla_q_call(proj, kr_raw, q_norm, kv_norm, rope_c, rope_s, w_uq_r, l, tm, ckv_prev, kr_prev):
    m = proj.shape[0]
    dq_blk = COL_DQ // MLA_Q_LORA
    qw = MLA_HEADS * HEAD_QK
    est = (2 * (2 * tm * MLA_Q_LORA * 4 + 3 * tm * V7X_LANES * 4) + 2 * MLA_Q_LORA * qw * 2
           + 2 * tm * qw * 2 + 2 * tm * MLA_KV_LORA * 4 + 2 * tm * V7X_LANES * 4 + 8 * tm * HEAD_QK * 4
           + (4 << 20))
    in_specs = [pl.BlockSpec((tm, MLA_Q_LORA), lambda i: (i, dq_blk)),
                pl.BlockSpec((tm, MLA_KV_LORA), lambda i: (i, dq_blk + 1)),
                pl.BlockSpec((tm, V7X_LANES), lambda i: (i, 0)),
                pl.BlockSpec((None, 1, MLA_Q_LORA), lambda i: (l, 0, 0)),
                pl.BlockSpec((None, 1, MLA_KV_LORA), lambda i: (l, 0, 0)),
                pl.BlockSpec((tm, V7X_LANES), lambda i: (i, 0)),
                pl.BlockSpec((tm, V7X_LANES), lambda i: (i, 0)),
                pl.BlockSpec((None, MLA_Q_LORA, qw), lambda i: (l, 0, 0))]
    args = [proj, proj, kr_raw, q_norm, kv_norm, rope_c, rope_s, w_uq_r]
    aliases = _stacked_out(ckv_prev, in_specs, args, 1)
    aliases.update(_stacked_out(kr_prev, in_specs, args, 3))
    return pl.pallas_call(
        _mla_q_kernel,
        grid=(m // tm,),
        in_specs=in_specs,
        out_specs=[pl.BlockSpec((tm, qw), lambda i: (i, 0)),
                   pl.BlockSpec((None, tm, MLA_KV_LORA), lambda i: (l, i, 0)),
                   pl.BlockSpec((tm, V7X_LANES), lambda i: (i, 0)),
                   pl.BlockSpec((None, tm, MLA_ROPE), lambda i: (l, i, 0))],
        out_shape=[jax.ShapeDtypeStruct((m, qw), BF16),
                   jax.ShapeDtypeStruct((DEPTH, m, MLA_KV_LORA), F32),
                   jax.ShapeDtypeStruct((m, V7X_LANES), F32),
                   jax.ShapeDtypeStruct((DEPTH, m, MLA_ROPE), F32)],
        input_output_aliases=aliases,
        compiler_params=_cparams(("parallel",), est),
        name="mla_latents",
    )(*args)


def _mla_q_sample_kernel(dq_ref, dkv_ref, kr_ref, qn_ref, kvn_ref, c_ref, s_ref, wuq_ref, wukv_ref,
                         qlat_out, qr_out, ckv_out, kro_out):
    c = c_ref[...]
    s = s_ref[...]
    cq = _rms(dq_ref[...].astype(F32), qn_ref[...]).astype(BF16)
    for h in range(MLA_HEADS):
        lo = h * HEAD_QK
        qh = jnp.dot(cq, wuq_ref[:, lo:lo + HEAD_QK], preferred_element_type=F32) * QK_SCALE_LOG2
        w_uk = wukv_ref[:, h * MLA_NOPE:(h + 1) * MLA_NOPE]
        q_lat = lax.dot_general(qh[:, :MLA_NOPE].astype(BF16), w_uk, (((1,), (1,)), ((), ())),
                                preferred_element_type=F32)
        qlat_out[:, h * MLA_KV_LORA:(h + 1) * MLA_KV_LORA] = q_lat.astype(BF16)
        qr_out[:, h * MLA_ROPE:(h + 1) * MLA_ROPE] = _unpack_rope_lanes(
            _rope128(qh[:, MLA_NOPE:], c, s)).astype(BF16)
    ckv_out[...] = _rms(dkv_ref[...].astype(F32), kvn_ref[...])
    kro_out[...] = _unpack_rope_lanes(_rope128(kr_ref[...], c, s))


def _mla_q_sample_call(proj, kr_raw, q_norm, kv_norm, rope_c, rope_s, w_uq_r, w_ukv_r, l):
    m = proj.shape[0]
    dq_blk = COL_DQ // MLA_Q_LORA
    qw = MLA_HEADS * HEAD_QK
    ww = MLA_HEADS * (MLA_NOPE + MLA_V)
    lat_w = MLA_HEADS * MLA_KV_LORA
    rope_w = MLA_HEADS * MLA_ROPE
    est = (2 * (2 * m * MLA_Q_LORA * 4 + 3 * m * V7X_LANES * 4) + 2 * MLA_Q_LORA * (qw + ww) * 2
           + 2 * m * (lat_w + rope_w) * 2 + 2 * m * (MLA_KV_LORA + MLA_ROPE) * 4 + 8 * m * MLA_KV_LORA * 4
           + (4 << 20))
    full = lambda shape: pl.BlockSpec(shape, lambda i: (0,) * len(shape))
    return pl.pallas_call(
        _mla_q_sample_kernel,
        grid=(1,),
        in_specs=[pl.BlockSpec((m, MLA_Q_LORA), lambda i: (0, dq_blk)),
                  pl.BlockSpec((m, MLA_KV_LORA), lambda i: (0, dq_blk + 1)),
                  full((m, V7X_LANES)),
                  pl.BlockSpec((None, 1, MLA_Q_LORA), lambda i: (l, 0, 0)),
                  pl.BlockSpec((None, 1, MLA_KV_LORA), lambda i: (l, 0, 0)),
                  full((m, V7X_LANES)), full((m, V7X_LANES)),
                  pl.BlockSpec((None, MLA_Q_LORA, qw), lambda i: (l, 0, 0)),
                  pl.BlockSpec((None, MLA_KV_LORA, ww), lambda i: (l, 0, 0))],
        out_specs=[full((m, lat_w)), full((m, rope_w)), full((m, MLA_KV_LORA)), full((m, MLA_ROPE))],
        out_shape=[jax.ShapeDtypeStruct((m, lat_w), BF16),
                   jax.ShapeDtypeStruct((m, rope_w), BF16),
                   jax.ShapeDtypeStruct((m, MLA_KV_LORA), F32),
                   jax.ShapeDtypeStruct((m, MLA_ROPE), F32)],
        compiler_params=_cparams(("arbitrary",), est),
        name="mla_latents_sample",
    )(proj, proj, kr_raw, q_norm, kv_norm, rope_c, rope_s, w_uq_r, w_ukv_r)


def _mla_kv_kernel(ckv_ref, kr_ref, wukv_ref, kcat_out, v_out):
    cb = ckv_ref[...].astype(BF16)
    krb = kr_ref[...].astype(BF16)
    for hp in range(MLA_HEADS // 2):
        kk = jnp.dot(cb, wukv_ref[:, hp * HEAD_QK:(hp + 1) * HEAD_QK], preferred_element_type=F32)
        for t in range(2):
            lo = (2 * hp + t) * HEAD_QK
            kcat_out[:, lo:lo + MLA_NOPE] = kk[:, t * MLA_NOPE:(t + 1) * MLA_NOPE].astype(BF16)
            kcat_out[:, lo + MLA_NOPE:lo + HEAD_QK] = krb
    nv = MLA_HEADS * MLA_NOPE
    for hp in range(nv // HEAD_QK):
        lo = hp * HEAD_QK
        v_out[:, lo:lo + HEAD_QK] = jnp.dot(cb, wukv_ref[:, nv + lo:nv + lo + HEAD_QK],
                                            preferred_element_type=F32).astype(BF16)


def _mla_kv_call(ckv_stack, kr128, w_ukv_r, l, rows, tm):
    kw = MLA_HEADS * HEAD_QK
    vw = MLA_HEADS * MLA_V
    ww = MLA_HEADS * (MLA_NOPE + MLA_V)
    est = (2 * tm * (MLA_KV_LORA + V7X_LANES) * 4 + 2 * MLA_KV_LORA * ww * 2 + 2 * tm * (kw + vw) * 2
           + 8 * tm * HEAD_QK * 4 + (4 << 20))
    return pl.pallas_call(
        _mla_kv_kernel,
        grid=(rows // tm,),
        in_specs=[pl.BlockSpec((None, tm, MLA_KV_LORA), lambda i: (l, i, 0)),
                  pl.BlockSpec((tm, V7X_LANES), lambda i: (i, 0)),
                  pl.BlockSpec((None, MLA_KV_LORA, ww), lambda i: (l, 0, 0))],
        out_specs=[pl.BlockSpec((tm, kw), lambda i: (i, 0)),
                   pl.BlockSpec((tm, vw), lambda i: (i, 0))],
        out_shape=[jax.ShapeDtypeStruct((rows, kw), BF16),
                   jax.ShapeDtypeStruct((rows, vw), BF16)],
        compiler_params=_cparams(("parallel",), est),
        name="mla_kv_expand",
    )(ckv_stack, kr128, w_ukv_r)


def _scores(q, k_blk):
    return lax.dot_general(q, k_blk, (((1,), (1,)), ((), ())), preferred_element_type=F32)


def _softmax_tile(s, v_blk, m_prev, l_prev, acc_prev, tri):
    n_groups = s.shape[1] // V7X_LANES
    groups = [s[:, g * V7X_LANES:(g + 1) * V7X_LANES] for g in range(n_groups)]
    if tri is not None:
        n_tri = tri.shape[1] // V7X_LANES
        for t in range(n_tri):
            g = n_groups - n_tri + t
            groups[g] = jnp.where(tri[:, t * V7X_LANES:(t + 1) * V7X_LANES], groups[g], NEG_INF)
    m_new = jnp.maximum(m_prev, jnp.max(functools.reduce(jnp.maximum, groups), axis=-1, keepdims=True))
    alpha = jnp.exp2(m_prev - m_new)
    ps = [jnp.exp2(g - m_new) for g in groups]
    l_new = alpha * l_prev + functools.reduce(jnp.add, ps)
    p = jnp.concatenate([x.astype(BF16) for x in ps], axis=-1)
    acc_new = alpha * acc_prev + jnp.dot(p, v_blk, preferred_element_type=F32)
    return m_new, l_new, acc_new


def _attn_prompt_kernel(q_ref, k_ref, v_ref, gb_ref, o_ref, m_scr, l_scr, acc_scr, *, tq, tk, n_split):
    i = pl.program_id(2)
    rows = tq // n_split
    m_scr[...] = jnp.full(m_scr.shape, NEG_INF, F32)
    l_scr[...] = jnp.zeros(l_scr.shape, F32)
    acc_scr[...] = jnp.zeros(acc_scr.shape, F32)

    def scores(r, k_blk):
        return _scores(q_ref[pl.ds(r * rows, rows), :], k_blk)

    def update(r, s, v_blk, mask):
        rs = pl.ds(r * rows, rows)
        m_new, l_new, acc_new = _softmax_tile(s, v_blk, m_scr[rs, :], l_scr[rs, :], acc_scr[rs, :], mask)
        m_scr[rs, :] = m_new
        l_scr[rs, :] = l_new
        acc_scr[rs, :] = acc_new

    blocks_per_tile = tq // tk

    def body(j, carry):
        pending = []
        for d in range(blocks_per_tile):
            start = pl.multiple_of(j * tq + d * tk, tk)
            k_blk = k_ref[pl.ds(start, tk), :]
            v_blk = v_ref[pl.ds(start, tk), :]
            pending.append(([scores(r, k_blk) for r in range(n_split)], v_blk))
        for ss, v_blk in pending:
            for r in range(n_split):
                update(r, ss[r], v_blk, None)
        return carry

    lax.fori_loop(0, i, body, 0)

    tri = (lax.broadcasted_iota(jnp.int32, (rows, rows), 1) // CHUNK
           <= lax.broadcasted_iota(jnp.int32, (rows, rows), 0) // CHUNK)
    for d in range(blocks_per_tile):
        k0 = d * tk
        work = []
        for r in range(n_split):
            r0, r1 = r * rows, (r + 1) * rows
            width = min(k0 + tk, r1) - k0
            if width <= 0:
                continue
            on_diagonal = k0 + width > r0
            assert not on_diagonal or (k0 + width == r1 and width >= rows)
            start = pl.multiple_of(i * tq + k0, tk)
            work.append((r, scores(r, k_ref[pl.ds(start, width), :]), v_ref[pl.ds(start, width), :],
                         tri if on_diagonal else None))
        for r, s, v_blk, mask in work:
            update(r, s, v_blk, mask)

    l_row = jnp.sum(l_scr[...], axis=-1, keepdims=True)
    o_ref[...] = (_sigmoid(gb_ref[...].astype(F32)) * (acc_scr[...] / l_row)).astype(BF16)


def _attn_prompt_call(q_cat, k_cat, v, proj, batch, seq, tq):
    nq = seq // tq
    gb_blk = COL_GB // MLA_V
    est = (2 * (tq * HEAD_QK * 2 + seq * HEAD_QK * 2 + seq * MLA_V * 2 + 2 * tq * MLA_V * 4)
           + 3 * tq * V7X_LANES * 4 + 6 * tq * ATT_TK * 4 + (4 << 20))
    return pl.pallas_call(
        functools.partial(_attn_prompt_kernel, tq=tq, tk=ATT_TK, n_split=ATT_SPLIT),
        grid=(batch, MLA_HEADS, nq),
        in_specs=[pl.BlockSpec((tq, HEAD_QK), lambda b, h, i: (b * nq + i, h)),
                  pl.BlockSpec((seq, HEAD_QK), lambda b, h, i: (b, h)),
                  pl.BlockSpec((seq, MLA_V), lambda b, h, i: (b, h)),
                  pl.BlockSpec((tq, MLA_V), lambda b, h, i: (b * nq + i, gb_blk + h))],
        out_specs=pl.BlockSpec((tq, MLA_V), lambda b, h, i: (b * nq + i, h)),
        out_shape=jax.ShapeDtypeStruct((batch * seq, MLA_HEADS * MLA_V), BF16),
        scratch_shapes=[pltpu.VMEM((tq, V7X_LANES), F32), pltpu.VMEM((tq, V7X_LANES), F32),
                        pltpu.VMEM((tq, MLA_V), F32)],
        compiler_params=_cparams(("parallel", "parallel", "arbitrary"), est),
        name="mla_attention_prompt",
    )(q_cat, k_cat, v, proj)


def _attn_sample_kernel(qlat_ref, qr_ref, cc_ref, ckr_ref, nc_ref, nkr_ref, gb0_ref, gb1_ref, wukv_ref,
                        o_ref, *, sq):
    nt = (((1,), (1,)), ((), ()))
    q_lat = jnp.concatenate([qlat_ref[:, h * MLA_KV_LORA:(h + 1) * MLA_KV_LORA] for h in range(MLA_HEADS)],
                            axis=0)
    q_r = jnp.concatenate([qr_ref[:, h * MLA_ROPE:(h + 1) * MLA_ROPE] for h in range(MLA_HEADS)], axis=0)
    kc = cc_ref[...].astype(BF16)
    kn = nc_ref[...].astype(BF16)
    s_c = (lax.dot_general(q_lat, kc, nt, preferred_element_type=F32)
           + lax.dot_general(q_r, ckr_ref[...].astype(BF16), nt, preferred_element_type=F32))
    s_n = (lax.dot_general(q_lat, kn, nt, preferred_element_type=F32)
           + lax.dot_general(q_r, nkr_ref[...].astype(BF16), nt, preferred_element_type=F32))
    m = jnp.maximum(jnp.max(s_c, axis=-1, keepdims=True), jnp.max(s_n, axis=-1, keepdims=True))
    p_c = jnp.exp2(s_c - m)
    p_n = jnp.exp2(s_n - m)
    l_row = jnp.sum(p_c, axis=-1, keepdims=True) + jnp.sum(p_n, axis=-1, keepdims=True)
    o_lat = (jnp.dot(p_c.astype(BF16), kc, preferred_element_type=F32)
             + jnp.dot(p_n.astype(BF16), kn, preferred_element_type=F32)) / l_row
    o_lat = o_lat.astype(BF16)
    nv = MLA_HEADS * MLA_NOPE
    half = MLA_HEADS // 2
    for h in range(MLA_HEADS):
        w_uv = wukv_ref[:, nv + h * MLA_V:nv + (h + 1) * MLA_V]
        o_h = jnp.dot(o_lat[h * sq:(h + 1) * sq, :], w_uv, preferred_element_type=F32)
        gb_ref = gb0_ref if h < half else gb1_ref
        gb = gb_ref[:, (h % half) * MLA_V:(h % half + 1) * MLA_V].astype(F32)
        o_ref[:, h * MLA_V:(h + 1) * MLA_V] = (_sigmoid(gb) * o_h).astype(BF16)


def _attn_sample_call(q_lat, q_rope, ckv_new, kr_new, cache_ckv, cache_kr, proj, w_ukv_r, l, batch, sq):
    assert (PAST_LEN + sq - 1) // CHUNK <= PAST_LEN // CHUNK
    past = cache_ckv.shape[2]
    lat_w = MLA_HEADS * MLA_KV_LORA
    rope_w = MLA_HEADS * MLA_ROPE
    ww = MLA_HEADS * (MLA_NOPE + MLA_V)
    gw = D_MODEL // 2
    gb_blk = COL_GB // gw
    rows = MLA_HEADS * sq
    est = (2 * (sq * (lat_w + rope_w) * 2 + past * (MLA_KV_LORA + V7X_LANES) * 4 + MLA_KV_LORA * ww * 2
                + 4 * sq * D_MODEL * 4) + past * (MLA_KV_LORA + V7X_LANES) * 2 + 6 * rows * past * 4
           + 4 * rows * MLA_KV_LORA * 4 + (4 << 20))
    return pl.pallas_call(
        functools.partial(_attn_sample_kernel, sq=sq),
        grid=(batch,),
        in_specs=[pl.BlockSpec((sq, lat_w), lambda b: (b, 0)),
                  pl.BlockSpec((sq, rope_w), lambda b: (b, 0)),
                  pl.BlockSpec((None, None, past, MLA_KV_LORA), lambda b: (l, b, 0, 0)),
                  pl.BlockSpec((None, None, past, MLA_ROPE), lambda b: (l, b, 0, 0)),
                  pl.BlockSpec((sq, MLA_KV_LORA), lambda b: (b, 0)),
                  pl.BlockSpec((sq, MLA_ROPE), lambda b: (b, 0)),
                  pl.BlockSpec((sq, gw), lambda b: (b, gb_blk)),
                  pl.BlockSpec((sq, gw), lambda b: (b, gb_blk + 1)),
                  pl.BlockSpec((None, MLA_KV_LORA, ww), lambda b: (l, 0, 0))],
        out_specs=pl.BlockSpec((sq, D_MODEL), lambda b: (b, 0)),
        out_shape=jax.ShapeDtypeStruct((batch * sq, D_MODEL), BF16),
        compiler_params=_cparams(("parallel",), est),
        name="mla_attention_sample",
    )(q_lat, q_rope, cache_ckv, cache_kr, ckv_new, kr_new, proj, proj, w_ukv_r)


def _merge_kernel(a_ref, b_ref, x_ref, g1_ref, w_ref, o_ref):
    m = (a_ref[...].astype(F32) + b_ref[...].astype(F32)).astype(BF16)
    mix = jnp.dot(m, w_ref[...], preferred_element_type=F32)
    o_ref[...] = _gate_res(x_ref[...], g1_ref[...], mix)


def _merge_call(a_part, b_part, x, mod, w_o_b, l, tm, groups):
    m = x.shape[0]
    est = (2 * 2 * tm * D_MODEL * 2 + 4 * tm * D_MODEL * 4 + D_MODEL * D_MODEL * 2 + 3 * tm * D_MODEL * 4
           + (4 << 20))
    return pl.pallas_call(
        _merge_kernel,
        grid=(m // tm,),
        in_specs=[pl.BlockSpec((tm, D_MODEL), lambda i: (i, 0)),
                  pl.BlockSpec((tm, D_MODEL), lambda i: (i, 0)),
                  pl.BlockSpec((tm, D_MODEL), lambda i: (i, 0)),
                  pl.BlockSpec((None, None, None, groups, D_MODEL), lambda i: (l, 2, i, 0, 0)),
                  pl.BlockSpec((None, D_MODEL, D_MODEL), lambda i: (l, 0, 0), pipeline_mode=pl.Buffered(1))],
        out_specs=pl.BlockSpec((tm, D_MODEL), lambda i: (i, 0)),
        out_shape=jax.ShapeDtypeStruct((m, D_MODEL), F32),
        compiler_params=_cparams(("parallel",), est),
        name="merge_out_proj",
    )(a_part, b_part, x, mod, w_o_b)


def _ffn_kernel(x_ref, sc_ref, sh_ref, g2_ref, gn_ref, fg_ref, wg_ref, wu_ref, wo_ref, o_ref, h_scr, acc_scr,
                *, nh, final):
    j = pl.program_id(1)

    @pl.when(j == 0)
    def _():
        h_scr[...] = _norm_mod(x_ref[...], gn_ref[...], sc_ref[...], sh_ref[...]).astype(BF16)
        acc_scr[...] = jnp.zeros(acc_scr.shape, F32)

    hb = h_scr[...]
    gate = jnp.dot(hb, wg_ref[...], preferred_element_type=F32)
    up = jnp.dot(hb, wu_ref[...], preferred_element_type=F32)
    act = ((gate * _sigmoid(gate)) * up).astype(BF16)
    acc_scr[...] += jnp.dot(act, wo_ref[...], preferred_element_type=F32)

    @pl.when(j == nh - 1)
    def _():
        y = _gate_res(x_ref[...], g2_ref[...], acc_scr[...])
        o_ref[...] = _rms(y, fg_ref[...]) if final else y


def _ffn_call(x, mod, norm_g, norm_final, w_in_b, w_out_b, l, tm, groups):
    m = x.shape[0]
    th = FFN_TH
    nh = FFN_HIDDEN // th
    est = (4 * tm * D_MODEL * 4 + tm * D_MODEL * 2 + tm * D_MODEL * 4 + 2 * 3 * D_MODEL * th * 2
           + 4 * tm * th * 4 + tm * D_MODEL * 4 + (4 << 20))
    mod_spec = lambda k: pl.BlockSpec((None, None, None, groups, D_MODEL), lambda i, j: (l, k, i, 0, 0))
    return pl.pallas_call(
        functools.partial(_ffn_kernel, nh=nh, final=(l == DEPTH - 1)),
        grid=(m // tm, nh),
        in_specs=[pl.BlockSpec((tm, D_MODEL), lambda i, j: (i, 0)),
                  mod_spec(4), mod_spec(3), mod_spec(5),
                  pl.BlockSpec((None, 1, D_MODEL), lambda i, j: (l, 0, 0)),
                  pl.BlockSpec((1, D_MODEL), lambda i, j: (0, 0)),
                  pl.BlockSpec((None, D_MODEL, th), lambda i, j: (l, 0, j)),
                  pl.BlockSpec((None, D_MODEL, th), lambda i, j: (l, 0, nh + j)),
                  pl.BlockSpec((None, th, D_MODEL), lambda i, j: (l, j, 0))],
        out_specs=pl.BlockSpec((tm, D_MODEL), lambda i, j: (i, 0)),
        out_shape=jax.ShapeDtypeStruct((m, D_MODEL), F32),
        scratch_shapes=[pltpu.VMEM((tm, D_MODEL), BF16), pltpu.VMEM((tm, D_MODEL), F32)],
        compiler_params=_cparams(("parallel", "arbitrary"), est),
        name="ffn_swiglu",
    )(x, mod, mod, mod, norm_g, norm_final, w_in_b, w_in_b, w_out_b)


def _rope_tables(pos, dim):
    inv = jnp.exp(-math.log(ROPE_BASE) * jnp.arange(0, dim, 2, dtype=F32) / dim)
    ang = pos.astype(F32)[:, None] * inv[None, :]
    return jnp.cos(ang), jnp.sin(ang)


def _mla_rope_tables(pos, reps):
    cos, sin = _rope_tables(pos, MLA_ROPE)
    z = jnp.zeros_like(cos)
    c = jnp.concatenate([cos, z, cos, z], axis=-1)
    s = jnp.concatenate([-sin, z, sin, z], axis=-1)
    return jnp.tile(c, (reps, 1)), jnp.tile(s, (reps, 1))


def _prep_weights(w_uq, w_ukv):
    uq = w_uq.reshape(DEPTH, MLA_Q_LORA, MLA_HEADS, MLA_NOPE + MLA_ROPE)
    uq = jnp.concatenate([uq[..., :MLA_NOPE], _pack_rope_lanes(uq[..., MLA_NOPE:])], axis=-1)
    w_uq_r = uq.reshape(DEPTH, MLA_Q_LORA, MLA_HEADS * HEAD_QK).astype(BF16)
    ukv = w_ukv.reshape(DEPTH, MLA_KV_LORA, MLA_HEADS, MLA_NOPE + MLA_V)
    w_ukv_r = jnp.concatenate([ukv[..., :MLA_NOPE].reshape(DEPTH, MLA_KV_LORA, -1),
                               ukv[..., MLA_NOPE:].reshape(DEPTH, MLA_KV_LORA, -1)], axis=-1).astype(BF16)
    return w_uq_r, w_ukv_r


def _layer_prompt(l, x, mod, tm, batch, seq, ret_tabs, mla_tabs, zero_state, log_g, W, stacks):
    proj, kr_raw = _inproj_call(x, mod, W["norm_mix"], W["w_in_t"], l, IN_TM_PROMPT, 1, IN_TM_PROMPT // tm)
    ckv_prev, kr_prev, st_prev = stacks
    a_part, st_stack = _ret_call(proj, log_g, ret_tabs[0], ret_tabs[1], zero_state, 0, batch, seq,
                                 RET_L_PROMPT, RET_HEADS_PROMPT, l, st_prev)
    q_cat, ckv_stack, kr128, kr_stack = _mla_q_call(proj, kr_raw, W["q_norm"], W["kv_norm"], mla_tabs[0],
                                                    mla_tabs[1], W["w_uq_r"], l, tm, ckv_prev, kr_prev)
    k_cat, v = _mla_kv_call(ckv_stack, kr128, W["w_ukv_r"], l, batch * seq, tm)
    b_part = _attn_prompt_call(q_cat, k_cat, v, proj, batch, seq, ATT_TQ)
    x = _merge_call(a_part, b_part, x, mod, W["w_o"], l, tm, 1)
    x = _ffn_call(x, mod, W["norm_ffn"], W["norm_final"], W["w_ffn_in"], W["w_ffn_out"], l, tm, 1)
    return x, (ckv_stack, kr_stack, st_stack)


def _layer_sample(l, x, mod, groups, tm, batch, seq, ret_tabs, mla_tabs, state_ret, cache_ckv, cache_kr,
                  log_g, W, st_prev):
    proj, kr_raw = _inproj_call(x, mod, W["norm_mix"], W["w_in_t"], l, tm, groups, 1)
    a_part, st_stack = _ret_call(proj, log_g, ret_tabs[0], ret_tabs[1], state_ret, l, batch, seq, seq,
                                 RET_HEADS_SAMPLE, l, st_prev)
    q_lat, q_rope, ckv, kr = _mla_q_sample_call(proj, kr_raw, W["q_norm"], W["kv_norm"], mla_tabs[0],
                                                mla_tabs[1], W["w_uq_r"], W["w_ukv_r"], l)
    b_part = _attn_sample_call(q_lat, q_rope, ckv, kr, cache_ckv, cache_kr, proj, W["w_ukv_r"], l, batch, seq)
    x = _merge_call(a_part, b_part, x, mod, W["w_o"], l, tm, groups)
    x = _ffn_call(x, mod, W["norm_ffn"], W["norm_final"], W["w_ffn_in"], W["w_ffn_out"], l, tm, groups)
    return x, ckv, kr, st_stack


def kernel(x_prompt, x_sample, c_prompt, c_sample, cache_mla_ckv, cache_mla_krope, state_ret, w_ada, b_ada,
           norm_mix, norm_ffn, w_in, mla_q_norm, w_uq, mla_kv_norm, w_ukv, w_o, w_ffn_in, w_ffn_out,
           norm_final):
    bp, sp, _ = x_prompt.shape
    bs, ss, _ = x_sample.shape
    tm = TOKEN_TILE
    assert sp % IN_TM_PROMPT == 0 and (bs * ss) % tm == 0 and tm % ss == 0

    w_uq_r, w_ukv_r = _prep_weights(w_uq, w_ukv)
    W = dict(w_in_t=jnp.swapaxes(w_in, 1, 2), w_uq_r=w_uq_r, w_ukv_r=w_ukv_r,
             w_o=w_o.astype(BF16), w_ffn_in=w_ffn_in.astype(BF16), w_ffn_out=w_ffn_out.astype(BF16),
             norm_mix=norm_mix.reshape(DEPTH, 1, D_MODEL), norm_ffn=norm_ffn.reshape(DEPTH, 1, D_MODEL),
             norm_final=norm_final.reshape(1, D_MODEL),
             q_norm=mla_q_norm.reshape(DEPTH, 1, MLA_Q_LORA), kv_norm=mla_kv_norm.reshape(DEPTH, 1, MLA_KV_LORA))

    c_rows = -(-(bp + bs) // ADA_ROW_ALIGN) * ADA_ROW_ALIGN
    c_all = jnp.concatenate([c_prompt, c_sample, jnp.zeros((c_rows - bp - bs, D_MODEL), F32)], axis=0)
    mod_all = _ada_call(c_all, w_ada, b_ada)[:, :bp + bs]
    mod_all = mod_all.reshape(DEPTH, bp + bs, 6, D_MODEL).transpose(0, 2, 1, 3)
    tiles_per_batch = sp // tm
    mod_p = jnp.repeat(mod_all[:, :, :bp], tiles_per_batch, axis=2)[:, :, :, None, :]
    groups_s = tm // ss
    mod_s = mod_all[:, :, bp:].reshape(DEPTH, 6, (bs * ss) // tm, groups_s, D_MODEL)

    log_g = jnp.log1p(-jnp.exp2(-RET_GAMMA_EXP0 - jnp.arange(RET_HEADS, dtype=F32)))
    pos_p = jnp.arange(sp)
    pos_s = PAST_LEN + jnp.arange(ss)
    ret_tabs_p = _rope_tables(pos_p, RET_DK)
    ret_tabs_s = _rope_tables(pos_s, RET_DK)
    mla_tabs_p = _mla_rope_tables(pos_p, bp)
    mla_tabs_s = _mla_rope_tables(pos_s, bs)
    zero_state = jnp.zeros((1, bp, RET_HEADS, RET_DK, RET_DV), F32)

    xp = x_prompt.reshape(bp * sp, D_MODEL)
    xs = x_sample.reshape(bs * ss, D_MODEL)
    stacks_p = (None, None, None)
    st_s = None
    ckv_s, kr_s = [], []
    for l in range(DEPTH):
        xp, stacks_p = _layer_prompt(l, xp, mod_p, tm, bp, sp, ret_tabs_p, mla_tabs_p, zero_state, log_g, W,
                                     stacks_p)
        xs, ckv, kr, st_s = _layer_sample(l, xs, mod_s, groups_s, tm, bs, ss, ret_tabs_s, mla_tabs_s, state_ret,
                                          cache_mla_ckv, cache_mla_krope, log_g, W, st_s)
        ckv_s.append(ckv.reshape(bs, ss, MLA_KV_LORA))
        kr_s.append(kr.reshape(bs, ss, MLA_ROPE))

    y_prompt = xp.reshape(bp, sp, D_MODEL)
    y_sample = xs.reshape(bs, ss, D_MODEL)
    ckv_p, kr_p, st_p = stacks_p
    return (y_prompt, y_sample, ckv_p.reshape(DEPTH, bp, sp, MLA_KV_LORA), kr_p.reshape(DEPTH, bp, sp, MLA_ROPE),
            st_p, jnp.stack(ckv_s), jnp.stack(kr_s), st_s)
```

```python
import functools
import math

import jax
import jax.numpy as jnp
from jax import lax
from jax.experimental import pallas as pl
from jax.experimental.pallas import tpu as pltpu

D_MODEL = 2048
DEPTH = 4
PAST_LEN = 1024
CHUNK = 64
RET_HEADS = 8
RET_DK = D_MODEL // RET_HEADS
RET_DV = D_MODEL // RET_HEADS
MLA_HEADS = 16
MLA_Q_LORA = D_MODEL // 4
MLA_KV_LORA = D_MODEL // 4
MLA_NOPE = 128
MLA_ROPE = 64
MLA_V = D_MODEL // MLA_HEADS
FFN_HIDDEN = -(-8 * D_MODEL // (3 * 256)) * 256
ROPE_BASE = 10000.0
RET_GAMMA_EXP0 = 5.0
RMS_EPS = 1e-6
GN_EPS = 1e-5
NEG_INF = -1e30

F32 = jnp.float32
BF16 = jnp.bfloat16

V7X_LANES = 128
V7X_VMEM_LIMIT_CAP = 56 * 1024 * 1024

COL_RQ, COL_RK, COL_RV, COL_RG = 0, D_MODEL, 2 * D_MODEL, 3 * D_MODEL
COL_DQ = 4 * D_MODEL
COL_DKV = COL_DQ + MLA_Q_LORA
COL_GA = COL_DKV + MLA_KV_LORA
COL_GB = COL_GA + D_MODEL
MAIN_COLS = COL_GB + D_MODEL
HEAD_QK = 2 * V7X_LANES
HALF_ROPE = MLA_ROPE // 2
ADA_ROW_ALIGN = 16
TOKEN_TILE = 512
IN_TM_PROMPT = 2048
IN_TN = 512
ADA_TN = 1024
FFN_TH = 512
FFN_TH_SAMPLE = 256
RET_L_PROMPT = 256
RET_HEADS_PROMPT = 2
RET_HEADS_SAMPLE = 4
ATT_TQ = 1024
ATT_TK = 1024
ATT_AHEAD = 2
ATT_SPLIT = 4
QK_SCALE_LOG2 = (MLA_NOPE + MLA_ROPE) ** -0.5 * math.log2(math.e)


def _cparams(sem, est_bytes):
    return pltpu.CompilerParams(dimension_semantics=sem,
                                vmem_limit_bytes=min(int(est_bytes), V7X_VMEM_LIMIT_CAP))


def _sigmoid(x):
    return jax.nn.sigmoid(x)


def _rms(x, g):
    return x * lax.rsqrt(jnp.mean(x * x, axis=-1, keepdims=True) + RMS_EPS) * g


def _norm_mod(x, g, sc, sh):
    tm, d = x.shape
    groups = sc.shape[0]
    y = _rms(x, g)
    if groups == 1:
        return y * (1.0 + sc) + sh
    y3 = y.reshape(groups, tm // groups, d)
    return (y3 * (1.0 + sc[:, None, :]) + sh[:, None, :]).reshape(tm, d)


def _gate_res(x, gate, upd):
    tm, n = x.shape
    groups = gate.shape[0]
    if groups == 1:
        return x + gate * upd
    return x + (gate[:, None, :] * upd.reshape(groups, tm // groups, n)).reshape(tm, n)


def _pack_rope_lanes(x):
    z = jnp.zeros(x.shape[:-1] + (HALF_ROPE,), x.dtype)
    return jnp.concatenate([x[..., :HALF_ROPE], z, x[..., HALF_ROPE:], z], axis=-1)


def _unpack_rope_lanes(x):
    return jnp.concatenate([x[..., :HALF_ROPE], x[..., 2 * HALF_ROPE:3 * HALF_ROPE]], axis=-1)


def _ada_kernel(c_ref, w_ref, b_ref, o_ref):
    c = c_ref[...]
    a = (c * _sigmoid(c)).astype(BF16)
    o_ref[0] = jnp.dot(a, w_ref[0].astype(BF16), preferred_element_type=F32) + b_ref[0]


def _ada_call(c_all, w_ada, b_ada):
    nb = c_all.shape[0]
    n = w_ada.shape[-1]
    tn = ADA_TN
    est = 2 * (D_MODEL * tn * 4) + D_MODEL * tn * 2 + 4 * nb * (D_MODEL + 2 * tn) * 4 + (4 << 20)
    return pl.pallas_call(
        _ada_kernel,
        grid=(DEPTH, n // tn),
        in_specs=[pl.BlockSpec((nb, D_MODEL), lambda l, j: (0, 0)),
                  pl.BlockSpec((1, D_MODEL, tn), lambda l, j: (l, 0, j)),
                  pl.BlockSpec((1, 1, tn), lambda l, j: (l, 0, j))],
        out_specs=pl.BlockSpec((1, nb, tn), lambda l, j: (l, 0, j)),
        out_shape=jax.ShapeDtypeStruct((DEPTH, nb, n), F32),
        compiler_params=_cparams(("parallel", "parallel"), est),
        name="ada_mod",
    )(c_all, w_ada, b_ada.reshape(DEPTH, 1, n))


def _inproj_kernel(x_ref, sc_ref, sh_ref, g_ref, w_ref, wn_ref, wkr_ref, o_ref, kr_ref, h_scr, *, n_main):
    j = pl.program_id(1)
    nt = (((1,), (1,)), ((), ()))

    @pl.when(j == 0)
    def _():
        wkr = wkr_ref[...]
        z = jnp.zeros((HALF_ROPE, wkr.shape[1]), wkr.dtype)
        wkr = jnp.concatenate([wkr[:HALF_ROPE], z, wkr[HALF_ROPE:], z], axis=0).astype(BF16)
        tm = x_ref.shape[0]
        for r0 in range(0, tm, TOKEN_TILE):
            rs = pl.ds(r0, TOKEN_TILE)
            hb = _norm_mod(x_ref[rs, :], g_ref[...], sc_ref[...], sh_ref[...]).astype(BF16)
            h_scr[rs, :] = hb
            kr_ref[rs, :] = lax.dot_general(hb, wkr, nt, preferred_element_type=F32)

    @pl.when(j < n_main)
    def _():
        o_ref[...] = lax.dot_general(h_scr[...], w_ref[...].astype(BF16), nt,
                                     preferred_element_type=F32).astype(BF16)

    @pl.when(j >= n_main)
    def _():
        w = jnp.concatenate([w_ref[MLA_ROPE:, :], wn_ref[...]], axis=0).astype(BF16)
        o_ref[...] = lax.dot_general(h_scr[...], w, nt, preferred_element_type=F32).astype(BF16)


def _inproj_call(x, mod, norm_g, w_in_t, l, tm, groups, mod_stride):
    m = x.shape[0]
    n_main = COL_GA // IN_TN
    sub = IN_TN // MLA_ROPE
    assert groups == 1 or tm == TOKEN_TILE
    est = (tm * D_MODEL * 4 + tm * D_MODEL * 2 + 2 * D_MODEL * (IN_TN + 2 * MLA_ROPE) * 4
           + D_MODEL * IN_TN * 2 + 3 * tm * IN_TN * 4 + 6 * TOKEN_TILE * D_MODEL * 4 + (4 << 20))
    mod_spec = lambda k: pl.BlockSpec((None, None, None, groups, D_MODEL),
                                      lambda i, j: (l, k, i * mod_stride, 0, 0))
    return pl.pallas_call(
        functools.partial(_inproj_kernel, n_main=n_main),
        grid=(m // tm, MAIN_COLS // IN_TN),
        in_specs=[pl.BlockSpec((tm, D_MODEL), lambda i, j: (i, 0), pipeline_mode=pl.Buffered(1)),
                  mod_spec(1), mod_spec(0),
                  pl.BlockSpec((None, 1, D_MODEL), lambda i, j: (l, 0, 0)),
                  pl.BlockSpec((None, IN_TN, D_MODEL), lambda i, j: (l, j, 0)),
                  pl.BlockSpec((None, MLA_ROPE, D_MODEL),
                               lambda i, j: (l, (jnp.maximum(j, n_main) + 1) * sub, 0)),
                  pl.BlockSpec((None, MLA_ROPE, D_MODEL), lambda i, j: (l, COL_GA // MLA_ROPE, 0))],
        out_specs=[pl.BlockSpec((tm, IN_TN), lambda i, j: (i, j)),
                   pl.BlockSpec((tm, V7X_LANES), lambda i, j: (i, 0))],
        out_shape=[jax.ShapeDtypeStruct((m, MAIN_COLS), BF16),
                   jax.ShapeDtypeStruct((m, V7X_LANES), F32)],
        scratch_shapes=[pltpu.VMEM((tm, D_MODEL), BF16)],
        compiler_params=_cparams(("parallel", "arbitrary"), est),
        name="in_proj",
    )(x, mod, mod, norm_g, w_in_t, w_in_t, w_in_t)


def _ret_kernel(lg_ref, q_ref, k_ref, v_ref, rg_ref, ga_ref, cos_ref, sin_ref, s0_ref, *rest,
                chunk_len, heads):
    a_ref, st_ref, dm_scr = rest[-3:]
    hg = pl.program_id(1)
    c = pl.program_id(2)
    L = chunk_len
    lgs = [lg_ref[hg * heads + t] for t in range(heads)]

    @pl.when(c == 0)
    def _():
        st_ref[...] = s0_ref[...]
        ri = lax.broadcasted_iota(jnp.int32, (L, L), 0)
        ci = lax.broadcasted_iota(jnp.int32, (L, L), 1)
        diff = (ri - ci).astype(F32)
        for t in range(heads):
            dm_scr[t] = jnp.where(diff >= 0, jnp.exp(jnp.maximum(diff, 0.0) * lgs[t]), 0.0)

    cos = cos_ref[...]
    sin = sin_ref[...]
    half = RET_DK // 2
    idx = lax.broadcasted_iota(jnp.int32, (L, 1), 0).astype(F32)
    nt = (((1,), (1,)), ((), ()))
    tn = (((0,), (0,)), ((), ()))

    def rope(x):
        x1, x2 = x[:, :half], x[:, half:]
        return jnp.concatenate([x1 * cos - x2 * sin, x1 * sin + x2 * cos], axis=-1)

    stage = []
    for t in range(heads):
        cs = pl.ds(t * RET_DK, RET_DK)
        q = rope(q_ref[:, cs].astype(F32))
        k = rope(k_ref[:, cs].astype(F32)) * (RET_DK ** -0.5)
        vb = v_ref[:, cs]
        qb = q.astype(BF16)
        zeta = jnp.exp((L - 1.0 - idx) * lgs[t])
        g_l = jnp.exp(jnp.full((1, 1), float(L), F32) * lgs[t])
        st = st_ref[0, t]
        scores = lax.dot_general(qb, k.astype(BF16), nt, preferred_element_type=F32)
        cross = jnp.dot(qb, st.astype(BF16), preferred_element_type=F32)
        st_ref[0, t] = st * g_l + lax.dot_general((k * zeta).astype(BF16), vb, tn, preferred_element_type=F32)
        stage.append((scores, cross, vb))

    outs = []
    for t in range(heads):
        scores, cross, vb = stage[t]
        xi = jnp.exp((idx + 1.0) * lgs[t])
        outs.append(jnp.dot((scores * dm_scr[t]).astype(BF16), vb, preferred_element_type=F32) + cross * xi)

    for t in range(heads):
        cs = pl.ds(t * RET_DK, RET_DK)
        o = outs[t]
        mu = jnp.mean(o, axis=-1, keepdims=True)
        d = o - mu
        var = jnp.mean(d * d, axis=-1, keepdims=True)
        on = d * lax.rsqrt(var + GN_EPS)
        rg = rg_ref[:, cs].astype(F32)
        a_ref[:, cs] = (_sigmoid(ga_ref[:, cs].astype(F32)) * ((rg * _sigmoid(rg)) * on)).astype(BF16)


def _stacked_out(stack_prev, in_specs, args, out_index):
    if stack_prev is None:
        return {}
    in_specs.append(pl.BlockSpec(memory_space=pl.ANY))
    args.append(stack_prev)
    return {len(args) - 1: out_index}


def _ret_call(proj, log_g, cos, sin, state0, state_layer, batch, seq, chunk_len, heads, l, stack_prev):
    L = chunk_len
    nc = seq // L
    w = heads * RET_DK
    assert COL_GA % w == 0 and RET_HEADS % heads == 0
    col = lambda base: (lambda b, h, c: (b * nc + c, base // w + h))
    blk = lambda base: pl.BlockSpec((L, w), col(base))
    nh = RET_HEADS
    est = (2 * 6 * L * w * 4 + 4 * heads * RET_DK * RET_DV * 4 + heads * L * L * 4
           + 8 * heads * L * max(L, RET_DK) * 4 + (4 << 20))
    in_specs = [pl.BlockSpec(memory_space=pltpu.SMEM),
                blk(COL_RQ), blk(COL_RK), blk(COL_RV), blk(COL_RG), blk(COL_GA),
                pl.BlockSpec((L, RET_DK // 2), lambda b, h, c: (c, 0)),
                pl.BlockSpec((L, RET_DK // 2), lambda b, h, c: (c, 0)),
                pl.BlockSpec((None, 1, heads, RET_DK, RET_DV), lambda b, h, c: (state_layer, b, h, 0, 0))]
    args = [log_g, proj, proj, proj, proj, proj, cos, sin, state0]
    aliases = _stacked_out(stack_prev, in_specs, args, 1)
    return pl.pallas_call(
        functools.partial(_ret_kernel, chunk_len=L, heads=heads),
        grid=(batch, nh // heads, nc),
        in_specs=in_specs,
        out_specs=[pl.BlockSpec((L, w), lambda b, h, c: (b * nc + c, h)),
                   pl.BlockSpec((None, 1, heads, RET_DK, RET_DV), lambda b, h, c: (l, b, h, 0, 0))],
        out_shape=[jax.ShapeDtypeStruct((batch * seq, D_MODEL), BF16),
                   jax.ShapeDtypeStruct((DEPTH, batch, nh, RET_DK, RET_DV), F32)],
        scratch_shapes=[pltpu.VMEM((heads, L, L), F32)],
        input_output_aliases=aliases,
        compiler_params=_cparams(("parallel", "parallel", "arbitrary"), est),
        name="retention",
    )(*args)


def _rope128(x, c, s):
    return x * c + pltpu.roll(x, V7X_LANES // 2, 1) * s


def _mla_q_kernel(dq_ref, dkv_ref, kr_ref, qn_ref, kvn_ref, c_ref, s_ref, wuq_ref, *rest):
    q_out, ckv_out, kro_out, kr64_out = rest[-4:]
    c = c_ref[...]
    s = s_ref[...]
    cq = _rms(dq_ref[...].astype(F32), qn_ref[...]).astype(BF16)
    for h in range(MLA_HEADS):
        lo = h * HEAD_QK
        qh = jnp.dot(cq, wuq_ref[:, lo:lo + HEAD_QK], preferred_element_type=F32) * QK_SCALE_LOG2
        q_out[:, lo:lo + MLA_NOPE] = qh[:, :MLA_NOPE].astype(BF16)
        q_out[:, lo + MLA_NOPE:lo + HEAD_QK] = _rope128(qh[:, MLA_NOPE:], c, s).astype(BF16)
    ckv_out[...] = _rms(dkv_ref[...].astype(F32), kvn_ref[...])
    kr = _rope128(kr_ref[...], c, s)
    kro_out[...] = kr
    kr64_out[...] = _unpack_rope_lanes(kr)


def _mla_q_call(proj, kr_raw, q_norm, kv_norm, rope_c, rope_s, w_uq_r, l, tm, ckv_prev, kr_prev):
    m = proj.shape[0]
    dq_blk = COL_DQ // MLA_Q_LORA
    qw = MLA_HEADS * HEAD_QK
    est = (2 * (2 * tm * MLA_Q_LORA * 4 + 3 * tm * V7X_LANES * 4) + 2 * MLA_Q_LORA * qw * 2
           + 2 * tm * qw * 2 + 2 * tm * MLA_KV_LORA * 4 + 2 * tm * V7X_LANES * 4 + 8 * tm * HEAD_QK * 4
           + (4 << 20))
    in_specs = [pl.BlockSpec((tm, MLA_Q_LORA), lambda i: (i, dq_blk)),
                pl.BlockSpec((tm, MLA_KV_LORA), lambda i: (i, dq_blk + 1)),
                pl.BlockSpec((tm, V7X_LANES), lambda i: (i, 0)),
                pl.BlockSpec((None, 1, MLA_Q_LORA), lambda i: (l, 0, 0)),
                pl.BlockSpec((None, 1, MLA_KV_LORA), lambda i: (l, 0, 0)),
                pl.BlockSpec((tm, V7X_LANES), lambda i: (i, 0)),
                pl.BlockSpec((tm, V7X_LANES), lambda i: (i, 0)),
                pl.BlockSpec((None, MLA_Q_LORA, qw), lambda i: (l, 0, 0))]
    args = [proj, proj, kr_raw, q_norm, kv_norm, rope_c, rope_s, w_uq_r]
    aliases = _stacked_out(ckv_prev, in_specs, args, 1)
    aliases.update(_stacked_out(kr_prev, in_specs, args, 3))
    return pl.pallas_call(
        _mla_q_kernel,
        grid=(m // tm,),
        in_specs=in_specs,
        out_specs=[pl.BlockSpec((tm, qw), lambda i: (i, 0)),
                   pl.BlockSpec((None, tm, MLA_KV_LORA), lambda i: (l, i, 0)),
                   pl.BlockSpec((tm, V7X_LANES), lambda i: (i, 0)),
                   pl.BlockSpec((None, tm, MLA_ROPE), lambda i: (l, i, 0))],
        out_shape=[jax.ShapeDtypeStruct((m, qw), BF16),
                   jax.ShapeDtypeStruct((DEPTH, m, MLA_KV_LORA), F32),
                   jax.ShapeDtypeStruct((m, V7X_LANES), F32),
                   jax.ShapeDtypeStruct((DEPTH, m, MLA_ROPE), F32)],
        input_output_aliases=aliases,
        compiler_params=_cparams(("parallel",), est),
        name="mla_latents",
    )(*args)


def _mla_q_sample_kernel(dq_ref, dkv_ref, kr_ref, qn_ref, kvn_ref, c_ref, s_ref, wuq_ref, wukv_ref,
                         qlat_out, qr_out, ckv_out, kro_out):
    c = c_ref[...]
    s = s_ref[...]
    cq = _rms(dq_ref[...].astype(F32), qn_ref[...]).astype(BF16)
    for h in range(MLA_HEADS):
        lo = h * HEAD_QK
        qh = jnp.dot(cq, wuq_ref[:, lo:lo + HEAD_QK], preferred_element_type=F32) * QK_SCALE_LOG2
        w_uk = wukv_ref[:, h * MLA_NOPE:(h + 1) * MLA_NOPE]
        q_lat = lax.dot_general(qh[:, :MLA_NOPE].astype(BF16), w_uk, (((1,), (1,)), ((), ())),
                                preferred_element_type=F32)
        qlat_out[:, h * MLA_KV_LORA:(h + 1) * MLA_KV_LORA] = q_lat.astype(BF16)
        qr_out[:, h * MLA_ROPE:(h + 1) * MLA_ROPE] = _unpack_rope_lanes(
            _rope128(qh[:, MLA_NOPE:], c, s)).astype(BF16)
    ckv_out[...] = _rms(dkv_ref[...].astype(F32), kvn_ref[...])
    kro_out[...] = _unpack_rope_lanes(_rope128(kr_ref[...], c, s))


def _mla_q_sample_call(proj, kr_raw, q_norm, kv_norm, rope_c, rope_s, w_uq_r, w_ukv_r, l):
    m = proj.shape[0]
    dq_blk = COL_DQ // MLA_Q_LORA
    qw = MLA_HEADS * HEAD_QK
    ww = MLA_HEADS * (MLA_NOPE + MLA_V)
    lat_w = MLA_HEADS * MLA_KV_LORA
    rope_w = MLA_HEADS * MLA_ROPE
    est = (2 * (2 * m * MLA_Q_LORA * 4 + 3 * m * V7X_LANES * 4) + 2 * MLA_Q_LORA * (qw + ww) * 2
           + 2 * m * (lat_w + rope_w) * 2 + 2 * m * (MLA_KV_LORA + MLA_ROPE) * 4 + 8 * m * MLA_KV_LORA * 4
           + (4 << 20))
    full = lambda shape: pl.BlockSpec(shape, lambda i: (0,) * len(shape))
    return pl.pallas_call(
        _mla_q_sample_kernel,
        grid=(1,),
        in_specs=[pl.BlockSpec((m, MLA_Q_LORA), lambda i: (0, dq_blk)),
                  pl.BlockSpec((m, MLA_KV_LORA), lambda i: (0, dq_blk + 1)),
                  full((m, V7X_LANES)),
                  pl.BlockSpec((None, 1, MLA_Q_LORA), lambda i: (l, 0, 0)),
                  pl.BlockSpec((None, 1, MLA_KV_LORA), lambda i: (l, 0, 0)),
                  full((m, V7X_LANES)), full((m, V7X_LANES)),
                  pl.BlockSpec((None, MLA_Q_LORA, qw), lambda i: (l, 0, 0)),
                  pl.BlockSpec((None, MLA_KV_LORA, ww), lambda i: (l, 0, 0))],
        out_specs=[full((m, lat_w)), full((m, rope_w)), full((m, MLA_KV_LORA)), full((m, MLA_ROPE))],
        out_shape=[jax.ShapeDtypeStruct((m, lat_w), BF16),
                   jax.ShapeDtypeStruct((m, rope_w), BF16),
                   jax.ShapeDtypeStruct((m, MLA_KV_LORA), F32),
                   jax.ShapeDtypeStruct((m, MLA_ROPE), F32)],
        compiler_params=_cparams(("arbitrary",), est),
        name="mla_latents_sample",
    )(proj, proj, kr_raw, q_norm, kv_norm, rope_c, rope_s, w_uq_r, w_ukv_r)


def _mla_kv_kernel(ckv_ref, kr_ref, wukv_ref, kcat_out, v_out):
    cb = ckv_ref[...].astype(BF16)
    krb = kr_ref[...].astype(BF16)
    for hp in range(MLA_HEADS // 2):
        kk = jnp.dot(cb, wukv_ref[:, hp * HEAD_QK:(hp + 1) * HEAD_QK], preferred_element_type=F32)
        for t in range(2):
            lo = (2 * hp + t) * HEAD_QK
            kcat_out[:, lo:lo + MLA_NOPE] = kk[:, t * MLA_NOPE:(t + 1) * MLA_NOPE].astype(BF16)
            kcat_out[:, lo + MLA_NOPE:lo + HEAD_QK] = krb
    nv = MLA_HEADS * MLA_NOPE
    for hp in range(nv // HEAD_QK):
        lo = hp * HEAD_QK
        v_out[:, lo:lo + HEAD_QK] = jnp.dot(cb, wukv_ref[:, nv + lo:nv + lo + HEAD_QK],
                                            preferred_element_type=F32).astype(BF16)


def _mla_kv_call(ckv_stack, kr128, w_ukv_r, l, rows, tm):
    kw = MLA_HEADS * HEAD_QK
    vw = MLA_HEADS * MLA_V
    ww = MLA_HEADS * (MLA_NOPE + MLA_V)
    est = (2 * tm * (MLA_KV_LORA + V7X_LANES) * 4 + 2 * MLA_KV_LORA * ww * 2 + 2 * tm * (kw + vw) * 2
           + 8 * tm * HEAD_QK * 4 + (4 << 20))
    return pl.pallas_call(
        _mla_kv_kernel,
        grid=(rows // tm,),
        in_specs=[pl.BlockSpec((None, tm, MLA_KV_LORA), lambda i: (l, i, 0)),
                  pl.BlockSpec((tm, V7X_LANES), lambda i: (i, 0)),
                  pl.BlockSpec((None, MLA_KV_LORA, ww), lambda i: (l, 0, 0))],
        out_specs=[pl.BlockSpec((tm, kw), lambda i: (i, 0)),
                   pl.BlockSpec((tm, vw), lambda i: (i, 0))],
        out_shape=[jax.ShapeDtypeStruct((rows, kw), BF16),
                   jax.ShapeDtypeStruct((rows, vw), BF16)],
        compiler_params=_cparams(("parallel",), est),
        name="mla_kv_expand",
    )(ckv_stack, kr128, w_ukv_r)


def _scores(q, k_blk):
    return lax.dot_general(q, k_blk, (((1,), (1,)), ((), ())), preferred_element_type=F32)


def _softmax_tile(s, v_blk, m_prev, l_prev, acc_prev, tri):
    n_groups = s.shape[1] // V7X_LANES
    groups = [s[:, g * V7X_LANES:(g + 1) * V7X_LANES] for g in range(n_groups)]
    if tri is not None:
        n_tri = tri.shape[1] // V7X_LANES
        for t in range(n_tri):
            g = n_groups - n_tri + t
            groups[g] = jnp.where(tri[:, t * V7X_LANES:(t + 1) * V7X_LANES], groups[g], NEG_INF)
    m_new = jnp.maximum(m_prev, jnp.max(functools.reduce(jnp.maximum, groups), axis=-1, keepdims=True))
    alpha = jnp.exp2(m_prev - m_new)
    ps = [jnp.exp2(g - m_new) for g in groups]
    l_new = alpha * l_prev + functools.reduce(jnp.add, ps)
    p = jnp.concatenate([x.astype(BF16) for x in ps], axis=-1)
    acc_new = alpha * acc_prev + jnp.dot(p, v_blk, preferred_element_type=F32)
    return m_new, l_new, acc_new


def _attn_prompt_kernel(q_ref, k_ref, v_ref, gb_ref, o_ref, m_scr, l_scr, acc_scr, *, tq, tk, n_split):
    i = pl.program_id(2)
    rows = tq // n_split
    m_scr[...] = jnp.full(m_scr.shape, NEG_INF, F32)
    l_scr[...] = jnp.zeros(l_scr.shape, F32)
    acc_scr[...] = jnp.zeros(acc_scr.shape, F32)

    def scores(r, k_blk):
        return _scores(q_ref[pl.ds(r * rows, rows), :], k_blk)

    def update(r, s, v_blk, mask):
        rs = pl.ds(r * rows, rows)
        m_new, l_new, acc_new = _softmax_tile(s, v_blk, m_scr[rs, :], l_scr[rs, :], acc_scr[rs, :], mask)
        m_scr[rs, :] = m_new
        l_scr[rs, :] = l_new
        acc_scr[rs, :] = acc_new

    blocks_per_tile = tq // tk

    def body(j, carry):
        for d in range(blocks_per_tile):
            start = pl.multiple_of(j * tq + d * tk, tk)
            k_blk = k_ref[pl.ds(start, tk), :]
            v_blk = v_ref[pl.ds(start, tk), :]
            ss = [scores(r, k_blk) for r in range(min(ATT_AHEAD, n_split))]
            for r in range(n_split):
                if r + ATT_AHEAD < n_split:
                    ss.append(scores(r + ATT_AHEAD, k_blk))
                update(r, ss[r], v_blk, None)
        return carry

    lax.fori_loop(0, i, body, 0)

    tri = (lax.broadcasted_iota(jnp.int32, (rows, rows), 1) // CHUNK
           <= lax.broadcasted_iota(jnp.int32, (rows, rows), 0) // CHUNK)
    for d in range(blocks_per_tile):
        k0 = d * tk
        todo = []
        for r in range(n_split):
            r0, r1 = r * rows, (r + 1) * rows
            width = min(k0 + tk, r1) - k0
            if width <= 0:
                continue
            on_diagonal = k0 + width > r0
            assert not on_diagonal or (k0 + width == r1 and width >= rows)
            todo.append((r, width, tri if on_diagonal else None))
        start = pl.multiple_of(i * tq + k0, tk)
        diag_scores = lambda t: scores(todo[t][0], k_ref[pl.ds(start, todo[t][1]), :])
        ss = [diag_scores(t) for t in range(min(ATT_AHEAD, len(todo)))]
        for t, (r, width, mask) in enumerate(todo):
            if t + ATT_AHEAD < len(todo):
                ss.append(diag_scores(t + ATT_AHEAD))
            update(r, ss[t], v_ref[pl.ds(start, width), :], mask)

    l_row = jnp.sum(l_scr[...], axis=-1, keepdims=True)
    o_ref[...] = (_sigmoid(gb_ref[...].astype(F32)) * (acc_scr[...] / l_row)).astype(BF16)


def _attn_prompt_call(q_cat, k_cat, v, proj, batch, seq, tq):
    nq = seq // tq
    gb_blk = COL_GB // MLA_V
    est = (2 * (tq * HEAD_QK * 2 + seq * HEAD_QK * 2 + seq * MLA_V * 2 + 2 * tq * MLA_V * 4)
           + 3 * tq * V7X_LANES * 4 + 6 * tq * ATT_TK * 4 + (4 << 20))
    return pl.pallas_call(
        functools.partial(_attn_prompt_kernel, tq=tq, tk=ATT_TK, n_split=ATT_SPLIT),
        grid=(batch, MLA_HEADS, nq),
        in_specs=[pl.BlockSpec((tq, HEAD_QK), lambda b, h, i: (b * nq + i, h)),
                  pl.BlockSpec((seq, HEAD_QK), lambda b, h, i: (b, h)),
                  pl.BlockSpec((seq, MLA_V), lambda b, h, i: (b, h)),
                  pl.BlockSpec((tq, MLA_V), lambda b, h, i: (b * nq + i, gb_blk + h))],
        out_specs=pl.BlockSpec((tq, MLA_V), lambda b, h, i: (b * nq + i, h)),
        out_shape=jax.ShapeDtypeStruct((batch * seq, MLA_HEADS * MLA_V), BF16),
        scratch_shapes=[pltpu.VMEM((tq, V7X_LANES), F32), pltpu.VMEM((tq, V7X_LANES), F32),
                        pltpu.VMEM((tq, MLA_V), F32)],
        compiler_params=_cparams(("parallel", "parallel", "arbitrary"), est),
        name="mla_attention_prompt",
    )(q_cat, k_cat, v, proj)


def _attn_sample_kernel(qlat_ref, qr_ref, cc_ref, ckr_ref, nc_ref, nkr_ref, gb0_ref, gb1_ref, wukv_ref,
                        o_ref, *, sq):
    nt = (((1,), (1,)), ((), ()))
    q_lat = jnp.concatenate([qlat_ref[:, h * MLA_KV_LORA:(h + 1) * MLA_KV_LORA] for h in range(MLA_HEADS)],
                            axis=0)
    q_r = jnp.concatenate([qr_ref[:, h * MLA_ROPE:(h + 1) * MLA_ROPE] for h in range(MLA_HEADS)], axis=0)
    kc = cc_ref[...].astype(BF16)
    kn = nc_ref[...].astype(BF16)
    s_c = (lax.dot_general(q_lat, kc, nt, preferred_element_type=F32)
           + lax.dot_general(q_r, ckr_ref[...].astype(BF16), nt, preferred_element_type=F32))
    s_n = (lax.dot_general(q_lat, kn, nt, preferred_element_type=F32)
           + lax.dot_general(q_r, nkr_ref[...].astype(BF16), nt, preferred_element_type=F32))
    m = jnp.maximum(jnp.max(s_c, axis=-1, keepdims=True), jnp.max(s_n, axis=-1, keepdims=True))
    p_c = jnp.exp2(s_c - m)
    p_n = jnp.exp2(s_n - m)
    l_row = jnp.sum(p_c, axis=-1, keepdims=True) + jnp.sum(p_n, axis=-1, keepdims=True)
    o_lat = (jnp.dot(p_c.astype(BF16), kc, preferred_element_type=F32)
             + jnp.dot(p_n.astype(BF16), kn, preferred_element_type=F32)) / l_row
    o_lat = o_lat.astype(BF16)
    nv = MLA_HEADS * MLA_NOPE
    half = MLA_HEADS // 2
    for h in range(MLA_HEADS):
        w_uv = wukv_ref[:, nv + h * MLA_V:nv + (h + 1) * MLA_V]
        o_h = jnp.dot(o_lat[h * sq:(h + 1) * sq, :], w_uv, preferred_element_type=F32)
        gb_ref = gb0_ref if h < half else gb1_ref
        gb = gb_ref[:, (h % half) * MLA_V:(h % half + 1) * MLA_V].astype(F32)
        o_ref[:, h * MLA_V:(h + 1) * MLA_V] = (_sigmoid(gb) * o_h).astype(BF16)


def _attn_sample_call(q_lat, q_rope, ckv_new, kr_new, cache_ckv, cache_kr, proj, w_ukv_r, l, batch, sq):
    assert (PAST_LEN + sq - 1) // CHUNK <= PAST_LEN // CHUNK
    past = cache_ckv.shape[2]
    lat_w = MLA_HEADS * MLA_KV_LORA
    rope_w = MLA_HEADS * MLA_ROPE
    ww = MLA_HEADS * (MLA_NOPE + MLA_V)
    gw = D_MODEL // 2
    gb_blk = COL_GB // gw
    rows = MLA_HEADS * sq
    est = (2 * (sq * (lat_w + rope_w) * 2 + past * (MLA_KV_LORA + V7X_LANES) * 4 + MLA_KV_LORA * ww * 2
                + 4 * sq * D_MODEL * 4) + past * (MLA_KV_LORA + V7X_LANES) * 2 + 6 * rows * past * 4
           + 4 * rows * MLA_KV_LORA * 4 + (4 << 20))
    return pl.pallas_call(
        functools.partial(_attn_sample_kernel, sq=sq),
        grid=(batch,),
        in_specs=[pl.BlockSpec((sq, lat_w), lambda b: (b, 0)),
                  pl.BlockSpec((sq, rope_w), lambda b: (b, 0)),
                  pl.BlockSpec((None, None, past, MLA_KV_LORA), lambda b: (l, b, 0, 0)),
                  pl.BlockSpec((None, None, past, MLA_ROPE), lambda b: (l, b, 0, 0)),
                  pl.BlockSpec((sq, MLA_KV_LORA), lambda b: (b, 0)),
                  pl.BlockSpec((sq, MLA_ROPE), lambda b: (b, 0)),
                  pl.BlockSpec((sq, gw), lambda b: (b, gb_blk)),
                  pl.BlockSpec((sq, gw), lambda b: (b, gb_blk + 1)),
                  pl.BlockSpec((None, MLA_KV_LORA, ww), lambda b: (l, 0, 0))],
        out_specs=pl.BlockSpec((sq, D_MODEL), lambda b: (b, 0)),
        out_shape=jax.ShapeDtypeStruct((batch * sq, D_MODEL), BF16),
        compiler_params=_cparams(("parallel",), est),
        name="mla_attention_sample",
    )(q_lat, q_rope, cache_ckv, cache_kr, ckv_new, kr_new, proj, proj, w_ukv_r)


def _merge_kernel(a_ref, b_ref, x_ref, g1_ref, w_ref, o_ref):
    m = (a_ref[...].astype(F32) + b_ref[...].astype(F32)).astype(BF16)
    mix = jnp.dot(m, w_ref[...], preferred_element_type=F32)
    o_ref[...] = _gate_res(x_ref[...], g1_ref[...], mix)


def _merge_call(a_part, b_part, x, mod, w_o_b, l, tm, groups):
    m = x.shape[0]
    est = (2 * 2 * tm * D_MODEL * 2 + 4 * tm * D_MODEL * 4 + D_MODEL * D_MODEL * 2 + 3 * tm * D_MODEL * 4
           + (4 << 20))
    return pl.pallas_call(
        _merge_kernel,
        grid=(m // tm,),
        in_specs=[pl.BlockSpec((tm, D_MODEL), lambda i: (i, 0)),
                  pl.BlockSpec((tm, D_MODEL), lambda i: (i, 0)),
                  pl.BlockSpec((tm, D_MODEL), lambda i: (i, 0)),
                  pl.BlockSpec((None, None, None, groups, D_MODEL), lambda i: (l, 2, i, 0, 0)),
                  pl.BlockSpec((None, D_MODEL, D_MODEL), lambda i: (l, 0, 0), pipeline_mode=pl.Buffered(1))],
        out_specs=pl.BlockSpec((tm, D_MODEL), lambda i: (i, 0)),
        out_shape=jax.ShapeDtypeStruct((m, D_MODEL), F32),
        compiler_params=_cparams(("parallel",), est),
        name="merge_out_proj",
    )(a_part, b_part, x, mod, w_o_b)


def _ffn_kernel(x_ref, sc_ref, sh_ref, g2_ref, gn_ref, fg_ref, wg_ref, wu_ref, wo_ref, o_ref, *rest,
                nh, final, emit_bf16):
    h_scr, acc_scr = rest[-2:]
    j = pl.program_id(1)

    @pl.when(j == 0)
    def _():
        h_scr[...] = _norm_mod(x_ref[...], gn_ref[...], sc_ref[...], sh_ref[...]).astype(BF16)
        acc_scr[...] = jnp.zeros(acc_scr.shape, F32)

    wg = wg_ref[...].astype(BF16)
    wu = wu_ref[...].astype(BF16)
    wo = wo_ref[...].astype(BF16)
    if emit_bf16:
        wgb_ref, wub_ref, wob_ref = rest[:3]
        wgb_ref[...] = wg
        wub_ref[...] = wu
        wob_ref[...] = wo
    hb = h_scr[...]
    gate = jnp.dot(hb, wg, preferred_element_type=F32)
    up = jnp.dot(hb, wu, preferred_element_type=F32)
    act = ((gate * _sigmoid(gate)) * up).astype(BF16)
    acc_scr[...] += jnp.dot(act, wo, preferred_element_type=F32)

    @pl.when(j == nh - 1)
    def _():
        y = _gate_res(x_ref[...], g2_ref[...], acc_scr[...])
        o_ref[...] = _rms(y, fg_ref[...]) if final else y


def _ffn_call(x, mod, norm_g, norm_final, weights, l, tm, groups, th):
    m = x.shape[0]
    nh = FFN_HIDDEN // th
    emit_bf16 = len(weights) == 2
    assert not emit_bf16 or m == tm
    wbytes = 4 if emit_bf16 else 2
    est = (4 * tm * D_MODEL * 4 + tm * D_MODEL * 2 + tm * D_MODEL * 4 + 2 * 3 * D_MODEL * th * wbytes
           + (2 * 3 + 3) * D_MODEL * th * 2 * emit_bf16 + 4 * tm * th * 4 + tm * D_MODEL * 4 + (4 << 20))
    mod_spec = lambda k: pl.BlockSpec((None, None, None, groups, D_MODEL), lambda i, j: (l, k, i, 0, 0))
    out_specs = [pl.BlockSpec((tm, D_MODEL), lambda i, j: (i, 0))]
    out_shape = [jax.ShapeDtypeStruct((m, D_MODEL), F32)]
    if emit_bf16:
        w_in, w_out = weights
        w_specs = [pl.BlockSpec((None, D_MODEL, th), lambda i, j: (l, 0, j)),
                   pl.BlockSpec((None, D_MODEL, th), lambda i, j: (l, 0, nh + j)),
                   pl.BlockSpec((None, th, D_MODEL), lambda i, j: (l, j, 0))]
        w_args = [w_in, w_in, w_out]
        out_specs += [pl.BlockSpec((D_MODEL, th), lambda i, j: (0, j)),
                      pl.BlockSpec((D_MODEL, th), lambda i, j: (0, j)),
                      pl.BlockSpec((th, D_MODEL), lambda i, j: (j, 0))]
        out_shape += [jax.ShapeDtypeStruct((D_MODEL, FFN_HIDDEN), BF16),
                      jax.ShapeDtypeStruct((D_MODEL, FFN_HIDDEN), BF16),
                      jax.ShapeDtypeStruct((FFN_HIDDEN, D_MODEL), BF16)]
    else:
        w_specs = [pl.BlockSpec((D_MODEL, th), lambda i, j: (0, j)),
                   pl.BlockSpec((D_MODEL, th), lambda i, j: (0, j)),
                   pl.BlockSpec((th, D_MODEL), lambda i, j: (j, 0))]
        w_args = list(weights)
    return pl.pallas_call(
        functools.partial(_ffn_kernel, nh=nh, final=(l == DEPTH - 1), emit_bf16=emit_bf16),
        grid=(m // tm, nh),
        in_specs=[pl.BlockSpec((tm, D_MODEL), lambda i, j: (i, 0)),
                  mod_spec(4), mod_spec(3), mod_spec(5),
                  pl.BlockSpec((None, 1, D_MODEL), lambda i, j: (l, 0, 0)),
                  pl.BlockSpec((1, D_MODEL), lambda i, j: (0, 0))] + w_specs,
        out_specs=out_specs,
        out_shape=out_shape,
        scratch_shapes=[pltpu.VMEM((tm, D_MODEL), BF16), pltpu.VMEM((tm, D_MODEL), F32)],
        compiler_params=_cparams(("parallel", "arbitrary"), est),
        name="ffn_swiglu",
    )(x, mod, mod, mod, norm_g, norm_final, *w_args)


def _rope_tables(pos, dim):
    inv = jnp.exp(-math.log(ROPE_BASE) * jnp.arange(0, dim, 2, dtype=F32) / dim)
    ang = pos.astype(F32)[:, None] * inv[None, :]
    return jnp.cos(ang), jnp.sin(ang)


def _mla_rope_tables(pos, reps):
    cos, sin = _rope_tables(pos, MLA_ROPE)
    z = jnp.zeros_like(cos)
    c = jnp.concatenate([cos, z, cos, z], axis=-1)
    s = jnp.concatenate([-sin, z, sin, z], axis=-1)
    return jnp.tile(c, (reps, 1)), jnp.tile(s, (reps, 1))


def _prep_weights(w_uq, w_ukv):
    uq = w_uq.reshape(DEPTH, MLA_Q_LORA, MLA_HEADS, MLA_NOPE + MLA_ROPE)
    uq = jnp.concatenate([uq[..., :MLA_NOPE], _pack_rope_lanes(uq[..., MLA_NOPE:])], axis=-1)
    w_uq_r = uq.reshape(DEPTH, MLA_Q_LORA, MLA_HEADS * HEAD_QK).astype(BF16)
    ukv = w_ukv.reshape(DEPTH, MLA_KV_LORA, MLA_HEADS, MLA_NOPE + MLA_V)
    w_ukv_r = jnp.concatenate([ukv[..., :MLA_NOPE].reshape(DEPTH, MLA_KV_LORA, -1),
                               ukv[..., MLA_NOPE:].reshape(DEPTH, MLA_KV_LORA, -1)], axis=-1).astype(BF16)
    return w_uq_r, w_ukv_r


def _layer_prompt(l, x, mod, tm, batch, seq, ret_tabs, mla_tabs, zero_state, log_g, W, stacks, ffn_w):
    proj, kr_raw = _inproj_call(x, mod, W["norm_mix"], W["w_in_t"], l, IN_TM_PROMPT, 1, IN_TM_PROMPT // tm)
    ckv_prev, kr_prev, st_prev = stacks
    a_part, st_stack = _ret_call(proj, log_g, ret_tabs[0], ret_tabs[1], zero_state, 0, batch, seq,
                                 RET_L_PROMPT, RET_HEADS_PROMPT, l, st_prev)
    q_cat, ckv_stack, kr128, kr_stack = _mla_q_call(proj, kr_raw, W["q_norm"], W["kv_norm"], mla_tabs[0],
                                                    mla_tabs[1], W["w_uq_r"], l, tm, ckv_prev, kr_prev)
    k_cat, v = _mla_kv_call(ckv_stack, kr128, W["w_ukv_r"], l, batch * seq, tm)
    b_part = _attn_prompt_call(q_cat, k_cat, v, proj, batch, seq, ATT_TQ)
    x = _merge_call(a_part, b_part, x, mod, W["w_o"], l, tm, 1)
    x, = _ffn_call(x, mod, W["norm_ffn"], W["norm_final"], ffn_w, l, tm, 1, FFN_TH)
    return x, (ckv_stack, kr_stack, st_stack)


def _layer_sample(l, x, mod, groups, tm, batch, seq, ret_tabs, mla_tabs, state_ret, cache_ckv, cache_kr,
                  log_g, W, st_prev):
    proj, kr_raw = _inproj_call(x, mod, W["norm_mix"], W["w_in_t"], l, tm, groups, 1)
    a_part, st_stack = _ret_call(proj, log_g, ret_tabs[0], ret_tabs[1], state_ret, l, batch, seq, seq,
                                 RET_HEADS_SAMPLE, l, st_prev)
    q_lat, q_rope, ckv, kr = _mla_q_sample_call(proj, kr_raw, W["q_norm"], W["kv_norm"], mla_tabs[0],
                                                mla_tabs[1], W["w_uq_r"], W["w_ukv_r"], l)
    b_part = _attn_sample_call(q_lat, q_rope, ckv, kr, cache_ckv, cache_kr, proj, W["w_ukv_r"], l, batch, seq)
    x = _merge_call(a_part, b_part, x, mod, W["w_o"], l, tm, groups)
    x, *ffn_w = _ffn_call(x, mod, W["norm_ffn"], W["norm_final"], (W["w_ffn_in"], W["w_ffn_out"]), l, tm,
                          groups, FFN_TH_SAMPLE)
    return x, ckv, kr, st_stack, tuple(ffn_w)


def kernel(x_prompt, x_sample, c_prompt, c_sample, cache_mla_ckv, cache_mla_krope, state_ret, w_ada, b_ada,
           norm_mix, norm_ffn, w_in, mla_q_norm, w_uq, mla_kv_norm, w_ukv, w_o, w_ffn_in, w_ffn_out,
           norm_final):
    bp, sp, _ = x_prompt.shape
    bs, ss, _ = x_sample.shape
    tm = TOKEN_TILE
    assert sp % IN_TM_PROMPT == 0 and (bs * ss) % tm == 0 and tm % ss == 0

    w_uq_r, w_ukv_r = _prep_weights(w_uq, w_ukv)
    W = dict(w_in_t=jnp.swapaxes(w_in, 1, 2), w_uq_r=w_uq_r, w_ukv_r=w_ukv_r,
             w_o=w_o.astype(BF16), w_ffn_in=w_ffn_in, w_ffn_out=w_ffn_out,
             norm_mix=norm_mix.reshape(DEPTH, 1, D_MODEL), norm_ffn=norm_ffn.reshape(DEPTH, 1, D_MODEL),
             norm_final=norm_final.reshape(1, D_MODEL),
             q_norm=mla_q_norm.reshape(DEPTH, 1, MLA_Q_LORA), kv_norm=mla_kv_norm.reshape(DEPTH, 1, MLA_KV_LORA))

    c_rows = -(-(bp + bs) // ADA_ROW_ALIGN) * ADA_ROW_ALIGN
    c_all = jnp.concatenate([c_prompt, c_sample, jnp.zeros((c_rows - bp - bs, D_MODEL), F32)], axis=0)
    mod_all = _ada_call(c_all, w_ada, b_ada)[:, :bp + bs]
    mod_all = mod_all.reshape(DEPTH, bp + bs, 6, D_MODEL).transpose(0, 2, 1, 3)
    tiles_per_batch = sp // tm
    mod_p = jnp.repeat(mod_all[:, :, :bp], tiles_per_batch, axis=2)[:, :, :, None, :]
    groups_s = tm // ss
    mod_s = mod_all[:, :, bp:].reshape(DEPTH, 6, (bs * ss) // tm, groups_s, D_MODEL)

    log_g = jnp.log1p(-jnp.exp2(-RET_GAMMA_EXP0 - jnp.arange(RET_HEADS, dtype=F32)))
    pos_p = jnp.arange(sp)
    pos_s = PAST_LEN + jnp.arange(ss)
    ret_tabs_p = _rope_tables(pos_p, RET_DK)
    ret_tabs_s = _rope_tables(pos_s, RET_DK)
    mla_tabs_p = _mla_rope_tables(pos_p, bp)
    mla_tabs_s = _mla_rope_tables(pos_s, bs)
    zero_state = jnp.zeros((1, bp, RET_HEADS, RET_DK, RET_DV), F32)

    xp = x_prompt.reshape(bp * sp, D_MODEL)
    xs = x_sample.reshape(bs * ss, D_MODEL)
    stacks_p = (None, None, None)
    st_s = None
    ckv_s, kr_s = [], []
    for l in range(DEPTH):
        xs, ckv, kr, st_s, ffn_w = _layer_sample(l, xs, mod_s, groups_s, tm, bs, ss, ret_tabs_s, mla_tabs_s,
                                                 state_ret, cache_mla_ckv, cache_mla_krope, log_g, W, st_s)
        xp, stacks_p = _layer_prompt(l, xp, mod_p, tm, bp, sp, ret_tabs_p, mla_tabs_p, zero_state, log_g, W,
                                     stacks_p, ffn_w)
        ckv_s.append(ckv.reshape(bs, ss, MLA_KV_LORA))
        kr_s.append(kr.reshape(bs, ss, MLA_ROPE))

    y_prompt = xp.reshape(bp, sp, D_MODEL)
    y_sample = xs.reshape(bs, ss, D_MODEL)
    ckv_p, kr_p, st_p = stacks_p
    return (y_prompt, y_sample, ckv_p.reshape(DEPTH, bp, sp, MLA_KV_LORA), kr_p.reshape(DEPTH, bp, sp, MLA_ROPE),
            st_p, jnp.stack(ckv_s), jnp.stack(kr_s), st_s)
```

```python
import functools
import math

import jax
import jax.numpy as jnp
from jax import lax
from jax.experimental import pallas as pl
from jax.experimental.pallas import tpu as pltpu

D_MODEL = 2048
DEPTH = 4
PAST_LEN = 1024
CHUNK = 64
RET_HEADS = 8
RET_DK = D_MODEL // RET_HEADS
RET_DV = D_MODEL // RET_HEADS
MLA_HEADS = 16
MLA_Q_LORA = D_MODEL // 4
MLA_KV_LORA = D_MODEL // 4
MLA_NOPE = 128
MLA_ROPE = 64
MLA_V = D_MODEL // MLA_HEADS
FFN_HIDDEN = -(-8 * D_MODEL // (3 * 256)) * 256
ROPE_BASE = 10000.0
RET_GAMMA_EXP0 = 5.0
RMS_EPS = 1e-6
GN_EPS = 1e-5
NEG_INF = -1e30

F32 = jnp.float32
BF16 = jnp.bfloat16

V7X_LANES = 128
V7X_VMEM_LIMIT_CAP = 56 * 1024 * 1024

COL_RQ, COL_RK, COL_RV, COL_RG = 0, D_MODEL, 2 * D_MODEL, 3 * D_MODEL
COL_DQ = 4 * D_MODEL
COL_DKV = COL_DQ + MLA_Q_LORA
COL_GA = COL_DKV + MLA_KV_LORA
COL_GB = COL_GA + D_MODEL
MAIN_COLS = COL_GB + D_MODEL
HEAD_QK = 2 * V7X_LANES
HALF_ROPE = MLA_ROPE // 2
ADA_ROW_ALIGN = 16
TOKEN_TILE = 512
IN_TM_PROMPT = 2048
IN_TN = 512
ADA_TN = 1024
FFN_TH = 512
FFN_TH_SAMPLE = 256
RET_L_PROMPT = 256
RET_HEADS_PROMPT = 2
RET_HEADS_SAMPLE = 4
ATT_TQ = 1024
ATT_TK = 1024
ATT_AHEAD = 2
ATT_SPLIT = 4
QK_SCALE_LOG2 = (MLA_NOPE + MLA_ROPE) ** -0.5 * math.log2(math.e)


def _cparams(sem, est_bytes):
    return pltpu.CompilerParams(dimension_semantics=sem,
                                vmem_limit_bytes=min(int(est_bytes), V7X_VMEM_LIMIT_CAP))


def _sigmoid(x):
    return jax.nn.sigmoid(x)


def _rms(x, g):
    return x * lax.rsqrt(jnp.mean(x * x, axis=-1, keepdims=True) + RMS_EPS) * g


def _norm_mod(x, g, sc, sh):
    tm, d = x.shape
    groups = sc.shape[0]
    y = _rms(x, g)
    if groups == 1:
        return y * (1.0 + sc) + sh
    y3 = y.reshape(groups, tm // groups, d)
    return (y3 * (1.0 + sc[:, None, :]) + sh[:, None, :]).reshape(tm, d)


def _gate_res(x, gate, upd):
    tm, n = x.shape
    groups = gate.shape[0]
    if groups == 1:
        return x + gate * upd
    return x + (gate[:, None, :] * upd.reshape(groups, tm // groups, n)).reshape(tm, n)


def _pack_rope_lanes(x):
    z = jnp.zeros(x.shape[:-1] + (HALF_ROPE,), x.dtype)
    return jnp.concatenate([x[..., :HALF_ROPE], z, x[..., HALF_ROPE:], z], axis=-1)


def _unpack_rope_lanes(x):
    return jnp.concatenate([x[..., :HALF_ROPE], x[..., 2 * HALF_ROPE:3 * HALF_ROPE]], axis=-1)


def _ada_kernel(c_ref, w_ref, b_ref, o_ref):
    c = c_ref[...]
    a = (c * _sigmoid(c)).astype(BF16)
    o_ref[0] = jnp.dot(a, w_ref[0].astype(BF16), preferred_element_type=F32) + b_ref[0]


def _ada_call(c_all, w_ada, b_ada):
    nb = c_all.shape[0]
    n = w_ada.shape[-1]
    tn = ADA_TN
    est = 2 * (D_MODEL * tn * 4) + D_MODEL * tn * 2 + 4 * nb * (D_MODEL + 2 * tn) * 4 + (4 << 20)
    return pl.pallas_call(
        _ada_kernel,
        grid=(DEPTH, n // tn),
        in_specs=[pl.BlockSpec((nb, D_MODEL), lambda l, j: (0, 0)),
                  pl.BlockSpec((1, D_MODEL, tn), lambda l, j: (l, 0, j)),
                  pl.BlockSpec((1, 1, tn), lambda l, j: (l, 0, j))],
        out_specs=pl.BlockSpec((1, nb, tn), lambda l, j: (l, 0, j)),
        out_shape=jax.ShapeDtypeStruct((DEPTH, nb, n), F32),
        compiler_params=_cparams(("parallel", "parallel"), est),
        name="ada_mod",
    )(c_all, w_ada, b_ada.reshape(DEPTH, 1, n))


def _inproj_kernel(x_ref, sc_ref, sh_ref, g_ref, w_ref, wn_ref, wkr_ref, o_ref, kr_ref, h_scr, *, n_main):
    j = pl.program_id(1)
    nt = (((1,), (1,)), ((), ()))

    @pl.when(j == 0)
    def _():
        wkr = wkr_ref[...]
        z = jnp.zeros((HALF_ROPE, wkr.shape[1]), wkr.dtype)
        wkr = jnp.concatenate([wkr[:HALF_ROPE], z, wkr[HALF_ROPE:], z], axis=0).astype(BF16)
        tm = x_ref.shape[0]
        for r0 in range(0, tm, TOKEN_TILE):
            rs = pl.ds(r0, TOKEN_TILE)
            hb = _norm_mod(x_ref[rs, :], g_ref[...], sc_ref[...], sh_ref[...]).astype(BF16)
            h_scr[rs, :] = hb
            kr_ref[rs, :] = lax.dot_general(hb, wkr, nt, preferred_element_type=F32)

    @pl.when(j < n_main)
    def _():
        o_ref[...] = lax.dot_general(h_scr[...], w_ref[...].astype(BF16), nt,
                                     preferred_element_type=F32).astype(BF16)

    @pl.when(j >= n_main)
    def _():
        w = jnp.concatenate([w_ref[MLA_ROPE:, :], wn_ref[...]], axis=0).astype(BF16)
        o_ref[...] = lax.dot_general(h_scr[...], w, nt, preferred_element_type=F32).astype(BF16)


def _inproj_call(x, mod, norm_g, w_in_t, l, tm, groups, mod_stride):
    m = x.shape[0]
    n_main = COL_GA // IN_TN
    sub = IN_TN // MLA_ROPE
    assert groups == 1 or tm == TOKEN_TILE
    est = (tm * D_MODEL * 4 + tm * D_MODEL * 2 + 2 * D_MODEL * (IN_TN + 2 * MLA_ROPE) * 4
           + D_MODEL * IN_TN * 2 + 3 * tm * IN_TN * 4 + 6 * TOKEN_TILE * D_MODEL * 4 + (4 << 20))
    mod_spec = lambda k: pl.BlockSpec((None, None, None, groups, D_MODEL),
                                      lambda i, j: (l, k, i * mod_stride, 0, 0))
    return pl.pallas_call(
        functools.partial(_inproj_kernel, n_main=n_main),
        grid=(m // tm, MAIN_COLS // IN_TN),
        in_specs=[pl.BlockSpec((tm, D_MODEL), lambda i, j: (i, 0), pipeline_mode=pl.Buffered(1)),
                  mod_spec(1), mod_spec(0),
                  pl.BlockSpec((None, 1, D_MODEL), lambda i, j: (l, 0, 0)),
                  pl.BlockSpec((None, IN_TN, D_MODEL), lambda i, j: (l, j, 0)),
                  pl.BlockSpec((None, MLA_ROPE, D_MODEL),
                               lambda i, j: (l, (jnp.maximum(j, n_main) + 1) * sub, 0)),
                  pl.BlockSpec((None, MLA_ROPE, D_MODEL), lambda i, j: (l, COL_GA // MLA_ROPE, 0))],
        out_specs=[pl.BlockSpec((tm, IN_TN), lambda i, j: (i, j)),
                   pl.BlockSpec((tm, V7X_LANES), lambda i, j: (i, 0))],
        out_shape=[jax.ShapeDtypeStruct((m, MAIN_COLS), BF16),
                   jax.ShapeDtypeStruct((m, V7X_LANES), F32)],
        scratch_shapes=[pltpu.VMEM((tm, D_MODEL), BF16)],
        compiler_params=_cparams(("parallel", "arbitrary"), est),
        name="in_proj",
    )(x, mod, mod, norm_g, w_in_t, w_in_t, w_in_t)


def _ret_kernel(lg_ref, q_ref, k_ref, v_ref, rg_ref, ga_ref, cos_ref, sin_ref, s0_ref, *rest,
                chunk_len, heads):
    a_ref, st_ref, dm_scr = rest[-3:]
    hg = pl.program_id(1)
    c = pl.program_id(2)
    L = chunk_len
    lgs = [lg_ref[hg * heads + t] for t in range(heads)]

    @pl.when(c == 0)
    def _():
        st_ref[...] = s0_ref[...]
        ri = lax.broadcasted_iota(jnp.int32, (L, L), 0)
        ci = lax.broadcasted_iota(jnp.int32, (L, L), 1)
        diff = (ri - ci).astype(F32)
        for t in range(heads):
            dm_scr[t] = jnp.where(diff >= 0, jnp.exp(jnp.maximum(diff, 0.0) * lgs[t]), 0.0)

    cos = cos_ref[...]
    sin = sin_ref[...]
    half = RET_DK // 2
    idx = lax.broadcasted_iota(jnp.int32, (L, 1), 0).astype(F32)
    nt = (((1,), (1,)), ((), ()))
    tn = (((0,), (0,)), ((), ()))

    def rope(x):
        x1, x2 = x[:, :half], x[:, half:]
        return jnp.concatenate([x1 * cos - x2 * sin, x1 * sin + x2 * cos], axis=-1)

    stage = []
    for t in range(heads):
        cs = pl.ds(t * RET_DK, RET_DK)
        q = rope(q_ref[:, cs].astype(F32))
        k = rope(k_ref[:, cs].astype(F32)) * (RET_DK ** -0.5)
        vb = v_ref[:, cs]
        qb = q.astype(BF16)
        zeta = jnp.exp((L - 1.0 - idx) * lgs[t])
        g_l = jnp.exp(jnp.full((1, 1), float(L), F32) * lgs[t])
        st = st_ref[0, t]
        scores = lax.dot_general(qb, k.astype(BF16), nt, preferred_element_type=F32)
        cross = jnp.dot(qb, st.astype(BF16), preferred_element_type=F32)
        st_ref[0, t] = st * g_l + lax.dot_general((k * zeta).astype(BF16), vb, tn, preferred_element_type=F32)
        stage.append((scores, cross, vb))

    outs = []
    for t in range(heads):
        scores, cross, vb = stage[t]
        xi = jnp.exp((idx + 1.0) * lgs[t])
        outs.append(jnp.dot((scores * dm_scr[t]).astype(BF16), vb, preferred_element_type=F32) + cross * xi)

    for t in range(heads):
        cs = pl.ds(t * RET_DK, RET_DK)
        o = outs[t]
        mu = jnp.mean(o, axis=-1, keepdims=True)
        d = o - mu
        var = jnp.mean(d * d, axis=-1, keepdims=True)
        on = d * lax.rsqrt(var + GN_EPS)
        rg = rg_ref[:, cs].astype(F32)
        a_ref[:, cs] = (_sigmoid(ga_ref[:, cs].astype(F32)) * ((rg * _sigmoid(rg)) * on)).astype(BF16)


def _stacked_out(stack_prev, in_specs, args, out_index):
    if stack_prev is None:
        return {}
    in_specs.append(pl.BlockSpec(memory_space=pl.ANY))
    args.append(stack_prev)
    return {len(args) - 1: out_index}


def _ret_call(proj, log_g, cos, sin, state0, state_layer, batch, seq, chunk_len, heads, l, stack_prev):
    L = chunk_len
    nc = seq // L
    w = heads * RET_DK
    assert COL_GA % w == 0 and RET_HEADS % heads == 0
    col = lambda base: (lambda b, h, c: (b * nc + c, base // w + h))
    blk = lambda base: pl.BlockSpec((L, w), col(base))
    nh = RET_HEADS
    est = (2 * 6 * L * w * 4 + 4 * heads * RET_DK * RET_DV * 4 + heads * L * L * 4
           + 8 * heads * L * max(L, RET_DK) * 4 + (4 << 20))
    in_specs = [pl.BlockSpec(memory_space=pltpu.SMEM),
                blk(COL_RQ), blk(COL_RK), blk(COL_RV), blk(COL_RG), blk(COL_GA),
                pl.BlockSpec((L, RET_DK // 2), lambda b, h, c: (c, 0)),
                pl.BlockSpec((L, RET_DK // 2), lambda b, h, c: (c, 0)),
                pl.BlockSpec((None, 1, heads, RET_DK, RET_DV), lambda b, h, c: (state_layer, b, h, 0, 0))]
    args = [log_g, proj, proj, proj, proj, proj, cos, sin, state0]
    aliases = _stacked_out(stack_prev, in_specs, args, 1)
    return pl.pallas_call(
        functools.partial(_ret_kernel, chunk_len=L, heads=heads),
        grid=(batch, nh // heads, nc),
        in_specs=in_specs,
        out_specs=[pl.BlockSpec((L, w), lambda b, h, c: (b * nc + c, h)),
                   pl.BlockSpec((None, 1, heads, RET_DK, RET_DV), lambda b, h, c: (l, b, h, 0, 0))],
        out_shape=[jax.ShapeDtypeStruct((batch * seq, D_MODEL), BF16),
                   jax.ShapeDtypeStruct((DEPTH, batch, nh, RET_DK, RET_DV), F32)],
        scratch_shapes=[pltpu.VMEM((heads, L, L), F32)],
        input_output_aliases=aliases,
        compiler_params=_cparams(("parallel", "parallel", "arbitrary"), est),
        name="retention",
    )(*args)


def _rope128(x, c, s):
    return x * c + pltpu.roll(x, V7X_LANES // 2, 1) * s


def _mla_q_kernel(dq_ref, dkv_ref, kr_ref, qn_ref, kvn_ref, c_ref, s_ref, wuq_ref, *rest):
    q_out, ckv_out, kro_out, kr64_out = rest[-4:]
    c = c_ref[...]
    s = s_ref[...]
    cq = _rms(dq_ref[...].astype(F32), qn_ref[...]).astype(BF16)
    for h in range(MLA_HEADS):
        lo = h * HEAD_QK
        qh = jnp.dot(cq, wuq_ref[:, lo:lo + HEAD_QK], preferred_element_type=F32) * QK_SCALE_LOG2
        q_out[:, lo:lo + MLA_NOPE] = qh[:, :MLA_NOPE].astype(BF16)
        q_out[:, lo + MLA_NOPE:lo + HEAD_QK] = _rope128(qh[:, MLA_NOPE:], c, s).astype(BF16)
    ckv_out[...] = _rms(dkv_ref[...].astype(F32), kvn_ref[...])
    kr = _rope128(kr_ref[...], c, s)
    kro_out[...] = kr
    kr64_out[...] = _unpack_rope_lanes(kr)


def _mla_q_call(proj, kr_raw, q_norm, kv_norm, rope_c, rope_s, w_uq_r, l, tm, ckv_prev, kr_prev):
    m = proj.shape[0]
    dq_blk = COL_DQ // MLA_Q_LORA
    qw = MLA_HEADS * HEAD_QK
    est = (2 * (2 * tm * MLA_Q_LORA * 4 + 3 * tm * V7X_LANES * 4) + 2 * MLA_Q_LORA * qw * 2
           + 2 * tm * qw * 2 + 2 * tm * MLA_KV_LORA * 4 + 2 * tm * V7X_LANES * 4 + 8 * tm * HEAD_QK * 4
           + (4 << 20))
    in_specs = [pl.BlockSpec((tm, MLA_Q_LORA), lambda i: (i, dq_blk)),
                pl.BlockSpec((tm, MLA_KV_LORA), lambda i: (i, dq_blk + 1)),
                pl.BlockSpec((tm, V7X_LANES), lambda i: (i, 0)),
                pl.BlockSpec((None, 1, MLA_Q_LORA), lambda i: (l, 0, 0)),
                pl.BlockSpec((None, 1, MLA_KV_LORA), lambda i: (l, 0, 0)),
                pl.BlockSpec((tm, V7X_LANES), lambda i: (i, 0)),
                pl.BlockSpec((tm, V7X_LANES), lambda i: (i, 0)),
                pl.BlockSpec((None, MLA_Q_LORA, qw), lambda i: (l, 0, 0))]
    args = [proj, proj, kr_raw, q_norm, kv_norm, rope_c, rope_s, w_uq_r]
    aliases = _stacked_out(ckv_prev, in_specs, args, 1)
    aliases.update(_stacked_out(kr_prev, in_specs, args, 3))
    return pl.pallas_call(
        _mla_q_kernel,
        grid=(m // tm,),
        in_specs=in_specs,
        out_specs=[pl.BlockSpec((tm, qw), lambda i: (i, 0)),
                   pl.BlockSpec((None, tm, MLA_KV_LORA), lambda i: (l, i, 0)),
                   pl.BlockSpec((tm, V7X_LANES), lambda i: (i, 0)),
                   pl.BlockSpec((None, tm, MLA_ROPE), lambda i: (l, i, 0))],
        out_shape=[jax.ShapeDtypeStruct((m, qw), BF16),
                   jax.ShapeDtypeStruct((DEPTH, m, MLA_KV_LORA), F32),
                   jax.ShapeDtypeStruct((m, V7X_LANES), F32),
                   jax.ShapeDtypeStruct((DEPTH, m, MLA_ROPE), F32)],
        input_output_aliases=aliases,
        compiler_params=_cparams(("parallel",), est),
        name="mla_latents",
    )(*args)


def _mla_q_sample_kernel(dq_ref, dkv_ref, kr_ref, qn_ref, kvn_ref, c_ref, s_ref, wuq_ref, wukv_ref,
                         qlat_out, qr_out, ckv_out, kro_out):
    c = c_ref[...]
    s = s_ref[...]
    cq = _rms(dq_ref[...].astype(F32), qn_ref[...]).astype(BF16)
    for h in range(MLA_HEADS):
        lo = h * HEAD_QK
        qh = jnp.dot(cq, wuq_ref[:, lo:lo + HEAD_QK], preferred_element_type=F32) * QK_SCALE_LOG2
        w_uk = wukv_ref[:, h * (MLA_NOPE + MLA_V):h * (MLA_NOPE + MLA_V) + MLA_NOPE]
        q_lat = lax.dot_general(qh[:, :MLA_NOPE].astype(BF16), w_uk, (((1,), (1,)), ((), ())),
                                preferred_element_type=F32)
        qlat_out[:, h * MLA_KV_LORA:(h + 1) * MLA_KV_LORA] = q_lat.astype(BF16)
        qr_out[:, h * MLA_ROPE:(h + 1) * MLA_ROPE] = _unpack_rope_lanes(
            _rope128(qh[:, MLA_NOPE:], c, s)).astype(BF16)
    ckv_out[...] = _rms(dkv_ref[...].astype(F32), kvn_ref[...])
    kro_out[...] = _unpack_rope_lanes(_rope128(kr_ref[...], c, s))


def _mla_q_sample_call(proj, kr_raw, q_norm, kv_norm, rope_c, rope_s, w_uq_r, w_ukv_r, l):
    m = proj.shape[0]
    dq_blk = COL_DQ // MLA_Q_LORA
    qw = MLA_HEADS * HEAD_QK
    ww = MLA_HEADS * (MLA_NOPE + MLA_V)
    lat_w = MLA_HEADS * MLA_KV_LORA
    rope_w = MLA_HEADS * MLA_ROPE
    est = (2 * (2 * m * MLA_Q_LORA * 4 + 3 * m * V7X_LANES * 4) + 2 * MLA_Q_LORA * (qw + ww) * 2
           + 2 * m * (lat_w + rope_w) * 2 + 2 * m * (MLA_KV_LORA + MLA_ROPE) * 4 + 8 * m * MLA_KV_LORA * 4
           + (4 << 20))
    full = lambda shape: pl.BlockSpec(shape, lambda i: (0,) * len(shape))
    return pl.pallas_call(
        _mla_q_sample_kernel,
        grid=(1,),
        in_specs=[pl.BlockSpec((m, MLA_Q_LORA), lambda i: (0, dq_blk)),
                  pl.BlockSpec((m, MLA_KV_LORA), lambda i: (0, dq_blk + 1)),
                  full((m, V7X_LANES)),
                  pl.BlockSpec((None, 1, MLA_Q_LORA), lambda i: (l, 0, 0)),
                  pl.BlockSpec((None, 1, MLA_KV_LORA), lambda i: (l, 0, 0)),
                  full((m, V7X_LANES)), full((m, V7X_LANES)),
                  pl.BlockSpec((None, MLA_Q_LORA, qw), lambda i: (l, 0, 0)),
                  pl.BlockSpec((None, MLA_KV_LORA, ww), lambda i: (l, 0, 0))],
        out_specs=[full((m, lat_w)), full((m, rope_w)), full((m, MLA_KV_LORA)), full((m, MLA_ROPE))],
        out_shape=[jax.ShapeDtypeStruct((m, lat_w), BF16),
                   jax.ShapeDtypeStruct((m, rope_w), BF16),
                   jax.ShapeDtypeStruct((m, MLA_KV_LORA), F32),
                   jax.ShapeDtypeStruct((m, MLA_ROPE), F32)],
        compiler_params=_cparams(("arbitrary",), est),
        name="mla_latents_sample",
    )(proj, proj, kr_raw, q_norm, kv_norm, rope_c, rope_s, w_uq_r, w_ukv_r)


def _mla_kv_kernel(ckv_ref, kr_ref, wukv_ref, kcat_out, v_out):
    cb = ckv_ref[...].astype(BF16)
    krb = kr_ref[...].astype(BF16)
    hw = MLA_NOPE + MLA_V
    for h in range(MLA_HEADS):
        kv = jnp.dot(cb, wukv_ref[:, h * hw:(h + 1) * hw], preferred_element_type=F32).astype(BF16)
        lo = h * HEAD_QK
        kcat_out[:, lo:lo + MLA_NOPE] = kv[:, :MLA_NOPE]
        kcat_out[:, lo + MLA_NOPE:lo + HEAD_QK] = krb
        v_out[:, h * MLA_V:(h + 1) * MLA_V] = kv[:, MLA_NOPE:]


def _mla_kv_call(ckv_stack, kr128, w_ukv_r, l, rows, tm):
    kw = MLA_HEADS * HEAD_QK
    vw = MLA_HEADS * MLA_V
    ww = MLA_HEADS * (MLA_NOPE + MLA_V)
    est = (2 * tm * (MLA_KV_LORA + V7X_LANES) * 4 + 2 * MLA_KV_LORA * ww * 2 + 2 * tm * (kw + vw) * 2
           + 8 * tm * HEAD_QK * 4 + (4 << 20))
    return pl.pallas_call(
        _mla_kv_kernel,
        grid=(rows // tm,),
        in_specs=[pl.BlockSpec((None, tm, MLA_KV_LORA), lambda i: (l, i, 0)),
                  pl.BlockSpec((tm, V7X_LANES), lambda i: (i, 0)),
                  pl.BlockSpec((None, MLA_KV_LORA, ww), lambda i: (l, 0, 0))],
        out_specs=[pl.BlockSpec((tm, kw), lambda i: (i, 0)),
                   pl.BlockSpec((tm, vw), lambda i: (i, 0))],
        out_shape=[jax.ShapeDtypeStruct((rows, kw), BF16),
                   jax.ShapeDtypeStruct((rows, vw), BF16)],
        compiler_params=_cparams(("parallel",), est),
        name="mla_kv_expand",
    )(ckv_stack, kr128, w_ukv_r)


def _scores(q, k_blk):
    return lax.dot_general(q, k_blk, (((1,), (1,)), ((), ())), preferred_element_type=F32)


def _softmax_tile(s, v_blk, m_prev, l_prev, acc_prev, tri):
    n_groups = s.shape[1] // V7X_LANES
    groups = [s[:, g * V7X_LANES:(g + 1) * V7X_LANES] for g in range(n_groups)]
    if tri is not None:
        n_tri = tri.shape[1] // V7X_LANES
        for t in range(n_tri):
            g = n_groups - n_tri + t
            groups[g] = jnp.where(tri[:, t * V7X_LANES:(t + 1) * V7X_LANES], groups[g], NEG_INF)
    m_new = jnp.maximum(m_prev, jnp.max(functools.reduce(jnp.maximum, groups), axis=-1, keepdims=True))
    alpha = jnp.exp2(m_prev - m_new)
    ps = [jnp.exp2(g - m_new) for g in groups]
    l_new = alpha * l_prev + functools.reduce(jnp.add, ps)
    p = jnp.concatenate([x.astype(BF16) for x in ps], axis=-1)
    acc_new = alpha * acc_prev + jnp.dot(p, v_blk, preferred_element_type=F32)
    return m_new, l_new, acc_new


def _attn_prompt_kernel(q_ref, k_ref, v_ref, gb_ref, o_ref, m_scr, l_scr, acc_scr, *, tq, tk, n_split):
    i = pl.program_id(2)
    rows = tq // n_split
    m_scr[...] = jnp.full(m_scr.shape, NEG_INF, F32)
    l_scr[...] = jnp.zeros(l_scr.shape, F32)
    acc_scr[...] = jnp.zeros(acc_scr.shape, F32)

    def scores(r, k_blk):
        return _scores(q_ref[pl.ds(r * rows, rows), :], k_blk)

    def update(r, s, v_blk, mask):
        rs = pl.ds(r * rows, rows)
        m_new, l_new, acc_new = _softmax_tile(s, v_blk, m_scr[rs, :], l_scr[rs, :], acc_scr[rs, :], mask)
        m_scr[rs, :] = m_new
        l_scr[rs, :] = l_new
        acc_scr[rs, :] = acc_new

    blocks_per_tile = tq // tk

    def body(j, carry):
        for d in range(blocks_per_tile):
            start = pl.multiple_of(j * tq + d * tk, tk)
            k_blk = k_ref[pl.ds(start, tk), :]
            v_blk = v_ref[pl.ds(start, tk), :]
            ss = [scores(r, k_blk) for r in range(min(ATT_AHEAD, n_split))]
            for r in range(n_split):
                if r + ATT_AHEAD < n_split:
                    ss.append(scores(r + ATT_AHEAD, k_blk))
                update(r, ss[r], v_blk, None)
        return carry

    lax.fori_loop(0, i, body, 0)

    tri = (lax.broadcasted_iota(jnp.int32, (rows, rows), 1) // CHUNK
           <= lax.broadcasted_iota(jnp.int32, (rows, rows), 0) // CHUNK)
    for d in range(blocks_per_tile):
        k0 = d * tk
        todo = []
        for r in range(n_split):
            r0, r1 = r * rows, (r + 1) * rows
            width = min(k0 + tk, r1) - k0
            if width <= 0:
                continue
            on_diagonal = k0 + width > r0
            assert not on_diagonal or (k0 + width == r1 and width >= rows)
            todo.append((r, width, tri if on_diagonal else None))
        start = pl.multiple_of(i * tq + k0, tk)
        diag_scores = lambda t: scores(todo[t][0], k_ref[pl.ds(start, todo[t][1]), :])
        ss = [diag_scores(t) for t in range(min(ATT_AHEAD, len(todo)))]
        for t, (r, width, mask) in enumerate(todo):
            if t + ATT_AHEAD < len(todo):
                ss.append(diag_scores(t + ATT_AHEAD))
            update(r, ss[t], v_ref[pl.ds(start, width), :], mask)

    l_row = jnp.sum(l_scr[...], axis=-1, keepdims=True)
    o_ref[...] = (_sigmoid(gb_ref[...].astype(F32)) * (acc_scr[...] / l_row)).astype(BF16)


def _attn_prompt_call(q_cat, k_cat, v, proj, batch, seq, tq):
    nq = seq // tq
    gb_blk = COL_GB // MLA_V
    est = (2 * (tq * HEAD_QK * 2 + seq * HEAD_QK * 2 + seq * MLA_V * 2 + 2 * tq * MLA_V * 4)
           + 3 * tq * V7X_LANES * 4 + 6 * tq * ATT_TK * 4 + (4 << 20))
    return pl.pallas_call(
        functools.partial(_attn_prompt_kernel, tq=tq, tk=ATT_TK, n_split=ATT_SPLIT),
        grid=(batch, MLA_HEADS, nq),
        in_specs=[pl.BlockSpec((tq, HEAD_QK), lambda b, h, i: (b * nq + i, h)),
                  pl.BlockSpec((seq, HEAD_QK), lambda b, h, i: (b, h)),
                  pl.BlockSpec((seq, MLA_V), lambda b, h, i: (b, h)),
                  pl.BlockSpec((tq, MLA_V), lambda b, h, i: (b * nq + i, gb_blk + h))],
        out_specs=pl.BlockSpec((tq, MLA_V), lambda b, h, i: (b * nq + i, h)),
        out_shape=jax.ShapeDtypeStruct((batch * seq, MLA_HEADS * MLA_V), BF16),
        scratch_shapes=[pltpu.VMEM((tq, V7X_LANES), F32), pltpu.VMEM((tq, V7X_LANES), F32),
                        pltpu.VMEM((tq, MLA_V), F32)],
        compiler_params=_cparams(("parallel", "parallel", "arbitrary"), est),
        name="mla_attention_prompt",
    )(q_cat, k_cat, v, proj)


def _attn_sample_kernel(qlat_ref, qr_ref, cc_ref, ckr_ref, nc_ref, nkr_ref, gb0_ref, gb1_ref, wukv_ref,
                        o_ref, *, sq):
    nt = (((1,), (1,)), ((), ()))
    q_lat = jnp.concatenate([qlat_ref[:, h * MLA_KV_LORA:(h + 1) * MLA_KV_LORA] for h in range(MLA_HEADS)],
                            axis=0)
    q_r = jnp.concatenate([qr_ref[:, h * MLA_ROPE:(h + 1) * MLA_ROPE] for h in range(MLA_HEADS)], axis=0)
    kc = cc_ref[...].astype(BF16)
    kn = nc_ref[...].astype(BF16)
    s_c = (lax.dot_general(q_lat, kc, nt, preferred_element_type=F32)
           + jnp.dot(q_r, ckr_ref[...].astype(BF16), preferred_element_type=F32))
    s_n = (lax.dot_general(q_lat, kn, nt, preferred_element_type=F32)
           + lax.dot_general(q_r, nkr_ref[...].astype(BF16), nt, preferred_element_type=F32))
    m = jnp.maximum(jnp.max(s_c, axis=-1, keepdims=True), jnp.max(s_n, axis=-1, keepdims=True))
    p_c = jnp.exp2(s_c - m)
    p_n = jnp.exp2(s_n - m)
    l_row = jnp.sum(p_c, axis=-1, keepdims=True) + jnp.sum(p_n, axis=-1, keepdims=True)
    o_lat = (jnp.dot(p_c.astype(BF16), kc, preferred_element_type=F32)
             + jnp.dot(p_n.astype(BF16), kn, preferred_element_type=F32)) / l_row
    o_lat = o_lat.astype(BF16)
    hw = MLA_NOPE + MLA_V
    half = MLA_HEADS // 2
    for h in range(MLA_HEADS):
        w_uv = wukv_ref[:, h * hw + MLA_NOPE:(h + 1) * hw]
        o_h = jnp.dot(o_lat[h * sq:(h + 1) * sq, :], w_uv, preferred_element_type=F32)
        gb_ref = gb0_ref if h < half else gb1_ref
        gb = gb_ref[:, (h % half) * MLA_V:(h % half + 1) * MLA_V].astype(F32)
        o_ref[:, h * MLA_V:(h + 1) * MLA_V] = (_sigmoid(gb) * o_h).astype(BF16)


def _attn_sample_call(q_lat, q_rope, ckv_new, kr_new, cache_ckv, cache_kr_t, proj, w_ukv_r, l, batch, sq):
    assert (PAST_LEN + sq - 1) // CHUNK <= PAST_LEN // CHUNK
    past = cache_ckv.shape[2]
    lat_w = MLA_HEADS * MLA_KV_LORA
    rope_w = MLA_HEADS * MLA_ROPE
    ww = MLA_HEADS * (MLA_NOPE + MLA_V)
    gw = D_MODEL // 2
    gb_blk = COL_GB // gw
    rows = MLA_HEADS * sq
    est = (2 * (sq * (lat_w + rope_w) * 2 + past * (MLA_KV_LORA + V7X_LANES) * 4 + MLA_KV_LORA * ww * 2
                + 4 * sq * D_MODEL * 4) + past * (MLA_KV_LORA + V7X_LANES) * 2 + 6 * rows * past * 4
           + 4 * rows * MLA_KV_LORA * 4 + (4 << 20))
    return pl.pallas_call(
        functools.partial(_attn_sample_kernel, sq=sq),
        grid=(batch,),
        in_specs=[pl.BlockSpec((sq, lat_w), lambda b: (b, 0)),
                  pl.BlockSpec((sq, rope_w), lambda b: (b, 0)),
                  pl.BlockSpec((None, None, past, MLA_KV_LORA), lambda b: (l, b, 0, 0)),
                  pl.BlockSpec((None, None, MLA_ROPE, past), lambda b: (l, b, 0, 0)),
                  pl.BlockSpec((sq, MLA_KV_LORA), lambda b: (b, 0)),
                  pl.BlockSpec((sq, MLA_ROPE), lambda b: (b, 0)),
                  pl.BlockSpec((sq, gw), lambda b: (b, gb_blk)),
                  pl.BlockSpec((sq, gw), lambda b: (b, gb_blk + 1)),
                  pl.BlockSpec((None, MLA_KV_LORA, ww), lambda b: (l, 0, 0))],
        out_specs=pl.BlockSpec((sq, D_MODEL), lambda b: (b, 0)),
        out_shape=jax.ShapeDtypeStruct((batch * sq, D_MODEL), BF16),
        compiler_params=_cparams(("parallel",), est),
        name="mla_attention_sample",
    )(q_lat, q_rope, cache_ckv, cache_kr_t, ckv_new, kr_new, proj, proj, w_ukv_r)


def _merge_kernel(a_ref, b_ref, x_ref, g1_ref, w_ref, o_ref):
    m = (a_ref[...].astype(F32) + b_ref[...].astype(F32)).astype(BF16)
    mix = jnp.dot(m, w_ref[...], preferred_element_type=F32)
    o_ref[...] = _gate_res(x_ref[...], g1_ref[...], mix)


def _merge_call(a_part, b_part, x, mod, w_o_b, l, tm, groups):
    m = x.shape[0]
    est = (2 * 2 * tm * D_MODEL * 2 + 4 * tm * D_MODEL * 4 + D_MODEL * D_MODEL * 2 + 3 * tm * D_MODEL * 4
           + (4 << 20))
    return pl.pallas_call(
        _merge_kernel,
        grid=(m // tm,),
        in_specs=[pl.BlockSpec((tm, D_MODEL), lambda i: (i, 0)),
                  pl.BlockSpec((tm, D_MODEL), lambda i: (i, 0)),
                  pl.BlockSpec((tm, D_MODEL), lambda i: (i, 0)),
                  pl.BlockSpec((None, None, None, groups, D_MODEL), lambda i: (l, 2, i, 0, 0)),
                  pl.BlockSpec((None, D_MODEL, D_MODEL), lambda i: (l, 0, 0), pipeline_mode=pl.Buffered(1))],
        out_specs=pl.BlockSpec((tm, D_MODEL), lambda i: (i, 0)),
        out_shape=jax.ShapeDtypeStruct((m, D_MODEL), F32),
        compiler_params=_cparams(("parallel",), est),
        name="merge_out_proj",
    )(a_part, b_part, x, mod, w_o_b)


def _ffn_kernel(x_ref, sc_ref, sh_ref, g2_ref, gn_ref, fg_ref, wg_ref, wu_ref, wo_ref, o_ref, *rest,
                nh, final, emit_bf16):
    h_scr, acc_scr = rest[-2:]
    j = pl.program_id(1)

    @pl.when(j == 0)
    def _():
        h_scr[...] = _norm_mod(x_ref[...], gn_ref[...], sc_ref[...], sh_ref[...]).astype(BF16)
        acc_scr[...] = jnp.zeros(acc_scr.shape, F32)

    wg = wg_ref[...].astype(BF16)
    wu = wu_ref[...].astype(BF16)
    wo = wo_ref[...].astype(BF16)
    if emit_bf16:
        wgb_ref, wub_ref, wob_ref = rest[:3]
        wgb_ref[...] = wg
        wub_ref[...] = wu
        wob_ref[...] = wo
    hb = h_scr[...]
    gate = jnp.dot(hb, wg, preferred_element_type=F32)
    up = jnp.dot(hb, wu, preferred_element_type=F32)
    act = ((gate * _sigmoid(gate)) * up).astype(BF16)
    acc_scr[...] += jnp.dot(act, wo, preferred_element_type=F32)

    @pl.when(j == nh - 1)
    def _():
        y = _gate_res(x_ref[...], g2_ref[...], acc_scr[...])
        o_ref[...] = _rms(y, fg_ref[...]) if final else y


def _ffn_call(x, mod, norm_g, norm_final, weights, l, tm, groups, th):
    m = x.shape[0]
    nh = FFN_HIDDEN // th
    emit_bf16 = len(weights) == 2
    assert not emit_bf16 or m == tm
    wbytes = 4 if emit_bf16 else 2
    est = (4 * tm * D_MODEL * 4 + tm * D_MODEL * 2 + tm * D_MODEL * 4 + 2 * 3 * D_MODEL * th * wbytes
           + (2 * 3 + 3) * D_MODEL * th * 2 * emit_bf16 + 4 * tm * th * 4 + tm * D_MODEL * 4 + (4 << 20))
    mod_spec = lambda k: pl.BlockSpec((None, None, None, groups, D_MODEL), lambda i, j: (l, k, i, 0, 0))
    out_specs = [pl.BlockSpec((tm, D_MODEL), lambda i, j: (i, 0))]
    out_shape = [jax.ShapeDtypeStruct((m, D_MODEL), F32)]
    if emit_bf16:
        w_in, w_out = weights
        w_specs = [pl.BlockSpec((None, D_MODEL, th), lambda i, j: (l, 0, j)),
                   pl.BlockSpec((None, D_MODEL, th), lambda i, j: (l, 0, nh + j)),
                   pl.BlockSpec((None, th, D_MODEL), lambda i, j: (l, j, 0))]
        w_args = [w_in, w_in, w_out]
        out_specs += [pl.BlockSpec((D_MODEL, th), lambda i, j: (0, j)),
                      pl.BlockSpec((D_MODEL, th), lambda i, j: (0, j)),
                      pl.BlockSpec((th, D_MODEL), lambda i, j: (j, 0))]
        out_shape += [jax.ShapeDtypeStruct((D_MODEL, FFN_HIDDEN), BF16),
                      jax.ShapeDtypeStruct((D_MODEL, FFN_HIDDEN), BF16),
                      jax.ShapeDtypeStruct((FFN_HIDDEN, D_MODEL), BF16)]
    else:
        w_specs = [pl.BlockSpec((D_MODEL, th), lambda i, j: (0, j)),
                   pl.BlockSpec((D_MODEL, th), lambda i, j: (0, j)),
                   pl.BlockSpec((th, D_MODEL), lambda i, j: (j, 0))]
        w_args = list(weights)
    return pl.pallas_call(
        functools.partial(_ffn_kernel, nh=nh, final=(l == DEPTH - 1), emit_bf16=emit_bf16),
        grid=(m // tm, nh),
        in_specs=[pl.BlockSpec((tm, D_MODEL), lambda i, j: (i, 0)),
                  mod_spec(4), mod_spec(3), mod_spec(5),
                  pl.BlockSpec((None, 1, D_MODEL), lambda i, j: (l, 0, 0)),
                  pl.BlockSpec((1, D_MODEL), lambda i, j: (0, 0))] + w_specs,
        out_specs=out_specs,
        out_shape=out_shape,
        scratch_shapes=[pltpu.VMEM((tm, D_MODEL), BF16), pltpu.VMEM((tm, D_MODEL), F32)],
        compiler_params=_cparams(("parallel", "arbitrary"), est),
        name="ffn_swiglu",
    )(x, mod, mod, mod, norm_g, norm_final, *w_args)


def _rope_tables(pos, dim):
    inv = jnp.exp(-math.log(ROPE_BASE) * jnp.arange(0, dim, 2, dtype=F32) / dim)
    ang = pos.astype(F32)[:, None] * inv[None, :]
    return jnp.cos(ang), jnp.sin(ang)


def _mla_rope_tables(pos, reps):
    cos, sin = _rope_tables(pos, MLA_ROPE)
    z = jnp.zeros_like(cos)
    c = jnp.concatenate([cos, z, cos, z], axis=-1)
    s = jnp.concatenate([-sin, z, sin, z], axis=-1)
    return jnp.tile(c, (reps, 1)), jnp.tile(s, (reps, 1))


def _prep_weights(w_uq, w_ukv):
    uq = w_uq.reshape(DEPTH, MLA_Q_LORA, MLA_HEADS, MLA_NOPE + MLA_ROPE)
    uq = jnp.concatenate([uq[..., :MLA_NOPE], _pack_rope_lanes(uq[..., MLA_NOPE:])], axis=-1)
    w_uq_r = uq.reshape(DEPTH, MLA_Q_LORA, MLA_HEADS * HEAD_QK).astype(BF16)
    return w_uq_r, w_ukv.astype(BF16)


def _layer_prompt(l, x, mod, tm, batch, seq, ret_tabs, mla_tabs, zero_state, log_g, W, stacks, ffn_w):
    proj, kr_raw = _inproj_call(x, mod, W["norm_mix"], W["w_in_t"], l, IN_TM_PROMPT, 1, IN_TM_PROMPT // tm)
    ckv_prev, kr_prev, st_prev = stacks
    a_part, st_stack = _ret_call(proj, log_g, ret_tabs[0], ret_tabs[1], zero_state, 0, batch, seq,
                                 RET_L_PROMPT, RET_HEADS_PROMPT, l, st_prev)
    q_cat, ckv_stack, kr128, kr_stack = _mla_q_call(proj, kr_raw, W["q_norm"], W["kv_norm"], mla_tabs[0],
                                                    mla_tabs[1], W["w_uq_r"], l, tm, ckv_prev, kr_prev)
    k_cat, v = _mla_kv_call(ckv_stack, kr128, W["w_ukv_r"], l, batch * seq, tm)
    b_part = _attn_prompt_call(q_cat, k_cat, v, proj, batch, seq, ATT_TQ)
    x = _merge_call(a_part, b_part, x, mod, W["w_o"], l, tm, 1)
    x, = _ffn_call(x, mod, W["norm_ffn"], W["norm_final"], ffn_w, l, tm, 1, FFN_TH)
    return x, (ckv_stack, kr_stack, st_stack)


def _layer_sample(l, x, mod, groups, tm, batch, seq, ret_tabs, mla_tabs, state_ret, cache_ckv, cache_kr,
                  log_g, W, st_prev):
    proj, kr_raw = _inproj_call(x, mod, W["norm_mix"], W["w_in_t"], l, tm, groups, 1)
    a_part, st_stack = _ret_call(proj, log_g, ret_tabs[0], ret_tabs[1], state_ret, l, batch, seq, seq,
                                 RET_HEADS_SAMPLE, l, st_prev)
    q_lat, q_rope, ckv, kr = _mla_q_sample_call(proj, kr_raw, W["q_norm"], W["kv_norm"], mla_tabs[0],
                                                mla_tabs[1], W["w_uq_r"], W["w_ukv_r"], l)
    b_part = _attn_sample_call(q_lat, q_rope, ckv, kr, cache_ckv, cache_kr, proj, W["w_ukv_r"], l, batch, seq)
    x = _merge_call(a_part, b_part, x, mod, W["w_o"], l, tm, groups)
    x, *ffn_w = _ffn_call(x, mod, W["norm_ffn"], W["norm_final"], (W["w_ffn_in"], W["w_ffn_out"]), l, tm,
                          groups, FFN_TH_SAMPLE)
    return x, ckv, kr, st_stack, tuple(ffn_w)


def kernel(x_prompt, x_sample, c_prompt, c_sample, cache_mla_ckv, cache_mla_krope, state_ret, w_ada, b_ada,
           norm_mix, norm_ffn, w_in, mla_q_norm, w_uq, mla_kv_norm, w_ukv, w_o, w_ffn_in, w_ffn_out,
           norm_final):
    bp, sp, _ = x_prompt.shape
    bs, ss, _ = x_sample.shape
    tm = TOKEN_TILE
    assert sp % IN_TM_PROMPT == 0 and (bs * ss) % tm == 0 and tm % ss == 0

    w_uq_r, w_ukv_r = _prep_weights(w_uq, w_ukv)
    W = dict(w_in_t=jnp.swapaxes(w_in, 1, 2), w_uq_r=w_uq_r, w_ukv_r=w_ukv_r,
             w_o=w_o.astype(BF16), w_ffn_in=w_ffn_in, w_ffn_out=w_ffn_out,
             norm_mix=norm_mix.reshape(DEPTH, 1, D_MODEL), norm_ffn=norm_ffn.reshape(DEPTH, 1, D_MODEL),
             norm_final=norm_final.reshape(1, D_MODEL),
             q_norm=mla_q_norm.reshape(DEPTH, 1, MLA_Q_LORA), kv_norm=mla_kv_norm.reshape(DEPTH, 1, MLA_KV_LORA))

    c_rows = -(-(bp + bs) // ADA_ROW_ALIGN) * ADA_ROW_ALIGN
    c_all = jnp.concatenate([c_prompt, c_sample, jnp.zeros((c_rows - bp - bs, D_MODEL), F32)], axis=0)
    mod_all = _ada_call(c_all, w_ada, b_ada)[:, :bp + bs]
    mod_all = mod_all.reshape(DEPTH, bp + bs, 6, D_MODEL).transpose(0, 2, 1, 3)
    tiles_per_batch = sp // tm
    mod_p = jnp.repeat(mod_all[:, :, :bp], tiles_per_batch, axis=2)[:, :, :, None, :]
    groups_s = tm // ss
    mod_s = mod_all[:, :, bp:].reshape(DEPTH, 6, (bs * ss) // tm, groups_s, D_MODEL)

    log_g = jnp.log1p(-jnp.exp2(-RET_GAMMA_EXP0 - jnp.arange(RET_HEADS, dtype=F32)))
    pos_p = jnp.arange(sp)
    pos_s = PAST_LEN + jnp.arange(ss)
    ret_tabs_p = _rope_tables(pos_p, RET_DK)
    ret_tabs_s = _rope_tables(pos_s, RET_DK)
    mla_tabs_p = _mla_rope_tables(pos_p, bp)
    mla_tabs_s = _mla_rope_tables(pos_s, bs)
    zero_state = jnp.zeros((1, bp, RET_HEADS, RET_DK, RET_DV), F32)
    cache_kr_t = jnp.swapaxes(cache_mla_krope, 2, 3)

    xp = x_prompt.reshape(bp * sp, D_MODEL)
    xs = x_sample.reshape(bs * ss, D_MODEL)
    stacks_p = (None, None, None)
    st_s = None
    ckv_s, kr_s = [], []
    for l in range(DEPTH):
        xs, ckv, kr, st_s, ffn_w = _layer_sample(l, xs, mod_s, groups_s, tm, bs, ss, ret_tabs_s, mla_tabs_s,
                                                 state_ret, cache_mla_ckv, cache_kr_t, log_g, W, st_s)
        xp, stacks_p = _layer_prompt(l, xp, mod_p, tm, bp, sp, ret_tabs_p, mla_tabs_p, zero_state, log_g, W,
                                     stacks_p, ffn_w)
        ckv_s.append(ckv.reshape(bs, ss, MLA_KV_LORA))
        kr_s.append(kr.reshape(bs, ss, MLA_ROPE))

    y_prompt = xp.reshape(bp, sp, D_MODEL)
    y_sample = xs.reshape(bs, ss, D_MODEL)
    ckv_p, kr_p, st_p = stacks_p
    return (y_prompt, y_sample, ckv_p.reshape(DEPTH, bp, sp, MLA_KV_LORA), kr_p.reshape(DEPTH, bp, sp, MLA_ROPE),
            st_p, jnp.stack(ckv_s), jnp.stack(kr_s), st_s)
```

```python
import functools
import math

import jax
import jax.numpy as jnp
from jax import lax
from jax.experimental import pallas as pl
from jax.experimental.pallas import tpu as pltpu

D_MODEL = 2048
DEPTH = 4
PAST_LEN = 1024
CHUNK = 64
RET_HEADS = 8
RET_DK = D_MODEL // RET_HEADS
RET_DV = D_MODEL // RET_HEADS
MLA_HEADS = 16
MLA_Q_LORA = D_MODEL // 4
MLA_KV_LORA = D_MODEL // 4
MLA_NOPE = 128
MLA_ROPE = 64
MLA_V = D_MODEL // MLA_HEADS
FFN_HIDDEN = -(-8 * D_MODEL // (3 * 256)) * 256
ROPE_BASE = 10000.0
RET_GAMMA_EXP0 = 5.0
RMS_EPS = 1e-6
GN_EPS = 1e-5
NEG_INF = -1e30

F32 = jnp.float32
BF16 = jnp.bfloat16

V7X_LANES = 128
V7X_VMEM_LIMIT_CAP = 56 * 1024 * 1024

COL_RQ, COL_RK, COL_RV, COL_RG = 0, D_MODEL, 2 * D_MODEL, 3 * D_MODEL
COL_DQ = 4 * D_MODEL
COL_DKV = COL_DQ + MLA_Q_LORA
COL_GA = COL_DKV + MLA_KV_LORA
COL_GB = COL_GA + D_MODEL
MAIN_COLS = COL_GB + D_MODEL
HEAD_QK = 2 * V7X_LANES
HALF_ROPE = MLA_ROPE // 2
ADA_ROW_ALIGN = 16
TOKEN_TILE = 512
IN_TM_PROMPT = 2048
IN_TN = 512
ADA_TN = 1024
FFN_TH = 512
FFN_TH_SAMPLE = 256
RET_L_PROMPT = 256
RET_HEADS_PROMPT = 2
RET_HEADS_SAMPLE = 4
ATT_TQ = 1024
ATT_TK = 1024
ATT_AHEAD = 2
ATT_SPLIT = 4
QK_SCALE_LOG2 = (MLA_NOPE + MLA_ROPE) ** -0.5 * math.log2(math.e)


def _cparams(sem, est_bytes):
    return pltpu.CompilerParams(dimension_semantics=sem,
                                vmem_limit_bytes=min(int(est_bytes), V7X_VMEM_LIMIT_CAP))


def _sigmoid(x):
    return jax.nn.sigmoid(x)


def _rms(x, g):
    return x * lax.rsqrt(jnp.mean(x * x, axis=-1, keepdims=True) + RMS_EPS) * g


def _norm_mod(x, g, sc, sh):
    tm, d = x.shape
    groups = sc.shape[0]
    y = _rms(x, g)
    if groups == 1:
        return y * (1.0 + sc) + sh
    y3 = y.reshape(groups, tm // groups, d)
    return (y3 * (1.0 + sc[:, None, :]) + sh[:, None, :]).reshape(tm, d)


def _gate_res(x, gate, upd):
    tm, n = x.shape
    groups = gate.shape[0]
    if groups == 1:
        return x + gate * upd
    return x + (gate[:, None, :] * upd.reshape(groups, tm // groups, n)).reshape(tm, n)


def _pack_rope_lanes(x):
    z = jnp.zeros(x.shape[:-1] + (HALF_ROPE,), x.dtype)
    return jnp.concatenate([x[..., :HALF_ROPE], z, x[..., HALF_ROPE:], z], axis=-1)


def _unpack_rope_lanes(x):
    return jnp.concatenate([x[..., :HALF_ROPE], x[..., 2 * HALF_ROPE:3 * HALF_ROPE]], axis=-1)


def _ada_kernel(c_ref, w_ref, b_ref, o_ref):
    c = c_ref[...]
    a = (c * _sigmoid(c)).astype(BF16)
    o_ref[0] = jnp.dot(a, w_ref[0].astype(BF16), preferred_element_type=F32) + b_ref[0]


def _ada_call(c_all, w_ada, b_ada):
    nb = c_all.shape[0]
    n = w_ada.shape[-1]
    tn = ADA_TN
    est = 2 * (D_MODEL * tn * 4) + D_MODEL * tn * 2 + 4 * nb * (D_MODEL + 2 * tn) * 4 + (4 << 20)
    return pl.pallas_call(
        _ada_kernel,
        grid=(DEPTH, n // tn),
        in_specs=[pl.BlockSpec((nb, D_MODEL), lambda l, j: (0, 0)),
                  pl.BlockSpec((1, D_MODEL, tn), lambda l, j: (l, 0, j)),
                  pl.BlockSpec((1, 1, tn), lambda l, j: (l, 0, j))],
        out_specs=pl.BlockSpec((1, nb, tn), lambda l, j: (l, 0, j)),
        out_shape=jax.ShapeDtypeStruct((DEPTH, nb, n), F32),
        compiler_params=_cparams(("parallel", "parallel"), est),
        name="ada_mod",
    )(c_all, w_ada, b_ada.reshape(DEPTH, 1, n))


def _inproj_kernel(x_ref, sc_ref, sh_ref, g_ref, w_ref, wn_ref, wkr_ref, o_ref, kr_ref, h_scr, *, n_main):
    j = pl.program_id(1)
    nt = (((1,), (1,)), ((), ()))

    @pl.when(j == 0)
    def _():
        wkr = wkr_ref[...]
        z = jnp.zeros((HALF_ROPE, wkr.shape[1]), wkr.dtype)
        wkr = jnp.concatenate([wkr[:HALF_ROPE], z, wkr[HALF_ROPE:], z], axis=0).astype(BF16)
        tm = x_ref.shape[0]
        for r0 in range(0, tm, TOKEN_TILE):
            rs = pl.ds(r0, TOKEN_TILE)
            hb = _norm_mod(x_ref[rs, :], g_ref[...], sc_ref[...], sh_ref[...]).astype(BF16)
            h_scr[rs, :] = hb
            kr_ref[rs, :] = lax.dot_general(hb, wkr, nt, preferred_element_type=F32)

    @pl.when(j < n_main)
    def _():
        o_ref[...] = lax.dot_general(h_scr[...], w_ref[...].astype(BF16), nt,
                                     preferred_element_type=F32).astype(BF16)

    @pl.when(j >= n_main)
    def _():
        w = jnp.concatenate([w_ref[MLA_ROPE:, :], wn_ref[...]], axis=0).astype(BF16)
        o_ref[...] = lax.dot_general(h_scr[...], w, nt, preferred_element_type=F32).astype(BF16)


def _inproj_call(x, mod, norm_g, w_in_t, l, tm, groups, mod_stride):
    m = x.shape[0]
    n_main = COL_GA // IN_TN
    sub = IN_TN // MLA_ROPE
    assert groups == 1 or tm == TOKEN_TILE
    est = (tm * D_MODEL * 4 + tm * D_MODEL * 2 + 2 * D_MODEL * (IN_TN + 2 * MLA_ROPE) * 4
           + D_MODEL * IN_TN * 2 + 3 * tm * IN_TN * 4 + 6 * TOKEN_TILE * D_MODEL * 4 + (4 << 20))
    mod_spec = lambda k: pl.BlockSpec((None, None, None, groups, D_MODEL),
                                      lambda i, j: (l, k, i * mod_stride, 0, 0))
    return pl.pallas_call(
        functools.partial(_inproj_kernel, n_main=n_main),
        grid=(m // tm, MAIN_COLS // IN_TN),
        in_specs=[pl.BlockSpec((tm, D_MODEL), lambda i, j: (i, 0), pipeline_mode=pl.Buffered(1)),
                  mod_spec(1), mod_spec(0),
                  pl.BlockSpec((None, 1, D_MODEL), lambda i, j: (l, 0, 0)),
                  pl.BlockSpec((None, IN_TN, D_MODEL), lambda i, j: (l, j, 0)),
                  pl.BlockSpec((None, MLA_ROPE, D_MODEL),
                               lambda i, j: (l, (jnp.maximum(j, n_main) + 1) * sub, 0)),
                  pl.BlockSpec((None, MLA_ROPE, D_MODEL), lambda i, j: (l, COL_GA // MLA_ROPE, 0))],
        out_specs=[pl.BlockSpec((tm, IN_TN), lambda i, j: (i, j)),
                   pl.BlockSpec((tm, V7X_LANES), lambda i, j: (i, 0))],
        out_shape=[jax.ShapeDtypeStruct((m, MAIN_COLS), BF16),
                   jax.ShapeDtypeStruct((m, V7X_LANES), F32)],
        scratch_shapes=[pltpu.VMEM((tm, D_MODEL), BF16)],
        compiler_params=_cparams(("parallel", "arbitrary"), est),
        name="in_proj",
    )(x, mod, mod, norm_g, w_in_t, w_in_t, w_in_t)


def _ret_kernel(lg_ref, q_ref, k_ref, v_ref, rg_ref, ga_ref, cos_ref, sin_ref, s0_ref, *rest,
                chunk_len, heads):
    a_ref, st_ref, dm_scr = rest[-3:]
    hg = pl.program_id(1)
    c = pl.program_id(2)
    L = chunk_len
    lgs = [lg_ref[hg * heads + t] for t in range(heads)]

    @pl.when(c == 0)
    def _():
        st_ref[...] = s0_ref[...]
        ri = lax.broadcasted_iota(jnp.int32, (L, L), 0)
        ci = lax.broadcasted_iota(jnp.int32, (L, L), 1)
        diff = (ri - ci).astype(F32)
        for t in range(heads):
            dm_scr[t] = jnp.where(diff >= 0, jnp.exp(jnp.maximum(diff, 0.0) * lgs[t]), 0.0)

    cos = cos_ref[...]
    sin = sin_ref[...]
    half = RET_DK // 2
    idx = lax.broadcasted_iota(jnp.int32, (L, 1), 0).astype(F32)
    nt = (((1,), (1,)), ((), ()))
    tn = (((0,), (0,)), ((), ()))

    def rope(x):
        x1, x2 = x[:, :half], x[:, half:]
        return jnp.concatenate([x1 * cos - x2 * sin, x1 * sin + x2 * cos], axis=-1)

    stage = []
    for t in range(heads):
        cs = pl.ds(t * RET_DK, RET_DK)
        q = rope(q_ref[:, cs].astype(F32))
        k = rope(k_ref[:, cs].astype(F32)) * (RET_DK ** -0.5)
        vb = v_ref[:, cs]
        qb = q.astype(BF16)
        zeta = jnp.exp((L - 1.0 - idx) * lgs[t])
        g_l = jnp.exp(jnp.full((1, 1), float(L), F32) * lgs[t])
        st = st_ref[0, t]
        scores = lax.dot_general(qb, k.astype(BF16), nt, preferred_element_type=F32)
        cross = jnp.dot(qb, st.astype(BF16), preferred_element_type=F32)
        st_ref[0, t] = st * g_l + lax.dot_general((k * zeta).astype(BF16), vb, tn, preferred_element_type=F32)
        stage.append((scores, cross, vb))

    outs = []
    for t in range(heads):
        scores, cross, vb = stage[t]
        xi = jnp.exp((idx + 1.0) * lgs[t])
        outs.append(jnp.dot((scores * dm_scr[t]).astype(BF16), vb, preferred_element_type=F32) + cross * xi)

    for t in range(heads):
        cs = pl.ds(t * RET_DK, RET_DK)
        o = outs[t]
        mu = jnp.mean(o, axis=-1, keepdims=True)
        d = o - mu
        var = jnp.mean(d * d, axis=-1, keepdims=True)
        on = d * lax.rsqrt(var + GN_EPS)
        rg = rg_ref[:, cs].astype(F32)
        a_ref[:, cs] = (_sigmoid(ga_ref[:, cs].astype(F32)) * ((rg * _sigmoid(rg)) * on)).astype(BF16)


def _stacked_out(stack_prev, in_specs, args, out_index):
    in_specs.append(pl.BlockSpec(memory_space=pl.ANY))
    args.append(stack_prev)
    return {len(args) - 1: out_index}


def _ret_call(proj, log_g, cos, sin, state0, state_layer, batch, seq, chunk_len, heads, l, stack_prev):
    L = chunk_len
    nc = seq // L
    w = heads * RET_DK
    assert COL_GA % w == 0 and RET_HEADS % heads == 0
    col = lambda base: (lambda b, h, c: (b * nc + c, base // w + h))
    blk = lambda base: pl.BlockSpec((L, w), col(base))
    nh = RET_HEADS
    est = (2 * 6 * L * w * 4 + 4 * heads * RET_DK * RET_DV * 4 + heads * L * L * 4
           + 8 * heads * L * max(L, RET_DK) * 4 + (4 << 20))
    in_specs = [pl.BlockSpec(memory_space=pltpu.SMEM),
                blk(COL_RQ), blk(COL_RK), blk(COL_RV), blk(COL_RG), blk(COL_GA),
                pl.BlockSpec((L, RET_DK // 2), lambda b, h, c: (c, 0)),
                pl.BlockSpec((L, RET_DK // 2), lambda b, h, c: (c, 0)),
                pl.BlockSpec((None, 1, heads, RET_DK, RET_DV), lambda b, h, c: (state_layer, b, h, 0, 0))]
    args = [log_g, proj, proj, proj, proj, proj, cos, sin, state0]
    aliases = _stacked_out(stack_prev, in_specs, args, 1)
    return pl.pallas_call(
        functools.partial(_ret_kernel, chunk_len=L, heads=heads),
        grid=(batch, nh // heads, nc),
        in_specs=in_specs,
        out_specs=[pl.BlockSpec((L, w), lambda b, h, c: (b * nc + c, h)),
                   pl.BlockSpec((None, 1, heads, RET_DK, RET_DV), lambda b, h, c: (l, b, h, 0, 0))],
        out_shape=[jax.ShapeDtypeStruct((batch * seq, D_MODEL), BF16),
                   jax.ShapeDtypeStruct((DEPTH, batch, nh, RET_DK, RET_DV), F32)],
        scratch_shapes=[pltpu.VMEM((heads, L, L), F32)],
        input_output_aliases=aliases,
        compiler_params=_cparams(("parallel", "parallel", "arbitrary"), est),
        name="retention",
    )(*args)


def _rope128(x, c, s):
    return x * c + pltpu.roll(x, V7X_LANES // 2, 1) * s


def _mla_q_kernel(dq_ref, dkv_ref, kr_ref, qn_ref, kvn_ref, c_ref, s_ref, wuq_ref, *rest):
    q_out, ckv_out, kro_out, kr64_out = rest[-4:]
    c = c_ref[...]
    s = s_ref[...]
    cq = _rms(dq_ref[...].astype(F32), qn_ref[...]).astype(BF16)
    for h in range(MLA_HEADS):
        lo = h * HEAD_QK
        qh = jnp.dot(cq, wuq_ref[:, lo:lo + HEAD_QK], preferred_element_type=F32) * QK_SCALE_LOG2
        q_out[:, lo:lo + MLA_NOPE] = qh[:, :MLA_NOPE].astype(BF16)
        q_out[:, lo + MLA_NOPE:lo + HEAD_QK] = _rope128(qh[:, MLA_NOPE:], c, s).astype(BF16)
    ckv_out[...] = _rms(dkv_ref[...].astype(F32), kvn_ref[...])
    kr = _rope128(kr_ref[...], c, s)
    kro_out[...] = kr
    kr64_out[...] = _unpack_rope_lanes(kr)


def _mla_q_call(proj, kr_raw, q_norm, kv_norm, rope_c, rope_s, w_uq_r, l, tm, ckv_prev, kr_prev):
    m = proj.shape[0]
    dq_blk = COL_DQ // MLA_Q_LORA
    qw = MLA_HEADS * HEAD_QK
    est = (2 * (2 * tm * MLA_Q_LORA * 4 + 3 * tm * V7X_LANES * 4) + 2 * MLA_Q_LORA * qw * 2
           + 2 * tm * qw * 2 + 2 * tm * MLA_KV_LORA * 4 + 2 * tm * V7X_LANES * 4 + 8 * tm * HEAD_QK * 4
           + (4 << 20))
    in_specs = [pl.BlockSpec((tm, MLA_Q_LORA), lambda i: (i, dq_blk)),
                pl.BlockSpec((tm, MLA_KV_LORA), lambda i: (i, dq_blk + 1)),
                pl.BlockSpec((tm, V7X_LANES), lambda i: (i, 0)),
                pl.BlockSpec((None, 1, MLA_Q_LORA), lambda i: (l, 0, 0)),
                pl.BlockSpec((None, 1, MLA_KV_LORA), lambda i: (l, 0, 0)),
                pl.BlockSpec((tm, V7X_LANES), lambda i: (i, 0)),
                pl.BlockSpec((tm, V7X_LANES), lambda i: (i, 0)),
                pl.BlockSpec((None, MLA_Q_LORA, qw), lambda i: (l, 0, 0))]
    args = [proj, proj, kr_raw, q_norm, kv_norm, rope_c, rope_s, w_uq_r]
    aliases = _stacked_out(ckv_prev, in_specs, args, 1)
    aliases.update(_stacked_out(kr_prev, in_specs, args, 3))
    return pl.pallas_call(
        _mla_q_kernel,
        grid=(m // tm,),
        in_specs=in_specs,
        out_specs=[pl.BlockSpec((tm, qw), lambda i: (i, 0)),
                   pl.BlockSpec((None, tm, MLA_KV_LORA), lambda i: (l, i, 0)),
                   pl.BlockSpec((tm, V7X_LANES), lambda i: (i, 0)),
                   pl.BlockSpec((None, tm, MLA_ROPE), lambda i: (l, i, 0))],
        out_shape=[jax.ShapeDtypeStruct((m, qw), BF16),
                   jax.ShapeDtypeStruct((DEPTH, m, MLA_KV_LORA), F32),
                   jax.ShapeDtypeStruct((m, V7X_LANES), F32),
                   jax.ShapeDtypeStruct((DEPTH, m, MLA_ROPE), F32)],
        input_output_aliases=aliases,
        compiler_params=_cparams(("parallel",), est),
        name="mla_latents",
    )(*args)


def _mla_q_sample_kernel(dq_ref, dkv_ref, kr_ref, qn_ref, kvn_ref, c_ref, s_ref, wuq_ref, wukv_ref,
                         qlat_out, qr_out, ckv_out, kro_out):
    c = c_ref[...]
    s = s_ref[...]
    cq = _rms(dq_ref[...].astype(F32), qn_ref[...]).astype(BF16)
    for h in range(MLA_HEADS):
        lo = h * HEAD_QK
        qh = jnp.dot(cq, wuq_ref[:, lo:lo + HEAD_QK], preferred_element_type=F32) * QK_SCALE_LOG2
        w_uk = wukv_ref[:, h * (MLA_NOPE + MLA_V):h * (MLA_NOPE + MLA_V) + MLA_NOPE]
        q_lat = lax.dot_general(qh[:, :MLA_NOPE].astype(BF16), w_uk, (((1,), (1,)), ((), ())),
                                preferred_element_type=F32)
        qlat_out[:, h * MLA_KV_LORA:(h + 1) * MLA_KV_LORA] = q_lat.astype(BF16)
        qr_out[:, h * MLA_ROPE:(h + 1) * MLA_ROPE] = _unpack_rope_lanes(
            _rope128(qh[:, MLA_NOPE:], c, s)).astype(BF16)
    ckv_out[...] = _rms(dkv_ref[...].astype(F32), kvn_ref[...])
    kro_out[...] = _unpack_rope_lanes(_rope128(kr_ref[...], c, s))


def _mla_q_sample_call(proj, kr_raw, q_norm, kv_norm, rope_c, rope_s, w_uq_r, w_ukv_r, l):
    m = proj.shape[0]
    dq_blk = COL_DQ // MLA_Q_LORA
    qw = MLA_HEADS * HEAD_QK
    ww = MLA_HEADS * (MLA_NOPE + MLA_V)
    lat_w = MLA_HEADS * MLA_KV_LORA
    rope_w = MLA_HEADS * MLA_ROPE
    est = (2 * (2 * m * MLA_Q_LORA * 4 + 3 * m * V7X_LANES * 4) + 2 * MLA_Q_LORA * (qw + ww) * 2
           + 2 * m * (lat_w + rope_w) * 2 + 2 * m * (MLA_KV_LORA + MLA_ROPE) * 4 + 8 * m * MLA_KV_LORA * 4
           + (4 << 20))
    full = lambda shape: pl.BlockSpec(shape, lambda i: (0,) * len(shape))
    return pl.pallas_call(
        _mla_q_sample_kernel,
        grid=(1,),
        in_specs=[pl.BlockSpec((m, MLA_Q_LORA), lambda i: (0, dq_blk)),
                  pl.BlockSpec((m, MLA_KV_LORA), lambda i: (0, dq_blk + 1)),
                  full((m, V7X_LANES)),
                  pl.BlockSpec((None, 1, MLA_Q_LORA), lambda i: (l, 0, 0)),
                  pl.BlockSpec((None, 1, MLA_KV_LORA), lambda i: (l, 0, 0)),
                  full((m, V7X_LANES)), full((m, V7X_LANES)),
                  pl.BlockSpec((None, MLA_Q_LORA, qw), lambda i: (l, 0, 0)),
                  pl.BlockSpec((None, MLA_KV_LORA, ww), lambda i: (l, 0, 0))],
        out_specs=[full((m, lat_w)), full((m, rope_w)), full((m, MLA_KV_LORA)), full((m, MLA_ROPE))],
        out_shape=[jax.ShapeDtypeStruct((m, lat_w), BF16),
                   jax.ShapeDtypeStruct((m, rope_w), BF16),
                   jax.ShapeDtypeStruct((m, MLA_KV_LORA), F32),
                   jax.ShapeDtypeStruct((m, MLA_ROPE), F32)],
        compiler_params=_cparams(("arbitrary",), est),
        name="mla_latents_sample",
    )(proj, proj, kr_raw, q_norm, kv_norm, rope_c, rope_s, w_uq_r, w_ukv_r)


def _mla_kv_kernel(ckv_ref, kr_ref, wukv_ref, kcat_out, v_out):
    cb = ckv_ref[...].astype(BF16)
    krb = kr_ref[...].astype(BF16)
    hw = MLA_NOPE + MLA_V
    for h in range(MLA_HEADS):
        kv = jnp.dot(cb, wukv_ref[:, h * hw:(h + 1) * hw], preferred_element_type=F32).astype(BF16)
        lo = h * HEAD_QK
        kcat_out[:, lo:lo + MLA_NOPE] = kv[:, :MLA_NOPE]
        kcat_out[:, lo + MLA_NOPE:lo + HEAD_QK] = krb
        v_out[:, h * MLA_V:(h + 1) * MLA_V] = kv[:, MLA_NOPE:]


def _mla_kv_call(ckv_stack, kr128, w_ukv_r, l, rows, tm):
    kw = MLA_HEADS * HEAD_QK
    vw = MLA_HEADS * MLA_V
    ww = MLA_HEADS * (MLA_NOPE + MLA_V)
    est = (2 * tm * (MLA_KV_LORA + V7X_LANES) * 4 + 2 * MLA_KV_LORA * ww * 2 + 2 * tm * (kw + vw) * 2
           + 8 * tm * HEAD_QK * 4 + (4 << 20))
    return pl.pallas_call(
        _mla_kv_kernel,
        grid=(rows // tm,),
        in_specs=[pl.BlockSpec((None, tm, MLA_KV_LORA), lambda i: (l, i, 0)),
                  pl.BlockSpec((tm, V7X_LANES), lambda i: (i, 0)),
                  pl.BlockSpec((None, MLA_KV_LORA, ww), lambda i: (l, 0, 0))],
        out_specs=[pl.BlockSpec((tm, kw), lambda i: (i, 0)),
                   pl.BlockSpec((tm, vw), lambda i: (i, 0))],
        out_shape=[jax.ShapeDtypeStruct((rows, kw), BF16),
                   jax.ShapeDtypeStruct((rows, vw), BF16)],
        compiler_params=_cparams(("parallel",), est),
        name="mla_kv_expand",
    )(ckv_stack, kr128, w_ukv_r)


def _scores(q, k_blk):
    return lax.dot_general(q, k_blk, (((1,), (1,)), ((), ())), preferred_element_type=F32)


def _softmax_tile(s, v_blk, m_prev, l_prev, acc_prev, tri):
    n_groups = s.shape[1] // V7X_LANES
    groups = [s[:, g * V7X_LANES:(g + 1) * V7X_LANES] for g in range(n_groups)]
    if tri is not None:
        n_tri = tri.shape[1] // V7X_LANES
        for t in range(n_tri):
            g = n_groups - n_tri + t
            groups[g] = jnp.where(tri[:, t * V7X_LANES:(t + 1) * V7X_LANES], groups[g], NEG_INF)
    m_new = jnp.maximum(m_prev, jnp.max(functools.reduce(jnp.maximum, groups), axis=-1, keepdims=True))
    alpha = jnp.exp2(m_prev - m_new)
    ps = [jnp.exp2(g - m_new) for g in groups]
    l_new = alpha * l_prev + functools.reduce(jnp.add, ps)
    p = jnp.concatenate([x.astype(BF16) for x in ps], axis=-1)
    acc_new = alpha * acc_prev + jnp.dot(p, v_blk, preferred_element_type=F32)
    return m_new, l_new, acc_new


def _attn_prompt_kernel(q_ref, k_ref, v_ref, gb_ref, o_ref, m_scr, l_scr, acc_scr, *, tq, tk, n_split):
    i = pl.program_id(2)
    rows = tq // n_split
    m_scr[...] = jnp.full(m_scr.shape, NEG_INF, F32)
    l_scr[...] = jnp.zeros(l_scr.shape, F32)
    acc_scr[...] = jnp.zeros(acc_scr.shape, F32)

    def scores(r, k_blk):
        return _scores(q_ref[pl.ds(r * rows, rows), :], k_blk)

    def update(r, s, v_blk, mask):
        rs = pl.ds(r * rows, rows)
        m_new, l_new, acc_new = _softmax_tile(s, v_blk, m_scr[rs, :], l_scr[rs, :], acc_scr[rs, :], mask)
        m_scr[rs, :] = m_new
        l_scr[rs, :] = l_new
        acc_scr[rs, :] = acc_new

    blocks_per_tile = tq // tk

    def body(j, carry):
        for d in range(blocks_per_tile):
            start = pl.multiple_of(j * tq + d * tk, tk)
            k_blk = k_ref[pl.ds(start, tk), :]
            v_blk = v_ref[pl.ds(start, tk), :]
            ss = [scores(r, k_blk) for r in range(min(ATT_AHEAD, n_split))]
            for r in range(n_split):
                if r + ATT_AHEAD < n_split:
                    ss.append(scores(r + ATT_AHEAD, k_blk))
                update(r, ss[r], v_blk, None)
        return carry

    lax.fori_loop(0, i, body, 0)

    tri = (lax.broadcasted_iota(jnp.int32, (rows, rows), 1) // CHUNK
           <= lax.broadcasted_iota(jnp.int32, (rows, rows), 0) // CHUNK)
    for d in range(blocks_per_tile):
        k0 = d * tk
        todo = []
        for r in range(n_split):
            r0, r1 = r * rows, (r + 1) * rows
            width = min(k0 + tk, r1) - k0
            if width <= 0:
                continue
            on_diagonal = k0 + width > r0
            assert not on_diagonal or (k0 + width == r1 and width >= rows)
            todo.append((r, width, tri if on_diagonal else None))
        start = pl.multiple_of(i * tq + k0, tk)
        diag_scores = lambda t: scores(todo[t][0], k_ref[pl.ds(start, todo[t][1]), :])
        ss = [diag_scores(t) for t in range(min(ATT_AHEAD, len(todo)))]
        for t, (r, width, mask) in enumerate(todo):
            if t + ATT_AHEAD < len(todo):
                ss.append(diag_scores(t + ATT_AHEAD))
            update(r, ss[t], v_ref[pl.ds(start, width), :], mask)

    l_row = jnp.sum(l_scr[...], axis=-1, keepdims=True)
    o_ref[...] = (_sigmoid(gb_ref[...].astype(F32)) * (acc_scr[...] / l_row)).astype(BF16)


def _attn_prompt_call(q_cat, k_cat, v, proj, batch, seq, tq):
    nq = seq // tq
    gb_blk = COL_GB // MLA_V
    est = (2 * (tq * HEAD_QK * 2 + seq * HEAD_QK * 2 + seq * MLA_V * 2 + 2 * tq * MLA_V * 4)
           + 3 * tq * V7X_LANES * 4 + 6 * tq * ATT_TK * 4 + (4 << 20))
    return pl.pallas_call(
        functools.partial(_attn_prompt_kernel, tq=tq, tk=ATT_TK, n_split=ATT_SPLIT),
        grid=(batch, MLA_HEADS, nq),
        in_specs=[pl.BlockSpec((tq, HEAD_QK), lambda b, h, i: (b * nq + i, h)),
                  pl.BlockSpec((seq, HEAD_QK), lambda b, h, i: (b, h)),
                  pl.BlockSpec((seq, MLA_V), lambda b, h, i: (b, h)),
                  pl.BlockSpec((tq, MLA_V), lambda b, h, i: (b * nq + i, gb_blk + h))],
        out_specs=pl.BlockSpec((tq, MLA_V), lambda b, h, i: (b * nq + i, h)),
        out_shape=jax.ShapeDtypeStruct((batch * seq, MLA_HEADS * MLA_V), BF16),
        scratch_shapes=[pltpu.VMEM((tq, V7X_LANES), F32), pltpu.VMEM((tq, V7X_LANES), F32),
                        pltpu.VMEM((tq, MLA_V), F32)],
        compiler_params=_cparams(("parallel", "parallel", "arbitrary"), est),
        name="mla_attention_prompt",
    )(q_cat, k_cat, v, proj)


def _attn_sample_kernel(qlat_ref, qr_ref, cc_ref, ckr_ref, nc_ref, nkr_ref, gb0_ref, gb1_ref, wukv_ref,
                        o_ref, *, sq):
    nt = (((1,), (1,)), ((), ()))
    q_lat = jnp.concatenate([qlat_ref[:, h * MLA_KV_LORA:(h + 1) * MLA_KV_LORA] for h in range(MLA_HEADS)],
                            axis=0)
    q_r = jnp.concatenate([qr_ref[:, h * MLA_ROPE:(h + 1) * MLA_ROPE] for h in range(MLA_HEADS)], axis=0)
    kc = cc_ref[...].astype(BF16)
    kn = nc_ref[...].astype(BF16)
    s_c = (lax.dot_general(q_lat, kc, nt, preferred_element_type=F32)
           + jnp.dot(q_r, ckr_ref[...].astype(BF16), preferred_element_type=F32))
    s_n = (lax.dot_general(q_lat, kn, nt, preferred_element_type=F32)
           + lax.dot_general(q_r, nkr_ref[...].astype(BF16), nt, preferred_element_type=F32))
    m = jnp.maximum(jnp.max(s_c, axis=-1, keepdims=True), jnp.max(s_n, axis=-1, keepdims=True))
    p_c = jnp.exp2(s_c - m)
    p_n = jnp.exp2(s_n - m)
    l_row = jnp.sum(p_c, axis=-1, keepdims=True) + jnp.sum(p_n, axis=-1, keepdims=True)
    o_lat = (jnp.dot(p_c.astype(BF16), kc, preferred_element_type=F32)
             + jnp.dot(p_n.astype(BF16), kn, preferred_element_type=F32)) / l_row
    o_lat = o_lat.astype(BF16)
    hw = MLA_NOPE + MLA_V
    half = MLA_HEADS // 2
    for h in range(MLA_HEADS):
        w_uv = wukv_ref[:, h * hw + MLA_NOPE:(h + 1) * hw]
        o_h = jnp.dot(o_lat[h * sq:(h + 1) * sq, :], w_uv, preferred_element_type=F32)
        gb_ref = gb0_ref if h < half else gb1_ref
        gb = gb_ref[:, (h % half) * MLA_V:(h % half + 1) * MLA_V].astype(F32)
        o_ref[:, h * MLA_V:(h + 1) * MLA_V] = (_sigmoid(gb) * o_h).astype(BF16)


def _attn_sample_call(q_lat, q_rope, ckv_new, kr_new, cache_ckv, cache_kr_t, proj, w_ukv_r, l, batch, sq):
    assert (PAST_LEN + sq - 1) // CHUNK <= PAST_LEN // CHUNK
    past = cache_ckv.shape[2]
    lat_w = MLA_HEADS * MLA_KV_LORA
    rope_w = MLA_HEADS * MLA_ROPE
    ww = MLA_HEADS * (MLA_NOPE + MLA_V)
    gw = D_MODEL // 2
    gb_blk = COL_GB // gw
    rows = MLA_HEADS * sq
    est = (2 * (sq * (lat_w + rope_w) * 2 + past * (MLA_KV_LORA + V7X_LANES) * 4 + MLA_KV_LORA * ww * 2
                + 4 * sq * D_MODEL * 4) + past * (MLA_KV_LORA + V7X_LANES) * 2 + 6 * rows * past * 4
           + 4 * rows * MLA_KV_LORA * 4 + (4 << 20))
    return pl.pallas_call(
        functools.partial(_attn_sample_kernel, sq=sq),
        grid=(batch,),
        in_specs=[pl.BlockSpec((sq, lat_w), lambda b: (b, 0)),
                  pl.BlockSpec((sq, rope_w), lambda b: (b, 0)),
                  pl.BlockSpec((None, None, past, MLA_KV_LORA), lambda b: (l, b, 0, 0)),
                  pl.BlockSpec((None, None, MLA_ROPE, past), lambda b: (l, b, 0, 0)),
                  pl.BlockSpec((sq, MLA_KV_LORA), lambda b: (b, 0)),
                  pl.BlockSpec((sq, MLA_ROPE), lambda b: (b, 0)),
                  pl.BlockSpec((sq, gw), lambda b: (b, gb_blk)),
                  pl.BlockSpec((sq, gw), lambda b: (b, gb_blk + 1)),
                  pl.BlockSpec((None, MLA_KV_LORA, ww), lambda b: (l, 0, 0))],
        out_specs=pl.BlockSpec((sq, D_MODEL), lambda b: (b, 0)),
        out_shape=jax.ShapeDtypeStruct((batch * sq, D_MODEL), BF16),
        compiler_params=_cparams(("parallel",), est),
        name="mla_attention_sample",
    )(q_lat, q_rope, cache_ckv, cache_kr_t, ckv_new, kr_new, proj, proj, w_ukv_r)


def _merge_kernel(a_ref, b_ref, x_ref, g1_ref, w_ref, o_ref):
    m = (a_ref[...].astype(F32) + b_ref[...].astype(F32)).astype(BF16)
    mix = jnp.dot(m, w_ref[...], preferred_element_type=F32)
    o_ref[...] = _gate_res(x_ref[...], g1_ref[...], mix)


def _merge_call(a_part, b_part, x, mod, w_o_b, l, tm, groups):
    m = x.shape[0]
    est = (2 * 2 * tm * D_MODEL * 2 + 4 * tm * D_MODEL * 4 + D_MODEL * D_MODEL * 2 + 3 * tm * D_MODEL * 4
           + (4 << 20))
    return pl.pallas_call(
        _merge_kernel,
        grid=(m // tm,),
        in_specs=[pl.BlockSpec((tm, D_MODEL), lambda i: (i, 0)),
                  pl.BlockSpec((tm, D_MODEL), lambda i: (i, 0)),
                  pl.BlockSpec((tm, D_MODEL), lambda i: (i, 0)),
                  pl.BlockSpec((None, None, None, groups, D_MODEL), lambda i: (l, 2, i, 0, 0)),
                  pl.BlockSpec((None, D_MODEL, D_MODEL), lambda i: (l, 0, 0), pipeline_mode=pl.Buffered(1))],
        out_specs=pl.BlockSpec((tm, D_MODEL), lambda i: (i, 0)),
        out_shape=jax.ShapeDtypeStruct((m, D_MODEL), F32),
        compiler_params=_cparams(("parallel",), est),
        name="merge_out_proj",
    )(a_part, b_part, x, mod, w_o_b)


def _ffn_kernel(x_ref, sc_ref, sh_ref, g2_ref, gn_ref, fg_ref, wg_ref, wu_ref, wo_ref, o_ref, *rest,
                nh, final, emit_bf16):
    h_scr, acc_scr = rest[-2:]
    j = pl.program_id(1)

    @pl.when(j == 0)
    def _():
        h_scr[...] = _norm_mod(x_ref[...], gn_ref[...], sc_ref[...], sh_ref[...]).astype(BF16)
        acc_scr[...] = jnp.zeros(acc_scr.shape, F32)

    wg = wg_ref[...].astype(BF16)
    wu = wu_ref[...].astype(BF16)
    wo = wo_ref[...].astype(BF16)
    if emit_bf16:
        wgb_ref, wub_ref, wob_ref = rest[:3]
        wgb_ref[...] = wg
        wub_ref[...] = wu
        wob_ref[...] = wo
    hb = h_scr[...]
    gate = jnp.dot(hb, wg, preferred_element_type=F32)
    up = jnp.dot(hb, wu, preferred_element_type=F32)
    act = ((gate * _sigmoid(gate)) * up).astype(BF16)
    acc_scr[...] += jnp.dot(act, wo, preferred_element_type=F32)

    @pl.when(j == nh - 1)
    def _():
        y = _gate_res(x_ref[...], g2_ref[...], acc_scr[...])
        o_ref[...] = _rms(y, fg_ref[...]) if final else y


def _ffn_call(x, mod, norm_g, norm_final, weights, l, tm, groups, th):
    m = x.shape[0]
    nh = FFN_HIDDEN // th
    emit_bf16 = len(weights) == 2
    assert not emit_bf16 or m == tm
    wbytes = 4 if emit_bf16 else 2
    est = (4 * tm * D_MODEL * 4 + tm * D_MODEL * 2 + tm * D_MODEL * 4 + 2 * 3 * D_MODEL * th * wbytes
           + (2 * 3 + 3) * D_MODEL * th * 2 * emit_bf16 + 4 * tm * th * 4 + tm * D_MODEL * 4 + (4 << 20))
    mod_spec = lambda k: pl.BlockSpec((None, None, None, groups, D_MODEL), lambda i, j: (l, k, i, 0, 0))
    out_specs = [pl.BlockSpec((tm, D_MODEL), lambda i, j: (i, 0))]
    out_shape = [jax.ShapeDtypeStruct((m, D_MODEL), F32)]
    if emit_bf16:
        w_in, w_out = weights
        w_specs = [pl.BlockSpec((None, D_MODEL, th), lambda i, j: (l, 0, j)),
                   pl.BlockSpec((None, D_MODEL, th), lambda i, j: (l, 0, nh + j)),
                   pl.BlockSpec((None, th, D_MODEL), lambda i, j: (l, j, 0))]
        w_args = [w_in, w_in, w_out]
        out_specs += [pl.BlockSpec((D_MODEL, th), lambda i, j: (0, j)),
                      pl.BlockSpec((D_MODEL, th), lambda i, j: (0, j)),
                      pl.BlockSpec((th, D_MODEL), lambda i, j: (j, 0))]
        out_shape += [jax.ShapeDtypeStruct((D_MODEL, FFN_HIDDEN), BF16),
                      jax.ShapeDtypeStruct((D_MODEL, FFN_HIDDEN), BF16),
                      jax.ShapeDtypeStruct((FFN_HIDDEN, D_MODEL), BF16)]
    else:
        w_specs = [pl.BlockSpec((D_MODEL, th), lambda i, j: (0, j)),
                   pl.BlockSpec((D_MODEL, th), lambda i, j: (0, j)),
                   pl.BlockSpec((th, D_MODEL), lambda i, j: (j, 0))]
        w_args = list(weights)
    return pl.pallas_call(
        functools.partial(_ffn_kernel, nh=nh, final=(l == DEPTH - 1), emit_bf16=emit_bf16),
        grid=(m // tm, nh),
        in_specs=[pl.BlockSpec((tm, D_MODEL), lambda i, j: (i, 0)),
                  mod_spec(4), mod_spec(3), mod_spec(5),
                  pl.BlockSpec((None, 1, D_MODEL), lambda i, j: (l, 0, 0)),
                  pl.BlockSpec((1, D_MODEL), lambda i, j: (0, 0))] + w_specs,
        out_specs=out_specs,
        out_shape=out_shape,
        scratch_shapes=[pltpu.VMEM((tm, D_MODEL), BF16), pltpu.VMEM((tm, D_MODEL), F32)],
        compiler_params=_cparams(("parallel", "arbitrary"), est),
        name="ffn_swiglu",
    )(x, mod, mod, mod, norm_g, norm_final, *w_args)


def _rope_tables(pos, dim):
    inv = jnp.exp(-math.log(ROPE_BASE) * jnp.arange(0, dim, 2, dtype=F32) / dim)
    ang = pos.astype(F32)[:, None] * inv[None, :]
    return jnp.cos(ang), jnp.sin(ang)


def _mla_rope_tables(pos, reps):
    cos, sin = _rope_tables(pos, MLA_ROPE)
    z = jnp.zeros_like(cos)
    c = jnp.concatenate([cos, z, cos, z], axis=-1)
    s = jnp.concatenate([-sin, z, sin, z], axis=-1)
    return jnp.tile(c, (reps, 1)), jnp.tile(s, (reps, 1))


def _prep_weights(w_uq, w_ukv):
    uq = w_uq.reshape(DEPTH, MLA_Q_LORA, MLA_HEADS, MLA_NOPE + MLA_ROPE)
    uq = jnp.concatenate([uq[..., :MLA_NOPE], _pack_rope_lanes(uq[..., MLA_NOPE:])], axis=-1)
    w_uq_r = uq.reshape(DEPTH, MLA_Q_LORA, MLA_HEADS * HEAD_QK).astype(BF16)
    return w_uq_r, w_ukv.astype(BF16)


def _layer_prompt(l, x, mod, tm, batch, seq, ret_tabs, mla_tabs, zero_state, log_g, W, stacks, ffn_w):
    proj, kr_raw = _inproj_call(x, mod, W["norm_mix"], W["w_in_t"], l, IN_TM_PROMPT, 1, IN_TM_PROMPT // tm)
    ckv_prev, kr_prev, st_prev = stacks
    a_part, st_stack = _ret_call(proj, log_g, ret_tabs[0], ret_tabs[1], zero_state, 0, batch, seq,
                                 RET_L_PROMPT, RET_HEADS_PROMPT, l, st_prev)
    q_cat, ckv_stack, kr128, kr_stack = _mla_q_call(proj, kr_raw, W["q_norm"], W["kv_norm"], mla_tabs[0],
                                                    mla_tabs[1], W["w_uq_r"], l, tm, ckv_prev, kr_prev)
    k_cat, v = _mla_kv_call(ckv_stack, kr128, W["w_ukv_r"], l, batch * seq, tm)
    b_part = _attn_prompt_call(q_cat, k_cat, v, proj, batch, seq, ATT_TQ)
    x = _merge_call(a_part, b_part, x, mod, W["w_o"], l, tm, 1)
    x, = _ffn_call(x, mod, W["norm_ffn"], W["norm_final"], ffn_w, l, tm, 1, FFN_TH)
    return x, (ckv_stack, kr_stack, st_stack)


def _layer_sample(l, x, mod, groups, tm, batch, seq, ret_tabs, mla_tabs, state_ret, cache_ckv, cache_kr,
                  log_g, W, st_prev):
    proj, kr_raw = _inproj_call(x, mod, W["norm_mix"], W["w_in_t"], l, tm, groups, 1)
    a_part, st_stack = _ret_call(proj, log_g, ret_tabs[0], ret_tabs[1], state_ret, l, batch, seq, seq,
                                 RET_HEADS_SAMPLE, l, st_prev)
    q_lat, q_rope, ckv, kr = _mla_q_sample_call(proj, kr_raw, W["q_norm"], W["kv_norm"], mla_tabs[0],
                                                mla_tabs[1], W["w_uq_r"], W["w_ukv_r"], l)
    b_part = _attn_sample_call(q_lat, q_rope, ckv, kr, cache_ckv, cache_kr, proj, W["w_ukv_r"], l, batch, seq)
    x = _merge_call(a_part, b_part, x, mod, W["w_o"], l, tm, groups)
    x, *ffn_w = _ffn_call(x, mod, W["norm_ffn"], W["norm_final"], (W["w_ffn_in"], W["w_ffn_out"]), l, tm,
                          groups, FFN_TH_SAMPLE)
    return x, ckv, kr, st_stack, tuple(ffn_w)


def kernel(x_prompt, x_sample, c_prompt, c_sample, cache_mla_ckv, cache_mla_krope, state_ret, w_ada, b_ada,
           norm_mix, norm_ffn, w_in, mla_q_norm, w_uq, mla_kv_norm, w_ukv, w_o, w_ffn_in, w_ffn_out,
           norm_final):
    bp, sp, _ = x_prompt.shape
    bs, ss, _ = x_sample.shape
    tm = TOKEN_TILE
    assert sp % IN_TM_PROMPT == 0 and (bs * ss) % tm == 0 and tm % ss == 0

    w_uq_r, w_ukv_r = _prep_weights(w_uq, w_ukv)
    W = dict(w_in_t=jnp.swapaxes(w_in, 1, 2), w_uq_r=w_uq_r, w_ukv_r=w_ukv_r,
             w_o=w_o.astype(BF16), w_ffn_in=w_ffn_in, w_ffn_out=w_ffn_out,
             norm_mix=norm_mix.reshape(DEPTH, 1, D_MODEL), norm_ffn=norm_ffn.reshape(DEPTH, 1, D_MODEL),
             norm_final=norm_final.reshape(1, D_MODEL),
             q_norm=mla_q_norm.reshape(DEPTH, 1, MLA_Q_LORA), kv_norm=mla_kv_norm.reshape(DEPTH, 1, MLA_KV_LORA))

    c_rows = -(-(bp + bs) // ADA_ROW_ALIGN) * ADA_ROW_ALIGN
    c_all = jnp.concatenate([c_prompt, c_sample, jnp.zeros((c_rows - bp - bs, D_MODEL), F32)], axis=0)
    mod_all = _ada_call(c_all, w_ada, b_ada)[:, :bp + bs]
    mod_all = mod_all.reshape(DEPTH, bp + bs, 6, D_MODEL).transpose(0, 2, 1, 3)
    tiles_per_batch = sp // tm
    mod_p = jnp.repeat(mod_all[:, :, :bp], tiles_per_batch, axis=2)[:, :, :, None, :]
    groups_s = tm // ss
    mod_s = mod_all[:, :, bp:].reshape(DEPTH, 6, (bs * ss) // tm, groups_s, D_MODEL)

    log_g = jnp.log1p(-jnp.exp2(-RET_GAMMA_EXP0 - jnp.arange(RET_HEADS, dtype=F32)))
    pos_p = jnp.arange(sp)
    pos_s = PAST_LEN + jnp.arange(ss)
    ret_tabs_p = _rope_tables(pos_p, RET_DK)
    ret_tabs_s = _rope_tables(pos_s, RET_DK)
    mla_tabs_p = _mla_rope_tables(pos_p, bp)
    mla_tabs_s = _mla_rope_tables(pos_s, bs)
    zero_state = jnp.zeros((1, bp, RET_HEADS, RET_DK, RET_DV), F32)
    cache_kr_t = jnp.swapaxes(cache_mla_krope, 2, 3)

    xp = x_prompt.reshape(bp * sp, D_MODEL)
    xs = x_sample.reshape(bs * ss, D_MODEL)
    stacks_p = (jnp.zeros((DEPTH, bp * sp, MLA_KV_LORA), F32), jnp.zeros((DEPTH, bp * sp, MLA_ROPE), F32),
                jnp.zeros((DEPTH, bp, RET_HEADS, RET_DK, RET_DV), F32))
    st_s = jnp.zeros((DEPTH, bs, RET_HEADS, RET_DK, RET_DV), F32)
    ckv_s, kr_s = [], []
    for l in range(DEPTH):
        xs, ckv, kr, st_s, ffn_w = _layer_sample(l, xs, mod_s, groups_s, tm, bs, ss, ret_tabs_s, mla_tabs_s,
                                                 state_ret, cache_mla_ckv, cache_kr_t, log_g, W, st_s)
        xp, stacks_p = _layer_prompt(l, xp, mod_p, tm, bp, sp, ret_tabs_p, mla_tabs_p, zero_state, log_g, W,
                                     stacks_p, ffn_w)
        ckv_s.append(ckv.reshape(bs, ss, MLA_KV_LORA))
        kr_s.append(kr.reshape(bs, ss, MLA_ROPE))

    y_prompt = xp.reshape(bp, sp, D_MODEL)
    y_sample = xs.reshape(bs, ss, D_MODEL)
    ckv_p, kr_p, st_p = stacks_p
    return (y_prompt, y_sample, ckv_p.reshape(DEPTH, bp, sp, MLA_KV_LORA), kr_p.reshape(DEPTH, bp, sp, MLA_ROPE),
            st_p, jnp.stack(ckv_s), jnp.stack(kr_s), st_s)
```

```python
import functools
import math

import jax
import jax.numpy as jnp
from jax import lax
from jax.experimental import pallas as pl
from jax.experimental.pallas import tpu as pltpu

D_MODEL = 2048
DEPTH = 4
PAST_LEN = 1024
CHUNK = 64
RET_HEADS = 8
RET_DK = D_MODEL // RET_HEADS
RET_DV = D_MODEL // RET_HEADS
MLA_HEADS = 16
MLA_Q_LORA = D_MODEL // 4
MLA_KV_LORA = D_MODEL // 4
MLA_NOPE = 128
MLA_ROPE = 64
MLA_V = D_MODEL // MLA_HEADS
FFN_HIDDEN = -(-8 * D_MODEL // (3 * 256)) * 256
ROPE_BASE = 10000.0
RET_GAMMA_EXP0 = 5.0
RMS_EPS = 1e-6
GN_EPS = 1e-5
NEG_INF = -1e30

F32 = jnp.float32
BF16 = jnp.bfloat16

V7X_LANES = 128
V7X_VMEM_LIMIT_CAP = 56 * 1024 * 1024
VMEM_TEMP_ALLOWANCE = 4 * 1024 * 1024

COL_RQ, COL_RK, COL_RV, COL_RG = 0, D_MODEL, 2 * D_MODEL, 3 * D_MODEL
COL_DQ = 4 * D_MODEL
COL_DKV = COL_DQ + MLA_Q_LORA
COL_GA = COL_DKV + MLA_KV_LORA
COL_GB = COL_GA + D_MODEL
MAIN_COLS = COL_GB + D_MODEL
HEAD_QK = 2 * V7X_LANES
HALF_ROPE = MLA_ROPE // 2
ADA_ROW_ALIGN = 16
TOKEN_TILE = 512
IN_TM_PROMPT = 2048
IN_TN = 512
ADA_TN = 1024
FFN_TH = 512
FFN_TH_SAMPLE = 256
RET_L_PROMPT = 256
RET_HEADS_PROMPT = 2
RET_HEADS_SAMPLE = 4
ATT_TQ = 1024
ATT_TK = 1024
ATT_AHEAD = 2
ATT_SPLIT = 4
QK_SCALE_LOG2 = (MLA_NOPE + MLA_ROPE) ** -0.5 * math.log2(math.e)


def _cparams(sem, est_bytes):
    return pltpu.CompilerParams(dimension_semantics=sem,
                                vmem_limit_bytes=min(int(est_bytes), V7X_VMEM_LIMIT_CAP))


def _sigmoid(x):
    return jax.nn.sigmoid(x)


def _rms(x, g):
    return x * lax.rsqrt(jnp.mean(x * x, axis=-1, keepdims=True) + RMS_EPS) * g


def _norm_mod(x, g, sc, sh):
    tm, d = x.shape
    groups = sc.shape[0]
    y = _rms(x, g)
    if groups == 1:
        return y * (1.0 + sc) + sh
    y3 = y.reshape(groups, tm // groups, d)
    return (y3 * (1.0 + sc[:, None, :]) + sh[:, None, :]).reshape(tm, d)


def _gate_res(x, gate, upd):
    tm, n = x.shape
    groups = gate.shape[0]
    if groups == 1:
        return x + gate * upd
    return x + (gate[:, None, :] * upd.reshape(groups, tm // groups, n)).reshape(tm, n)


def _pack_rope_lanes(x):
    z = jnp.zeros(x.shape[:-1] + (HALF_ROPE,), x.dtype)
    return jnp.concatenate([x[..., :HALF_ROPE], z, x[..., HALF_ROPE:], z], axis=-1)


def _unpack_rope_lanes(x):
    return jnp.concatenate([x[..., :HALF_ROPE], x[..., 2 * HALF_ROPE:3 * HALF_ROPE]], axis=-1)


def _ada_kernel(c_ref, w_ref, b_ref, o_ref):
    c = c_ref[...]
    a = (c * _sigmoid(c)).astype(BF16)
    o_ref[0] = jnp.dot(a, w_ref[0].astype(BF16), preferred_element_type=F32) + b_ref[0]


def _ada_call(c_all, w_ada, b_ada):
    nb = c_all.shape[0]
    n = w_ada.shape[-1]
    tn = ADA_TN
    est = 2 * (D_MODEL * tn * 4) + D_MODEL * tn * 2 + 4 * nb * (D_MODEL + 2 * tn) * 4 + VMEM_TEMP_ALLOWANCE
    return pl.pallas_call(
        _ada_kernel,
        grid=(DEPTH, n // tn),
        in_specs=[pl.BlockSpec((nb, D_MODEL), lambda l, j: (0, 0)),
                  pl.BlockSpec((1, D_MODEL, tn), lambda l, j: (l, 0, j)),
                  pl.BlockSpec((1, 1, tn), lambda l, j: (l, 0, j))],
        out_specs=pl.BlockSpec((1, nb, tn), lambda l, j: (l, 0, j)),
        out_shape=jax.ShapeDtypeStruct((DEPTH, nb, n), F32),
        compiler_params=_cparams(("parallel", "parallel"), est),
        name="ada_mod",
    )(c_all, w_ada, b_ada.reshape(DEPTH, 1, n))


def _inproj_kernel(x_hbm, sc_ref, sh_ref, g_ref, w_ref, wn_ref, wkr_ref, o_ref, kr_ref, h_scr, x_buf, x_sem,
                   *, n_main):
    i = pl.program_id(0)
    j = pl.program_id(1)
    tm = x_buf.shape[0]
    nt = (((1,), (1,)), ((), ()))

    def x_copy(tile):
        return pltpu.make_async_copy(x_hbm.at[pl.ds(tile * tm, tm), :], x_buf, x_sem)

    @pl.when((j == 0) & (i == 0))
    def _():
        x_copy(0).start()

    @pl.when(j == 0)
    def _():
        wkr = wkr_ref[...]
        z = jnp.zeros((HALF_ROPE, wkr.shape[1]), wkr.dtype)
        wkr = jnp.concatenate([wkr[:HALF_ROPE], z, wkr[HALF_ROPE:], z], axis=0).astype(BF16)
        x_copy(i).wait()
        for r0 in range(0, tm, TOKEN_TILE):
            rs = pl.ds(r0, TOKEN_TILE)
            hb = _norm_mod(x_buf[rs, :], g_ref[...], sc_ref[...], sh_ref[...]).astype(BF16)
            h_scr[rs, :] = hb
            kr_ref[rs, :] = lax.dot_general(hb, wkr, nt, preferred_element_type=F32)

    @pl.when((j == 1) & (i + 1 < pl.num_programs(0)))
    def _():
        x_copy(i + 1).start()

    @pl.when(j < n_main)
    def _():
        o_ref[...] = lax.dot_general(h_scr[...], w_ref[...].astype(BF16), nt,
                                     preferred_element_type=F32).astype(BF16)

    @pl.when(j >= n_main)
    def _():
        w = jnp.concatenate([w_ref[MLA_ROPE:, :], wn_ref[...]], axis=0).astype(BF16)
        o_ref[...] = lax.dot_general(h_scr[...], w, nt, preferred_element_type=F32).astype(BF16)


def _inproj_call(x, mod, norm_g, w_in_t, l, tm, groups, mod_stride):
    m = x.shape[0]
    n_main = COL_GA // IN_TN
    sub = IN_TN // MLA_ROPE
    assert groups == 1 or tm == TOKEN_TILE
    est = (tm * D_MODEL * 4 + tm * D_MODEL * 2 + 2 * D_MODEL * (IN_TN + 2 * MLA_ROPE) * 4
           + D_MODEL * IN_TN * 2 + 3 * tm * IN_TN * 4 + 6 * TOKEN_TILE * D_MODEL * 4 + VMEM_TEMP_ALLOWANCE)
    mod_spec = lambda k: pl.BlockSpec((None, None, None, groups, D_MODEL),
                                      lambda i, j: (l, k, i * mod_stride, 0, 0))
    return pl.pallas_call(
        functools.partial(_inproj_kernel, n_main=n_main),
        grid=(m // tm, MAIN_COLS // IN_TN),
        in_specs=[pl.BlockSpec(memory_space=pl.ANY),
                  mod_spec(1), mod_spec(0),
                  pl.BlockSpec((None, 1, D_MODEL), lambda i, j: (l, 0, 0)),
                  pl.BlockSpec((None, IN_TN, D_MODEL), lambda i, j: (l, j, 0)),
                  pl.BlockSpec((None, MLA_ROPE, D_MODEL),
                               lambda i, j: (l, (jnp.maximum(j, n_main) + 1) * sub, 0)),
                  pl.BlockSpec((None, MLA_ROPE, D_MODEL), lambda i, j: (l, COL_GA // MLA_ROPE, 0))],
        out_specs=[pl.BlockSpec((tm, IN_TN), lambda i, j: (i, j)),
                   pl.BlockSpec((tm, V7X_LANES), lambda i, j: (i, 0))],
        out_shape=[jax.ShapeDtypeStruct((m, MAIN_COLS), BF16),
                   jax.ShapeDtypeStruct((m, V7X_LANES), F32)],
        scratch_shapes=[pltpu.VMEM((tm, D_MODEL), BF16), pltpu.VMEM((tm, D_MODEL), F32),
                        pltpu.SemaphoreType.DMA(())],
        compiler_params=_cparams(("arbitrary", "arbitrary"), est),
        name="in_proj",
    )(x, mod, mod, norm_g, w_in_t, w_in_t, w_in_t)


def _ret_kernel(lg_ref, q_ref, k_ref, v_ref, rg_ref, ga_ref, cos_ref, sin_ref, s0_ref, *rest,
                chunk_len, heads):
    a_ref, st_ref, dm_scr = rest[-3:]
    hg = pl.program_id(1)
    c = pl.program_id(2)
    L = chunk_len
    lgs = [lg_ref[hg * heads + t] for t in range(heads)]

    @pl.when(c == 0)
    def _():
        st_ref[...] = s0_ref[...]
        ri = lax.broadcasted_iota(jnp.int32, (L, L), 0)
        ci = lax.broadcasted_iota(jnp.int32, (L, L), 1)
        diff = (ri - ci).astype(F32)
        for t in range(heads):
            dm_scr[t] = jnp.where(diff >= 0, jnp.exp(jnp.maximum(diff, 0.0) * lgs[t]), 0.0)

    cos = cos_ref[...]
    sin = sin_ref[...]
    half = RET_DK // 2
    idx = lax.broadcasted_iota(jnp.int32, (L, 1), 0).astype(F32)
    nt = (((1,), (1,)), ((), ()))
    tn = (((0,), (0,)), ((), ()))

    def rope(x):
        x1, x2 = x[:, :half], x[:, half:]
        return jnp.concatenate([x1 * cos - x2 * sin, x1 * sin + x2 * cos], axis=-1)

    stage = []
    for t in range(heads):
        cs = pl.ds(t * RET_DK, RET_DK)
        q = rope(q_ref[:, cs].astype(F32))
        k = rope(k_ref[:, cs].astype(F32)) * (RET_DK ** -0.5)
        vb = v_ref[:, cs]
        qb = q.astype(BF16)
        zeta = jnp.exp((L - 1.0 - idx) * lgs[t])
        g_l = jnp.exp(jnp.full((1, 1), float(L), F32) * lgs[t])
        st = st_ref[0, t]
        scores = lax.dot_general(qb, k.astype(BF16), nt, preferred_element_type=F32)
        cross = jnp.dot(qb, st.astype(BF16), preferred_element_type=F32)
        st_ref[0, t] = st * g_l + lax.dot_general((k * zeta).astype(BF16), vb, tn, preferred_element_type=F32)
        stage.append((scores, cross, vb))

    outs = []
    for t in range(heads):
        scores, cross, vb = stage[t]
        xi = jnp.exp((idx + 1.0) * lgs[t])
        outs.append(jnp.dot((scores * dm_scr[t]).astype(BF16), vb, preferred_element_type=F32) + cross * xi)

    for t in range(heads):
        cs = pl.ds(t * RET_DK, RET_DK)
        o = outs[t]
        mu = jnp.mean(o, axis=-1, keepdims=True)
        d = o - mu
        var = jnp.mean(d * d, axis=-1, keepdims=True)
        on = d * lax.rsqrt(var + GN_EPS)
        rg = rg_ref[:, cs].astype(F32)
        a_ref[:, cs] = (_sigmoid(ga_ref[:, cs].astype(F32)) * ((rg * _sigmoid(rg)) * on)).astype(BF16)


def _stacked_out(stack_prev, in_specs, args, out_index):
    in_specs.append(pl.BlockSpec(memory_space=pl.ANY))
    args.append(stack_prev)
    return {len(args) - 1: out_index}


def _ret_call(proj, log_g, cos, sin, state0, state_layer, batch, seq, chunk_len, heads, l, stack_prev):
    L = chunk_len
    nc = seq // L
    w = heads * RET_DK
    assert COL_GA % w == 0 and RET_HEADS % heads == 0
    col = lambda base: (lambda b, h, c: (b * nc + c, base // w + h))
    blk = lambda base: pl.BlockSpec((L, w), col(base))
    nh = RET_HEADS
    est = (2 * 6 * L * w * 4 + 4 * heads * RET_DK * RET_DV * 4 + heads * L * L * 4
           + 8 * heads * L * max(L, RET_DK) * 4 + VMEM_TEMP_ALLOWANCE)
    in_specs = [pl.BlockSpec(memory_space=pltpu.SMEM),
                blk(COL_RQ), blk(COL_RK), blk(COL_RV), blk(COL_RG), blk(COL_GA),
                pl.BlockSpec((L, RET_DK // 2), lambda b, h, c: (c, 0)),
                pl.BlockSpec((L, RET_DK // 2), lambda b, h, c: (c, 0)),
                pl.BlockSpec((None, 1, heads, RET_DK, RET_DV), lambda b, h, c: (state_layer, b, h, 0, 0))]
    args = [log_g, proj, proj, proj, proj, proj, cos, sin, state0]
    aliases = _stacked_out(stack_prev, in_specs, args, 1)
    return pl.pallas_call(
        functools.partial(_ret_kernel, chunk_len=L, heads=heads),
        grid=(batch, nh // heads, nc),
        in_specs=in_specs,
        out_specs=[pl.BlockSpec((L, w), lambda b, h, c: (b * nc + c, h)),
                   pl.BlockSpec((None, 1, heads, RET_DK, RET_DV), lambda b, h, c: (l, b, h, 0, 0))],
        out_shape=[jax.ShapeDtypeStruct((batch * seq, D_MODEL), BF16),
                   jax.ShapeDtypeStruct((DEPTH, batch, nh, RET_DK, RET_DV), F32)],
        scratch_shapes=[pltpu.VMEM((heads, L, L), F32)],
        input_output_aliases=aliases,
        compiler_params=_cparams(("parallel", "parallel", "arbitrary"), est),
        name="retention",
    )(*args)


def _rope128(x, c, s):
    return x * c + pltpu.roll(x, V7X_LANES // 2, 1) * s


def _mla_q_kernel(dq_ref, dkv_ref, kr_ref, qn_ref, kvn_ref, c_ref, s_ref, wuq_ref, *rest):
    q_out, ckv_out, kro_out, kr64_out = rest[-4:]
    c = c_ref[...]
    s = s_ref[...]
    cq = _rms(dq_ref[...].astype(F32), qn_ref[...]).astype(BF16)
    for h in range(MLA_HEADS):
        lo = h * HEAD_QK
        qh = jnp.dot(cq, wuq_ref[:, lo:lo + HEAD_QK], preferred_element_type=F32) * QK_SCALE_LOG2
        q_out[:, lo:lo + MLA_NOPE] = qh[:, :MLA_NOPE].astype(BF16)
        q_out[:, lo + MLA_NOPE:lo + HEAD_QK] = _rope128(qh[:, MLA_NOPE:], c, s).astype(BF16)
    ckv_out[...] = _rms(dkv_ref[...].astype(F32), kvn_ref[...])
    kr = _rope128(kr_ref[...], c, s)
    kro_out[...] = kr
    kr64_out[...] = _unpack_rope_lanes(kr)


def _mla_q_call(proj, kr_raw, q_norm, kv_norm, rope_c, rope_s, w_uq_r, l, tm, ckv_prev, kr_prev):
    m = proj.shape[0]
    dq_blk = COL_DQ // MLA_Q_LORA
    qw = MLA_HEADS * HEAD_QK
    est = (2 * (2 * tm * MLA_Q_LORA * 4 + 3 * tm * V7X_LANES * 4) + 2 * MLA_Q_LORA * qw * 2
           + 2 * tm * qw * 2 + 2 * tm * MLA_KV_LORA * 4 + 2 * tm * V7X_LANES * 4 + 8 * tm * HEAD_QK * 4
           + VMEM_TEMP_ALLOWANCE)
    in_specs = [pl.BlockSpec((tm, MLA_Q_LORA), lambda i: (i, dq_blk)),
                pl.BlockSpec((tm, MLA_KV_LORA), lambda i: (i, dq_blk + 1)),
                pl.BlockSpec((tm, V7X_LANES), lambda i: (i, 0)),
                pl.BlockSpec((None, 1, MLA_Q_LORA), lambda i: (l, 0, 0)),
                pl.BlockSpec((None, 1, MLA_KV_LORA), lambda i: (l, 0, 0)),
                pl.BlockSpec((tm, V7X_LANES), lambda i: (i, 0)),
                pl.BlockSpec((tm, V7X_LANES), lambda i: (i, 0)),
                pl.BlockSpec((None, MLA_Q_LORA, qw), lambda i: (l, 0, 0))]
    args = [proj, proj, kr_raw, q_norm, kv_norm, rope_c, rope_s, w_uq_r]
    aliases = _stacked_out(ckv_prev, in_specs, args, 1)
    aliases.update(_stacked_out(kr_prev, in_specs, args, 3))
    return pl.pallas_call(
        _mla_q_kernel,
        grid=(m // tm,),
        in_specs=in_specs,
        out_specs=[pl.BlockSpec((tm, qw), lambda i: (i, 0)),
                   pl.BlockSpec((None, tm, MLA_KV_LORA), lambda i: (l, i, 0)),
                   pl.BlockSpec((tm, V7X_LANES), lambda i: (i, 0)),
                   pl.BlockSpec((None, tm, MLA_ROPE), lambda i: (l, i, 0))],
        out_shape=[jax.ShapeDtypeStruct((m, qw), BF16),
                   jax.ShapeDtypeStruct((DEPTH, m, MLA_KV_LORA), F32),
                   jax.ShapeDtypeStruct((m, V7X_LANES), F32),
                   jax.ShapeDtypeStruct((DEPTH, m, MLA_ROPE), F32)],
        input_output_aliases=aliases,
        compiler_params=_cparams(("parallel",), est),
        name="mla_latents",
    )(*args)


def _mla_q_sample_kernel(dq_ref, dkv_ref, kr_ref, qn_ref, kvn_ref, c_ref, s_ref, wuq_ref, wukv_ref,
                         qlat_out, qr_out, ckv_out, kro_out):
    c = c_ref[...]
    s = s_ref[...]
    cq = _rms(dq_ref[...].astype(F32), qn_ref[...]).astype(BF16)
    for h in range(MLA_HEADS):
        lo = h * HEAD_QK
        qh = jnp.dot(cq, wuq_ref[:, lo:lo + HEAD_QK], preferred_element_type=F32) * QK_SCALE_LOG2
        w_uk = wukv_ref[:, h * (MLA_NOPE + MLA_V):h * (MLA_NOPE + MLA_V) + MLA_NOPE]
        q_lat = lax.dot_general(qh[:, :MLA_NOPE].astype(BF16), w_uk, (((1,), (1,)), ((), ())),
                                preferred_element_type=F32)
        qlat_out[:, h * MLA_KV_LORA:(h + 1) * MLA_KV_LORA] = q_lat.astype(BF16)
        qr_out[:, h * MLA_ROPE:(h + 1) * MLA_ROPE] = _unpack_rope_lanes(
            _rope128(qh[:, MLA_NOPE:], c, s)).astype(BF16)
    ckv_out[...] = _rms(dkv_ref[...].astype(F32), kvn_ref[...])
    kro_out[...] = _unpack_rope_lanes(_rope128(kr_ref[...], c, s))


def _mla_q_sample_call(proj, kr_raw, q_norm, kv_norm, rope_c, rope_s, w_uq_r, w_ukv_r, l):
    m = proj.shape[0]
    dq_blk = COL_DQ // MLA_Q_LORA
    qw = MLA_HEADS * HEAD_QK
    ww = MLA_HEADS * (MLA_NOPE + MLA_V)
    lat_w = MLA_HEADS * MLA_KV_LORA
    rope_w = MLA_HEADS * MLA_ROPE
    est = (2 * (2 * m * MLA_Q_LORA * 4 + 3 * m * V7X_LANES * 4) + 2 * MLA_Q_LORA * (qw + ww) * 2
           + 2 * m * (lat_w + rope_w) * 2 + 2 * m * (MLA_KV_LORA + MLA_ROPE) * 4 + 8 * m * MLA_KV_LORA * 4
           + VMEM_TEMP_ALLOWANCE)
    full = lambda shape: pl.BlockSpec(shape, lambda i: (0,) * len(shape))
    return pl.pallas_call(
        _mla_q_sample_kernel,
        grid=(1,),
        in_specs=[pl.BlockSpec((m, MLA_Q_LORA), lambda i: (0, dq_blk)),
                  pl.BlockSpec((m, MLA_KV_LORA), lambda i: (0, dq_blk + 1)),
                  full((m, V7X_LANES)),
                  pl.BlockSpec((None, 1, MLA_Q_LORA), lambda i: (l, 0, 0)),
                  pl.BlockSpec((None, 1, MLA_KV_LORA), lambda i: (l, 0, 0)),
                  full((m, V7X_LANES)), full((m, V7X_LANES)),
                  pl.BlockSpec((None, MLA_Q_LORA, qw), lambda i: (l, 0, 0)),
                  pl.BlockSpec((None, MLA_KV_LORA, ww), lambda i: (l, 0, 0))],
        out_specs=[full((m, lat_w)), full((m, rope_w)), full((m, MLA_KV_LORA)), full((m, MLA_ROPE))],
        out_shape=[jax.ShapeDtypeStruct((m, lat_w), BF16),
                   jax.ShapeDtypeStruct((m, rope_w), BF16),
                   jax.ShapeDtypeStruct((m, MLA_KV_LORA), F32),
                   jax.ShapeDtypeStruct((m, MLA_ROPE), F32)],
        compiler_params=_cparams(("arbitrary",), est),
        name="mla_latents_sample",
    )(proj, proj, kr_raw, q_norm, kv_norm, rope_c, rope_s, w_uq_r, w_ukv_r)


def _mla_kv_kernel(ckv_ref, kr_ref, wukv_ref, kcat_out, v_out):
    cb = ckv_ref[...].astype(BF16)
    krb = kr_ref[...].astype(BF16)
    hw = MLA_NOPE + MLA_V
    for h in range(MLA_HEADS):
        kv = jnp.dot(cb, wukv_ref[:, h * hw:(h + 1) * hw], preferred_element_type=F32).astype(BF16)
        lo = h * HEAD_QK
        kcat_out[:, lo:lo + MLA_NOPE] = kv[:, :MLA_NOPE]
        kcat_out[:, lo + MLA_NOPE:lo + HEAD_QK] = krb
        v_out[:, h * MLA_V:(h + 1) * MLA_V] = kv[:, MLA_NOPE:]


def _mla_kv_call(ckv_stack, kr128, w_ukv_r, l, rows, tm):
    kw = MLA_HEADS * HEAD_QK
    vw = MLA_HEADS * MLA_V
    ww = MLA_HEADS * (MLA_NOPE + MLA_V)
    est = (2 * tm * (MLA_KV_LORA + V7X_LANES) * 4 + 2 * MLA_KV_LORA * ww * 2 + 2 * tm * (kw + vw) * 2
           + 8 * tm * HEAD_QK * 4 + VMEM_TEMP_ALLOWANCE)
    return pl.pallas_call(
        _mla_kv_kernel,
        grid=(rows // tm,),
        in_specs=[pl.BlockSpec((None, tm, MLA_KV_LORA), lambda i: (l, i, 0)),
                  pl.BlockSpec((tm, V7X_LANES), lambda i: (i, 0)),
                  pl.BlockSpec((None, MLA_KV_LORA, ww), lambda i: (l, 0, 0))],
        out_specs=[pl.BlockSpec((tm, kw), lambda i: (i, 0)),
                   pl.BlockSpec((tm, vw), lambda i: (i, 0))],
        out_shape=[jax.ShapeDtypeStruct((rows, kw), BF16),
                   jax.ShapeDtypeStruct((rows, vw), BF16)],
        compiler_params=_cparams(("parallel",), est),
        name="mla_kv_expand",
    )(ckv_stack, kr128, w_ukv_r)


def _scores(q, k_blk):
    return lax.dot_general(q, k_blk, (((1,), (1,)), ((), ())), preferred_element_type=F32)


def _softmax_tile(s, v_blk, m_prev, l_prev, acc_prev, tri):
    n_groups = s.shape[1] // V7X_LANES
    groups = [s[:, g * V7X_LANES:(g + 1) * V7X_LANES] for g in range(n_groups)]
    if tri is not None:
        n_tri = tri.shape[1] // V7X_LANES
        for t in range(n_tri):
            g = n_groups - n_tri + t
            groups[g] = jnp.where(tri[:, t * V7X_LANES:(t + 1) * V7X_LANES], groups[g], NEG_INF)
    m_new = jnp.maximum(m_prev, jnp.max(functools.reduce(jnp.maximum, groups), axis=-1, keepdims=True))
    alpha = jnp.exp2(m_prev - m_new)
    ps = [jnp.exp2(g - m_new) for g in groups]
    l_new = alpha * l_prev + functools.reduce(jnp.add, ps)
    p = jnp.concatenate([x.astype(BF16) for x in ps], axis=-1)
    acc_new = alpha * acc_prev + jnp.dot(p, v_blk, preferred_element_type=F32)
    return m_new, l_new, acc_new


def _attn_prompt_kernel(q_ref, k_ref, v_ref, gb_ref, o_ref, m_scr, l_scr, acc_scr, *, tq, tk, n_split):
    i = pl.program_id(2)
    rows = tq // n_split
    m_scr[...] = jnp.full(m_scr.shape, NEG_INF, F32)
    l_scr[...] = jnp.zeros(l_scr.shape, F32)
    acc_scr[...] = jnp.zeros(acc_scr.shape, F32)

    def scores(r, k_blk):
        return _scores(q_ref[pl.ds(r * rows, rows), :], k_blk)

    def update(r, s, v_blk, mask):
        rs = pl.ds(r * rows, rows)
        m_new, l_new, acc_new = _softmax_tile(s, v_blk, m_scr[rs, :], l_scr[rs, :], acc_scr[rs, :], mask)
        m_scr[rs, :] = m_new
        l_scr[rs, :] = l_new
        acc_scr[rs, :] = acc_new

    blocks_per_tile = tq // tk

    def body(j, carry):
        for d in range(blocks_per_tile):
            start = pl.multiple_of(j * tq + d * tk, tk)
            k_blk = k_ref[pl.ds(start, tk), :]
            v_blk = v_ref[pl.ds(start, tk), :]
            ss = [scores(r, k_blk) for r in range(min(ATT_AHEAD, n_split))]
            for r in range(n_split):
                if r + ATT_AHEAD < n_split:
                    ss.append(scores(r + ATT_AHEAD, k_blk))
                update(r, ss[r], v_blk, None)
        return carry

    lax.fori_loop(0, i, body, 0)

    tri = (lax.broadcasted_iota(jnp.int32, (rows, rows), 1) // CHUNK
           <= lax.broadcasted_iota(jnp.int32, (rows, rows), 0) // CHUNK)
    for d in range(blocks_per_tile):
        k0 = d * tk
        todo = []
        for r in range(n_split):
            r0, r1 = r * rows, (r + 1) * rows
            width = min(k0 + tk, r1) - k0
            if width <= 0:
                continue
            on_diagonal = k0 + width > r0
            assert not on_diagonal or (k0 + width == r1 and width >= rows)
            todo.append((r, width, tri if on_diagonal else None))
        start = pl.multiple_of(i * tq + k0, tk)
        diag_scores = lambda t: scores(todo[t][0], k_ref[pl.ds(start, todo[t][1]), :])
        ss = [diag_scores(t) for t in range(min(ATT_AHEAD, len(todo)))]
        for t, (r, width, mask) in enumerate(todo):
            if t + ATT_AHEAD < len(todo):
                ss.append(diag_scores(t + ATT_AHEAD))
            update(r, ss[t], v_ref[pl.ds(start, width), :], mask)

    l_row = jnp.sum(l_scr[...], axis=-1, keepdims=True)
    o_ref[...] = (_sigmoid(gb_ref[...].astype(F32)) * (acc_scr[...] / l_row)).astype(BF16)


def _attn_prompt_call(q_cat, k_cat, v, proj, batch, seq, tq):
    nq = seq // tq
    gb_blk = COL_GB // MLA_V
    est = (2 * (tq * HEAD_QK * 2 + seq * HEAD_QK * 2 + seq * MLA_V * 2 + 2 * tq * MLA_V * 4)
           + 3 * tq * V7X_LANES * 4 + 6 * tq * ATT_TK * 4 + VMEM_TEMP_ALLOWANCE)
    return pl.pallas_call(
        functools.partial(_attn_prompt_kernel, tq=tq, tk=ATT_TK, n_split=ATT_SPLIT),
        grid=(batch, MLA_HEADS, nq),
        in_specs=[pl.BlockSpec((tq, HEAD_QK), lambda b, h, i: (b * nq + i, h)),
                  pl.BlockSpec((seq, HEAD_QK), lambda b, h, i: (b, h)),
                  pl.BlockSpec((seq, MLA_V), lambda b, h, i: (b, h)),
                  pl.BlockSpec((tq, MLA_V), lambda b, h, i: (b * nq + i, gb_blk + h))],
        out_specs=pl.BlockSpec((tq, MLA_V), lambda b, h, i: (b * nq + i, h)),
        out_shape=jax.ShapeDtypeStruct((batch * seq, MLA_HEADS * MLA_V), BF16),
        scratch_shapes=[pltpu.VMEM((tq, V7X_LANES), F32), pltpu.VMEM((tq, V7X_LANES), F32),
                        pltpu.VMEM((tq, MLA_V), F32)],
        compiler_params=_cparams(("parallel", "parallel", "arbitrary"), est),
        name="mla_attention_prompt",
    )(q_cat, k_cat, v, proj)


def _attn_sample_kernel(qlat_ref, qr_ref, cc_ref, ckr_ref, nc_ref, nkr_ref, gb0_ref, gb1_ref, wukv_ref,
                        o_ref, *, sq):
    nt = (((1,), (1,)), ((), ()))
    q_lat = jnp.concatenate([qlat_ref[:, h * MLA_KV_LORA:(h + 1) * MLA_KV_LORA] for h in range(MLA_HEADS)],
                            axis=0)
    q_r = jnp.concatenate([qr_ref[:, h * MLA_ROPE:(h + 1) * MLA_ROPE] for h in range(MLA_HEADS)], axis=0)
    kc = cc_ref[...].astype(BF16)
    kn = nc_ref[...].astype(BF16)
    s_c = (lax.dot_general(q_lat, kc, nt, preferred_element_type=F32)
           + jnp.dot(q_r, ckr_ref[...].astype(BF16), preferred_element_type=F32))
    s_n = (lax.dot_general(q_lat, kn, nt, preferred_element_type=F32)
           + lax.dot_general(q_r, nkr_ref[...].astype(BF16), nt, preferred_element_type=F32))
    m = jnp.maximum(jnp.max(s_c, axis=-1, keepdims=True), jnp.max(s_n, axis=-1, keepdims=True))
    p_c = jnp.exp2(s_c - m)
    p_n = jnp.exp2(s_n - m)
    l_row = jnp.sum(p_c, axis=-1, keepdims=True) + jnp.sum(p_n, axis=-1, keepdims=True)
    o_lat = (jnp.dot(p_c.astype(BF16), kc, preferred_element_type=F32)
             + jnp.dot(p_n.astype(BF16), kn, preferred_element_type=F32)) / l_row
    o_lat = o_lat.astype(BF16)
    hw = MLA_NOPE + MLA_V
    half = MLA_HEADS // 2
    for h in range(MLA_HEADS):
        w_uv = wukv_ref[:, h * hw + MLA_NOPE:(h + 1) * hw]
        o_h = jnp.dot(o_lat[h * sq:(h + 1) * sq, :], w_uv, preferred_element_type=F32)
        gb_ref = gb0_ref if h < half else gb1_ref
        gb = gb_ref[:, (h % half) * MLA_V:(h % half + 1) * MLA_V].astype(F32)
        o_ref[:, h * MLA_V:(h + 1) * MLA_V] = (_sigmoid(gb) * o_h).astype(BF16)


def _attn_sample_call(q_lat, q_rope, ckv_new, kr_new, cache_ckv, cache_kr_t, proj, w_ukv_r, l, batch, sq):
    assert (PAST_LEN + sq - 1) // CHUNK <= PAST_LEN // CHUNK
    past = cache_ckv.shape[2]
    lat_w = MLA_HEADS * MLA_KV_LORA
    rope_w = MLA_HEADS * MLA_ROPE
    ww = MLA_HEADS * (MLA_NOPE + MLA_V)
    gw = D_MODEL // 2
    gb_blk = COL_GB // gw
    rows = MLA_HEADS * sq
    est = (2 * (sq * (lat_w + rope_w) * 2 + past * (MLA_KV_LORA + V7X_LANES) * 4 + MLA_KV_LORA * ww * 2
                + 4 * sq * D_MODEL * 4) + past * (MLA_KV_LORA + V7X_LANES) * 2 + 6 * rows * past * 4
           + 4 * rows * MLA_KV_LORA * 4 + VMEM_TEMP_ALLOWANCE)
    return pl.pallas_call(
        functools.partial(_attn_sample_kernel, sq=sq),
        grid=(batch,),
        in_specs=[pl.BlockSpec((sq, lat_w), lambda b: (b, 0)),
                  pl.BlockSpec((sq, rope_w), lambda b: (b, 0)),
                  pl.BlockSpec((None, None, past, MLA_KV_LORA), lambda b: (l, b, 0, 0)),
                  pl.BlockSpec((None, None, MLA_ROPE, past), lambda b: (l, b, 0, 0)),
                  pl.BlockSpec((sq, MLA_KV_LORA), lambda b: (b, 0)),
                  pl.BlockSpec((sq, MLA_ROPE), lambda b: (b, 0)),
                  pl.BlockSpec((sq, gw), lambda b: (b, gb_blk)),
                  pl.BlockSpec((sq, gw), lambda b: (b, gb_blk + 1)),
                  pl.BlockSpec((None, MLA_KV_LORA, ww), lambda b: (l, 0, 0))],
        out_specs=pl.BlockSpec((sq, D_MODEL), lambda b: (b, 0)),
        out_shape=jax.ShapeDtypeStruct((batch * sq, D_MODEL), BF16),
        compiler_params=_cparams(("parallel",), est),
        name="mla_attention_sample",
    )(q_lat, q_rope, cache_ckv, cache_kr_t, ckv_new, kr_new, proj, proj, w_ukv_r)


def _merge_kernel(a_ref, b_ref, x_ref, g1_ref, w_ref, o_ref):
    m = (a_ref[...].astype(F32) + b_ref[...].astype(F32)).astype(BF16)
    mix = jnp.dot(m, w_ref[...], preferred_element_type=F32)
    o_ref[...] = _gate_res(x_ref[...], g1_ref[...], mix)


def _merge_call(a_part, b_part, x, mod, w_o_b, l, tm, groups):
    m = x.shape[0]
    est = (2 * 2 * tm * D_MODEL * 2 + 4 * tm * D_MODEL * 4 + D_MODEL * D_MODEL * 2 + 3 * tm * D_MODEL * 4
           + VMEM_TEMP_ALLOWANCE)
    return pl.pallas_call(
        _merge_kernel,
        grid=(m // tm,),
        in_specs=[pl.BlockSpec((tm, D_MODEL), lambda i: (i, 0)),
                  pl.BlockSpec((tm, D_MODEL), lambda i: (i, 0)),
                  pl.BlockSpec((tm, D_MODEL), lambda i: (i, 0)),
                  pl.BlockSpec((None, None, None, groups, D_MODEL), lambda i: (l, 2, i, 0, 0)),
                  pl.BlockSpec((None, D_MODEL, D_MODEL), lambda i: (l, 0, 0), pipeline_mode=pl.Buffered(1))],
        out_specs=pl.BlockSpec((tm, D_MODEL), lambda i: (i, 0)),
        out_shape=jax.ShapeDtypeStruct((m, D_MODEL), F32),
        compiler_params=_cparams(("parallel",), est),
        name="merge_out_proj",
    )(a_part, b_part, x, mod, w_o_b)


def _ffn_kernel(x_ref, sc_ref, sh_ref, g2_ref, gn_ref, fg_ref, wg_ref, wu_ref, wo_ref, o_ref, *rest,
                nh, final, emit_bf16):
    h_scr, acc_scr = rest[-2:]
    j = pl.program_id(1)

    @pl.when(j == 0)
    def _():
        h_scr[...] = _norm_mod(x_ref[...], gn_ref[...], sc_ref[...], sh_ref[...]).astype(BF16)
        acc_scr[...] = jnp.zeros(acc_scr.shape, F32)

    wg = wg_ref[...].astype(BF16)
    wu = wu_ref[...].astype(BF16)
    wo = wo_ref[...].astype(BF16)
    if emit_bf16:
        wgb_ref, wub_ref, wob_ref = rest[:3]
        wgb_ref[...] = wg
        wub_ref[...] = wu
        wob_ref[...] = wo
    hb = h_scr[...]
    gate = jnp.dot(hb, wg, preferred_element_type=F32)
    up = jnp.dot(hb, wu, preferred_element_type=F32)
    act = ((gate * _sigmoid(gate)) * up).astype(BF16)
    acc_scr[...] += jnp.dot(act, wo, preferred_element_type=F32)

    @pl.when(j == nh - 1)
    def _():
        y = _gate_res(x_ref[...], g2_ref[...], acc_scr[...])
        o_ref[...] = _rms(y, fg_ref[...]) if final else y


def _ffn_call(x, mod, norm_g, norm_final, weights, l, tm, groups, th):
    m = x.shape[0]
    nh = FFN_HIDDEN // th
    emit_bf16 = len(weights) == 2
    assert not emit_bf16 or m == tm
    wbytes = 4 if emit_bf16 else 2
    est = (4 * tm * D_MODEL * 4 + tm * D_MODEL * 2 + tm * D_MODEL * 4 + 2 * 3 * D_MODEL * th * wbytes
           + (2 * 3 + 3) * D_MODEL * th * 2 * emit_bf16 + 4 * tm * th * 4 + tm * D_MODEL * 4 + VMEM_TEMP_ALLOWANCE)
    mod_spec = lambda k: pl.BlockSpec((None, None, None, groups, D_MODEL), lambda i, j: (l, k, i, 0, 0))
    out_specs = [pl.BlockSpec((tm, D_MODEL), lambda i, j: (i, 0))]
    out_shape = [jax.ShapeDtypeStruct((m, D_MODEL), F32)]
    if emit_bf16:
        w_in, w_out = weights
        w_specs = [pl.BlockSpec((None, D_MODEL, th), lambda i, j: (l, 0, j)),
                   pl.BlockSpec((None, D_MODEL, th), lambda i, j: (l, 0, nh + j)),
                   pl.BlockSpec((None, th, D_MODEL), lambda i, j: (l, j, 0))]
        w_args = [w_in, w_in, w_out]
        out_specs += [pl.BlockSpec((D_MODEL, th), lambda i, j: (0, j)),
                      pl.BlockSpec((D_MODEL, th), lambda i, j: (0, j)),
                      pl.BlockSpec((th, D_MODEL), lambda i, j: (j, 0))]
        out_shape += [jax.ShapeDtypeStruct((D_MODEL, FFN_HIDDEN), BF16),
                      jax.ShapeDtypeStruct((D_MODEL, FFN_HIDDEN), BF16),
                      jax.ShapeDtypeStruct((FFN_HIDDEN, D_MODEL), BF16)]
    else:
        w_specs = [pl.BlockSpec((D_MODEL, th), lambda i, j: (0, j)),
                   pl.BlockSpec((D_MODEL, th), lambda i, j: (0, j)),
                   pl.BlockSpec((th, D_MODEL), lambda i, j: (j, 0))]
        w_args = list(weights)
    return pl.pallas_call(
        functools.partial(_ffn_kernel, nh=nh, final=(l == DEPTH - 1), emit_bf16=emit_bf16),
        grid=(m // tm, nh),
        in_specs=[pl.BlockSpec((tm, D_MODEL), lambda i, j: (i, 0)),
                  mod_spec(4), mod_spec(3), mod_spec(5),
                  pl.BlockSpec((None, 1, D_MODEL), lambda i, j: (l, 0, 0)),
                  pl.BlockSpec((1, D_MODEL), lambda i, j: (0, 0))] + w_specs,
        out_specs=out_specs,
        out_shape=out_shape,
        scratch_shapes=[pltpu.VMEM((tm, D_MODEL), BF16), pltpu.VMEM((tm, D_MODEL), F32)],
        compiler_params=_cparams(("parallel", "arbitrary"), est),
        name="ffn_swiglu",
    )(x, mod, mod, mod, norm_g, norm_final, *w_args)


def _rope_tables(pos, dim):
    inv = jnp.exp(-math.log(ROPE_BASE) * jnp.arange(0, dim, 2, dtype=F32) / dim)
    ang = pos.astype(F32)[:, None] * inv[None, :]
    return jnp.cos(ang), jnp.sin(ang)


def _mla_rope_tables(pos, reps):
    cos, sin = _rope_tables(pos, MLA_ROPE)
    z = jnp.zeros_like(cos)
    c = jnp.concatenate([cos, z, cos, z], axis=-1)
    s = jnp.concatenate([-sin, z, sin, z], axis=-1)
    return jnp.tile(c, (reps, 1)), jnp.tile(s, (reps, 1))


def _prep_weights(w_uq, w_ukv):
    uq = w_uq.reshape(DEPTH, MLA_Q_LORA, MLA_HEADS, MLA_NOPE + MLA_ROPE)
    uq = jnp.concatenate([uq[..., :MLA_NOPE], _pack_rope_lanes(uq[..., MLA_NOPE:])], axis=-1)
    w_uq_r = uq.reshape(DEPTH, MLA_Q_LORA, MLA_HEADS * HEAD_QK).astype(BF16)
    return w_uq_r, w_ukv.astype(BF16)


def _layer_prompt(l, x, mod, tm, batch, seq, ret_tabs, mla_tabs, zero_state, log_g, W, stacks, ffn_w):
    proj, kr_raw = _inproj_call(x, mod, W["norm_mix"], W["w_in_t"], l, IN_TM_PROMPT, 1, IN_TM_PROMPT // tm)
    ckv_prev, kr_prev, st_prev = stacks
    a_part, st_stack = _ret_call(proj, log_g, ret_tabs[0], ret_tabs[1], zero_state, 0, batch, seq,
                                 RET_L_PROMPT, RET_HEADS_PROMPT, l, st_prev)
    q_cat, ckv_stack, kr128, kr_stack = _mla_q_call(proj, kr_raw, W["q_norm"], W["kv_norm"], mla_tabs[0],
                                                    mla_tabs[1], W["w_uq_r"], l, tm, ckv_prev, kr_prev)
    k_cat, v = _mla_kv_call(ckv_stack, kr128, W["w_ukv_r"], l, batch * seq, tm)
    b_part = _attn_prompt_call(q_cat, k_cat, v, proj, batch, seq, ATT_TQ)
    x = _merge_call(a_part, b_part, x, mod, W["w_o"], l, tm, 1)
    x, = _ffn_call(x, mod, W["norm_ffn"], W["norm_final"], ffn_w, l, tm, 1, FFN_TH)
    return x, (ckv_stack, kr_stack, st_stack)


def _layer_sample(l, x, mod, groups, tm, batch, seq, ret_tabs, mla_tabs, state_ret, cache_ckv, cache_kr,
                  log_g, W, st_prev):
    proj, kr_raw = _inproj_call(x, mod, W["norm_mix"], W["w_in_t"], l, tm, groups, 1)
    a_part, st_stack = _ret_call(proj, log_g, ret_tabs[0], ret_tabs[1], state_ret, l, batch, seq, seq,
                                 RET_HEADS_SAMPLE, l, st_prev)
    q_lat, q_rope, ckv, kr = _mla_q_sample_call(proj, kr_raw, W["q_norm"], W["kv_norm"], mla_tabs[0],
                                                mla_tabs[1], W["w_uq_r"], W["w_ukv_r"], l)
    b_part = _attn_sample_call(q_lat, q_rope, ckv, kr, cache_ckv, cache_kr, proj, W["w_ukv_r"], l, batch, seq)
    x = _merge_call(a_part, b_part, x, mod, W["w_o"], l, tm, groups)
    x, *ffn_w = _ffn_call(x, mod, W["norm_ffn"], W["norm_final"], (W["w_ffn_in"], W["w_ffn_out"]), l, tm,
                          groups, FFN_TH_SAMPLE)
    return x, ckv, kr, st_stack, tuple(ffn_w)


def kernel(x_prompt, x_sample, c_prompt, c_sample, cache_mla_ckv, cache_mla_krope, state_ret, w_ada, b_ada,
           norm_mix, norm_ffn, w_in, mla_q_norm, w_uq, mla_kv_norm, w_ukv, w_o, w_ffn_in, w_ffn_out,
           norm_final):
    bp, sp, _ = x_prompt.shape
    bs, ss, _ = x_sample.shape
    tm = TOKEN_TILE
    assert sp % IN_TM_PROMPT == 0 and (bs * ss) % tm == 0 and tm % ss == 0

    w_uq_r, w_ukv_r = _prep_weights(w_uq, w_ukv)
    W = dict(w_in_t=jnp.swapaxes(w_in, 1, 2), w_uq_r=w_uq_r, w_ukv_r=w_ukv_r,
             w_o=w_o.astype(BF16), w_ffn_in=w_ffn_in, w_ffn_out=w_ffn_out,
             norm_mix=norm_mix.reshape(DEPTH, 1, D_MODEL), norm_ffn=norm_ffn.reshape(DEPTH, 1, D_MODEL),
             norm_final=norm_final.reshape(1, D_MODEL),
             q_norm=mla_q_norm.reshape(DEPTH, 1, MLA_Q_LORA), kv_norm=mla_kv_norm.reshape(DEPTH, 1, MLA_KV_LORA))

    c_rows = -(-(bp + bs) // ADA_ROW_ALIGN) * ADA_ROW_ALIGN
    c_all = jnp.concatenate([c_prompt, c_sample, jnp.zeros((c_rows - bp - bs, D_MODEL), F32)], axis=0)
    mod_all = _ada_call(c_all, w_ada, b_ada)[:, :bp + bs]
    mod_all = mod_all.reshape(DEPTH, bp + bs, 6, D_MODEL).transpose(0, 2, 1, 3)
    tiles_per_batch = sp // tm
    mod_p = jnp.repeat(mod_all[:, :, :bp], tiles_per_batch, axis=2)[:, :, :, None, :]
    groups_s = tm // ss
    mod_s = mod_all[:, :, bp:].reshape(DEPTH, 6, (bs * ss) // tm, groups_s, D_MODEL)

    log_g = jnp.log1p(-jnp.exp2(-RET_GAMMA_EXP0 - jnp.arange(RET_HEADS, dtype=F32)))
    pos_p = jnp.arange(sp)
    pos_s = PAST_LEN + jnp.arange(ss)
    ret_tabs_p = _rope_tables(pos_p, RET_DK)
    ret_tabs_s = _rope_tables(pos_s, RET_DK)
    mla_tabs_p = _mla_rope_tables(pos_p, bp)
    mla_tabs_s = _mla_rope_tables(pos_s, bs)
    zero_state = jnp.zeros((1, bp, RET_HEADS, RET_DK, RET_DV), F32)
    cache_kr_t = jnp.swapaxes(cache_mla_krope, 2, 3)

    xp = x_prompt.reshape(bp * sp, D_MODEL)
    xs = x_sample.reshape(bs * ss, D_MODEL)
    stacks_p = (jnp.zeros((DEPTH, bp * sp, MLA_KV_LORA), F32), jnp.zeros((DEPTH, bp * sp, MLA_ROPE), F32),
                jnp.zeros((DEPTH, bp, RET_HEADS, RET_DK, RET_DV), F32))
    st_s = jnp.zeros((DEPTH, bs, RET_HEADS, RET_DK, RET_DV), F32)
    ckv_s, kr_s = [], []
    for l in range(DEPTH):
        xs, ckv, kr, st_s, ffn_w = _layer_sample(l, xs, mod_s, groups_s, tm, bs, ss, ret_tabs_s, mla_tabs_s,
                                                 state_ret, cache_mla_ckv, cache_kr_t, log_g, W, st_s)
        xp, stacks_p = _layer_prompt(l, xp, mod_p, tm, bp, sp, ret_tabs_p, mla_tabs_p, zero_state, log_g, W,
                                     stacks_p, ffn_w)
        ckv_s.append(ckv.reshape(bs, ss, MLA_KV_LORA))
        kr_s.append(kr.reshape(bs, ss, MLA_ROPE))

    y_prompt = xp.reshape(bp, sp, D_MODEL)
    y_sample = xs.reshape(bs, ss, D_MODEL)
    ckv_p, kr_p, st_p = stacks_p
    return (y_prompt, y_sample, ckv_p.reshape(DEPTH, bp, sp, MLA_KV_LORA), kr_p.reshape(DEPTH, bp, sp, MLA_ROPE),
            st_p, jnp.stack(ckv_s), jnp.stack(kr_s), st_s)
```

```python
import functools
import math

import jax
import jax.numpy as jnp
from jax import lax
from jax.experimental import pallas as pl
from jax.experimental.pallas import tpu as pltpu

D_MODEL = 2048
DEPTH = 4
PAST_LEN = 1024
CHUNK = 64
RET_HEADS = 8
RET_DK = D_MODEL // RET_HEADS
RET_DV = D_MODEL // RET_HEADS
MLA_HEADS = 16
MLA_Q_LORA = D_MODEL // 4
MLA_KV_LORA = D_MODEL // 4
MLA_NOPE = 128
MLA_ROPE = 64
MLA_V = D_MODEL // MLA_HEADS
FFN_HIDDEN = -(-8 * D_MODEL // (3 * 256)) * 256
ROPE_BASE = 10000.0
RET_GAMMA_EXP0 = 5.0
RMS_EPS = 1e-6
GN_EPS = 1e-5
NEG_INF = -1e30

F32 = jnp.float32
BF16 = jnp.bfloat16

V7X_LANES = 128
V7X_VMEM_LIMIT_CAP = 56 * 1024 * 1024
VMEM_TEMP_ALLOWANCE = 4 * 1024 * 1024

COL_RQ, COL_RK, COL_RV, COL_RG = 0, D_MODEL, 2 * D_MODEL, 3 * D_MODEL
COL_DQ = 4 * D_MODEL
COL_DKV = COL_DQ + MLA_Q_LORA
COL_GA = COL_DKV + MLA_KV_LORA
COL_GB = COL_GA + D_MODEL
MAIN_COLS = COL_GB + D_MODEL
HEAD_QK = 2 * V7X_LANES
HALF_ROPE = MLA_ROPE // 2
ADA_ROW_ALIGN = 16
TOKEN_TILE = 512
IN_TM_PROMPT = 2048
IN_TN = 512
ADA_TN = 1024
FFN_TH = 512
FFN_TH_SAMPLE = 256
RET_L_PROMPT = 256
RET_HEADS_PROMPT = 4
RET_HEADS_SAMPLE = 4
ATT_TQ = 1024
ATT_TK = 1024
ATT_AHEAD = 2
ATT_SPLIT = 4
QK_SCALE_LOG2 = (MLA_NOPE + MLA_ROPE) ** -0.5 * math.log2(math.e)


def _cparams(sem, est_bytes):
    return pltpu.CompilerParams(dimension_semantics=sem,
                                vmem_limit_bytes=min(int(est_bytes), V7X_VMEM_LIMIT_CAP))


def _sigmoid(x):
    return jax.nn.sigmoid(x)


def _rms(x, g):
    return x * lax.rsqrt(jnp.mean(x * x, axis=-1, keepdims=True) + RMS_EPS) * g


def _norm_mod(x, g, sc, sh):
    tm, d = x.shape
    groups = sc.shape[0]
    y = _rms(x, g)
    if groups == 1:
        return y * (1.0 + sc) + sh
    y3 = y.reshape(groups, tm // groups, d)
    return (y3 * (1.0 + sc[:, None, :]) + sh[:, None, :]).reshape(tm, d)


def _gate_res(x, gate, upd):
    tm, n = x.shape
    groups = gate.shape[0]
    if groups == 1:
        return x + gate * upd
    return x + (gate[:, None, :] * upd.reshape(groups, tm // groups, n)).reshape(tm, n)


def _pack_rope_lanes(x):
    z = jnp.zeros(x.shape[:-1] + (HALF_ROPE,), x.dtype)
    return jnp.concatenate([x[..., :HALF_ROPE], z, x[..., HALF_ROPE:], z], axis=-1)


def _unpack_rope_lanes(x):
    return jnp.concatenate([x[..., :HALF_ROPE], x[..., 2 * HALF_ROPE:3 * HALF_ROPE]], axis=-1)


def _ada_kernel(c_ref, w_ref, b_ref, o_ref):
    c = c_ref[...]
    a = (c * _sigmoid(c)).astype(BF16)
    o_ref[0] = jnp.dot(a, w_ref[0].astype(BF16), preferred_element_type=F32) + b_ref[0]


def _ada_call(c_all, w_ada, b_ada):
    nb = c_all.shape[0]
    n = w_ada.shape[-1]
    tn = ADA_TN
    est = 2 * (D_MODEL * tn * 4) + D_MODEL * tn * 2 + 4 * nb * (D_MODEL + 2 * tn) * 4 + VMEM_TEMP_ALLOWANCE
    return pl.pallas_call(
        _ada_kernel,
        grid=(DEPTH, n // tn),
        in_specs=[pl.BlockSpec((nb, D_MODEL), lambda l, j: (0, 0)),
                  pl.BlockSpec((1, D_MODEL, tn), lambda l, j: (l, 0, j)),
                  pl.BlockSpec((1, 1, tn), lambda l, j: (l, 0, j))],
        out_specs=pl.BlockSpec((1, nb, tn), lambda l, j: (l, 0, j)),
        out_shape=jax.ShapeDtypeStruct((DEPTH, nb, n), F32),
        compiler_params=_cparams(("parallel", "parallel"), est),
        name="ada_mod",
    )(c_all, w_ada, b_ada.reshape(DEPTH, 1, n))


def _inproj_kernel(x_hbm, sc_ref, sh_ref, g_ref, w_ref, wn_ref, wkr_ref, o_ref, kr_ref, h_scr, x_buf, x_sem,
                   *, n_main):
    i = pl.program_id(0)
    j = pl.program_id(1)
    tm = x_buf.shape[0]
    nt = (((1,), (1,)), ((), ()))

    def x_copy(tile):
        return pltpu.make_async_copy(x_hbm.at[pl.ds(tile * tm, tm), :], x_buf, x_sem)

    @pl.when((j == 0) & (i == 0))
    def _():
        x_copy(0).start()

    @pl.when(j == 0)
    def _():
        wkr = wkr_ref[...]
        z = jnp.zeros((HALF_ROPE, wkr.shape[1]), wkr.dtype)
        wkr = jnp.concatenate([wkr[:HALF_ROPE], z, wkr[HALF_ROPE:], z], axis=0).astype(BF16)
        x_copy(i).wait()
        for r0 in range(0, tm, TOKEN_TILE):
            rs = pl.ds(r0, TOKEN_TILE)
            hb = _norm_mod(x_buf[rs, :], g_ref[...], sc_ref[...], sh_ref[...]).astype(BF16)
            h_scr[rs, :] = hb
            kr_ref[rs, :] = lax.dot_general(hb, wkr, nt, preferred_element_type=F32)

    @pl.when((j == 1) & (i + 1 < pl.num_programs(0)))
    def _():
        x_copy(i + 1).start()

    @pl.when(j < n_main)
    def _():
        o_ref[...] = lax.dot_general(h_scr[...], w_ref[...].astype(BF16), nt,
                                     preferred_element_type=F32).astype(BF16)

    @pl.when(j >= n_main)
    def _():
        w = jnp.concatenate([w_ref[MLA_ROPE:, :], wn_ref[...]], axis=0).astype(BF16)
        o_ref[...] = lax.dot_general(h_scr[...], w, nt, preferred_element_type=F32).astype(BF16)


def _inproj_call(x, mod, norm_g, w_in_t, l, tm, groups, mod_stride):
    m = x.shape[0]
    n_main = COL_GA // IN_TN
    sub = IN_TN // MLA_ROPE
    assert groups == 1 or tm == TOKEN_TILE
    est = (tm * D_MODEL * 4 + tm * D_MODEL * 2 + 2 * D_MODEL * (IN_TN + 2 * MLA_ROPE) * 4
           + D_MODEL * IN_TN * 2 + 3 * tm * IN_TN * 4 + 6 * TOKEN_TILE * D_MODEL * 4 + VMEM_TEMP_ALLOWANCE)
    mod_spec = lambda k: pl.BlockSpec((None, None, None, groups, D_MODEL),
                                      lambda i, j: (l, k, i * mod_stride, 0, 0))
    return pl.pallas_call(
        functools.partial(_inproj_kernel, n_main=n_main),
        grid=(m // tm, MAIN_COLS // IN_TN),
        in_specs=[pl.BlockSpec(memory_space=pl.ANY),
                  mod_spec(1), mod_spec(0),
                  pl.BlockSpec((None, 1, D_MODEL), lambda i, j: (l, 0, 0)),
                  pl.BlockSpec((None, IN_TN, D_MODEL), lambda i, j: (l, j, 0)),
                  pl.BlockSpec((None, MLA_ROPE, D_MODEL),
                               lambda i, j: (l, (jnp.maximum(j, n_main) + 1) * sub, 0)),
                  pl.BlockSpec((None, MLA_ROPE, D_MODEL), lambda i, j: (l, COL_GA // MLA_ROPE, 0))],
        out_specs=[pl.BlockSpec((tm, IN_TN), lambda i, j: (i, j)),
                   pl.BlockSpec((tm, V7X_LANES), lambda i, j: (i, 0))],
        out_shape=[jax.ShapeDtypeStruct((m, MAIN_COLS), BF16),
                   jax.ShapeDtypeStruct((m, V7X_LANES), F32)],
        scratch_shapes=[pltpu.VMEM((tm, D_MODEL), BF16), pltpu.VMEM((tm, D_MODEL), F32),
                        pltpu.SemaphoreType.DMA(())],
        compiler_params=_cparams(("arbitrary", "arbitrary"), est),
        name="in_proj",
    )(x, mod, mod, norm_g, w_in_t, w_in_t, w_in_t)


def _ret_kernel(lg_ref, q_ref, k_ref, v_ref, rg_ref, ga_ref, cos_ref, sin_ref, s0_ref, *rest,
                chunk_len, heads):
    a_ref, st_ref, dm_scr = rest[-3:]
    hg = pl.program_id(1)
    c = pl.program_id(2)
    L = chunk_len
    lgs = [lg_ref[hg * heads + t] for t in range(heads)]

    @pl.when(c == 0)
    def _():
        st_ref[...] = s0_ref[...]
        ri = lax.broadcasted_iota(jnp.int32, (L, L), 0)
        ci = lax.broadcasted_iota(jnp.int32, (L, L), 1)
        diff = (ri - ci).astype(F32)
        for t in range(heads):
            dm_scr[t] = jnp.where(diff >= 0, jnp.exp(jnp.maximum(diff, 0.0) * lgs[t]), 0.0)

    cos = cos_ref[...]
    sin = sin_ref[...]
    half = RET_DK // 2
    idx = lax.broadcasted_iota(jnp.int32, (L, 1), 0).astype(F32)
    nt = (((1,), (1,)), ((), ()))
    tn = (((0,), (0,)), ((), ()))

    def rope(x):
        x1, x2 = x[:, :half], x[:, half:]
        return jnp.concatenate([x1 * cos - x2 * sin, x1 * sin + x2 * cos], axis=-1)

    stage = []
    for t in range(heads):
        cs = pl.ds(t * RET_DK, RET_DK)
        q = rope(q_ref[:, cs].astype(F32))
        k = rope(k_ref[:, cs].astype(F32)) * (RET_DK ** -0.5)
        vb = v_ref[:, cs]
        qb = q.astype(BF16)
        zeta = jnp.exp((L - 1.0 - idx) * lgs[t])
        g_l = jnp.exp(jnp.full((1, 1), float(L), F32) * lgs[t])
        st = st_ref[0, t]
        scores = lax.dot_general(qb, k.astype(BF16), nt, preferred_element_type=F32)
        cross = jnp.dot(qb, st.astype(BF16), preferred_element_type=F32)
        st_ref[0, t] = st * g_l + lax.dot_general((k * zeta).astype(BF16), vb, tn, preferred_element_type=F32)
        stage.append((scores, cross, vb))

    outs = []
    for t in range(heads):
        scores, cross, vb = stage[t]
        xi = jnp.exp((idx + 1.0) * lgs[t])
        outs.append(jnp.dot((scores * dm_scr[t]).astype(BF16), vb, preferred_element_type=F32) + cross * xi)

    for t in range(heads):
        cs = pl.ds(t * RET_DK, RET_DK)
        o = outs[t]
        mu = jnp.mean(o, axis=-1, keepdims=True)
        d = o - mu
        var = jnp.mean(d * d, axis=-1, keepdims=True)
        on = d * lax.rsqrt(var + GN_EPS)
        rg = rg_ref[:, cs].astype(F32)
        a_ref[:, cs] = (_sigmoid(ga_ref[:, cs].astype(F32)) * ((rg * _sigmoid(rg)) * on)).astype(BF16)


def _stacked_out(stack_prev, in_specs, args, out_index):
    in_specs.append(pl.BlockSpec(memory_space=pl.ANY))
    args.append(stack_prev)
    return {len(args) - 1: out_index}


def _ret_call(proj, log_g, cos, sin, state0, state_layer, batch, seq, chunk_len, heads, l, stack_prev):
    L = chunk_len
    nc = seq // L
    w = heads * RET_DK
    assert COL_GA % w == 0 and RET_HEADS % heads == 0
    col = lambda base: (lambda b, h, c: (b * nc + c, base // w + h))
    blk = lambda base: pl.BlockSpec((L, w), col(base))
    nh = RET_HEADS
    est = (2 * 6 * L * w * 4 + 4 * heads * RET_DK * RET_DV * 4 + heads * L * L * 4
           + 8 * heads * L * max(L, RET_DK) * 4 + VMEM_TEMP_ALLOWANCE)
    in_specs = [pl.BlockSpec(memory_space=pltpu.SMEM),
                blk(COL_RQ), blk(COL_RK), blk(COL_RV), blk(COL_RG), blk(COL_GA),
                pl.BlockSpec((L, RET_DK // 2), lambda b, h, c: (c, 0)),
                pl.BlockSpec((L, RET_DK // 2), lambda b, h, c: (c, 0)),
                pl.BlockSpec((None, 1, heads, RET_DK, RET_DV), lambda b, h, c: (state_layer, b, h, 0, 0))]
    args = [log_g, proj, proj, proj, proj, proj, cos, sin, state0]
    aliases = _stacked_out(stack_prev, in_specs, args, 1)
    return pl.pallas_call(
        functools.partial(_ret_kernel, chunk_len=L, heads=heads),
        grid=(batch, nh // heads, nc),
        in_specs=in_specs,
        out_specs=[pl.BlockSpec((L, w), lambda b, h, c: (b * nc + c, h)),
                   pl.BlockSpec((None, 1, heads, RET_DK, RET_DV), lambda b, h, c: (l, b, h, 0, 0))],
        out_shape=[jax.ShapeDtypeStruct((batch * seq, D_MODEL), BF16),
                   jax.ShapeDtypeStruct((DEPTH, batch, nh, RET_DK, RET_DV), F32)],
        scratch_shapes=[pltpu.VMEM((heads, L, L), F32)],
        input_output_aliases=aliases,
        compiler_params=_cparams(("parallel", "parallel", "arbitrary"), est),
        name="retention",
    )(*args)


def _rope128(x, c, s):
    return x * c + pltpu.roll(x, V7X_LANES // 2, 1) * s


def _mla_q_kernel(dq_ref, dkv_ref, kr_ref, qn_ref, kvn_ref, c_ref, s_ref, wuq_ref, *rest):
    q_out, ckv_out, kro_out, kr64_out = rest[-4:]
    c = c_ref[...]
    s = s_ref[...]
    cq = _rms(dq_ref[...].astype(F32), qn_ref[...]).astype(BF16)
    for h in range(MLA_HEADS):
        lo = h * HEAD_QK
        qh = jnp.dot(cq, wuq_ref[:, lo:lo + HEAD_QK], preferred_element_type=F32) * QK_SCALE_LOG2
        q_out[:, lo:lo + MLA_NOPE] = qh[:, :MLA_NOPE].astype(BF16)
        q_out[:, lo + MLA_NOPE:lo + HEAD_QK] = _rope128(qh[:, MLA_NOPE:], c, s).astype(BF16)
    ckv_out[...] = _rms(dkv_ref[...].astype(F32), kvn_ref[...])
    kr = _rope128(kr_ref[...], c, s)
    kro_out[...] = kr
    kr64_out[...] = _unpack_rope_lanes(kr)


def _mla_q_call(proj, kr_raw, q_norm, kv_norm, rope_c, rope_s, w_uq_r, l, tm, ckv_prev, kr_prev):
    m = proj.shape[0]
    dq_blk = COL_DQ // MLA_Q_LORA
    qw = MLA_HEADS * HEAD_QK
    est = (2 * (2 * tm * MLA_Q_LORA * 4 + 3 * tm * V7X_LANES * 4) + 2 * MLA_Q_LORA * qw * 2
           + 2 * tm * qw * 2 + 2 * tm * MLA_KV_LORA * 4 + 2 * tm * V7X_LANES * 4 + 8 * tm * HEAD_QK * 4
           + VMEM_TEMP_ALLOWANCE)
    in_specs = [pl.BlockSpec((tm, MLA_Q_LORA), lambda i: (i, dq_blk)),
                pl.BlockSpec((tm, MLA_KV_LORA), lambda i: (i, dq_blk + 1)),
                pl.BlockSpec((tm, V7X_LANES), lambda i: (i, 0)),
                pl.BlockSpec((None, 1, MLA_Q_LORA), lambda i: (l, 0, 0)),
                pl.BlockSpec((None, 1, MLA_KV_LORA), lambda i: (l, 0, 0)),
                pl.BlockSpec((tm, V7X_LANES), lambda i: (i, 0)),
                pl.BlockSpec((tm, V7X_LANES), lambda i: (i, 0)),
                pl.BlockSpec((None, MLA_Q_LORA, qw), lambda i: (l, 0, 0))]
    args = [proj, proj, kr_raw, q_norm, kv_norm, rope_c, rope_s, w_uq_r]
    aliases = _stacked_out(ckv_prev, in_specs, args, 1)
    aliases.update(_stacked_out(kr_prev, in_specs, args, 3))
    return pl.pallas_call(
        _mla_q_kernel,
        grid=(m // tm,),
        in_specs=in_specs,
        out_specs=[pl.BlockSpec((tm, qw), lambda i: (i, 0)),
                   pl.BlockSpec((None, tm, MLA_KV_LORA), lambda i: (l, i, 0)),
                   pl.BlockSpec((tm, V7X_LANES), lambda i: (i, 0)),
                   pl.BlockSpec((None, tm, MLA_ROPE), lambda i: (l, i, 0))],
        out_shape=[jax.ShapeDtypeStruct((m, qw), BF16),
                   jax.ShapeDtypeStruct((DEPTH, m, MLA_KV_LORA), F32),
                   jax.ShapeDtypeStruct((m, V7X_LANES), F32),
                   jax.ShapeDtypeStruct((DEPTH, m, MLA_ROPE), F32)],
        input_output_aliases=aliases,
        compiler_params=_cparams(("parallel",), est),
        name="mla_latents",
    )(*args)


def _mla_q_sample_kernel(dq_ref, dkv_ref, kr_ref, qn_ref, kvn_ref, c_ref, s_ref, wuq_ref, wukv_ref,
                         qlat_out, qr_out, ckv_out, kro_out):
    c = c_ref[...]
    s = s_ref[...]
    cq = _rms(dq_ref[...].astype(F32), qn_ref[...]).astype(BF16)
    for h in range(MLA_HEADS):
        lo = h * HEAD_QK
        qh = jnp.dot(cq, wuq_ref[:, lo:lo + HEAD_QK], preferred_element_type=F32) * QK_SCALE_LOG2
        w_uk = wukv_ref[:, h * (MLA_NOPE + MLA_V):h * (MLA_NOPE + MLA_V) + MLA_NOPE]
        q_lat = lax.dot_general(qh[:, :MLA_NOPE].astype(BF16), w_uk, (((1,), (1,)), ((), ())),
                                preferred_element_type=F32)
        qlat_out[:, h * MLA_KV_LORA:(h + 1) * MLA_KV_LORA] = q_lat.astype(BF16)
        qr_out[:, h * MLA_ROPE:(h + 1) * MLA_ROPE] = _unpack_rope_lanes(
            _rope128(qh[:, MLA_NOPE:], c, s)).astype(BF16)
    ckv_out[...] = _rms(dkv_ref[...].astype(F32), kvn_ref[...])
    kro_out[...] = _unpack_rope_lanes(_rope128(kr_ref[...], c, s))


def _mla_q_sample_call(proj, kr_raw, q_norm, kv_norm, rope_c, rope_s, w_uq_r, w_ukv_r, l):
    m = proj.shape[0]
    dq_blk = COL_DQ // MLA_Q_LORA
    qw = MLA_HEADS * HEAD_QK
    ww = MLA_HEADS * (MLA_NOPE + MLA_V)
    lat_w = MLA_HEADS * MLA_KV_LORA
    rope_w = MLA_HEADS * MLA_ROPE
    est = (2 * (2 * m * MLA_Q_LORA * 4 + 3 * m * V7X_LANES * 4) + 2 * MLA_Q_LORA * (qw + ww) * 2
           + 2 * m * (lat_w + rope_w) * 2 + 2 * m * (MLA_KV_LORA + MLA_ROPE) * 4 + 8 * m * MLA_KV_LORA * 4
           + VMEM_TEMP_ALLOWANCE)
    full = lambda shape: pl.BlockSpec(shape, lambda i: (0,) * len(shape))
    return pl.pallas_call(
        _mla_q_sample_kernel,
        grid=(1,),
        in_specs=[pl.BlockSpec((m, MLA_Q_LORA), lambda i: (0, dq_blk)),
                  pl.BlockSpec((m, MLA_KV_LORA), lambda i: (0, dq_blk + 1)),
                  full((m, V7X_LANES)),
                  pl.BlockSpec((None, 1, MLA_Q_LORA), lambda i: (l, 0, 0)),
                  pl.BlockSpec((None, 1, MLA_KV_LORA), lambda i: (l, 0, 0)),
                  full((m, V7X_LANES)), full((m, V7X_LANES)),
                  pl.BlockSpec((None, MLA_Q_LORA, qw), lambda i: (l, 0, 0)),
                  pl.BlockSpec((None, MLA_KV_LORA, ww), lambda i: (l, 0, 0))],
        out_specs=[full((m, lat_w)), full((m, rope_w)), full((m, MLA_KV_LORA)), full((m, MLA_ROPE))],
        out_shape=[jax.ShapeDtypeStruct((m, lat_w), BF16),
                   jax.ShapeDtypeStruct((m, rope_w), BF16),
                   jax.ShapeDtypeStruct((m, MLA_KV_LORA), F32),
                   jax.ShapeDtypeStruct((m, MLA_ROPE), F32)],
        compiler_params=_cparams(("arbitrary",), est),
        name="mla_latents_sample",
    )(proj, proj, kr_raw, q_norm, kv_norm, rope_c, rope_s, w_uq_r, w_ukv_r)


def _mla_kv_kernel(ckv_ref, kr_ref, wukv_ref, kcat_out, v_out):
    cb = ckv_ref[...].astype(BF16)
    krb = kr_ref[...].astype(BF16)
    hw = MLA_NOPE + MLA_V
    for h in range(MLA_HEADS):
        kv = jnp.dot(cb, wukv_ref[:, h * hw:(h + 1) * hw], preferred_element_type=F32).astype(BF16)
        lo = h * HEAD_QK
        kcat_out[:, lo:lo + MLA_NOPE] = kv[:, :MLA_NOPE]
        kcat_out[:, lo + MLA_NOPE:lo + HEAD_QK] = krb
        v_out[:, h * MLA_V:(h + 1) * MLA_V] = kv[:, MLA_NOPE:]


def _mla_kv_call(ckv_stack, kr128, w_ukv_r, l, rows, tm):
    kw = MLA_HEADS * HEAD_QK
    vw = MLA_HEADS * MLA_V
    ww = MLA_HEADS * (MLA_NOPE + MLA_V)
    est = (2 * tm * (MLA_KV_LORA + V7X_LANES) * 4 + 2 * MLA_KV_LORA * ww * 2 + 2 * tm * (kw + vw) * 2
           + 8 * tm * HEAD_QK * 4 + VMEM_TEMP_ALLOWANCE)
    return pl.pallas_call(
        _mla_kv_kernel,
        grid=(rows // tm,),
        in_specs=[pl.BlockSpec((None, tm, MLA_KV_LORA), lambda i: (l, i, 0)),
                  pl.BlockSpec((tm, V7X_LANES), lambda i: (i, 0)),
                  pl.BlockSpec((None, MLA_KV_LORA, ww), lambda i: (l, 0, 0))],
        out_specs=[pl.BlockSpec((tm, kw), lambda i: (i, 0)),
                   pl.BlockSpec((tm, vw), lambda i: (i, 0))],
        out_shape=[jax.ShapeDtypeStruct((rows, kw), BF16),
                   jax.ShapeDtypeStruct((rows, vw), BF16)],
        compiler_params=_cparams(("parallel",), est),
        name="mla_kv_expand",
    )(ckv_stack, kr128, w_ukv_r)


def _scores(q, k_blk):
    return lax.dot_general(q, k_blk, (((1,), (1,)), ((), ())), preferred_element_type=F32)


def _softmax_tile(s, v_blk, m_prev, l_prev, acc_prev, tri):
    n_groups = s.shape[1] // V7X_LANES
    groups = [s[:, g * V7X_LANES:(g + 1) * V7X_LANES] for g in range(n_groups)]
    if tri is not None:
        n_tri = tri.shape[1] // V7X_LANES
        for t in range(n_tri):
            g = n_groups - n_tri + t
            groups[g] = jnp.where(tri[:, t * V7X_LANES:(t + 1) * V7X_LANES], groups[g], NEG_INF)
    m_new = jnp.maximum(m_prev, jnp.max(functools.reduce(jnp.maximum, groups), axis=-1, keepdims=True))
    alpha = jnp.exp2(m_prev - m_new)
    ps = [jnp.exp2(g - m_new) for g in groups]
    l_new = alpha * l_prev + functools.reduce(jnp.add, ps)
    p = jnp.concatenate([x.astype(BF16) for x in ps], axis=-1)
    acc_new = alpha * acc_prev + jnp.dot(p, v_blk, preferred_element_type=F32)
    return m_new, l_new, acc_new


def _attn_prompt_kernel(q_ref, k_ref, v_ref, gb_ref, o_ref, m_scr, l_scr, acc_scr, *, tq, tk, n_split):
    i = pl.program_id(2)
    rows = tq // n_split
    m_scr[...] = jnp.full(m_scr.shape, NEG_INF, F32)
    l_scr[...] = jnp.zeros(l_scr.shape, F32)
    acc_scr[...] = jnp.zeros(acc_scr.shape, F32)

    def scores(r, k_blk):
        return _scores(q_ref[pl.ds(r * rows, rows), :], k_blk)

    def update(r, s, v_blk, mask):
        rs = pl.ds(r * rows, rows)
        m_new, l_new, acc_new = _softmax_tile(s, v_blk, m_scr[rs, :], l_scr[rs, :], acc_scr[rs, :], mask)
        m_scr[rs, :] = m_new
        l_scr[rs, :] = l_new
        acc_scr[rs, :] = acc_new

    blocks_per_tile = tq // tk

    def body(j, carry):
        for d in range(blocks_per_tile):
            start = pl.multiple_of(j * tq + d * tk, tk)
            k_blk = k_ref[pl.ds(start, tk), :]
            v_blk = v_ref[pl.ds(start, tk), :]
            ss = [scores(r, k_blk) for r in range(min(ATT_AHEAD, n_split))]
            for r in range(n_split):
                if r + ATT_AHEAD < n_split:
                    ss.append(scores(r + ATT_AHEAD, k_blk))
                update(r, ss[r], v_blk, None)
        return carry

    lax.fori_loop(0, i, body, 0)

    tri = (lax.broadcasted_iota(jnp.int32, (rows, rows), 1) // CHUNK
           <= lax.broadcasted_iota(jnp.int32, (rows, rows), 0) // CHUNK)
    for d in range(blocks_per_tile):
        k0 = d * tk
        todo = []
        for r in range(n_split):
            r0, r1 = r * rows, (r + 1) * rows
            width = min(k0 + tk, r1) - k0
            if width <= 0:
                continue
            on_diagonal = k0 + width > r0
            assert not on_diagonal or (k0 + width == r1 and width >= rows)
            todo.append((r, width, tri if on_diagonal else None))
        start = pl.multiple_of(i * tq + k0, tk)
        diag_scores = lambda t: scores(todo[t][0], k_ref[pl.ds(start, todo[t][1]), :])
        ss = [diag_scores(t) for t in range(min(ATT_AHEAD, len(todo)))]
        for t, (r, width, mask) in enumerate(todo):
            if t + ATT_AHEAD < len(todo):
                ss.append(diag_scores(t + ATT_AHEAD))
            update(r, ss[t], v_ref[pl.ds(start, width), :], mask)

    l_row = jnp.sum(l_scr[...], axis=-1, keepdims=True)
    o_ref[...] = (_sigmoid(gb_ref[...].astype(F32)) * (acc_scr[...] / l_row)).astype(BF16)


def _attn_prompt_call(q_cat, k_cat, v, proj, batch, seq, tq):
    nq = seq // tq
    gb_blk = COL_GB // MLA_V
    est = (2 * (tq * HEAD_QK * 2 + seq * HEAD_QK * 2 + seq * MLA_V * 2 + 2 * tq * MLA_V * 4)
           + 3 * tq * V7X_LANES * 4 + 6 * tq * ATT_TK * 4 + VMEM_TEMP_ALLOWANCE)
    return pl.pallas_call(
        functools.partial(_attn_prompt_kernel, tq=tq, tk=ATT_TK, n_split=ATT_SPLIT),
        grid=(batch, MLA_HEADS, nq),
        in_specs=[pl.BlockSpec((tq, HEAD_QK), lambda b, h, i: (b * nq + i, h)),
                  pl.BlockSpec((seq, HEAD_QK), lambda b, h, i: (b, h)),
                  pl.BlockSpec((seq, MLA_V), lambda b, h, i: (b, h)),
                  pl.BlockSpec((tq, MLA_V), lambda b, h, i: (b * nq + i, gb_blk + h))],
        out_specs=pl.BlockSpec((tq, MLA_V), lambda b, h, i: (b * nq + i, h)),
        out_shape=jax.ShapeDtypeStruct((batch * seq, MLA_HEADS * MLA_V), BF16),
        scratch_shapes=[pltpu.VMEM((tq, V7X_LANES), F32), pltpu.VMEM((tq, V7X_LANES), F32),
                        pltpu.VMEM((tq, MLA_V), F32)],
        compiler_params=_cparams(("parallel", "parallel", "arbitrary"), est),
        name="mla_attention_prompt",
    )(q_cat, k_cat, v, proj)


def _attn_sample_kernel(qlat_ref, qr_ref, cc_ref, ckr_ref, nc_ref, nkr_ref, gb0_ref, gb1_ref, wukv_ref,
                        o_ref, *, sq):
    nt = (((1,), (1,)), ((), ()))
    q_lat = jnp.concatenate([qlat_ref[:, h * MLA_KV_LORA:(h + 1) * MLA_KV_LORA] for h in range(MLA_HEADS)],
                            axis=0)
    q_r = jnp.concatenate([qr_ref[:, h * MLA_ROPE:(h + 1) * MLA_ROPE] for h in range(MLA_HEADS)], axis=0)
    kc = cc_ref[...].astype(BF16)
    kn = nc_ref[...].astype(BF16)
    s_c = (lax.dot_general(q_lat, kc, nt, preferred_element_type=F32)
           + jnp.dot(q_r, ckr_ref[...].astype(BF16), preferred_element_type=F32))
    s_n = (lax.dot_general(q_lat, kn, nt, preferred_element_type=F32)
           + lax.dot_general(q_r, nkr_ref[...].astype(BF16), nt, preferred_element_type=F32))
    m = jnp.maximum(jnp.max(s_c, axis=-1, keepdims=True), jnp.max(s_n, axis=-1, keepdims=True))
    p_c = jnp.exp2(s_c - m)
    p_n = jnp.exp2(s_n - m)
    l_row = jnp.sum(p_c, axis=-1, keepdims=True) + jnp.sum(p_n, axis=-1, keepdims=True)
    o_lat = (jnp.dot(p_c.astype(BF16), kc, preferred_element_type=F32)
             + jnp.dot(p_n.astype(BF16), kn, preferred_element_type=F32)) / l_row
    o_lat = o_lat.astype(BF16)
    hw = MLA_NOPE + MLA_V
    half = MLA_HEADS // 2
    for h in range(MLA_HEADS):
        w_uv = wukv_ref[:, h * hw + MLA_NOPE:(h + 1) * hw]
        o_h = jnp.dot(o_lat[h * sq:(h + 1) * sq, :], w_uv, preferred_element_type=F32)
        gb_ref = gb0_ref if h < half else gb1_ref
        gb = gb_ref[:, (h % half) * MLA_V:(h % half + 1) * MLA_V].astype(F32)
        o_ref[:, h * MLA_V:(h + 1) * MLA_V] = (_sigmoid(gb) * o_h).astype(BF16)


def _attn_sample_call(q_lat, q_rope, ckv_new, kr_new, cache_ckv, cache_kr_t, proj, w_ukv_r, l, batch, sq):
    assert (PAST_LEN + sq - 1) // CHUNK <= PAST_LEN // CHUNK
    past = cache_ckv.shape[2]
    lat_w = MLA_HEADS * MLA_KV_LORA
    rope_w = MLA_HEADS * MLA_ROPE
    ww = MLA_HEADS * (MLA_NOPE + MLA_V)
    gw = D_MODEL // 2
    gb_blk = COL_GB // gw
    rows = MLA_HEADS * sq
    est = (2 * (sq * (lat_w + rope_w) * 2 + past * (MLA_KV_LORA + V7X_LANES) * 4 + MLA_KV_LORA * ww * 2
                + 4 * sq * D_MODEL * 4) + past * (MLA_KV_LORA + V7X_LANES) * 2 + 6 * rows * past * 4
           + 4 * rows * MLA_KV_LORA * 4 + VMEM_TEMP_ALLOWANCE)
    return pl.pallas_call(
        functools.partial(_attn_sample_kernel, sq=sq),
        grid=(batch,),
        in_specs=[pl.BlockSpec((sq, lat_w), lambda b: (b, 0)),
                  pl.BlockSpec((sq, rope_w), lambda b: (b, 0)),
                  pl.BlockSpec((None, None, past, MLA_KV_LORA), lambda b: (l, b, 0, 0)),
                  pl.BlockSpec((None, None, MLA_ROPE, past), lambda b: (l, b, 0, 0)),
                  pl.BlockSpec((sq, MLA_KV_LORA), lambda b: (b, 0)),
                  pl.BlockSpec((sq, MLA_ROPE), lambda b: (b, 0)),
                  pl.BlockSpec((sq, gw), lambda b: (b, gb_blk)),
                  pl.BlockSpec((sq, gw), lambda b: (b, gb_blk + 1)),
                  pl.BlockSpec((None, MLA_KV_LORA, ww), lambda b: (l, 0, 0))],
        out_specs=pl.BlockSpec((sq, D_MODEL), lambda b: (b, 0)),
        out_shape=jax.ShapeDtypeStruct((batch * sq, D_MODEL), BF16),
        compiler_params=_cparams(("parallel",), est),
        name="mla_attention_sample",
    )(q_lat, q_rope, cache_ckv, cache_kr_t, ckv_new, kr_new, proj, proj, w_ukv_r)


def _merge_kernel(a_ref, b_ref, x_ref, g1_ref, w_ref, o_ref):
    m = (a_ref[...].astype(F32) + b_ref[...].astype(F32)).astype(BF16)
    mix = jnp.dot(m, w_ref[...], preferred_element_type=F32)
    o_ref[...] = _gate_res(x_ref[...], g1_ref[...], mix)


def _merge_call(a_part, b_part, x, mod, w_o_b, l, tm, groups):
    m = x.shape[0]
    est = (2 * 2 * tm * D_MODEL * 2 + 4 * tm * D_MODEL * 4 + D_MODEL * D_MODEL * 2 + 3 * tm * D_MODEL * 4
           + VMEM_TEMP_ALLOWANCE)
    return pl.pallas_call(
        _merge_kernel,
        grid=(m // tm,),
        in_specs=[pl.BlockSpec((tm, D_MODEL), lambda i: (i, 0)),
                  pl.BlockSpec((tm, D_MODEL), lambda i: (i, 0)),
                  pl.BlockSpec((tm, D_MODEL), lambda i: (i, 0)),
                  pl.BlockSpec((None, None, None, groups, D_MODEL), lambda i: (l, 2, i, 0, 0)),
                  pl.BlockSpec((None, D_MODEL, D_MODEL), lambda i: (l, 0, 0), pipeline_mode=pl.Buffered(1))],
        out_specs=pl.BlockSpec((tm, D_MODEL), lambda i: (i, 0)),
        out_shape=jax.ShapeDtypeStruct((m, D_MODEL), F32),
        compiler_params=_cparams(("parallel",), est),
        name="merge_out_proj",
    )(a_part, b_part, x, mod, w_o_b)


def _ffn_kernel(x_ref, sc_ref, sh_ref, g2_ref, gn_ref, fg_ref, wg_ref, wu_ref, wo_ref, o_ref, *rest,
                nh, final, emit_bf16):
    h_scr, acc_scr = rest[-2:]
    j = pl.program_id(1)

    @pl.when(j == 0)
    def _():
        h_scr[...] = _norm_mod(x_ref[...], gn_ref[...], sc_ref[...], sh_ref[...]).astype(BF16)
        acc_scr[...] = jnp.zeros(acc_scr.shape, F32)

    wg = wg_ref[...].astype(BF16)
    wu = wu_ref[...].astype(BF16)
    wo = wo_ref[...].astype(BF16)
    if emit_bf16:
        wgb_ref, wub_ref, wob_ref = rest[:3]
        wgb_ref[...] = wg
        wub_ref[...] = wu
        wob_ref[...] = wo
    hb = h_scr[...]
    gate = jnp.dot(hb, wg, preferred_element_type=F32)
    up = jnp.dot(hb, wu, preferred_element_type=F32)
    act = ((gate * _sigmoid(gate)) * up).astype(BF16)
    acc_scr[...] += jnp.dot(act, wo, preferred_element_type=F32)

    @pl.when(j == nh - 1)
    def _():
        y = _gate_res(x_ref[...], g2_ref[...], acc_scr[...])
        o_ref[...] = _rms(y, fg_ref[...]) if final else y


def _ffn_call(x, mod, norm_g, norm_final, weights, l, tm, groups, th):
    m = x.shape[0]
    nh = FFN_HIDDEN // th
    emit_bf16 = len(weights) == 2
    assert not emit_bf16 or m == tm
    wbytes = 4 if emit_bf16 else 2
    est = (4 * tm * D_MODEL * 4 + tm * D_MODEL * 2 + tm * D_MODEL * 4 + 2 * 3 * D_MODEL * th * wbytes
           + (2 * 3 + 3) * D_MODEL * th * 2 * emit_bf16 + 4 * tm * th * 4 + tm * D_MODEL * 4 + VMEM_TEMP_ALLOWANCE)
    mod_spec = lambda k: pl.BlockSpec((None, None, None, groups, D_MODEL), lambda i, j: (l, k, i, 0, 0))
    out_specs = [pl.BlockSpec((tm, D_MODEL), lambda i, j: (i, 0))]
    out_shape = [jax.ShapeDtypeStruct((m, D_MODEL), F32)]
    if emit_bf16:
        w_in, w_out = weights
        w_specs = [pl.BlockSpec((None, D_MODEL, th), lambda i, j: (l, 0, j)),
                   pl.BlockSpec((None, D_MODEL, th), lambda i, j: (l, 0, nh + j)),
                   pl.BlockSpec((None, th, D_MODEL), lambda i, j: (l, j, 0))]
        w_args = [w_in, w_in, w_out]
        out_specs += [pl.BlockSpec((D_MODEL, th), lambda i, j: (0, j)),
                      pl.BlockSpec((D_MODEL, th), lambda i, j: (0, j)),
                      pl.BlockSpec((th, D_MODEL), lambda i, j: (j, 0))]
        out_shape += [jax.ShapeDtypeStruct((D_MODEL, FFN_HIDDEN), BF16),
                      jax.ShapeDtypeStruct((D_MODEL, FFN_HIDDEN), BF16),
                      jax.ShapeDtypeStruct((FFN_HIDDEN, D_MODEL), BF16)]
    else:
        w_specs = [pl.BlockSpec((D_MODEL, th), lambda i, j: (0, j)),
                   pl.BlockSpec((D_MODEL, th), lambda i, j: (0, j)),
                   pl.BlockSpec((th, D_MODEL), lambda i, j: (j, 0))]
        w_args = list(weights)
    return pl.pallas_call(
        functools.partial(_ffn_kernel, nh=nh, final=(l == DEPTH - 1), emit_bf16=emit_bf16),
        grid=(m // tm, nh),
        in_specs=[pl.BlockSpec((tm, D_MODEL), lambda i, j: (i, 0)),
                  mod_spec(4), mod_spec(3), mod_spec(5),
                  pl.BlockSpec((None, 1, D_MODEL), lambda i, j: (l, 0, 0)),
                  pl.BlockSpec((1, D_MODEL), lambda i, j: (0, 0))] + w_specs,
        out_specs=out_specs,
        out_shape=out_shape,
        scratch_shapes=[pltpu.VMEM((tm, D_MODEL), BF16), pltpu.VMEM((tm, D_MODEL), F32)],
        compiler_params=_cparams(("parallel", "arbitrary"), est),
        name="ffn_swiglu",
    )(x, mod, mod, mod, norm_g, norm_final, *w_args)


def _rope_tables(pos, dim):
    inv = jnp.exp(-math.log(ROPE_BASE) * jnp.arange(0, dim, 2, dtype=F32) / dim)
    ang = pos.astype(F32)[:, None] * inv[None, :]
    return jnp.cos(ang), jnp.sin(ang)


def _mla_rope_tables(pos, reps):
    cos, sin = _rope_tables(pos, MLA_ROPE)
    z = jnp.zeros_like(cos)
    c = jnp.concatenate([cos, z, cos, z], axis=-1)
    s = jnp.concatenate([-sin, z, sin, z], axis=-1)
    return jnp.tile(c, (reps, 1)), jnp.tile(s, (reps, 1))


def _prep_weights(w_uq, w_ukv):
    uq = w_uq.reshape(DEPTH, MLA_Q_LORA, MLA_HEADS, MLA_NOPE + MLA_ROPE)
    uq = jnp.concatenate([uq[..., :MLA_NOPE], _pack_rope_lanes(uq[..., MLA_NOPE:])], axis=-1)
    w_uq_r = uq.reshape(DEPTH, MLA_Q_LORA, MLA_HEADS * HEAD_QK).astype(BF16)
    return w_uq_r, w_ukv.astype(BF16)


def _layer_prompt(l, x, mod, tm, batch, seq, ret_tabs, mla_tabs, zero_state, log_g, W, stacks, ffn_w):
    proj, kr_raw = _inproj_call(x, mod, W["norm_mix"], W["w_in_t"], l, IN_TM_PROMPT, 1, IN_TM_PROMPT // tm)
    ckv_prev, kr_prev, st_prev = stacks
    a_part, st_stack = _ret_call(proj, log_g, ret_tabs[0], ret_tabs[1], zero_state, 0, batch, seq,
                                 RET_L_PROMPT, RET_HEADS_PROMPT, l, st_prev)
    q_cat, ckv_stack, kr128, kr_stack = _mla_q_call(proj, kr_raw, W["q_norm"], W["kv_norm"], mla_tabs[0],
                                                    mla_tabs[1], W["w_uq_r"], l, tm, ckv_prev, kr_prev)
    k_cat, v = _mla_kv_call(ckv_stack, kr128, W["w_ukv_r"], l, batch * seq, tm)
    b_part = _attn_prompt_call(q_cat, k_cat, v, proj, batch, seq, ATT_TQ)
    x = _merge_call(a_part, b_part, x, mod, W["w_o"], l, tm, 1)
    x, = _ffn_call(x, mod, W["norm_ffn"], W["norm_final"], ffn_w, l, tm, 1, FFN_TH)
    return x, (ckv_stack, kr_stack, st_stack)


def _layer_sample(l, x, mod, groups, tm, batch, seq, ret_tabs, mla_tabs, state_ret, cache_ckv, cache_kr,
                  log_g, W, st_prev):
    proj, kr_raw = _inproj_call(x, mod, W["norm_mix"], W["w_in_t"], l, tm, groups, 1)
    a_part, st_stack = _ret_call(proj, log_g, ret_tabs[0], ret_tabs[1], state_ret, l, batch, seq, seq,
                                 RET_HEADS_SAMPLE, l, st_prev)
    q_lat, q_rope, ckv, kr = _mla_q_sample_call(proj, kr_raw, W["q_norm"], W["kv_norm"], mla_tabs[0],
                                                mla_tabs[1], W["w_uq_r"], W["w_ukv_r"], l)
    b_part = _attn_sample_call(q_lat, q_rope, ckv, kr, cache_ckv, cache_kr, proj, W["w_ukv_r"], l, batch, seq)
    x = _merge_call(a_part, b_part, x, mod, W["w_o"], l, tm, groups)
    x, *ffn_w = _ffn_call(x, mod, W["norm_ffn"], W["norm_final"], (W["w_ffn_in"], W["w_ffn_out"]), l, tm,
                          groups, FFN_TH_SAMPLE)
    return x, ckv, kr, st_stack, tuple(ffn_w)


def kernel(x_prompt, x_sample, c_prompt, c_sample, cache_mla_ckv, cache_mla_krope, state_ret, w_ada, b_ada,
           norm_mix, norm_ffn, w_in, mla_q_norm, w_uq, mla_kv_norm, w_ukv, w_o, w_ffn_in, w_ffn_out,
           norm_final):
    bp, sp, _ = x_prompt.shape
    bs, ss, _ = x_sample.shape
    tm = TOKEN_TILE
    assert sp % IN_TM_PROMPT == 0 and (bs * ss) % tm == 0 and tm % ss == 0

    w_uq_r, w_ukv_r = _prep_weights(w_uq, w_ukv)
    W = dict(w_in_t=jnp.swapaxes(w_in, 1, 2), w_uq_r=w_uq_r, w_ukv_r=w_ukv_r,
             w_o=w_o.astype(BF16), w_ffn_in=w_ffn_in, w_ffn_out=w_ffn_out,
             norm_mix=norm_mix.reshape(DEPTH, 1, D_MODEL), norm_ffn=norm_ffn.reshape(DEPTH, 1, D_MODEL),
             norm_final=norm_final.reshape(1, D_MODEL),
             q_norm=mla_q_norm.reshape(DEPTH, 1, MLA_Q_LORA), kv_norm=mla_kv_norm.reshape(DEPTH, 1, MLA_KV_LORA))

    c_rows = -(-(bp + bs) // ADA_ROW_ALIGN) * ADA_ROW_ALIGN
    c_all = jnp.concatenate([c_prompt, c_sample, jnp.zeros((c_rows - bp - bs, D_MODEL), F32)], axis=0)
    mod_all = _ada_call(c_all, w_ada, b_ada)[:, :bp + bs]
    mod_all = mod_all.reshape(DEPTH, bp + bs, 6, D_MODEL).transpose(0, 2, 1, 3)
    tiles_per_batch = sp // tm
    mod_p = jnp.repeat(mod_all[:, :, :bp], tiles_per_batch, axis=2)[:, :, :, None, :]
    groups_s = tm // ss
    mod_s = mod_all[:, :, bp:].reshape(DEPTH, 6, (bs * ss) // tm, groups_s, D_MODEL)

    log_g = jnp.log1p(-jnp.exp2(-RET_GAMMA_EXP0 - jnp.arange(RET_HEADS, dtype=F32)))
    pos_p = jnp.arange(sp)
    pos_s = PAST_LEN + jnp.arange(ss)
    ret_tabs_p = _rope_tables(pos_p, RET_DK)
    ret_tabs_s = _rope_tables(pos_s, RET_DK)
    mla_tabs_p = _mla_rope_tables(pos_p, bp)
    mla_tabs_s = _mla_rope_tables(pos_s, bs)
    zero_state = jnp.zeros((1, bp, RET_HEADS, RET_DK, RET_DV), F32)
    cache_kr_t = jnp.swapaxes(cache_mla_krope, 2, 3)

    xp = x_prompt.reshape(bp * sp, D_MODEL)
    xs = x_sample.reshape(bs * ss, D_MODEL)
    stacks_p = (jnp.zeros((DEPTH, bp * sp, MLA_KV_LORA), F32), jnp.zeros((DEPTH, bp * sp, MLA_ROPE), F32),
                jnp.zeros((DEPTH, bp, RET_HEADS, RET_DK, RET_DV), F32))
    st_s = jnp.zeros((DEPTH, bs, RET_HEADS, RET_DK, RET_DV), F32)
    ckv_s, kr_s = [], []
    for l in range(DEPTH):
        xs, ckv, kr, st_s, ffn_w = _layer_sample(l, xs, mod_s, groups_s, tm, bs, ss, ret_tabs_s, mla_tabs_s,
                                                 state_ret, cache_mla_ckv, cache_kr_t, log_g, W, st_s)
        xp, stacks_p = _layer_prompt(l, xp, mod_p, tm, bp, sp, ret_tabs_p, mla_tabs_p, zero_state, log_g, W,
                                     stacks_p, ffn_w)
        ckv_s.append(ckv.reshape(bs, ss, MLA_KV_LORA))
        kr_s.append(kr.reshape(bs, ss, MLA_ROPE))

    y_prompt = xp.reshape(bp, sp, D_MODEL)
    y_sample = xs.reshape(bs, ss, D_MODEL)
    ckv_p, kr_p, st_p = stacks_p
    return (y_prompt, y_sample, ckv_p.reshape(DEPTH, bp, sp, MLA_KV_LORA), kr_p.reshape(DEPTH, bp, sp, MLA_ROPE),
            st_p, jnp.stack(ckv_s), jnp.stack(kr_s), st_s)
```

```python
import functools
import math

import jax
import jax.numpy as jnp
from jax import lax
from jax.experimental import pallas as pl
from jax.experimental.pallas import tpu as pltpu

D_MODEL = 2048
DEPTH = 4
PAST_LEN = 1024
CHUNK = 64
RET_HEADS = 8
RET_DK = D_MODEL // RET_HEADS
RET_DV = D_MODEL // RET_HEADS
MLA_HEADS = 16
MLA_Q_LORA = D_MODEL // 4
MLA_KV_LORA = D_MODEL // 4
MLA_NOPE = 128
MLA_ROPE = 64
MLA_V = D_MODEL // MLA_HEADS
FFN_HIDDEN = -(-8 * D_MODEL // (3 * 256)) * 256
ROPE_BASE = 10000.0
RET_GAMMA_EXP0 = 5.0
RMS_EPS = 1e-6
GN_EPS = 1e-5
NEG_INF = -1e30

F32 = jnp.float32
BF16 = jnp.bfloat16

V7X_LANES = 128
V7X_VMEM_LIMIT_CAP = 56 * 1024 * 1024
VMEM_TEMP_ALLOWANCE = 4 * 1024 * 1024

COL_RQ, COL_RK, COL_RV, COL_RG = 0, D_MODEL, 2 * D_MODEL, 3 * D_MODEL
COL_DQ = 4 * D_MODEL
COL_DKV = COL_DQ + MLA_Q_LORA
COL_GA = COL_DKV + MLA_KV_LORA
COL_GB = COL_GA + D_MODEL
MAIN_COLS = COL_GB + D_MODEL
HEAD_QK = 2 * V7X_LANES
HALF_ROPE = MLA_ROPE // 2
ADA_ROW_ALIGN = 16
TOKEN_TILE = 512
IN_TM_PROMPT = 2048
IN_TN = 512
ADA_TN = 1024
FFN_TH = 512
FFN_TH_SAMPLE = 256
RET_L_PROMPT = 256
RET_HEADS_PROMPT = 4
RET_HEADS_SAMPLE = 4
ATT_TQ = 1024
ATT_TK = 1024
ATT_AHEAD = 2
ATT_HEADS = 2
ATT_SPLIT = 4
QK_SCALE_LOG2 = (MLA_NOPE + MLA_ROPE) ** -0.5 * math.log2(math.e)


def _cparams(sem, est_bytes):
    return pltpu.CompilerParams(dimension_semantics=sem,
                                vmem_limit_bytes=min(int(est_bytes), V7X_VMEM_LIMIT_CAP))


def _sigmoid(x):
    return jax.nn.sigmoid(x)


def _rms(x, g):
    return x * lax.rsqrt(jnp.mean(x * x, axis=-1, keepdims=True) + RMS_EPS) * g


def _norm_mod(x, g, sc, sh):
    tm, d = x.shape
    groups = sc.shape[0]
    y = _rms(x, g)
    if groups == 1:
        return y * (1.0 + sc) + sh
    y3 = y.reshape(groups, tm // groups, d)
    return (y3 * (1.0 + sc[:, None, :]) + sh[:, None, :]).reshape(tm, d)


def _gate_res(x, gate, upd):
    tm, n = x.shape
    groups = gate.shape[0]
    if groups == 1:
        return x + gate * upd
    return x + (gate[:, None, :] * upd.reshape(groups, tm // groups, n)).reshape(tm, n)


def _pack_rope_lanes(x):
    z = jnp.zeros(x.shape[:-1] + (HALF_ROPE,), x.dtype)
    return jnp.concatenate([x[..., :HALF_ROPE], z, x[..., HALF_ROPE:], z], axis=-1)


def _unpack_rope_lanes(x):
    return jnp.concatenate([x[..., :HALF_ROPE], x[..., 2 * HALF_ROPE:3 * HALF_ROPE]], axis=-1)


def _ada_kernel(c_ref, w_ref, b_ref, o_ref):
    c = c_ref[...]
    a = (c * _sigmoid(c)).astype(BF16)
    o_ref[0] = jnp.dot(a, w_ref[0].astype(BF16), preferred_element_type=F32) + b_ref[0]


def _ada_call(c_all, w_ada, b_ada):
    nb = c_all.shape[0]
    n = w_ada.shape[-1]
    tn = ADA_TN
    est = 2 * (D_MODEL * tn * 4) + D_MODEL * tn * 2 + 4 * nb * (D_MODEL + 2 * tn) * 4 + VMEM_TEMP_ALLOWANCE
    return pl.pallas_call(
        _ada_kernel,
        grid=(DEPTH, n // tn),
        in_specs=[pl.BlockSpec((nb, D_MODEL), lambda l, j: (0, 0)),
                  pl.BlockSpec((1, D_MODEL, tn), lambda l, j: (l, 0, j)),
                  pl.BlockSpec((1, 1, tn), lambda l, j: (l, 0, j))],
        out_specs=pl.BlockSpec((1, nb, tn), lambda l, j: (l, 0, j)),
        out_shape=jax.ShapeDtypeStruct((DEPTH, nb, n), F32),
        compiler_params=_cparams(("parallel", "parallel"), est),
        name="ada_mod",
    )(c_all, w_ada, b_ada.reshape(DEPTH, 1, n))


def _inproj_kernel(x_hbm, sc_ref, sh_ref, g_ref, w_ref, wn_ref, wkr_ref, o_ref, kr_ref, h_scr, x_buf, x_sem,
                   *, n_main):
    i = pl.program_id(0)
    j = pl.program_id(1)
    tm = x_buf.shape[0]
    nt = (((1,), (1,)), ((), ()))

    def x_copy(tile):
        return pltpu.make_async_copy(x_hbm.at[pl.ds(tile * tm, tm), :], x_buf, x_sem)

    @pl.when((j == 0) & (i == 0))
    def _():
        x_copy(0).start()

    @pl.when(j == 0)
    def _():
        wkr = wkr_ref[...]
        z = jnp.zeros((HALF_ROPE, wkr.shape[1]), wkr.dtype)
        wkr = jnp.concatenate([wkr[:HALF_ROPE], z, wkr[HALF_ROPE:], z], axis=0).astype(BF16)
        x_copy(i).wait()
        for r0 in range(0, tm, TOKEN_TILE):
            rs = pl.ds(r0, TOKEN_TILE)
            hb = _norm_mod(x_buf[rs, :], g_ref[...], sc_ref[...], sh_ref[...]).astype(BF16)
            h_scr[rs, :] = hb
            kr_ref[rs, :] = lax.dot_general(hb, wkr, nt, preferred_element_type=F32)

    @pl.when((j == 1) & (i + 1 < pl.num_programs(0)))
    def _():
        x_copy(i + 1).start()

    @pl.when(j < n_main)
    def _():
        o_ref[...] = lax.dot_general(h_scr[...], w_ref[...].astype(BF16), nt,
                                     preferred_element_type=F32).astype(BF16)

    @pl.when(j >= n_main)
    def _():
        w = jnp.concatenate([w_ref[MLA_ROPE:, :], wn_ref[...]], axis=0).astype(BF16)
        o_ref[...] = lax.dot_general(h_scr[...], w, nt, preferred_element_type=F32).astype(BF16)


def _inproj_call(x, mod, norm_g, w_in_t, l, tm, groups, mod_stride):
    m = x.shape[0]
    n_main = COL_GA // IN_TN
    sub = IN_TN // MLA_ROPE
    assert groups == 1 or tm == TOKEN_TILE
    est = (tm * D_MODEL * 4 + tm * D_MODEL * 2 + 2 * D_MODEL * (IN_TN + 2 * MLA_ROPE) * 4
           + D_MODEL * IN_TN * 2 + 3 * tm * IN_TN * 4 + 6 * TOKEN_TILE * D_MODEL * 4 + VMEM_TEMP_ALLOWANCE)
    mod_spec = lambda k: pl.BlockSpec((None, None, None, groups, D_MODEL),
                                      lambda i, j: (l, k, i * mod_stride, 0, 0))
    return pl.pallas_call(
        functools.partial(_inproj_kernel, n_main=n_main),
        grid=(m // tm, MAIN_COLS // IN_TN),
        in_specs=[pl.BlockSpec(memory_space=pl.ANY),
                  mod_spec(1), mod_spec(0),
                  pl.BlockSpec((None, 1, D_MODEL), lambda i, j: (l, 0, 0)),
                  pl.BlockSpec((None, IN_TN, D_MODEL), lambda i, j: (l, j, 0)),
                  pl.BlockSpec((None, MLA_ROPE, D_MODEL),
                               lambda i, j: (l, (jnp.maximum(j, n_main) + 1) * sub, 0)),
                  pl.BlockSpec((None, MLA_ROPE, D_MODEL), lambda i, j: (l, COL_GA // MLA_ROPE, 0))],
        out_specs=[pl.BlockSpec((tm, IN_TN), lambda i, j: (i, j)),
                   pl.BlockSpec((tm, V7X_LANES), lambda i, j: (i, 0))],
        out_shape=[jax.ShapeDtypeStruct((m, MAIN_COLS), BF16),
                   jax.ShapeDtypeStruct((m, V7X_LANES), F32)],
        scratch_shapes=[pltpu.VMEM((tm, D_MODEL), BF16), pltpu.VMEM((tm, D_MODEL), F32),
                        pltpu.SemaphoreType.DMA(())],
        compiler_params=_cparams(("arbitrary", "arbitrary"), est),
        name="in_proj",
    )(x, mod, mod, norm_g, w_in_t, w_in_t, w_in_t)


def _ret_kernel(lg_ref, q_ref, k_ref, v_ref, rg_ref, ga_ref, cos_ref, sin_ref, s0_ref, *rest,
                chunk_len, heads):
    a_ref, st_ref, dm_scr = rest[-3:]
    hg = pl.program_id(1)
    c = pl.program_id(2)
    L = chunk_len
    lgs = [lg_ref[hg * heads + t] for t in range(heads)]

    @pl.when(c == 0)
    def _():
        st_ref[...] = s0_ref[...]
        ri = lax.broadcasted_iota(jnp.int32, (L, L), 0)
        ci = lax.broadcasted_iota(jnp.int32, (L, L), 1)
        diff = (ri - ci).astype(F32)
        for t in range(heads):
            dm_scr[t] = jnp.where(diff >= 0, jnp.exp(jnp.maximum(diff, 0.0) * lgs[t]), 0.0)

    cos = cos_ref[...]
    sin = sin_ref[...]
    half = RET_DK // 2
    idx = lax.broadcasted_iota(jnp.int32, (L, 1), 0).astype(F32)
    nt = (((1,), (1,)), ((), ()))
    tn = (((0,), (0,)), ((), ()))

    def rope(x):
        x1, x2 = x[:, :half], x[:, half:]
        return jnp.concatenate([x1 * cos - x2 * sin, x1 * sin + x2 * cos], axis=-1)

    stage = []
    for t in range(heads):
        cs = pl.ds(t * RET_DK, RET_DK)
        q = rope(q_ref[:, cs].astype(F32))
        k = rope(k_ref[:, cs].astype(F32)) * (RET_DK ** -0.5)
        vb = v_ref[:, cs]
        qb = q.astype(BF16)
        zeta = jnp.exp((L - 1.0 - idx) * lgs[t])
        g_l = jnp.exp(jnp.full((1, 1), float(L), F32) * lgs[t])
        st = st_ref[0, t]
        scores = lax.dot_general(qb, k.astype(BF16), nt, preferred_element_type=F32)
        cross = jnp.dot(qb, st.astype(BF16), preferred_element_type=F32)
        st_ref[0, t] = st * g_l + lax.dot_general((k * zeta).astype(BF16), vb, tn, preferred_element_type=F32)
        stage.append((scores, cross, vb))

    outs = []
    for t in range(heads):
        scores, cross, vb = stage[t]
        xi = jnp.exp((idx + 1.0) * lgs[t])
        outs.append(jnp.dot((scores * dm_scr[t]).astype(BF16), vb, preferred_element_type=F32) + cross * xi)

    for t in range(heads):
        cs = pl.ds(t * RET_DK, RET_DK)
        o = outs[t]
        mu = jnp.mean(o, axis=-1, keepdims=True)
        d = o - mu
        var = jnp.mean(d * d, axis=-1, keepdims=True)
        on = d * lax.rsqrt(var + GN_EPS)
        rg = rg_ref[:, cs].astype(F32)
        a_ref[:, cs] = (_sigmoid(ga_ref[:, cs].astype(F32)) * ((rg * _sigmoid(rg)) * on)).astype(BF16)


def _stacked_out(stack_prev, in_specs, args, out_index):
    in_specs.append(pl.BlockSpec(memory_space=pl.ANY))
    args.append(stack_prev)
    return {len(args) - 1: out_index}


def _ret_call(proj, log_g, cos, sin, state0, state_layer, batch, seq, chunk_len, heads, l, stack_prev):
    L = chunk_len
    nc = seq // L
    w = heads * RET_DK
    assert COL_GA % w == 0 and RET_HEADS % heads == 0
    col = lambda base: (lambda b, h, c: (b * nc + c, base // w + h))
    blk = lambda base: pl.BlockSpec((L, w), col(base))
    nh = RET_HEADS
    est = (2 * 6 * L * w * 4 + 4 * heads * RET_DK * RET_DV * 4 + heads * L * L * 4
           + 8 * heads * L * max(L, RET_DK) * 4 + VMEM_TEMP_ALLOWANCE)
    in_specs = [pl.BlockSpec(memory_space=pltpu.SMEM),
                blk(COL_RQ), blk(COL_RK), blk(COL_RV), blk(COL_RG), blk(COL_GA),
                pl.BlockSpec((L, RET_DK // 2), lambda b, h, c: (c, 0)),
                pl.BlockSpec((L, RET_DK // 2), lambda b, h, c: (c, 0)),
                pl.BlockSpec((None, 1, heads, RET_DK, RET_DV), lambda b, h, c: (state_layer, b, h, 0, 0))]
    args = [log_g, proj, proj, proj, proj, proj, cos, sin, state0]
    aliases = _stacked_out(stack_prev, in_specs, args, 1)
    return pl.pallas_call(
        functools.partial(_ret_kernel, chunk_len=L, heads=heads),
        grid=(batch, nh // heads, nc),
        in_specs=in_specs,
        out_specs=[pl.BlockSpec((L, w), lambda b, h, c: (b * nc + c, h)),
                   pl.BlockSpec((None, 1, heads, RET_DK, RET_DV), lambda b, h, c: (l, b, h, 0, 0))],
        out_shape=[jax.ShapeDtypeStruct((batch * seq, D_MODEL), BF16),
                   jax.ShapeDtypeStruct((DEPTH, batch, nh, RET_DK, RET_DV), F32)],
        scratch_shapes=[pltpu.VMEM((heads, L, L), F32)],
        input_output_aliases=aliases,
        compiler_params=_cparams(("parallel", "parallel", "arbitrary"), est),
        name="retention",
    )(*args)


def _rope128(x, c, s):
    return x * c + pltpu.roll(x, V7X_LANES // 2, 1) * s


def _mla_q_kernel(dq_ref, dkv_ref, kr_ref, qn_ref, kvn_ref, c_ref, s_ref, wuq_ref, *rest):
    q_out, ckv_out, kro_out, kr64_out = rest[-4:]
    c = c_ref[...]
    s = s_ref[...]
    cq = _rms(dq_ref[...].astype(F32), qn_ref[...]).astype(BF16)
    for h in range(MLA_HEADS):
        lo = h * HEAD_QK
        qh = jnp.dot(cq, wuq_ref[:, lo:lo + HEAD_QK], preferred_element_type=F32) * QK_SCALE_LOG2
        q_out[:, lo:lo + MLA_NOPE] = qh[:, :MLA_NOPE].astype(BF16)
        q_out[:, lo + MLA_NOPE:lo + HEAD_QK] = _rope128(qh[:, MLA_NOPE:], c, s).astype(BF16)
    ckv_out[...] = _rms(dkv_ref[...].astype(F32), kvn_ref[...])
    kr = _rope128(kr_ref[...], c, s)
    kro_out[...] = kr
    kr64_out[...] = _unpack_rope_lanes(kr)


def _mla_q_call(proj, kr_raw, q_norm, kv_norm, rope_c, rope_s, w_uq_r, l, tm, ckv_prev, kr_prev):
    m = proj.shape[0]
    dq_blk = COL_DQ // MLA_Q_LORA
    qw = MLA_HEADS * HEAD_QK
    est = (2 * (2 * tm * MLA_Q_LORA * 4 + 3 * tm * V7X_LANES * 4) + 2 * MLA_Q_LORA * qw * 2
           + 2 * tm * qw * 2 + 2 * tm * MLA_KV_LORA * 4 + 2 * tm * V7X_LANES * 4 + 8 * tm * HEAD_QK * 4
           + VMEM_TEMP_ALLOWANCE)
    in_specs = [pl.BlockSpec((tm, MLA_Q_LORA), lambda i: (i, dq_blk)),
                pl.BlockSpec((tm, MLA_KV_LORA), lambda i: (i, dq_blk + 1)),
                pl.BlockSpec((tm, V7X_LANES), lambda i: (i, 0)),
                pl.BlockSpec((None, 1, MLA_Q_LORA), lambda i: (l, 0, 0)),
                pl.BlockSpec((None, 1, MLA_KV_LORA), lambda i: (l, 0, 0)),
                pl.BlockSpec((tm, V7X_LANES), lambda i: (i, 0)),
                pl.BlockSpec((tm, V7X_LANES), lambda i: (i, 0)),
                pl.BlockSpec((None, MLA_Q_LORA, qw), lambda i: (l, 0, 0))]
    args = [proj, proj, kr_raw, q_norm, kv_norm, rope_c, rope_s, w_uq_r]
    aliases = _stacked_out(ckv_prev, in_specs, args, 1)
    aliases.update(_stacked_out(kr_prev, in_specs, args, 3))
    return pl.pallas_call(
        _mla_q_kernel,
        grid=(m // tm,),
        in_specs=in_specs,
        out_specs=[pl.BlockSpec((tm, qw), lambda i: (i, 0)),
                   pl.BlockSpec((None, tm, MLA_KV_LORA), lambda i: (l, i, 0)),
                   pl.BlockSpec((tm, V7X_LANES), lambda i: (i, 0)),
                   pl.BlockSpec((None, tm, MLA_ROPE), lambda i: (l, i, 0))],
        out_shape=[jax.ShapeDtypeStruct((m, qw), BF16),
                   jax.ShapeDtypeStruct((DEPTH, m, MLA_KV_LORA), F32),
                   jax.ShapeDtypeStruct((m, V7X_LANES), F32),
                   jax.ShapeDtypeStruct((DEPTH, m, MLA_ROPE), F32)],
        input_output_aliases=aliases,
        compiler_params=_cparams(("parallel",), est),
        name="mla_latents",
    )(*args)


def _mla_q_sample_kernel(dq_ref, dkv_ref, kr_ref, qn_ref, kvn_ref, c_ref, s_ref, wuq_ref, wukv_ref,
                         qlat_out, qr_out, ckv_out, kro_out):
    c = c_ref[...]
    s = s_ref[...]
    cq = _rms(dq_ref[...].astype(F32), qn_ref[...]).astype(BF16)
    for h in range(MLA_HEADS):
        lo = h * HEAD_QK
        qh = jnp.dot(cq, wuq_ref[:, lo:lo + HEAD_QK], preferred_element_type=F32) * QK_SCALE_LOG2
        w_uk = wukv_ref[:, h * (MLA_NOPE + MLA_V):h * (MLA_NOPE + MLA_V) + MLA_NOPE]
        q_lat = lax.dot_general(qh[:, :MLA_NOPE].astype(BF16), w_uk, (((1,), (1,)), ((), ())),
                                preferred_element_type=F32)
        qlat_out[:, h * MLA_KV_LORA:(h + 1) * MLA_KV_LORA] = q_lat.astype(BF16)
        qr_out[:, h * MLA_ROPE:(h + 1) * MLA_ROPE] = _unpack_rope_lanes(
            _rope128(qh[:, MLA_NOPE:], c, s)).astype(BF16)
    ckv_out[...] = _rms(dkv_ref[...].astype(F32), kvn_ref[...])
    kro_out[...] = _unpack_rope_lanes(_rope128(kr_ref[...], c, s))


def _mla_q_sample_call(proj, kr_raw, q_norm, kv_norm, rope_c, rope_s, w_uq_r, w_ukv_r, l):
    m = proj.shape[0]
    dq_blk = COL_DQ // MLA_Q_LORA
    qw = MLA_HEADS * HEAD_QK
    ww = MLA_HEADS * (MLA_NOPE + MLA_V)
    lat_w = MLA_HEADS * MLA_KV_LORA
    rope_w = MLA_HEADS * MLA_ROPE
    est = (2 * (2 * m * MLA_Q_LORA * 4 + 3 * m * V7X_LANES * 4) + 2 * MLA_Q_LORA * (qw + ww) * 2
           + 2 * m * (lat_w + rope_w) * 2 + 2 * m * (MLA_KV_LORA + MLA_ROPE) * 4 + 8 * m * MLA_KV_LORA * 4
           + VMEM_TEMP_ALLOWANCE)
    full = lambda shape: pl.BlockSpec(shape, lambda i: (0,) * len(shape))
    return pl.pallas_call(
        _mla_q_sample_kernel,
        grid=(1,),
        in_specs=[pl.BlockSpec((m, MLA_Q_LORA), lambda i: (0, dq_blk)),
                  pl.BlockSpec((m, MLA_KV_LORA), lambda i: (0, dq_blk + 1)),
                  full((m, V7X_LANES)),
                  pl.BlockSpec((None, 1, MLA_Q_LORA), lambda i: (l, 0, 0)),
                  pl.BlockSpec((None, 1, MLA_KV_LORA), lambda i: (l, 0, 0)),
                  full((m, V7X_LANES)), full((m, V7X_LANES)),
                  pl.BlockSpec((None, MLA_Q_LORA, qw), lambda i: (l, 0, 0)),
                  pl.BlockSpec((None, MLA_KV_LORA, ww), lambda i: (l, 0, 0))],
        out_specs=[full((m, lat_w)), full((m, rope_w)), full((m, MLA_KV_LORA)), full((m, MLA_ROPE))],
        out_shape=[jax.ShapeDtypeStruct((m, lat_w), BF16),
                   jax.ShapeDtypeStruct((m, rope_w), BF16),
                   jax.ShapeDtypeStruct((m, MLA_KV_LORA), F32),
                   jax.ShapeDtypeStruct((m, MLA_ROPE), F32)],
        compiler_params=_cparams(("arbitrary",), est),
        name="mla_latents_sample",
    )(proj, proj, kr_raw, q_norm, kv_norm, rope_c, rope_s, w_uq_r, w_ukv_r)


def _mla_kv_kernel(ckv_ref, kr_ref, wukv_ref, kcat_out, v_out):
    cb = ckv_ref[...].astype(BF16)
    krb = kr_ref[...].astype(BF16)
    hw = MLA_NOPE + MLA_V
    for h in range(MLA_HEADS):
        kv = jnp.dot(cb, wukv_ref[:, h * hw:(h + 1) * hw], preferred_element_type=F32).astype(BF16)
        lo = h * HEAD_QK
        kcat_out[:, lo:lo + MLA_NOPE] = kv[:, :MLA_NOPE]
        kcat_out[:, lo + MLA_NOPE:lo + HEAD_QK] = krb
        v_out[:, h * MLA_V:(h + 1) * MLA_V] = kv[:, MLA_NOPE:]


def _mla_kv_call(ckv_stack, kr128, w_ukv_r, l, rows, tm):
    kw = MLA_HEADS * HEAD_QK
    vw = MLA_HEADS * MLA_V
    ww = MLA_HEADS * (MLA_NOPE + MLA_V)
    est = (2 * tm * (MLA_KV_LORA + V7X_LANES) * 4 + 2 * MLA_KV_LORA * ww * 2 + 2 * tm * (kw + vw) * 2
           + 8 * tm * HEAD_QK * 4 + VMEM_TEMP_ALLOWANCE)
    return pl.pallas_call(
        _mla_kv_kernel,
        grid=(rows // tm,),
        in_specs=[pl.BlockSpec((None, tm, MLA_KV_LORA), lambda i: (l, i, 0)),
                  pl.BlockSpec((tm, V7X_LANES), lambda i: (i, 0)),
                  pl.BlockSpec((None, MLA_KV_LORA, ww), lambda i: (l, 0, 0))],
        out_specs=[pl.BlockSpec((tm, kw), lambda i: (i, 0)),
                   pl.BlockSpec((tm, vw), lambda i: (i, 0))],
        out_shape=[jax.ShapeDtypeStruct((rows, kw), BF16),
                   jax.ShapeDtypeStruct((rows, vw), BF16)],
        compiler_params=_cparams(("parallel",), est),
        name="mla_kv_expand",
    )(ckv_stack, kr128, w_ukv_r)


def _scores(q, k_blk):
    return lax.dot_general(q, k_blk, (((1,), (1,)), ((), ())), preferred_element_type=F32)


def _softmax_tile(s, v_blk, m_prev, l_prev, acc_prev, tri):
    n_groups = s.shape[1] // V7X_LANES
    groups = [s[:, g * V7X_LANES:(g + 1) * V7X_LANES] for g in range(n_groups)]
    if tri is not None:
        n_tri = tri.shape[1] // V7X_LANES
        for t in range(n_tri):
            g = n_groups - n_tri + t
            groups[g] = jnp.where(tri[:, t * V7X_LANES:(t + 1) * V7X_LANES], groups[g], NEG_INF)
    m_new = jnp.maximum(m_prev, jnp.max(functools.reduce(jnp.maximum, groups), axis=-1, keepdims=True))
    alpha = jnp.exp2(m_prev - m_new)
    ps = [jnp.exp2(g - m_new) for g in groups]
    l_new = alpha * l_prev + functools.reduce(jnp.add, ps)
    p = jnp.concatenate([x.astype(BF16) for x in ps], axis=-1)
    acc_new = alpha * acc_prev + jnp.dot(p, v_blk, preferred_element_type=F32)
    return m_new, l_new, acc_new


def _attn_prompt_kernel(q_ref, k_ref, v_ref, gb_ref, o_ref, m_scr, l_scr, acc_scr, *, tq, tk, n_split, heads):
    for t in range(heads):
        qk = pl.ds(t * HEAD_QK, HEAD_QK)
        vv = pl.ds(t * MLA_V, MLA_V)
        _attend_head(q_ref.at[:, qk], k_ref.at[:, qk], v_ref.at[:, vv], gb_ref.at[:, vv], o_ref.at[:, vv],
                     m_scr, l_scr, acc_scr, tq=tq, tk=tk, n_split=n_split)


def _attend_head(q_ref, k_ref, v_ref, gb_ref, o_ref, m_scr, l_scr, acc_scr, *, tq, tk, n_split):
    i = pl.program_id(2)
    rows = tq // n_split
    m_scr[...] = jnp.full(m_scr.shape, NEG_INF, F32)
    l_scr[...] = jnp.zeros(l_scr.shape, F32)
    acc_scr[...] = jnp.zeros(acc_scr.shape, F32)

    def scores(r, k_blk):
        return _scores(q_ref[pl.ds(r * rows, rows), :], k_blk)

    def update(r, s, v_blk, mask):
        rs = pl.ds(r * rows, rows)
        m_new, l_new, acc_new = _softmax_tile(s, v_blk, m_scr[rs, :], l_scr[rs, :], acc_scr[rs, :], mask)
        m_scr[rs, :] = m_new
        l_scr[rs, :] = l_new
        acc_scr[rs, :] = acc_new

    blocks_per_tile = tq // tk

    def body(j, carry):
        for d in range(blocks_per_tile):
            start = pl.multiple_of(j * tq + d * tk, tk)
            k_blk = k_ref[pl.ds(start, tk), :]
            v_blk = v_ref[pl.ds(start, tk), :]
            ss = [scores(r, k_blk) for r in range(min(ATT_AHEAD, n_split))]
            for r in range(n_split):
                if r + ATT_AHEAD < n_split:
                    ss.append(scores(r + ATT_AHEAD, k_blk))
                update(r, ss[r], v_blk, None)
        return carry

    lax.fori_loop(0, i, body, 0)

    tri = (lax.broadcasted_iota(jnp.int32, (rows, rows), 1) // CHUNK
           <= lax.broadcasted_iota(jnp.int32, (rows, rows), 0) // CHUNK)
    for d in range(blocks_per_tile):
        k0 = d * tk
        todo = []
        for r in range(n_split):
            r0, r1 = r * rows, (r + 1) * rows
            width = min(k0 + tk, r1) - k0
            if width <= 0:
                continue
            on_diagonal = k0 + width > r0
            assert not on_diagonal or (k0 + width == r1 and width >= rows)
            todo.append((r, width, tri if on_diagonal else None))
        start = pl.multiple_of(i * tq + k0, tk)
        diag_scores = lambda t: scores(todo[t][0], k_ref[pl.ds(start, todo[t][1]), :])
        ss = [diag_scores(t) for t in range(min(ATT_AHEAD, len(todo)))]
        for t, (r, width, mask) in enumerate(todo):
            if t + ATT_AHEAD < len(todo):
                ss.append(diag_scores(t + ATT_AHEAD))
            update(r, ss[t], v_ref[pl.ds(start, width), :], mask)

    l_row = jnp.sum(l_scr[...], axis=-1, keepdims=True)
    o_ref[...] = (_sigmoid(gb_ref[...].astype(F32)) * (acc_scr[...] / l_row)).astype(BF16)


def _attn_prompt_call(q_cat, k_cat, v, proj, batch, seq, tq):
    nq = seq // tq
    hg = ATT_HEADS
    assert MLA_HEADS % hg == 0 and COL_GB % (hg * MLA_V) == 0
    gb_blk = COL_GB // (hg * MLA_V)
    est = (2 * hg * (tq * HEAD_QK * 2 + seq * HEAD_QK * 2 + seq * MLA_V * 2 + 2 * tq * MLA_V * 4)
           + 3 * tq * V7X_LANES * 4 + 6 * tq * ATT_TK * 4 + VMEM_TEMP_ALLOWANCE)
    return pl.pallas_call(
        functools.partial(_attn_prompt_kernel, tq=tq, tk=ATT_TK, n_split=ATT_SPLIT, heads=hg),
        grid=(batch, MLA_HEADS // hg, nq),
        in_specs=[pl.BlockSpec((tq, hg * HEAD_QK), lambda b, h, i: (b * nq + i, h)),
                  pl.BlockSpec((seq, hg * HEAD_QK), lambda b, h, i: (b, h)),
                  pl.BlockSpec((seq, hg * MLA_V), lambda b, h, i: (b, h)),
                  pl.BlockSpec((tq, hg * MLA_V), lambda b, h, i: (b * nq + i, gb_blk + h))],
        out_specs=pl.BlockSpec((tq, hg * MLA_V), lambda b, h, i: (b * nq + i, h)),
        out_shape=jax.ShapeDtypeStruct((batch * seq, MLA_HEADS * MLA_V), BF16),
        scratch_shapes=[pltpu.VMEM((tq, V7X_LANES), F32), pltpu.VMEM((tq, V7X_LANES), F32),
                        pltpu.VMEM((tq, MLA_V), F32)],
        compiler_params=_cparams(("parallel", "parallel", "arbitrary"), est),
        name="mla_attention_prompt",
    )(q_cat, k_cat, v, proj)


def _attn_sample_kernel(qlat_ref, qr_ref, cc_ref, ckr_ref, nc_ref, nkr_ref, gb0_ref, gb1_ref, wukv_ref,
                        o_ref, *, sq):
    nt = (((1,), (1,)), ((), ()))
    q_lat = jnp.concatenate([qlat_ref[:, h * MLA_KV_LORA:(h + 1) * MLA_KV_LORA] for h in range(MLA_HEADS)],
                            axis=0)
    q_r = jnp.concatenate([qr_ref[:, h * MLA_ROPE:(h + 1) * MLA_ROPE] for h in range(MLA_HEADS)], axis=0)
    kc = cc_ref[...].astype(BF16)
    kn = nc_ref[...].astype(BF16)
    s_c = (lax.dot_general(q_lat, kc, nt, preferred_element_type=F32)
           + jnp.dot(q_r, ckr_ref[...].astype(BF16), preferred_element_type=F32))
    s_n = (lax.dot_general(q_lat, kn, nt, preferred_element_type=F32)
           + lax.dot_general(q_r, nkr_ref[...].astype(BF16), nt, preferred_element_type=F32))
    m = jnp.maximum(jnp.max(s_c, axis=-1, keepdims=True), jnp.max(s_n, axis=-1, keepdims=True))
    p_c = jnp.exp2(s_c - m)
    p_n = jnp.exp2(s_n - m)
    l_row = jnp.sum(p_c, axis=-1, keepdims=True) + jnp.sum(p_n, axis=-1, keepdims=True)
    o_lat = (jnp.dot(p_c.astype(BF16), kc, preferred_element_type=F32)
             + jnp.dot(p_n.astype(BF16), kn, preferred_element_type=F32)) / l_row
    o_lat = o_lat.astype(BF16)
    hw = MLA_NOPE + MLA_V
    half = MLA_HEADS // 2
    for h in range(MLA_HEADS):
        w_uv = wukv_ref[:, h * hw + MLA_NOPE:(h + 1) * hw]
        o_h = jnp.dot(o_lat[h * sq:(h + 1) * sq, :], w_uv, preferred_element_type=F32)
        gb_ref = gb0_ref if h < half else gb1_ref
        gb = gb_ref[:, (h % half) * MLA_V:(h % half + 1) * MLA_V].astype(F32)
        o_ref[:, h * MLA_V:(h + 1) * MLA_V] = (_sigmoid(gb) * o_h).astype(BF16)


def _attn_sample_call(q_lat, q_rope, ckv_new, kr_new, cache_ckv, cache_kr_t, proj, w_ukv_r, l, batch, sq):
    assert (PAST_LEN + sq - 1) // CHUNK <= PAST_LEN // CHUNK
    past = cache_ckv.shape[2]
    lat_w = MLA_HEADS * MLA_KV_LORA
    rope_w = MLA_HEADS * MLA_ROPE
    ww = MLA_HEADS * (MLA_NOPE + MLA_V)
    gw = D_MODEL // 2
    gb_blk = COL_GB // gw
    rows = MLA_HEADS * sq
    est = (2 * (sq * (lat_w + rope_w) * 2 + past * (MLA_KV_LORA + V7X_LANES) * 4 + MLA_KV_LORA * ww * 2
                + 4 * sq * D_MODEL * 4) + past * (MLA_KV_LORA + V7X_LANES) * 2 + 6 * rows * past * 4
           + 4 * rows * MLA_KV_LORA * 4 + VMEM_TEMP_ALLOWANCE)
    return pl.pallas_call(
        functools.partial(_attn_sample_kernel, sq=sq),
        grid=(batch,),
        in_specs=[pl.BlockSpec((sq, lat_w), lambda b: (b, 0)),
                  pl.BlockSpec((sq, rope_w), lambda b: (b, 0)),
                  pl.BlockSpec((None, None, past, MLA_KV_LORA), lambda b: (l, b, 0, 0)),
                  pl.BlockSpec((None, None, MLA_ROPE, past), lambda b: (l, b, 0, 0)),
                  pl.BlockSpec((sq, MLA_KV_LORA), lambda b: (b, 0)),
                  pl.BlockSpec((sq, MLA_ROPE), lambda b: (b, 0)),
                  pl.BlockSpec((sq, gw), lambda b: (b, gb_blk)),
                  pl.BlockSpec((sq, gw), lambda b: (b, gb_blk + 1)),
                  pl.BlockSpec((None, MLA_KV_LORA, ww), lambda b: (l, 0, 0))],
        out_specs=pl.BlockSpec((sq, D_MODEL), lambda b: (b, 0)),
        out_shape=jax.ShapeDtypeStruct((batch * sq, D_MODEL), BF16),
        compiler_params=_cparams(("parallel",), est),
        name="mla_attention_sample",
    )(q_lat, q_rope, cache_ckv, cache_kr_t, ckv_new, kr_new, proj, proj, w_ukv_r)


def _merge_kernel(a_ref, b_ref, x_ref, g1_ref, w_ref, o_ref):
    m = (a_ref[...].astype(F32) + b_ref[...].astype(F32)).astype(BF16)
    mix = jnp.dot(m, w_ref[...], preferred_element_type=F32)
    o_ref[...] = _gate_res(x_ref[...], g1_ref[...], mix)


def _merge_call(a_part, b_part, x, mod, w_o_b, l, tm, groups):
    m = x.shape[0]
    est = (2 * 2 * tm * D_MODEL * 2 + 4 * tm * D_MODEL * 4 + D_MODEL * D_MODEL * 2 + 3 * tm * D_MODEL * 4
           + VMEM_TEMP_ALLOWANCE)
    return pl.pallas_call(
        _merge_kernel,
        grid=(m // tm,),
        in_specs=[pl.BlockSpec((tm, D_MODEL), lambda i: (i, 0)),
                  pl.BlockSpec((tm, D_MODEL), lambda i: (i, 0)),
                  pl.BlockSpec((tm, D_MODEL), lambda i: (i, 0)),
                  pl.BlockSpec((None, None, None, groups, D_MODEL), lambda i: (l, 2, i, 0, 0)),
                  pl.BlockSpec((None, D_MODEL, D_MODEL), lambda i: (l, 0, 0), pipeline_mode=pl.Buffered(1))],
        out_specs=pl.BlockSpec((tm, D_MODEL), lambda i: (i, 0)),
        out_shape=jax.ShapeDtypeStruct((m, D_MODEL), F32),
        compiler_params=_cparams(("parallel",), est),
        name="merge_out_proj",
    )(a_part, b_part, x, mod, w_o_b)


def _ffn_kernel(x_ref, sc_ref, sh_ref, g2_ref, gn_ref, fg_ref, wg_ref, wu_ref, wo_ref, o_ref, *rest,
                nh, final, emit_bf16):
    h_scr, acc_scr = rest[-2:]
    j = pl.program_id(1)

    @pl.when(j == 0)
    def _():
        h_scr[...] = _norm_mod(x_ref[...], gn_ref[...], sc_ref[...], sh_ref[...]).astype(BF16)
        acc_scr[...] = jnp.zeros(acc_scr.shape, F32)

    wg = wg_ref[...].astype(BF16)
    wu = wu_ref[...].astype(BF16)
    wo = wo_ref[...].astype(BF16)
    if emit_bf16:
        wgb_ref, wub_ref, wob_ref = rest[:3]
        wgb_ref[...] = wg
        wub_ref[...] = wu
        wob_ref[...] = wo
    hb = h_scr[...]
    gate = jnp.dot(hb, wg, preferred_element_type=F32)
    up = jnp.dot(hb, wu, preferred_element_type=F32)
    act = ((gate * _sigmoid(gate)) * up).astype(BF16)
    acc_scr[...] += jnp.dot(act, wo, preferred_element_type=F32)

    @pl.when(j == nh - 1)
    def _():
        y = _gate_res(x_ref[...], g2_ref[...], acc_scr[...])
        o_ref[...] = _rms(y, fg_ref[...]) if final else y


def _ffn_call(x, mod, norm_g, norm_final, weights, l, tm, groups, th):
    m = x.shape[0]
    nh = FFN_HIDDEN // th
    emit_bf16 = len(weights) == 2
    assert not emit_bf16 or m == tm
    wbytes = 4 if emit_bf16 else 2
    est = (4 * tm * D_MODEL * 4 + tm * D_MODEL * 2 + tm * D_MODEL * 4 + 2 * 3 * D_MODEL * th * wbytes
           + (2 * 3 + 3) * D_MODEL * th * 2 * emit_bf16 + 4 * tm * th * 4 + tm * D_MODEL * 4 + VMEM_TEMP_ALLOWANCE)
    mod_spec = lambda k: pl.BlockSpec((None, None, None, groups, D_MODEL), lambda i, j: (l, k, i, 0, 0))
    out_specs = [pl.BlockSpec((tm, D_MODEL), lambda i, j: (i, 0))]
    out_shape = [jax.ShapeDtypeStruct((m, D_MODEL), F32)]
    if emit_bf16:
        w_in, w_out = weights
        w_specs = [pl.BlockSpec((None, D_MODEL, th), lambda i, j: (l, 0, j)),
                   pl.BlockSpec((None, D_MODEL, th), lambda i, j: (l, 0, nh + j)),
                   pl.BlockSpec((None, th, D_MODEL), lambda i, j: (l, j, 0))]
        w_args = [w_in, w_in, w_out]
        out_specs += [pl.BlockSpec((D_MODEL, th), lambda i, j: (0, j)),
                      pl.BlockSpec((D_MODEL, th), lambda i, j: (0, j)),
                      pl.BlockSpec((th, D_MODEL), lambda i, j: (j, 0))]
        out_shape += [jax.ShapeDtypeStruct((D_MODEL, FFN_HIDDEN), BF16),
                      jax.ShapeDtypeStruct((D_MODEL, FFN_HIDDEN), BF16),
                      jax.ShapeDtypeStruct((FFN_HIDDEN, D_MODEL), BF16)]
    else:
        w_specs = [pl.BlockSpec((D_MODEL, th), lambda i, j: (0, j)),
                   pl.BlockSpec((D_MODEL, th), lambda i, j: (0, j)),
                   pl.BlockSpec((th, D_MODEL), lambda i, j: (j, 0))]
        w_args = list(weights)
    return pl.pallas_call(
        functools.partial(_ffn_kernel, nh=nh, final=(l == DEPTH - 1), emit_bf16=emit_bf16),
        grid=(m // tm, nh),
        in_specs=[pl.BlockSpec((tm, D_MODEL), lambda i, j: (i, 0)),
                  mod_spec(4), mod_spec(3), mod_spec(5),
                  pl.BlockSpec((None, 1, D_MODEL), lambda i, j: (l, 0, 0)),
                  pl.BlockSpec((1, D_MODEL), lambda i, j: (0, 0))] + w_specs,
        out_specs=out_specs,
        out_shape=out_shape,
        scratch_shapes=[pltpu.VMEM((tm, D_MODEL), BF16), pltpu.VMEM((tm, D_MODEL), F32)],
        compiler_params=_cparams(("parallel", "arbitrary"), est),
        name="ffn_swiglu",
    )(x, mod, mod, mod, norm_g, norm_final, *w_args)


def _rope_tables(pos, dim):
    inv = jnp.exp(-math.log(ROPE_BASE) * jnp.arange(0, dim, 2, dtype=F32) / dim)
    ang = pos.astype(F32)[:, None] * inv[None, :]
    return jnp.cos(ang), jnp.sin(ang)


def _mla_rope_tables(pos, reps):
    cos, sin = _rope_tables(pos, MLA_ROPE)
    z = jnp.zeros_like(cos)
    c = jnp.concatenate([cos, z, cos, z], axis=-1)
    s = jnp.concatenate([-sin, z, sin, z], axis=-1)
    return jnp.tile(c, (reps, 1)), jnp.tile(s, (reps, 1))


def _prep_weights(w_uq, w_ukv):
    uq = w_uq.reshape(DEPTH, MLA_Q_LORA, MLA_HEADS, MLA_NOPE + MLA_ROPE)
    uq = jnp.concatenate([uq[..., :MLA_NOPE], _pack_rope_lanes(uq[..., MLA_NOPE:])], axis=-1)
    w_uq_r = uq.reshape(DEPTH, MLA_Q_LORA, MLA_HEADS * HEAD_QK).astype(BF16)
    return w_uq_r, w_ukv.astype(BF16)


def _layer_prompt(l, x, mod, tm, batch, seq, ret_tabs, mla_tabs, zero_state, log_g, W, stacks, ffn_w):
    proj, kr_raw = _inproj_call(x, mod, W["norm_mix"], W["w_in_t"], l, IN_TM_PROMPT, 1, IN_TM_PROMPT // tm)
    ckv_prev, kr_prev, st_prev = stacks
    a_part, st_stack = _ret_call(proj, log_g, ret_tabs[0], ret_tabs[1], zero_state, 0, batch, seq,
                                 RET_L_PROMPT, RET_HEADS_PROMPT, l, st_prev)
    q_cat, ckv_stack, kr128, kr_stack = _mla_q_call(proj, kr_raw, W["q_norm"], W["kv_norm"], mla_tabs[0],
                                                    mla_tabs[1], W["w_uq_r"], l, tm, ckv_prev, kr_prev)
    k_cat, v = _mla_kv_call(ckv_stack, kr128, W["w_ukv_r"], l, batch * seq, tm)
    b_part = _attn_prompt_call(q_cat, k_cat, v, proj, batch, seq, ATT_TQ)
    x = _merge_call(a_part, b_part, x, mod, W["w_o"], l, tm, 1)
    x, = _ffn_call(x, mod, W["norm_ffn"], W["norm_final"], ffn_w, l, tm, 1, FFN_TH)
    return x, (ckv_stack, kr_stack, st_stack)


def _layer_sample(l, x, mod, groups, tm, batch, seq, ret_tabs, mla_tabs, state_ret, cache_ckv, cache_kr,
                  log_g, W, st_prev):
    proj, kr_raw = _inproj_call(x, mod, W["norm_mix"], W["w_in_t"], l, tm, groups, 1)
    a_part, st_stack = _ret_call(proj, log_g, ret_tabs[0], ret_tabs[1], state_ret, l, batch, seq, seq,
                                 RET_HEADS_SAMPLE, l, st_prev)
    q_lat, q_rope, ckv, kr = _mla_q_sample_call(proj, kr_raw, W["q_norm"], W["kv_norm"], mla_tabs[0],
                                                mla_tabs[1], W["w_uq_r"], W["w_ukv_r"], l)
    b_part = _attn_sample_call(q_lat, q_rope, ckv, kr, cache_ckv, cache_kr, proj, W["w_ukv_r"], l, batch, seq)
    x = _merge_call(a_part, b_part, x, mod, W["w_o"], l, tm, groups)
    x, *ffn_w = _ffn_call(x, mod, W["norm_ffn"], W["norm_final"], (W["w_ffn_in"], W["w_ffn_out"]), l, tm,
                          groups, FFN_TH_SAMPLE)
    return x, ckv, kr, st_stack, tuple(ffn_w)


def kernel(x_prompt, x_sample, c_prompt, c_sample, cache_mla_ckv, cache_mla_krope, state_ret, w_ada, b_ada,
           norm_mix, norm_ffn, w_in, mla_q_norm, w_uq, mla_kv_norm, w_ukv, w_o, w_ffn_in, w_ffn_out,
           norm_final):
    bp, sp, _ = x_prompt.shape
    bs, ss, _ = x_sample.shape
    tm = TOKEN_TILE
    assert sp % IN_TM_PROMPT == 0 and (bs * ss) % tm == 0 and tm % ss == 0

    w_uq_r, w_ukv_r = _prep_weights(w_uq, w_ukv)
    W = dict(w_in_t=jnp.swapaxes(w_in, 1, 2), w_uq_r=w_uq_r, w_ukv_r=w_ukv_r,
             w_o=w_o.astype(BF16), w_ffn_in=w_ffn_in, w_ffn_out=w_ffn_out,
             norm_mix=norm_mix.reshape(DEPTH, 1, D_MODEL), norm_ffn=norm_ffn.reshape(DEPTH, 1, D_MODEL),
             norm_final=norm_final.reshape(1, D_MODEL),
             q_norm=mla_q_norm.reshape(DEPTH, 1, MLA_Q_LORA), kv_norm=mla_kv_norm.reshape(DEPTH, 1, MLA_KV_LORA))

    c_rows = -(-(bp + bs) // ADA_ROW_ALIGN) * ADA_ROW_ALIGN
    c_all = jnp.concatenate([c_prompt, c_sample, jnp.zeros((c_rows - bp - bs, D_MODEL), F32)], axis=0)
    mod_all = _ada_call(c_all, w_ada, b_ada)[:, :bp + bs]
    mod_all = mod_all.reshape(DEPTH, bp + bs, 6, D_MODEL).transpose(0, 2, 1, 3)
    tiles_per_batch = sp // tm
    mod_p = jnp.repeat(mod_all[:, :, :bp], tiles_per_batch, axis=2)[:, :, :, None, :]
    groups_s = tm // ss
    mod_s = mod_all[:, :, bp:].reshape(DEPTH, 6, (bs * ss) // tm, groups_s, D_MODEL)

    log_g = jnp.log1p(-jnp.exp2(-RET_GAMMA_EXP0 - jnp.arange(RET_HEADS, dtype=F32)))
    pos_p = jnp.arange(sp)
    pos_s = PAST_LEN + jnp.arange(ss)
    ret_tabs_p = _rope_tables(pos_p, RET_DK)
    ret_tabs_s = _rope_tables(pos_s, RET_DK)
    mla_tabs_p = _mla_rope_tables(pos_p, bp)
    mla_tabs_s = _mla_rope_tables(pos_s, bs)
    zero_state = jnp.zeros((1, bp, RET_HEADS, RET_DK, RET_DV), F32)
    cache_kr_t = jnp.swapaxes(cache_mla_krope, 2, 3)

    xp = x_prompt.reshape(bp * sp, D_MODEL)
    xs = x_sample.reshape(bs * ss, D_MODEL)
    stacks_p = (jnp.zeros((DEPTH, bp * sp, MLA_KV_LORA), F32), jnp.zeros((DEPTH, bp * sp, MLA_ROPE), F32),
                jnp.zeros((DEPTH, bp, RET_HEADS, RET_DK, RET_DV), F32))
    st_s = jnp.zeros((DEPTH, bs, RET_HEADS, RET_DK, RET_DV), F32)
    ckv_s, kr_s = [], []
    for l in range(DEPTH):
        xs, ckv, kr, st_s, ffn_w = _layer_sample(l, xs, mod_s, groups_s, tm, bs, ss, ret_tabs_s, mla_tabs_s,
                                                 state_ret, cache_mla_ckv, cache_kr_t, log_g, W, st_s)
        xp, stacks_p = _layer_prompt(l, xp, mod_p, tm, bp, sp, ret_tabs_p, mla_tabs_p, zero_state, log_g, W,
                                     stacks_p, ffn_w)
        ckv_s.append(ckv.reshape(bs, ss, MLA_KV_LORA))
        kr_s.append(kr.reshape(bs, ss, MLA_ROPE))

    y_prompt = xp.reshape(bp, sp, D_MODEL)
    y_sample = xs.reshape(bs, ss, D_MODEL)
    ckv_p, kr_p, st_p = stacks_p
    return (y_prompt, y_sample, ckv_p.reshape(DEPTH, bp, sp, MLA_KV_LORA), kr_p.reshape(DEPTH, bp, sp, MLA_ROPE),
            st_p, jnp.stack(ckv_s), jnp.stack(kr_s), st_s)
```

```python
import functools
import math

import jax
import jax.numpy as jnp
from jax import lax
from jax.experimental import pallas as pl
from jax.experimental.pallas import tpu as pltpu

D_MODEL = 2048
DEPTH = 4
PAST_LEN = 1024
CHUNK = 64
RET_HEADS = 8
RET_DK = D_MODEL // RET_HEADS
RET_DV = D_MODEL // RET_HEADS
MLA_HEADS = 16
MLA_Q_LORA = D_MODEL // 4
MLA_KV_LORA = D_MODEL // 4
MLA_NOPE = 128
MLA_ROPE = 64
MLA_V = D_MODEL // MLA_HEADS
FFN_HIDDEN = -(-8 * D_MODEL // (3 * 256)) * 256
ROPE_BASE = 10000.0
RET_GAMMA_EXP0 = 5.0
RMS_EPS = 1e-6
GN_EPS = 1e-5
NEG_INF = -1e30

F32 = jnp.float32
BF16 = jnp.bfloat16

V7X_LANES = 128
V7X_VMEM_LIMIT_CAP = 56 * 1024 * 1024
VMEM_TEMP_ALLOWANCE = 4 * 1024 * 1024

COL_RQ, COL_RK, COL_RV, COL_RG = 0, D_MODEL, 2 * D_MODEL, 3 * D_MODEL
COL_DQ = 4 * D_MODEL
COL_DKV = COL_DQ + MLA_Q_LORA
COL_GA = COL_DKV + MLA_KV_LORA
COL_GB = COL_GA + D_MODEL
MAIN_COLS = COL_GB + D_MODEL
HEAD_QK = 2 * V7X_LANES
HALF_ROPE = MLA_ROPE // 2
ADA_ROW_ALIGN = 16
TOKEN_TILE = 512
IN_TM_PROMPT = 2048
IN_TN = 512
ADA_TN = 1024
FFN_TH = 512
FFN_TH_SAMPLE = 256
RET_L_PROMPT = 256
RET_HEADS_PROMPT = 4
RET_HEADS_SAMPLE = 4
ATT_TQ = 1024
ATT_TK = 1024
ATT_AHEAD = 2
ATT_SPLIT = 4
QK_SCALE_LOG2 = (MLA_NOPE + MLA_ROPE) ** -0.5 * math.log2(math.e)


def _cparams(sem, est_bytes):
    return pltpu.CompilerParams(dimension_semantics=sem,
                                vmem_limit_bytes=min(int(est_bytes), V7X_VMEM_LIMIT_CAP))


def _sigmoid(x):
    return jax.nn.sigmoid(x)


def _rms(x, g):
    return x * lax.rsqrt(jnp.mean(x * x, axis=-1, keepdims=True) + RMS_EPS) * g


def _norm_mod(x, g, sc, sh):
    tm, d = x.shape
    groups = sc.shape[0]
    y = _rms(x, g)
    if groups == 1:
        return y * (1.0 + sc) + sh
    y3 = y.reshape(groups, tm // groups, d)
    return (y3 * (1.0 + sc[:, None, :]) + sh[:, None, :]).reshape(tm, d)


def _gate_res(x, gate, upd):
    tm, n = x.shape
    groups = gate.shape[0]
    if groups == 1:
        return x + gate * upd
    return x + (gate[:, None, :] * upd.reshape(groups, tm // groups, n)).reshape(tm, n)


def _pack_rope_lanes(x):
    z = jnp.zeros(x.shape[:-1] + (HALF_ROPE,), x.dtype)
    return jnp.concatenate([x[..., :HALF_ROPE], z, x[..., HALF_ROPE:], z], axis=-1)


def _unpack_rope_lanes(x):
    return jnp.concatenate([x[..., :HALF_ROPE], x[..., 2 * HALF_ROPE:3 * HALF_ROPE]], axis=-1)


def _ada_kernel(c_ref, w_ref, b_ref, o_ref):
    c = c_ref[...]
    a = (c * _sigmoid(c)).astype(BF16)
    o_ref[0] = jnp.dot(a, w_ref[0].astype(BF16), preferred_element_type=F32) + b_ref[0]


def _ada_call(c_all, w_ada, b_ada):
    nb = c_all.shape[0]
    n = w_ada.shape[-1]
    tn = ADA_TN
    est = 2 * (D_MODEL * tn * 4) + D_MODEL * tn * 2 + 4 * nb * (D_MODEL + 2 * tn) * 4 + VMEM_TEMP_ALLOWANCE
    return pl.pallas_call(
        _ada_kernel,
        grid=(DEPTH, n // tn),
        in_specs=[pl.BlockSpec((nb, D_MODEL), lambda l, j: (0, 0)),
                  pl.BlockSpec((1, D_MODEL, tn), lambda l, j: (l, 0, j)),
                  pl.BlockSpec((1, 1, tn), lambda l, j: (l, 0, j))],
        out_specs=pl.BlockSpec((1, nb, tn), lambda l, j: (l, 0, j)),
        out_shape=jax.ShapeDtypeStruct((DEPTH, nb, n), F32),
        compiler_params=_cparams(("parallel", "parallel"), est),
        name="ada_mod",
    )(c_all, w_ada, b_ada.reshape(DEPTH, 1, n))


def _inproj_kernel(x_hbm, sc_ref, sh_ref, g_ref, w_ref, wn_ref, wkr_ref, o_ref, kr_ref, h_scr, x_buf, x_sem,
                   *, n_main):
    i = pl.program_id(0)
    j = pl.program_id(1)
    tm = x_buf.shape[0]
    nt = (((1,), (1,)), ((), ()))

    def x_copy(tile):
        return pltpu.make_async_copy(x_hbm.at[pl.ds(tile * tm, tm), :], x_buf, x_sem)

    @pl.when((j == 0) & (i == 0))
    def _():
        x_copy(0).start()

    @pl.when(j == 0)
    def _():
        wkr = wkr_ref[...]
        z = jnp.zeros((HALF_ROPE, wkr.shape[1]), wkr.dtype)
        wkr = jnp.concatenate([wkr[:HALF_ROPE], z, wkr[HALF_ROPE:], z], axis=0).astype(BF16)
        x_copy(i).wait()
        for r0 in range(0, tm, TOKEN_TILE):
            rs = pl.ds(r0, TOKEN_TILE)
            hb = _norm_mod(x_buf[rs, :], g_ref[...], sc_ref[...], sh_ref[...]).astype(BF16)
            h_scr[rs, :] = hb
            kr_ref[rs, :] = lax.dot_general(hb, wkr, nt, preferred_element_type=F32)

    @pl.when((j == 1) & (i + 1 < pl.num_programs(0)))
    def _():
        x_copy(i + 1).start()

    @pl.when(j < n_main)
    def _():
        o_ref[...] = lax.dot_general(h_scr[...], w_ref[...].astype(BF16), nt,
                                     preferred_element_type=F32).astype(BF16)

    @pl.when(j >= n_main)
    def _():
        w = jnp.concatenate([w_ref[MLA_ROPE:, :], wn_ref[...]], axis=0).astype(BF16)
        o_ref[...] = lax.dot_general(h_scr[...], w, nt, preferred_element_type=F32).astype(BF16)


def _inproj_call(x, mod, norm_g, w_in_t, l, tm, groups, mod_stride):
    m = x.shape[0]
    n_main = COL_GA // IN_TN
    sub = IN_TN // MLA_ROPE
    assert groups == 1 or tm == TOKEN_TILE
    est = (tm * D_MODEL * 4 + tm * D_MODEL * 2 + 2 * D_MODEL * (IN_TN + 2 * MLA_ROPE) * 4
           + D_MODEL * IN_TN * 2 + 3 * tm * IN_TN * 4 + 6 * TOKEN_TILE * D_MODEL * 4 + VMEM_TEMP_ALLOWANCE)
    mod_spec = lambda k: pl.BlockSpec((None, None, None, groups, D_MODEL),
                                      lambda i, j: (l, k, i * mod_stride, 0, 0))
    return pl.pallas_call(
        functools.partial(_inproj_kernel, n_main=n_main),
        grid=(m // tm, MAIN_COLS // IN_TN),
        in_specs=[pl.BlockSpec(memory_space=pl.ANY),
                  mod_spec(1), mod_spec(0),
                  pl.BlockSpec((None, 1, D_MODEL), lambda i, j: (l, 0, 0)),
                  pl.BlockSpec((None, IN_TN, D_MODEL), lambda i, j: (l, j, 0)),
                  pl.BlockSpec((None, MLA_ROPE, D_MODEL),
                               lambda i, j: (l, (jnp.maximum(j, n_main) + 1) * sub, 0)),
                  pl.BlockSpec((None, MLA_ROPE, D_MODEL), lambda i, j: (l, COL_GA // MLA_ROPE, 0))],
        out_specs=[pl.BlockSpec((tm, IN_TN), lambda i, j: (i, j)),
                   pl.BlockSpec((tm, V7X_LANES), lambda i, j: (i, 0))],
        out_shape=[jax.ShapeDtypeStruct((m, MAIN_COLS), BF16),
                   jax.ShapeDtypeStruct((m, V7X_LANES), F32)],
        scratch_shapes=[pltpu.VMEM((tm, D_MODEL), BF16), pltpu.VMEM((tm, D_MODEL), F32),
                        pltpu.SemaphoreType.DMA(())],
        compiler_params=_cparams(("arbitrary", "arbitrary"), est),
        name="in_proj",
    )(x, mod, mod, norm_g, w_in_t, w_in_t, w_in_t)


def _ret_kernel(lg_ref, q_ref, k_ref, v_ref, rg_ref, ga_ref, cos_ref, sin_ref, s0_ref, *rest,
                chunk_len, heads):
    a_ref, st_ref, dm_scr = rest[-3:]
    hg = pl.program_id(1)
    c = pl.program_id(2)
    L = chunk_len
    lgs = [lg_ref[hg * heads + t] for t in range(heads)]

    @pl.when(c == 0)
    def _():
        st_ref[...] = s0_ref[...]
        ri = lax.broadcasted_iota(jnp.int32, (L, L), 0)
        ci = lax.broadcasted_iota(jnp.int32, (L, L), 1)
        diff = (ri - ci).astype(F32)
        for t in range(heads):
            dm_scr[t] = jnp.where(diff >= 0, jnp.exp(jnp.maximum(diff, 0.0) * lgs[t]), 0.0)

    cos = cos_ref[...]
    sin = sin_ref[...]
    half = RET_DK // 2
    idx = lax.broadcasted_iota(jnp.int32, (L, 1), 0).astype(F32)
    nt = (((1,), (1,)), ((), ()))
    tn = (((0,), (0,)), ((), ()))

    def rope(x):
        x1, x2 = x[:, :half], x[:, half:]
        return jnp.concatenate([x1 * cos - x2 * sin, x1 * sin + x2 * cos], axis=-1)

    stage = []
    for t in range(heads):
        cs = pl.ds(t * RET_DK, RET_DK)
        q = rope(q_ref[:, cs].astype(F32))
        k = rope(k_ref[:, cs].astype(F32)) * (RET_DK ** -0.5)
        vb = v_ref[:, cs]
        qb = q.astype(BF16)
        zeta = jnp.exp((L - 1.0 - idx) * lgs[t])
        g_l = jnp.exp(jnp.full((1, 1), float(L), F32) * lgs[t])
        st = st_ref[0, t]
        scores = lax.dot_general(qb, k.astype(BF16), nt, preferred_element_type=F32)
        cross = jnp.dot(qb, st.astype(BF16), preferred_element_type=F32)
        st_ref[0, t] = st * g_l + lax.dot_general((k * zeta).astype(BF16), vb, tn, preferred_element_type=F32)
        stage.append((scores, cross, vb))

    outs = []
    for t in range(heads):
        scores, cross, vb = stage[t]
        xi = jnp.exp((idx + 1.0) * lgs[t])
        outs.append(jnp.dot((scores * dm_scr[t]).astype(BF16), vb, preferred_element_type=F32) + cross * xi)

    for t in range(heads):
        cs = pl.ds(t * RET_DK, RET_DK)
        o = outs[t]
        mu = jnp.mean(o, axis=-1, keepdims=True)
        d = o - mu
        var = jnp.mean(d * d, axis=-1, keepdims=True)
        on = d * lax.rsqrt(var + GN_EPS)
        rg = rg_ref[:, cs].astype(F32)
        a_ref[:, cs] = (_sigmoid(ga_ref[:, cs].astype(F32)) * ((rg * _sigmoid(rg)) * on)).astype(BF16)


def _stacked_out(stack_prev, in_specs, args, out_index):
    in_specs.append(pl.BlockSpec(memory_space=pl.ANY))
    args.append(stack_prev)
    return {len(args) - 1: out_index}


def _ret_call(proj, log_g, cos, sin, state0, state_layer, batch, seq, chunk_len, heads, l, stack_prev):
    L = chunk_len
    nc = seq // L
    w = heads * RET_DK
    assert COL_GA % w == 0 and RET_HEADS % heads == 0
    col = lambda base: (lambda b, h, c: (b * nc + c, base // w + h))
    blk = lambda base: pl.BlockSpec((L, w), col(base))
    nh = RET_HEADS
    est = (2 * 6 * L * w * 4 + 4 * heads * RET_DK * RET_DV * 4 + heads * L * L * 4
           + 8 * heads * L * max(L, RET_DK) * 4 + VMEM_TEMP_ALLOWANCE)
    in_specs = [pl.BlockSpec(memory_space=pltpu.SMEM),
                blk(COL_RQ), blk(COL_RK), blk(COL_RV), blk(COL_RG), blk(COL_GA),
                pl.BlockSpec((L, RET_DK // 2), lambda b, h, c: (c, 0)),
                pl.BlockSpec((L, RET_DK // 2), lambda b, h, c: (c, 0)),
                pl.BlockSpec((None, 1, heads, RET_DK, RET_DV), lambda b, h, c: (state_layer, b, h, 0, 0))]
    args = [log_g, proj, proj, proj, proj, proj, cos, sin, state0]
    aliases = _stacked_out(stack_prev, in_specs, args, 1)
    return pl.pallas_call(
        functools.partial(_ret_kernel, chunk_len=L, heads=heads),
        grid=(batch, nh // heads, nc),
        in_specs=in_specs,
        out_specs=[pl.BlockSpec((L, w), lambda b, h, c: (b * nc + c, h)),
                   pl.BlockSpec((None, 1, heads, RET_DK, RET_DV), lambda b, h, c: (l, b, h, 0, 0))],
        out_shape=[jax.ShapeDtypeStruct((batch * seq, D_MODEL), BF16),
                   jax.ShapeDtypeStruct((DEPTH, batch, nh, RET_DK, RET_DV), F32)],
        scratch_shapes=[pltpu.VMEM((heads, L, L), F32)],
        input_output_aliases=aliases,
        compiler_params=_cparams(("parallel", "parallel", "arbitrary"), est),
        name="retention",
    )(*args)


def _rope128(x, c, s):
    return x * c + pltpu.roll(x, V7X_LANES // 2, 1) * s


def _mla_q_kernel(dq_ref, dkv_ref, kr_ref, qn_ref, kvn_ref, c_ref, s_ref, wuq_ref, wukv_ref, *rest):
    q_out, ckv_out, kr64_out, kcat_out, v_out = rest[-5:]
    c = c_ref[...]
    s = s_ref[...]
    cq = _rms(dq_ref[...].astype(F32), qn_ref[...]).astype(BF16)
    for h in range(MLA_HEADS):
        lo = h * HEAD_QK
        qh = jnp.dot(cq, wuq_ref[:, lo:lo + HEAD_QK], preferred_element_type=F32) * QK_SCALE_LOG2
        q_out[:, lo:lo + MLA_NOPE] = qh[:, :MLA_NOPE].astype(BF16)
        q_out[:, lo + MLA_NOPE:lo + HEAD_QK] = _rope128(qh[:, MLA_NOPE:], c, s).astype(BF16)
    ckv = _rms(dkv_ref[...].astype(F32), kvn_ref[...])
    ckv_out[...] = ckv
    kr = _rope128(kr_ref[...], c, s)
    kr64_out[...] = _unpack_rope_lanes(kr)
    cb = ckv.astype(BF16)
    krb = kr.astype(BF16)
    hw = MLA_NOPE + MLA_V
    for h in range(MLA_HEADS):
        kv = jnp.dot(cb, wukv_ref[:, h * hw:(h + 1) * hw], preferred_element_type=F32).astype(BF16)
        lo = h * HEAD_QK
        kcat_out[:, lo:lo + MLA_NOPE] = kv[:, :MLA_NOPE]
        kcat_out[:, lo + MLA_NOPE:lo + HEAD_QK] = krb
        v_out[:, h * MLA_V:(h + 1) * MLA_V] = kv[:, MLA_NOPE:]


def _mla_q_call(proj, kr_raw, q_norm, kv_norm, rope_c, rope_s, w_uq_r, w_ukv_b, l, tm, ckv_prev, kr_prev):
    m = proj.shape[0]
    dq_blk = COL_DQ // MLA_Q_LORA
    qw = MLA_HEADS * HEAD_QK
    ww = MLA_HEADS * (MLA_NOPE + MLA_V)
    vw = MLA_HEADS * MLA_V
    est = (2 * (2 * tm * MLA_Q_LORA * 4 + 3 * tm * V7X_LANES * 4) + 2 * MLA_Q_LORA * (qw + ww) * 2
           + 2 * tm * (2 * qw + vw) * 2 + 2 * tm * MLA_KV_LORA * 4 + 2 * tm * V7X_LANES * 4
           + 8 * tm * HEAD_QK * 4 + VMEM_TEMP_ALLOWANCE)
    in_specs = [pl.BlockSpec((tm, MLA_Q_LORA), lambda i: (i, dq_blk)),
                pl.BlockSpec((tm, MLA_KV_LORA), lambda i: (i, dq_blk + 1)),
                pl.BlockSpec((tm, V7X_LANES), lambda i: (i, 0)),
                pl.BlockSpec((None, 1, MLA_Q_LORA), lambda i: (l, 0, 0)),
                pl.BlockSpec((None, 1, MLA_KV_LORA), lambda i: (l, 0, 0)),
                pl.BlockSpec((tm, V7X_LANES), lambda i: (i, 0)),
                pl.BlockSpec((tm, V7X_LANES), lambda i: (i, 0)),
                pl.BlockSpec((None, MLA_Q_LORA, qw), lambda i: (l, 0, 0)),
                pl.BlockSpec((None, MLA_KV_LORA, ww), lambda i: (l, 0, 0))]
    args = [proj, proj, kr_raw, q_norm, kv_norm, rope_c, rope_s, w_uq_r, w_ukv_b]
    aliases = _stacked_out(ckv_prev, in_specs, args, 1)
    aliases.update(_stacked_out(kr_prev, in_specs, args, 2))
    return pl.pallas_call(
        _mla_q_kernel,
        grid=(m // tm,),
        in_specs=in_specs,
        out_specs=[pl.BlockSpec((tm, qw), lambda i: (i, 0)),
                   pl.BlockSpec((None, tm, MLA_KV_LORA), lambda i: (l, i, 0)),
                   pl.BlockSpec((None, tm, MLA_ROPE), lambda i: (l, i, 0)),
                   pl.BlockSpec((tm, qw), lambda i: (i, 0)),
                   pl.BlockSpec((tm, vw), lambda i: (i, 0))],
        out_shape=[jax.ShapeDtypeStruct((m, qw), BF16),
                   jax.ShapeDtypeStruct((DEPTH, m, MLA_KV_LORA), F32),
                   jax.ShapeDtypeStruct((DEPTH, m, MLA_ROPE), F32),
                   jax.ShapeDtypeStruct((m, qw), BF16),
                   jax.ShapeDtypeStruct((m, vw), BF16)],
        input_output_aliases=aliases,
        compiler_params=_cparams(("parallel",), est),
        name="mla_latents",
    )(*args)


def _mla_q_sample_kernel(dq_ref, dkv_ref, kr_ref, qn_ref, kvn_ref, c_ref, s_ref, wuq_ref, wukv_ref,
                         qlat_out, qr_out, ckv_out, kro_out):
    c = c_ref[...]
    s = s_ref[...]
    cq = _rms(dq_ref[...].astype(F32), qn_ref[...]).astype(BF16)
    for h in range(MLA_HEADS):
        lo = h * HEAD_QK
        qh = jnp.dot(cq, wuq_ref[:, lo:lo + HEAD_QK], preferred_element_type=F32) * QK_SCALE_LOG2
        w_uk = wukv_ref[:, h * (MLA_NOPE + MLA_V):h * (MLA_NOPE + MLA_V) + MLA_NOPE]
        q_lat = lax.dot_general(qh[:, :MLA_NOPE].astype(BF16), w_uk, (((1,), (1,)), ((), ())),
                                preferred_element_type=F32)
        qlat_out[:, h * MLA_KV_LORA:(h + 1) * MLA_KV_LORA] = q_lat.astype(BF16)
        qr_out[:, h * MLA_ROPE:(h + 1) * MLA_ROPE] = _unpack_rope_lanes(
            _rope128(qh[:, MLA_NOPE:], c, s)).astype(BF16)
    ckv_out[...] = _rms(dkv_ref[...].astype(F32), kvn_ref[...])
    kro_out[...] = _unpack_rope_lanes(_rope128(kr_ref[...], c, s))


def _mla_q_sample_call(proj, kr_raw, q_norm, kv_norm, rope_c, rope_s, w_uq_r, w_ukv_r, l):
    m = proj.shape[0]
    dq_blk = COL_DQ // MLA_Q_LORA
    qw = MLA_HEADS * HEAD_QK
    ww = MLA_HEADS * (MLA_NOPE + MLA_V)
    lat_w = MLA_HEADS * MLA_KV_LORA
    rope_w = MLA_HEADS * MLA_ROPE
    est = (2 * (2 * m * MLA_Q_LORA * 4 + 3 * m * V7X_LANES * 4) + 2 * MLA_Q_LORA * (qw + ww) * 2
           + 2 * m * (lat_w + rope_w) * 2 + 2 * m * (MLA_KV_LORA + MLA_ROPE) * 4 + 8 * m * MLA_KV_LORA * 4
           + VMEM_TEMP_ALLOWANCE)
    full = lambda shape: pl.BlockSpec(shape, lambda i: (0,) * len(shape))
    return pl.pallas_call(
        _mla_q_sample_kernel,
        grid=(1,),
        in_specs=[pl.BlockSpec((m, MLA_Q_LORA), lambda i: (0, dq_blk)),
                  pl.BlockSpec((m, MLA_KV_LORA), lambda i: (0, dq_blk + 1)),
                  full((m, V7X_LANES)),
                  pl.BlockSpec((None, 1, MLA_Q_LORA), lambda i: (l, 0, 0)),
                  pl.BlockSpec((None, 1, MLA_KV_LORA), lambda i: (l, 0, 0)),
                  full((m, V7X_LANES)), full((m, V7X_LANES)),
                  pl.BlockSpec((None, MLA_Q_LORA, qw), lambda i: (l, 0, 0)),
                  pl.BlockSpec((None, MLA_KV_LORA, ww), lambda i: (l, 0, 0))],
        out_specs=[full((m, lat_w)), full((m, rope_w)), full((m, MLA_KV_LORA)), full((m, MLA_ROPE))],
        out_shape=[jax.ShapeDtypeStruct((m, lat_w), BF16),
                   jax.ShapeDtypeStruct((m, rope_w), BF16),
                   jax.ShapeDtypeStruct((m, MLA_KV_LORA), F32),
                   jax.ShapeDtypeStruct((m, MLA_ROPE), F32)],
        compiler_params=_cparams(("arbitrary",), est),
        name="mla_latents_sample",
    )(proj, proj, kr_raw, q_norm, kv_norm, rope_c, rope_s, w_uq_r, w_ukv_r)


def _mla_kv_kernel(ckv_ref, kr_ref, wukv_ref, kcat_out, v_out):
    cb = ckv_ref[...].astype(BF16)
    krb = kr_ref[...].astype(BF16)
    hw = MLA_NOPE + MLA_V
    for h in range(MLA_HEADS):
        kv = jnp.dot(cb, wukv_ref[:, h * hw:(h + 1) * hw], preferred_element_type=F32).astype(BF16)
        lo = h * HEAD_QK
        kcat_out[:, lo:lo + MLA_NOPE] = kv[:, :MLA_NOPE]
        kcat_out[:, lo + MLA_NOPE:lo + HEAD_QK] = krb
        v_out[:, h * MLA_V:(h + 1) * MLA_V] = kv[:, MLA_NOPE:]


def _mla_kv_call(ckv_stack, kr128, w_ukv_r, l, rows, tm):
    kw = MLA_HEADS * HEAD_QK
    vw = MLA_HEADS * MLA_V
    ww = MLA_HEADS * (MLA_NOPE + MLA_V)
    est = (2 * tm * (MLA_KV_LORA + V7X_LANES) * 4 + 2 * MLA_KV_LORA * ww * 2 + 2 * tm * (kw + vw) * 2
           + 8 * tm * HEAD_QK * 4 + VMEM_TEMP_ALLOWANCE)
    return pl.pallas_call(
        _mla_kv_kernel,
        grid=(rows // tm,),
        in_specs=[pl.BlockSpec((None, tm, MLA_KV_LORA), lambda i: (l, i, 0)),
                  pl.BlockSpec((tm, V7X_LANES), lambda i: (i, 0)),
                  pl.BlockSpec((None, MLA_KV_LORA, ww), lambda i: (l, 0, 0))],
        out_specs=[pl.BlockSpec((tm, kw), lambda i: (i, 0)),
                   pl.BlockSpec((tm, vw), lambda i: (i, 0))],
        out_shape=[jax.ShapeDtypeStruct((rows, kw), BF16),
                   jax.ShapeDtypeStruct((rows, vw), BF16)],
        compiler_params=_cparams(("parallel",), est),
        name="mla_kv_expand",
    )(ckv_stack, kr128, w_ukv_r)


def _scores(q, k_blk):
    return lax.dot_general(q, k_blk, (((1,), (1,)), ((), ())), preferred_element_type=F32)


def _softmax_tile(s, v_blk, m_prev, l_prev, acc_prev, tri):
    n_groups = s.shape[1] // V7X_LANES
    groups = [s[:, g * V7X_LANES:(g + 1) * V7X_LANES] for g in range(n_groups)]
    if tri is not None:
        n_tri = tri.shape[1] // V7X_LANES
        for t in range(n_tri):
            g = n_groups - n_tri + t
            groups[g] = jnp.where(tri[:, t * V7X_LANES:(t + 1) * V7X_LANES], groups[g], NEG_INF)
    m_new = jnp.maximum(m_prev, jnp.max(functools.reduce(jnp.maximum, groups), axis=-1, keepdims=True))
    alpha = jnp.exp2(m_prev - m_new)
    ps = [jnp.exp2(g - m_new) for g in groups]
    l_new = alpha * l_prev + functools.reduce(jnp.add, ps)
    p = jnp.concatenate([x.astype(BF16) for x in ps], axis=-1)
    acc_new = alpha * acc_prev + jnp.dot(p, v_blk, preferred_element_type=F32)
    return m_new, l_new, acc_new


def _attn_prompt_kernel(q_ref, k_ref, v_ref, gb_ref, o_ref, m_scr, l_scr, acc_scr, *, tq, tk, n_split):
    i = pl.program_id(2)
    rows = tq // n_split
    m_scr[...] = jnp.full(m_scr.shape, NEG_INF, F32)
    l_scr[...] = jnp.zeros(l_scr.shape, F32)
    acc_scr[...] = jnp.zeros(acc_scr.shape, F32)

    def scores(r, k_blk):
        return _scores(q_ref[pl.ds(r * rows, rows), :], k_blk)

    def update(r, s, v_blk, mask):
        rs = pl.ds(r * rows, rows)
        m_new, l_new, acc_new = _softmax_tile(s, v_blk, m_scr[rs, :], l_scr[rs, :], acc_scr[rs, :], mask)
        m_scr[rs, :] = m_new
        l_scr[rs, :] = l_new
        acc_scr[rs, :] = acc_new

    blocks_per_tile = tq // tk

    def body(j, carry):
        for d in range(blocks_per_tile):
            start = pl.multiple_of(j * tq + d * tk, tk)
            k_blk = k_ref[pl.ds(start, tk), :]
            v_blk = v_ref[pl.ds(start, tk), :]
            ss = [scores(r, k_blk) for r in range(min(ATT_AHEAD, n_split))]
            for r in range(n_split):
                if r + ATT_AHEAD < n_split:
                    ss.append(scores(r + ATT_AHEAD, k_blk))
                update(r, ss[r], v_blk, None)
        return carry

    lax.fori_loop(0, i, body, 0)

    tri = (lax.broadcasted_iota(jnp.int32, (rows, rows), 1) // CHUNK
           <= lax.broadcasted_iota(jnp.int32, (rows, rows), 0) // CHUNK)
    for d in range(blocks_per_tile):
        k0 = d * tk
        todo = []
        for r in range(n_split):
            r0, r1 = r * rows, (r + 1) * rows
            width = min(k0 + tk, r1) - k0
            if width <= 0:
                continue
            on_diagonal = k0 + width > r0
            assert not on_diagonal or (k0 + width == r1 and width >= rows)
            todo.append((r, width, tri if on_diagonal else None))
        start = pl.multiple_of(i * tq + k0, tk)
        diag_scores = lambda t: scores(todo[t][0], k_ref[pl.ds(start, todo[t][1]), :])
        ss = [diag_scores(t) for t in range(min(ATT_AHEAD, len(todo)))]
        for t, (r, width, mask) in enumerate(todo):
            if t + ATT_AHEAD < len(todo):
                ss.append(diag_scores(t + ATT_AHEAD))
            update(r, ss[t], v_ref[pl.ds(start, width), :], mask)

    l_row = jnp.sum(l_scr[...], axis=-1, keepdims=True)
    o_ref[...] = (_sigmoid(gb_ref[...].astype(F32)) * (acc_scr[...] / l_row)).astype(BF16)


def _attn_prompt_call(q_cat, k_cat, v, proj, batch, seq, tq):
    nq = seq // tq
    gb_blk = COL_GB // MLA_V
    est = (2 * (tq * HEAD_QK * 2 + seq * HEAD_QK * 2 + seq * MLA_V * 2 + 2 * tq * MLA_V * 4)
           + 3 * tq * V7X_LANES * 4 + 6 * tq * ATT_TK * 4 + VMEM_TEMP_ALLOWANCE)
    return pl.pallas_call(
        functools.partial(_attn_prompt_kernel, tq=tq, tk=ATT_TK, n_split=ATT_SPLIT),
        grid=(batch, MLA_HEADS, nq),
        in_specs=[pl.BlockSpec((tq, HEAD_QK), lambda b, h, i: (b * nq + i, h)),
                  pl.BlockSpec((seq, HEAD_QK), lambda b, h, i: (b, h)),
                  pl.BlockSpec((seq, MLA_V), lambda b, h, i: (b, h)),
                  pl.BlockSpec((tq, MLA_V), lambda b, h, i: (b * nq + i, gb_blk + h))],
        out_specs=pl.BlockSpec((tq, MLA_V), lambda b, h, i: (b * nq + i, h)),
        out_shape=jax.ShapeDtypeStruct((batch * seq, MLA_HEADS * MLA_V), BF16),
        scratch_shapes=[pltpu.VMEM((tq, V7X_LANES), F32), pltpu.VMEM((tq, V7X_LANES), F32),
                        pltpu.VMEM((tq, MLA_V), F32)],
        compiler_params=_cparams(("parallel", "parallel", "arbitrary"), est),
        name="mla_attention_prompt",
    )(q_cat, k_cat, v, proj)


def _attn_sample_kernel(qlat_ref, qr_ref, cc_ref, ckr_ref, nc_ref, nkr_ref, gb0_ref, gb1_ref, wukv_ref,
                        o_ref, *, sq):
    nt = (((1,), (1,)), ((), ()))
    q_lat = jnp.concatenate([qlat_ref[:, h * MLA_KV_LORA:(h + 1) * MLA_KV_LORA] for h in range(MLA_HEADS)],
                            axis=0)
    q_r = jnp.concatenate([qr_ref[:, h * MLA_ROPE:(h + 1) * MLA_ROPE] for h in range(MLA_HEADS)], axis=0)
    kc = cc_ref[...].astype(BF16)
    kn = nc_ref[...].astype(BF16)
    s_c = (lax.dot_general(q_lat, kc, nt, preferred_element_type=F32)
           + jnp.dot(q_r, ckr_ref[...].astype(BF16), preferred_element_type=F32))
    s_n = (lax.dot_general(q_lat, kn, nt, preferred_element_type=F32)
           + lax.dot_general(q_r, nkr_ref[...].astype(BF16), nt, preferred_element_type=F32))
    m = jnp.maximum(jnp.max(s_c, axis=-1, keepdims=True), jnp.max(s_n, axis=-1, keepdims=True))
    p_c = jnp.exp2(s_c - m)
    p_n = jnp.exp2(s_n - m)
    l_row = jnp.sum(p_c, axis=-1, keepdims=True) + jnp.sum(p_n, axis=-1, keepdims=True)
    o_lat = (jnp.dot(p_c.astype(BF16), kc, preferred_element_type=F32)
             + jnp.dot(p_n.astype(BF16), kn, preferred_element_type=F32)) / l_row
    o_lat = o_lat.astype(BF16)
    hw = MLA_NOPE + MLA_V
    half = MLA_HEADS // 2
    for h in range(MLA_HEADS):
        w_uv = wukv_ref[:, h * hw + MLA_NOPE:(h + 1) * hw]
        o_h = jnp.dot(o_lat[h * sq:(h + 1) * sq, :], w_uv, preferred_element_type=F32)
        gb_ref = gb0_ref if h < half else gb1_ref
        gb = gb_ref[:, (h % half) * MLA_V:(h % half + 1) * MLA_V].astype(F32)
        o_ref[:, h * MLA_V:(h + 1) * MLA_V] = (_sigmoid(gb) * o_h).astype(BF16)


def _attn_sample_call(q_lat, q_rope, ckv_new, kr_new, cache_ckv, cache_kr_t, proj, w_ukv_r, l, batch, sq):
    assert (PAST_LEN + sq - 1) // CHUNK <= PAST_LEN // CHUNK
    past = cache_ckv.shape[2]
    lat_w = MLA_HEADS * MLA_KV_LORA
    rope_w = MLA_HEADS * MLA_ROPE
    ww = MLA_HEADS * (MLA_NOPE + MLA_V)
    gw = D_MODEL // 2
    gb_blk = COL_GB // gw
    rows = MLA_HEADS * sq
    est = (2 * (sq * (lat_w + rope_w) * 2 + past * (MLA_KV_LORA + V7X_LANES) * 4 + MLA_KV_LORA * ww * 2
                + 4 * sq * D_MODEL * 4) + past * (MLA_KV_LORA + V7X_LANES) * 2 + 6 * rows * past * 4
           + 4 * rows * MLA_KV_LORA * 4 + VMEM_TEMP_ALLOWANCE)
    return pl.pallas_call(
        functools.partial(_attn_sample_kernel, sq=sq),
        grid=(batch,),
        in_specs=[pl.BlockSpec((sq, lat_w), lambda b: (b, 0)),
                  pl.BlockSpec((sq, rope_w), lambda b: (b, 0)),
                  pl.BlockSpec((None, None, past, MLA_KV_LORA), lambda b: (l, b, 0, 0)),
                  pl.BlockSpec((None, None, MLA_ROPE, past), lambda b: (l, b, 0, 0)),
                  pl.BlockSpec((sq, MLA_KV_LORA), lambda b: (b, 0)),
                  pl.BlockSpec((sq, MLA_ROPE), lambda b: (b, 0)),
                  pl.BlockSpec((sq, gw), lambda b: (b, gb_blk)),
                  pl.BlockSpec((sq, gw), lambda b: (b, gb_blk + 1)),
                  pl.BlockSpec((None, MLA_KV_LORA, ww), lambda b: (l, 0, 0))],
        out_specs=pl.BlockSpec((sq, D_MODEL), lambda b: (b, 0)),
        out_shape=jax.ShapeDtypeStruct((batch * sq, D_MODEL), BF16),
        compiler_params=_cparams(("parallel",), est),
        name="mla_attention_sample",
    )(q_lat, q_rope, cache_ckv, cache_kr_t, ckv_new, kr_new, proj, proj, w_ukv_r)


def _merge_kernel(a_ref, b_ref, x_ref, g1_ref, w_ref, o_ref):
    m = (a_ref[...].astype(F32) + b_ref[...].astype(F32)).astype(BF16)
    mix = jnp.dot(m, w_ref[...], preferred_element_type=F32)
    o_ref[...] = _gate_res(x_ref[...], g1_ref[...], mix)


def _merge_call(a_part, b_part, x, mod, w_o_b, l, tm, groups):
    m = x.shape[0]
    est = (2 * 2 * tm * D_MODEL * 2 + 4 * tm * D_MODEL * 4 + D_MODEL * D_MODEL * 2 + 3 * tm * D_MODEL * 4
           + VMEM_TEMP_ALLOWANCE)
    return pl.pallas_call(
        _merge_kernel,
        grid=(m // tm,),
        in_specs=[pl.BlockSpec((tm, D_MODEL), lambda i: (i, 0)),
                  pl.BlockSpec((tm, D_MODEL), lambda i: (i, 0)),
                  pl.BlockSpec((tm, D_MODEL), lambda i: (i, 0)),
                  pl.BlockSpec((None, None, None, groups, D_MODEL), lambda i: (l, 2, i, 0, 0)),
                  pl.BlockSpec((None, D_MODEL, D_MODEL), lambda i: (l, 0, 0), pipeline_mode=pl.Buffered(1))],
        out_specs=pl.BlockSpec((tm, D_MODEL), lambda i: (i, 0)),
        out_shape=jax.ShapeDtypeStruct((m, D_MODEL), F32),
        compiler_params=_cparams(("parallel",), est),
        name="merge_out_proj",
    )(a_part, b_part, x, mod, w_o_b)


def _ffn_kernel(x_ref, sc_ref, sh_ref, g2_ref, gn_ref, fg_ref, wg_ref, wu_ref, wo_ref, o_ref, *rest,
                nh, final, emit_bf16):
    h_scr, acc_scr = rest[-2:]
    j = pl.program_id(1)

    @pl.when(j == 0)
    def _():
        h_scr[...] = _norm_mod(x_ref[...], gn_ref[...], sc_ref[...], sh_ref[...]).astype(BF16)
        acc_scr[...] = jnp.zeros(acc_scr.shape, F32)

    wg = wg_ref[...].astype(BF16)
    wu = wu_ref[...].astype(BF16)
    wo = wo_ref[...].astype(BF16)
    if emit_bf16:
        wgb_ref, wub_ref, wob_ref = rest[:3]
        wgb_ref[...] = wg
        wub_ref[...] = wu
        wob_ref[...] = wo
    hb = h_scr[...]
    gate = jnp.dot(hb, wg, preferred_element_type=F32)
    up = jnp.dot(hb, wu, preferred_element_type=F32)
    act = ((gate * _sigmoid(gate)) * up).astype(BF16)
    acc_scr[...] += jnp.dot(act, wo, preferred_element_type=F32)

    @pl.when(j == nh - 1)
    def _():
        y = _gate_res(x_ref[...], g2_ref[...], acc_scr[...])
        o_ref[...] = _rms(y, fg_ref[...]) if final else y


def _ffn_call(x, mod, norm_g, norm_final, weights, l, tm, groups, th):
    m = x.shape[0]
    nh = FFN_HIDDEN // th
    emit_bf16 = len(weights) == 2
    assert not emit_bf16 or m == tm
    wbytes = 4 if emit_bf16 else 2
    est = (4 * tm * D_MODEL * 4 + tm * D_MODEL * 2 + tm * D_MODEL * 4 + 2 * 3 * D_MODEL * th * wbytes
           + (2 * 3 + 3) * D_MODEL * th * 2 * emit_bf16 + 4 * tm * th * 4 + tm * D_MODEL * 4 + VMEM_TEMP_ALLOWANCE)
    mod_spec = lambda k: pl.BlockSpec((None, None, None, groups, D_MODEL), lambda i, j: (l, k, i, 0, 0))
    out_specs = [pl.BlockSpec((tm, D_MODEL), lambda i, j: (i, 0))]
    out_shape = [jax.ShapeDtypeStruct((m, D_MODEL), F32)]
    if emit_bf16:
        w_in, w_out = weights
        w_specs = [pl.BlockSpec((None, D_MODEL, th), lambda i, j: (l, 0, j)),
                   pl.BlockSpec((None, D_MODEL, th), lambda i, j: (l, 0, nh + j)),
                   pl.BlockSpec((None, th, D_MODEL), lambda i, j: (l, j, 0))]
        w_args = [w_in, w_in, w_out]
        out_specs += [pl.BlockSpec((D_MODEL, th), lambda i, j: (0, j)),
                      pl.BlockSpec((D_MODEL, th), lambda i, j: (0, j)),
                      pl.BlockSpec((th, D_MODEL), lambda i, j: (j, 0))]
        out_shape += [jax.ShapeDtypeStruct((D_MODEL, FFN_HIDDEN), BF16),
                      jax.ShapeDtypeStruct((D_MODEL, FFN_HIDDEN), BF16),
                      jax.ShapeDtypeStruct((FFN_HIDDEN, D_MODEL), BF16)]
    else:
        w_specs = [pl.BlockSpec((D_MODEL, th), lambda i, j: (0, j)),
                   pl.BlockSpec((D_MODEL, th), lambda i, j: (0, j)),
                   pl.BlockSpec((th, D_MODEL), lambda i, j: (j, 0))]
        w_args = list(weights)
    return pl.pallas_call(
        functools.partial(_ffn_kernel, nh=nh, final=(l == DEPTH - 1), emit_bf16=emit_bf16),
        grid=(m // tm, nh),
        in_specs=[pl.BlockSpec((tm, D_MODEL), lambda i, j: (i, 0)),
                  mod_spec(4), mod_spec(3), mod_spec(5),
                  pl.BlockSpec((None, 1, D_MODEL), lambda i, j: (l, 0, 0)),
                  pl.BlockSpec((1, D_MODEL), lambda i, j: (0, 0))] + w_specs,
        out_specs=out_specs,
        out_shape=out_shape,
        scratch_shapes=[pltpu.VMEM((tm, D_MODEL), BF16), pltpu.VMEM((tm, D_MODEL), F32)],
        compiler_params=_cparams(("parallel", "arbitrary"), est),
        name="ffn_swiglu",
    )(x, mod, mod, mod, norm_g, norm_final, *w_args)


def _rope_tables(pos, dim):
    inv = jnp.exp(-math.log(ROPE_BASE) * jnp.arange(0, dim, 2, dtype=F32) / dim)
    ang = pos.astype(F32)[:, None] * inv[None, :]
    return jnp.cos(ang), jnp.sin(ang)


def _mla_rope_tables(pos, reps):
    cos, sin = _rope_tables(pos, MLA_ROPE)
    z = jnp.zeros_like(cos)
    c = jnp.concatenate([cos, z, cos, z], axis=-1)
    s = jnp.concatenate([-sin, z, sin, z], axis=-1)
    return jnp.tile(c, (reps, 1)), jnp.tile(s, (reps, 1))


def _prep_weights(w_uq, w_ukv):
    uq = w_uq.reshape(DEPTH, MLA_Q_LORA, MLA_HEADS, MLA_NOPE + MLA_ROPE)
    uq = jnp.concatenate([uq[..., :MLA_NOPE], _pack_rope_lanes(uq[..., MLA_NOPE:])], axis=-1)
    w_uq_r = uq.reshape(DEPTH, MLA_Q_LORA, MLA_HEADS * HEAD_QK).astype(BF16)
    return w_uq_r, w_ukv.astype(BF16)


def _layer_prompt(l, x, mod, tm, batch, seq, ret_tabs, mla_tabs, zero_state, log_g, W, stacks, ffn_w):
    proj, kr_raw = _inproj_call(x, mod, W["norm_mix"], W["w_in_t"], l, IN_TM_PROMPT, 1, IN_TM_PROMPT // tm)
    ckv_prev, kr_prev, st_prev = stacks
    a_part, st_stack = _ret_call(proj, log_g, ret_tabs[0], ret_tabs[1], zero_state, 0, batch, seq,
                                 RET_L_PROMPT, RET_HEADS_PROMPT, l, st_prev)
    q_cat, ckv_stack, kr_stack, k_cat, v = _mla_q_call(proj, kr_raw, W["q_norm"], W["kv_norm"], mla_tabs[0],
                                                       mla_tabs[1], W["w_uq_r"], W["w_ukv_r"], l, tm,
                                                       ckv_prev, kr_prev)
    b_part = _attn_prompt_call(q_cat, k_cat, v, proj, batch, seq, ATT_TQ)
    x = _merge_call(a_part, b_part, x, mod, W["w_o"], l, tm, 1)
    x, = _ffn_call(x, mod, W["norm_ffn"], W["norm_final"], ffn_w, l, tm, 1, FFN_TH)
    return x, (ckv_stack, kr_stack, st_stack)


def _layer_sample(l, x, mod, groups, tm, batch, seq, ret_tabs, mla_tabs, state_ret, cache_ckv, cache_kr,
                  log_g, W, st_prev):
    proj, kr_raw = _inproj_call(x, mod, W["norm_mix"], W["w_in_t"], l, tm, groups, 1)
    a_part, st_stack = _ret_call(proj, log_g, ret_tabs[0], ret_tabs[1], state_ret, l, batch, seq, seq,
                                 RET_HEADS_SAMPLE, l, st_prev)
    q_lat, q_rope, ckv, kr = _mla_q_sample_call(proj, kr_raw, W["q_norm"], W["kv_norm"], mla_tabs[0],
                                                mla_tabs[1], W["w_uq_r"], W["w_ukv_r"], l)
    b_part = _attn_sample_call(q_lat, q_rope, ckv, kr, cache_ckv, cache_kr, proj, W["w_ukv_r"], l, batch, seq)
    x = _merge_call(a_part, b_part, x, mod, W["w_o"], l, tm, groups)
    x, *ffn_w = _ffn_call(x, mod, W["norm_ffn"], W["norm_final"], (W["w_ffn_in"], W["w_ffn_out"]), l, tm,
                          groups, FFN_TH_SAMPLE)
    return x, ckv, kr, st_stack, tuple(ffn_w)


def kernel(x_prompt, x_sample, c_prompt, c_sample, cache_mla_ckv, cache_mla_krope, state_ret, w_ada, b_ada,
           norm_mix, norm_ffn, w_in, mla_q_norm, w_uq, mla_kv_norm, w_ukv, w_o, w_ffn_in, w_ffn_out,
           norm_final):
    bp, sp, _ = x_prompt.shape
    bs, ss, _ = x_sample.shape
    tm = TOKEN_TILE
    assert sp % IN_TM_PROMPT == 0 and (bs * ss) % tm == 0 and tm % ss == 0

    w_uq_r, w_ukv_r = _prep_weights(w_uq, w_ukv)
    W = dict(w_in_t=jnp.swapaxes(w_in, 1, 2), w_uq_r=w_uq_r, w_ukv_r=w_ukv_r,
             w_o=w_o.astype(BF16), w_ffn_in=w_ffn_in, w_ffn_out=w_ffn_out,
             norm_mix=norm_mix.reshape(DEPTH, 1, D_MODEL), norm_ffn=norm_ffn.reshape(DEPTH, 1, D_MODEL),
             norm_final=norm_final.reshape(1, D_MODEL),
             q_norm=mla_q_norm.reshape(DEPTH, 1, MLA_Q_LORA), kv_norm=mla_kv_norm.reshape(DEPTH, 1, MLA_KV_LORA))

    c_rows = -(-(bp + bs) // ADA_ROW_ALIGN) * ADA_ROW_ALIGN
    c_all = jnp.concatenate([c_prompt, c_sample, jnp.zeros((c_rows - bp - bs, D_MODEL), F32)], axis=0)
    mod_all = _ada_call(c_all, w_ada, b_ada)[:, :bp + bs]
    mod_all = mod_all.reshape(DEPTH, bp + bs, 6, D_MODEL).transpose(0, 2, 1, 3)
    tiles_per_batch = sp // tm
    mod_p = jnp.repeat(mod_all[:, :, :bp], tiles_per_batch, axis=2)[:, :, :, None, :]
    groups_s = tm // ss
    mod_s = mod_all[:, :, bp:].reshape(DEPTH, 6, (bs * ss) // tm, groups_s, D_MODEL)

    log_g = jnp.log1p(-jnp.exp2(-RET_GAMMA_EXP0 - jnp.arange(RET_HEADS, dtype=F32)))
    pos_p = jnp.arange(sp)
    pos_s = PAST_LEN + jnp.arange(ss)
    ret_tabs_p = _rope_tables(pos_p, RET_DK)
    ret_tabs_s = _rope_tables(pos_s, RET_DK)
    mla_tabs_p = _mla_rope_tables(pos_p, bp)
    mla_tabs_s = _mla_rope_tables(pos_s, bs)
    zero_state = jnp.zeros((1, bp, RET_HEADS, RET_DK, RET_DV), F32)
    cache_kr_t = jnp.swapaxes(cache_mla_krope, 2, 3)

    xp = x_prompt.reshape(bp * sp, D_MODEL)
    xs = x_sample.reshape(bs * ss, D_MODEL)
    stacks_p = (jnp.zeros((DEPTH, bp * sp, MLA_KV_LORA), F32), jnp.zeros((DEPTH, bp * sp, MLA_ROPE), F32),
                jnp.zeros((DEPTH, bp, RET_HEADS, RET_DK, RET_DV), F32))
    st_s = jnp.zeros((DEPTH, bs, RET_HEADS, RET_DK, RET_DV), F32)
    ckv_s, kr_s = [], []
    for l in range(DEPTH):
        xs, ckv, kr, st_s, ffn_w = _layer_sample(l, xs, mod_s, groups_s, tm, bs, ss, ret_tabs_s, mla_tabs_s,
                                                 state_ret, cache_mla_ckv, cache_kr_t, log_g, W, st_s)
        xp, stacks_p = _layer_prompt(l, xp, mod_p, tm, bp, sp, ret_tabs_p, mla_tabs_p, zero_state, log_g, W,
                                     stacks_p, ffn_w)
        ckv_s.append(ckv.reshape(bs, ss, MLA_KV_LORA))
        kr_s.append(kr.reshape(bs, ss, MLA_ROPE))

    y_prompt = xp.reshape(bp, sp, D_MODEL)
    y_sample = xs.reshape(bs, ss, D_MODEL)
    ckv_p, kr_p, st_p = stacks_p
    return (y_prompt, y_sample, ckv_p.reshape(DEPTH, bp, sp, MLA_KV_LORA), kr_p.reshape(DEPTH, bp, sp, MLA_ROPE),
            st_p, jnp.stack(ckv_s), jnp.stack(kr_s), st_s)
```

```python
import functools
import math

import jax
import jax.numpy as jnp
from jax import lax
from jax.experimental import pallas as pl
from jax.experimental.pallas import tpu as pltpu

D_MODEL = 2048
DEPTH = 4
PAST_LEN = 1024
CHUNK = 64
RET_HEADS = 8
RET_DK = D_MODEL // RET_HEADS
RET_DV = D_MODEL // RET_HEADS
MLA_HEADS = 16
MLA_Q_LORA = D_MODEL // 4
MLA_KV_LORA = D_MODEL // 4
MLA_NOPE = 128
MLA_ROPE = 64
MLA_V = D_MODEL // MLA_HEADS
FFN_HIDDEN = -(-8 * D_MODEL // (3 * 256)) * 256
ROPE_BASE = 10000.0
RET_GAMMA_EXP0 = 5.0
RMS_EPS = 1e-6
GN_EPS = 1e-5
NEG_INF = -1e30

F32 = jnp.float32
BF16 = jnp.bfloat16

V7X_LANES = 128
V7X_VMEM_LIMIT_CAP = 56 * 1024 * 1024
VMEM_TEMP_ALLOWANCE = 4 * 1024 * 1024

COL_RQ, COL_RK, COL_RV, COL_RG = 0, D_MODEL, 2 * D_MODEL, 3 * D_MODEL
COL_DQ = 4 * D_MODEL
COL_DKV = COL_DQ + MLA_Q_LORA
COL_GA = COL_DKV + MLA_KV_LORA
COL_GB = COL_GA + D_MODEL
MAIN_COLS = COL_GB + D_MODEL
HEAD_QK = 2 * V7X_LANES
HALF_ROPE = MLA_ROPE // 2
ADA_ROW_ALIGN = 16
TOKEN_TILE = 512
IN_TM_PROMPT = 2048
IN_TN = 512
ADA_TN = 1024
WEIGHT_STREAM_BUFFERS = 3
FFN_TH = 512
FFN_TH_SAMPLE = 256
RET_L_PROMPT = 256
RET_HEADS_PROMPT = 4
RET_HEADS_SAMPLE = 4
ATT_TQ = 1024
ATT_TK = 1024
ATT_AHEAD = 2
ATT_SPLIT = 4
QK_SCALE_LOG2 = (MLA_NOPE + MLA_ROPE) ** -0.5 * math.log2(math.e)


def _cparams(sem, est_bytes):
    return pltpu.CompilerParams(dimension_semantics=sem,
                                vmem_limit_bytes=min(int(est_bytes), V7X_VMEM_LIMIT_CAP))


def _sigmoid(x):
    return jax.nn.sigmoid(x)


def _rms(x, g):
    return x * lax.rsqrt(jnp.mean(x * x, axis=-1, keepdims=True) + RMS_EPS) * g


def _norm_mod(x, g, sc, sh):
    tm, d = x.shape
    groups = sc.shape[0]
    y = _rms(x, g)
    if groups == 1:
        return y * (1.0 + sc) + sh
    y3 = y.reshape(groups, tm // groups, d)
    return (y3 * (1.0 + sc[:, None, :]) + sh[:, None, :]).reshape(tm, d)


def _gate_res(x, gate, upd):
    tm, n = x.shape
    groups = gate.shape[0]
    if groups == 1:
        return x + gate * upd
    return x + (gate[:, None, :] * upd.reshape(groups, tm // groups, n)).reshape(tm, n)


def _pack_rope_lanes(x):
    z = jnp.zeros(x.shape[:-1] + (HALF_ROPE,), x.dtype)
    return jnp.concatenate([x[..., :HALF_ROPE], z, x[..., HALF_ROPE:], z], axis=-1)


def _unpack_rope_lanes(x):
    return jnp.concatenate([x[..., :HALF_ROPE], x[..., 2 * HALF_ROPE:3 * HALF_ROPE]], axis=-1)


def _ada_kernel(c_ref, w_ref, b_ref, o_ref):
    c = c_ref[...]
    a = (c * _sigmoid(c)).astype(BF16)
    o_ref[0] = jnp.dot(a, w_ref[0].astype(BF16), preferred_element_type=F32) + b_ref[0]


def _ada_call(c_all, w_ada, b_ada):
    nb = c_all.shape[0]
    n = w_ada.shape[-1]
    tn = ADA_TN
    est = 2 * (D_MODEL * tn * 4) + D_MODEL * tn * 2 + 4 * nb * (D_MODEL + 2 * tn) * 4 + VMEM_TEMP_ALLOWANCE
    return pl.pallas_call(
        _ada_kernel,
        grid=(DEPTH, n // tn),
        in_specs=[pl.BlockSpec((nb, D_MODEL), lambda l, j: (0, 0)),
                  pl.BlockSpec((1, D_MODEL, tn), lambda l, j: (l, 0, j)),
                  pl.BlockSpec((1, 1, tn), lambda l, j: (l, 0, j))],
        out_specs=pl.BlockSpec((1, nb, tn), lambda l, j: (l, 0, j)),
        out_shape=jax.ShapeDtypeStruct((DEPTH, nb, n), F32),
        compiler_params=_cparams(("parallel", "parallel"), est),
        name="ada_mod",
    )(c_all, w_ada, b_ada.reshape(DEPTH, 1, n))


def _inproj_kernel(x_hbm, sc_ref, sh_ref, g_ref, w_hbm, wkr_ref, o_ref, kr_ref, h_scr, x_buf, x_sem,
                   w_buf, w_sem, *, n_main, layer):
    i = pl.program_id(0)
    j = pl.program_id(1)
    n_j = pl.num_programs(1)
    t = i * n_j + j
    total = pl.num_programs(0) * n_j
    ahead = WEIGHT_STREAM_BUFFERS - 1
    tm = x_buf.shape[0]
    nt = (((1,), (1,)), ((), ()))

    def x_copy(tile):
        return pltpu.make_async_copy(x_hbm.at[pl.ds(tile * tm, tm), :], x_buf, x_sem)

    def w_copy(step):
        jj = lax.rem(step, n_j)
        row = pl.multiple_of(jj * IN_TN + jnp.where(jj >= n_main, MLA_ROPE, 0), MLA_ROPE)
        slot = lax.rem(step, WEIGHT_STREAM_BUFFERS)
        return pltpu.make_async_copy(w_hbm.at[layer, pl.ds(row, IN_TN), :], w_buf.at[slot], w_sem.at[slot])

    @pl.when(t == 0)
    def _():
        x_copy(0).start()
        for s in range(ahead):
            w_copy(s).start()

    @pl.when(t + ahead < total)
    def _():
        w_copy(t + ahead).start()

    @pl.when(j == 0)
    def _():
        wkr = wkr_ref[...]
        z = jnp.zeros((HALF_ROPE, wkr.shape[1]), wkr.dtype)
        wkr = jnp.concatenate([wkr[:HALF_ROPE], z, wkr[HALF_ROPE:], z], axis=0).astype(BF16)
        x_copy(i).wait()
        for r0 in range(0, tm, TOKEN_TILE):
            rs = pl.ds(r0, TOKEN_TILE)
            hb = _norm_mod(x_buf[rs, :], g_ref[...], sc_ref[...], sh_ref[...]).astype(BF16)
            h_scr[rs, :] = hb
            kr_ref[rs, :] = lax.dot_general(hb, wkr, nt, preferred_element_type=F32)

    @pl.when((j == 1) & (i + 1 < pl.num_programs(0)))
    def _():
        x_copy(i + 1).start()

    w_copy(t).wait()
    w = w_buf[lax.rem(t, WEIGHT_STREAM_BUFFERS)].astype(BF16)
    o_ref[...] = lax.dot_general(h_scr[...], w, nt, preferred_element_type=F32).astype(BF16)


def _inproj_call(x, mod, norm_g, w_in_t, l, tm, groups, mod_stride):
    m = x.shape[0]
    n_main = COL_GA // IN_TN
    n_steps = (m // tm) * (MAIN_COLS // IN_TN)
    assert groups == 1 or tm == TOKEN_TILE
    assert n_steps >= WEIGHT_STREAM_BUFFERS - 1
    est = (tm * D_MODEL * 4 + tm * D_MODEL * 2 + (WEIGHT_STREAM_BUFFERS * IN_TN + 2 * MLA_ROPE) * D_MODEL * 4
           + D_MODEL * IN_TN * 2 + 3 * tm * IN_TN * 4 + 6 * TOKEN_TILE * D_MODEL * 4 + VMEM_TEMP_ALLOWANCE)
    mod_spec = lambda k: pl.BlockSpec((None, None, None, groups, D_MODEL),
                                      lambda i, j: (l, k, i * mod_stride, 0, 0))
    return pl.pallas_call(
        functools.partial(_inproj_kernel, n_main=n_main, layer=l),
        grid=(m // tm, MAIN_COLS // IN_TN),
        in_specs=[pl.BlockSpec(memory_space=pl.ANY),
                  mod_spec(1), mod_spec(0),
                  pl.BlockSpec((None, 1, D_MODEL), lambda i, j: (l, 0, 0)),
                  pl.BlockSpec(memory_space=pl.ANY),
                  pl.BlockSpec((None, MLA_ROPE, D_MODEL), lambda i, j: (l, COL_GA // MLA_ROPE, 0))],
        out_specs=[pl.BlockSpec((tm, IN_TN), lambda i, j: (i, j)),
                   pl.BlockSpec((tm, V7X_LANES), lambda i, j: (i, 0))],
        out_shape=[jax.ShapeDtypeStruct((m, MAIN_COLS), BF16),
                   jax.ShapeDtypeStruct((m, V7X_LANES), F32)],
        scratch_shapes=[pltpu.VMEM((tm, D_MODEL), BF16), pltpu.VMEM((tm, D_MODEL), F32),
                        pltpu.SemaphoreType.DMA(()),
                        pltpu.VMEM((WEIGHT_STREAM_BUFFERS, IN_TN, D_MODEL), F32),
                        pltpu.SemaphoreType.DMA((WEIGHT_STREAM_BUFFERS,))],
        compiler_params=_cparams(("arbitrary", "arbitrary"), est),
        name="in_proj",
    )(x, mod, mod, norm_g, w_in_t, w_in_t)


def _ret_kernel(lg_ref, q_ref, k_ref, v_ref, rg_ref, ga_ref, cos_ref, sin_ref, s0_ref, *rest,
                chunk_len, heads):
    a_ref, st_ref, dm_scr = rest[-3:]
    hg = pl.program_id(1)
    c = pl.program_id(2)
    L = chunk_len
    lgs = [lg_ref[hg * heads + t] for t in range(heads)]

    @pl.when(c == 0)
    def _():
        st_ref[...] = s0_ref[...]
        ri = lax.broadcasted_iota(jnp.int32, (L, L), 0)
        ci = lax.broadcasted_iota(jnp.int32, (L, L), 1)
        diff = (ri - ci).astype(F32)
        for t in range(heads):
            dm_scr[t] = jnp.where(diff >= 0, jnp.exp(jnp.maximum(diff, 0.0) * lgs[t]), 0.0)

    cos = cos_ref[...]
    sin = sin_ref[...]
    half = RET_DK // 2
    idx = lax.broadcasted_iota(jnp.int32, (L, 1), 0).astype(F32)
    nt = (((1,), (1,)), ((), ()))
    tn = (((0,), (0,)), ((), ()))

    def rope(x):
        x1, x2 = x[:, :half], x[:, half:]
        return jnp.concatenate([x1 * cos - x2 * sin, x1 * sin + x2 * cos], axis=-1)

    stage = []
    for t in range(heads):
        cs = pl.ds(t * RET_DK, RET_DK)
        q = rope(q_ref[:, cs].astype(F32))
        k = rope(k_ref[:, cs].astype(F32)) * (RET_DK ** -0.5)
        vb = v_ref[:, cs]
        qb = q.astype(BF16)
        zeta = jnp.exp((L - 1.0 - idx) * lgs[t])
        g_l = jnp.exp(jnp.full((1, 1), float(L), F32) * lgs[t])
        st = st_ref[0, t]
        scores = lax.dot_general(qb, k.astype(BF16), nt, preferred_element_type=F32)
        cross = jnp.dot(qb, st.astype(BF16), preferred_element_type=F32)
        st_ref[0, t] = st * g_l + lax.dot_general((k * zeta).astype(BF16), vb, tn, preferred_element_type=F32)
        stage.append((scores, cross, vb))

    outs = []
    for t in range(heads):
        scores, cross, vb = stage[t]
        xi = jnp.exp((idx + 1.0) * lgs[t])
        outs.append(jnp.dot((scores * dm_scr[t]).astype(BF16), vb, preferred_element_type=F32) + cross * xi)

    for t in range(heads):
        cs = pl.ds(t * RET_DK, RET_DK)
        o = outs[t]
        mu = jnp.mean(o, axis=-1, keepdims=True)
        d = o - mu
        var = jnp.mean(d * d, axis=-1, keepdims=True)
        on = d * lax.rsqrt(var + GN_EPS)
        rg = rg_ref[:, cs].astype(F32)
        a_ref[:, cs] = (_sigmoid(ga_ref[:, cs].astype(F32)) * ((rg * _sigmoid(rg)) * on)).astype(BF16)


def _stacked_out(stack_prev, in_specs, args, out_index):
    in_specs.append(pl.BlockSpec(memory_space=pl.ANY))
    args.append(stack_prev)
    return {len(args) - 1: out_index}


def _ret_call(proj, log_g, cos, sin, state0, state_layer, batch, seq, chunk_len, heads, l, stack_prev):
    L = chunk_len
    nc = seq // L
    w = heads * RET_DK
    assert COL_GA % w == 0 and RET_HEADS % heads == 0
    col = lambda base: (lambda b, h, c: (b * nc + c, base // w + h))
    blk = lambda base: pl.BlockSpec((L, w), col(base))
    nh = RET_HEADS
    est = (2 * 6 * L * w * 4 + 4 * heads * RET_DK * RET_DV * 4 + heads * L * L * 4
           + 8 * heads * L * max(L, RET_DK) * 4 + VMEM_TEMP_ALLOWANCE)
    in_specs = [pl.BlockSpec(memory_space=pltpu.SMEM),
                blk(COL_RQ), blk(COL_RK), blk(COL_RV), blk(COL_RG), blk(COL_GA),
                pl.BlockSpec((L, RET_DK // 2), lambda b, h, c: (c, 0)),
                pl.BlockSpec((L, RET_DK // 2), lambda b, h, c: (c, 0)),
                pl.BlockSpec((None, 1, heads, RET_DK, RET_DV), lambda b, h, c: (state_layer, b, h, 0, 0))]
    args = [log_g, proj, proj, proj, proj, proj, cos, sin, state0]
    aliases = _stacked_out(stack_prev, in_specs, args, 1)
    return pl.pallas_call(
        functools.partial(_ret_kernel, chunk_len=L, heads=heads),
        grid=(batch, nh // heads, nc),
        in_specs=in_specs,
        out_specs=[pl.BlockSpec((L, w), lambda b, h, c: (b * nc + c, h)),
                   pl.BlockSpec((None, 1, heads, RET_DK, RET_DV), lambda b, h, c: (l, b, h, 0, 0))],
        out_shape=[jax.ShapeDtypeStruct((batch * seq, D_MODEL), BF16),
                   jax.ShapeDtypeStruct((DEPTH, batch, nh, RET_DK, RET_DV), F32)],
        scratch_shapes=[pltpu.VMEM((heads, L, L), F32)],
        input_output_aliases=aliases,
        compiler_params=_cparams(("parallel", "parallel", "arbitrary"), est),
        name="retention",
    )(*args)


def _rope128(x, c, s):
    return x * c + pltpu.roll(x, V7X_LANES // 2, 1) * s


def _mla_q_kernel(dq_ref, dkv_ref, kr_ref, qn_ref, kvn_ref, c_ref, s_ref, wuq_ref, wukv_ref, *rest):
    q_out, ckv_out, kr64_out, kcat_out, v_out = rest[-5:]
    c = c_ref[...]
    s = s_ref[...]
    cq = _rms(dq_ref[...].astype(F32), qn_ref[...]).astype(BF16)
    for h in range(MLA_HEADS):
        lo = h * HEAD_QK
        qh = jnp.dot(cq, wuq_ref[:, lo:lo + HEAD_QK], preferred_element_type=F32) * QK_SCALE_LOG2
        q_out[:, lo:lo + MLA_NOPE] = qh[:, :MLA_NOPE].astype(BF16)
        q_out[:, lo + MLA_NOPE:lo + HEAD_QK] = _rope128(qh[:, MLA_NOPE:], c, s).astype(BF16)
    ckv = _rms(dkv_ref[...].astype(F32), kvn_ref[...])
    ckv_out[...] = ckv
    kr = _rope128(kr_ref[...], c, s)
    kr64_out[...] = _unpack_rope_lanes(kr)
    cb = ckv.astype(BF16)
    krb = kr.astype(BF16)
    hw = MLA_NOPE + MLA_V
    for h in range(MLA_HEADS):
        kv = jnp.dot(cb, wukv_ref[:, h * hw:(h + 1) * hw], preferred_element_type=F32).astype(BF16)
        lo = h * HEAD_QK
        kcat_out[:, lo:lo + MLA_NOPE] = kv[:, :MLA_NOPE]
        kcat_out[:, lo + MLA_NOPE:lo + HEAD_QK] = krb
        v_out[:, h * MLA_V:(h + 1) * MLA_V] = kv[:, MLA_NOPE:]


def _mla_q_call(proj, kr_raw, q_norm, kv_norm, rope_c, rope_s, w_uq_r, w_ukv_b, l, tm, ckv_prev, kr_prev):
    m = proj.shape[0]
    dq_blk = COL_DQ // MLA_Q_LORA
    qw = MLA_HEADS * HEAD_QK
    ww = MLA_HEADS * (MLA_NOPE + MLA_V)
    vw = MLA_HEADS * MLA_V
    est = (2 * (2 * tm * MLA_Q_LORA * 4 + 3 * tm * V7X_LANES * 4) + 2 * MLA_Q_LORA * (qw + ww) * 2
           + 2 * tm * (2 * qw + vw) * 2 + 2 * tm * MLA_KV_LORA * 4 + 2 * tm * V7X_LANES * 4
           + 8 * tm * HEAD_QK * 4 + VMEM_TEMP_ALLOWANCE)
    in_specs = [pl.BlockSpec((tm, MLA_Q_LORA), lambda i: (i, dq_blk)),
                pl.BlockSpec((tm, MLA_KV_LORA), lambda i: (i, dq_blk + 1)),
                pl.BlockSpec((tm, V7X_LANES), lambda i: (i, 0)),
                pl.BlockSpec((None, 1, MLA_Q_LORA), lambda i: (l, 0, 0)),
                pl.BlockSpec((None, 1, MLA_KV_LORA), lambda i: (l, 0, 0)),
                pl.BlockSpec((tm, V7X_LANES), lambda i: (i, 0)),
                pl.BlockSpec((tm, V7X_LANES), lambda i: (i, 0)),
                pl.BlockSpec((None, MLA_Q_LORA, qw), lambda i: (l, 0, 0)),
                pl.BlockSpec((None, MLA_KV_LORA, ww), lambda i: (l, 0, 0))]
    args = [proj, proj, kr_raw, q_norm, kv_norm, rope_c, rope_s, w_uq_r, w_ukv_b]
    aliases = _stacked_out(ckv_prev, in_specs, args, 1)
    aliases.update(_stacked_out(kr_prev, in_specs, args, 2))
    return pl.pallas_call(
        _mla_q_kernel,
        grid=(m // tm,),
        in_specs=in_specs,
        out_specs=[pl.BlockSpec((tm, qw), lambda i: (i, 0)),
                   pl.BlockSpec((None, tm, MLA_KV_LORA), lambda i: (l, i, 0)),
                   pl.BlockSpec((None, tm, MLA_ROPE), lambda i: (l, i, 0)),
                   pl.BlockSpec((tm, qw), lambda i: (i, 0)),
                   pl.BlockSpec((tm, vw), lambda i: (i, 0))],
        out_shape=[jax.ShapeDtypeStruct((m, qw), BF16),
                   jax.ShapeDtypeStruct((DEPTH, m, MLA_KV_LORA), F32),
                   jax.ShapeDtypeStruct((DEPTH, m, MLA_ROPE), F32),
                   jax.ShapeDtypeStruct((m, qw), BF16),
                   jax.ShapeDtypeStruct((m, vw), BF16)],
        input_output_aliases=aliases,
        compiler_params=_cparams(("parallel",), est),
        name="mla_latents",
    )(*args)


def _mla_q_sample_kernel(dq_ref, dkv_ref, kr_ref, qn_ref, kvn_ref, c_ref, s_ref, wuq_ref, wukv_ref,
                         qlat_out, qr_out, ckv_out, kro_out):
    c = c_ref[...]
    s = s_ref[...]
    cq = _rms(dq_ref[...].astype(F32), qn_ref[...]).astype(BF16)
    for h in range(MLA_HEADS):
        lo = h * HEAD_QK
        qh = jnp.dot(cq, wuq_ref[:, lo:lo + HEAD_QK], preferred_element_type=F32) * QK_SCALE_LOG2
        w_uk = wukv_ref[:, h * (MLA_NOPE + MLA_V):h * (MLA_NOPE + MLA_V) + MLA_NOPE]
        q_lat = lax.dot_general(qh[:, :MLA_NOPE].astype(BF16), w_uk, (((1,), (1,)), ((), ())),
                                preferred_element_type=F32)
        qlat_out[:, h * MLA_KV_LORA:(h + 1) * MLA_KV_LORA] = q_lat.astype(BF16)
        qr_out[:, h * MLA_ROPE:(h + 1) * MLA_ROPE] = _unpack_rope_lanes(
            _rope128(qh[:, MLA_NOPE:], c, s)).astype(BF16)
    ckv_out[...] = _rms(dkv_ref[...].astype(F32), kvn_ref[...])
    kro_out[...] = _unpack_rope_lanes(_rope128(kr_ref[...], c, s))


def _mla_q_sample_call(proj, kr_raw, q_norm, kv_norm, rope_c, rope_s, w_uq_r, w_ukv_r, l):
    m = proj.shape[0]
    dq_blk = COL_DQ // MLA_Q_LORA
    qw = MLA_HEADS * HEAD_QK
    ww = MLA_HEADS * (MLA_NOPE + MLA_V)
    lat_w = MLA_HEADS * MLA_KV_LORA
    rope_w = MLA_HEADS * MLA_ROPE
    est = (2 * (2 * m * MLA_Q_LORA * 4 + 3 * m * V7X_LANES * 4) + 2 * MLA_Q_LORA * (qw + ww) * 2
           + 2 * m * (lat_w + rope_w) * 2 + 2 * m * (MLA_KV_LORA + MLA_ROPE) * 4 + 8 * m * MLA_KV_LORA * 4
           + VMEM_TEMP_ALLOWANCE)
    full = lambda shape: pl.BlockSpec(shape, lambda i: (0,) * len(shape))
    return pl.pallas_call(
        _mla_q_sample_kernel,
        grid=(1,),
        in_specs=[pl.BlockSpec((m, MLA_Q_LORA), lambda i: (0, dq_blk)),
                  pl.BlockSpec((m, MLA_KV_LORA), lambda i: (0, dq_blk + 1)),
                  full((m, V7X_LANES)),
                  pl.BlockSpec((None, 1, MLA_Q_LORA), lambda i: (l, 0, 0)),
                  pl.BlockSpec((None, 1, MLA_KV_LORA), lambda i: (l, 0, 0)),
                  full((m, V7X_LANES)), full((m, V7X_LANES)),
                  pl.BlockSpec((None, MLA_Q_LORA, qw), lambda i: (l, 0, 0)),
                  pl.BlockSpec((None, MLA_KV_LORA, ww), lambda i: (l, 0, 0))],
        out_specs=[full((m, lat_w)), full((m, rope_w)), full((m, MLA_KV_LORA)), full((m, MLA_ROPE))],
        out_shape=[jax.ShapeDtypeStruct((m, lat_w), BF16),
                   jax.ShapeDtypeStruct((m, rope_w), BF16),
                   jax.ShapeDtypeStruct((m, MLA_KV_LORA), F32),
                   jax.ShapeDtypeStruct((m, MLA_ROPE), F32)],
        compiler_params=_cparams(("arbitrary",), est),
        name="mla_latents_sample",
    )(proj, proj, kr_raw, q_norm, kv_norm, rope_c, rope_s, w_uq_r, w_ukv_r)


def _mla_kv_kernel(ckv_ref, kr_ref, wukv_ref, kcat_out, v_out):
    cb = ckv_ref[...].astype(BF16)
    krb = kr_ref[...].astype(BF16)
    hw = MLA_NOPE + MLA_V
    for h in range(MLA_HEADS):
        kv = jnp.dot(cb, wukv_ref[:, h * hw:(h + 1) * hw], preferred_element_type=F32).astype(BF16)
        lo = h * HEAD_QK
        kcat_out[:, lo:lo + MLA_NOPE] = kv[:, :MLA_NOPE]
        kcat_out[:, lo + MLA_NOPE:lo + HEAD_QK] = krb
        v_out[:, h * MLA_V:(h + 1) * MLA_V] = kv[:, MLA_NOPE:]


def _mla_kv_call(ckv_stack, kr128, w_ukv_r, l, rows, tm):
    kw = MLA_HEADS * HEAD_QK
    vw = MLA_HEADS * MLA_V
    ww = MLA_HEADS * (MLA_NOPE + MLA_V)
    est = (2 * tm * (MLA_KV_LORA + V7X_LANES) * 4 + 2 * MLA_KV_LORA * ww * 2 + 2 * tm * (kw + vw) * 2
           + 8 * tm * HEAD_QK * 4 + VMEM_TEMP_ALLOWANCE)
    return pl.pallas_call(
        _mla_kv_kernel,
        grid=(rows // tm,),
        in_specs=[pl.BlockSpec((None, tm, MLA_KV_LORA), lambda i: (l, i, 0)),
                  pl.BlockSpec((tm, V7X_LANES), lambda i: (i, 0)),
                  pl.BlockSpec((None, MLA_KV_LORA, ww), lambda i: (l, 0, 0))],
        out_specs=[pl.BlockSpec((tm, kw), lambda i: (i, 0)),
                   pl.BlockSpec((tm, vw), lambda i: (i, 0))],
        out_shape=[jax.ShapeDtypeStruct((rows, kw), BF16),
                   jax.ShapeDtypeStruct((rows, vw), BF16)],
        compiler_params=_cparams(("parallel",), est),
        name="mla_kv_expand",
    )(ckv_stack, kr128, w_ukv_r)


def _scores(q, k_blk):
    return lax.dot_general(q, k_blk, (((1,), (1,)), ((), ())), preferred_element_type=F32)


def _softmax_tile(s, v_blk, m_prev, l_prev, acc_prev, tri):
    n_groups = s.shape[1] // V7X_LANES
    groups = [s[:, g * V7X_LANES:(g + 1) * V7X_LANES] for g in range(n_groups)]
    if tri is not None:
        n_tri = tri.shape[1] // V7X_LANES
        for t in range(n_tri):
            g = n_groups - n_tri + t
            groups[g] = jnp.where(tri[:, t * V7X_LANES:(t + 1) * V7X_LANES], groups[g], NEG_INF)
    m_new = jnp.maximum(m_prev, jnp.max(functools.reduce(jnp.maximum, groups), axis=-1, keepdims=True))
    alpha = jnp.exp2(m_prev - m_new)
    ps = [jnp.exp2(g - m_new) for g in groups]
    l_new = alpha * l_prev + functools.reduce(jnp.add, ps)
    p = jnp.concatenate([x.astype(BF16) for x in ps], axis=-1)
    acc_new = alpha * acc_prev + jnp.dot(p, v_blk, preferred_element_type=F32)
    return m_new, l_new, acc_new


def _attn_prompt_kernel(q_ref, k_ref, v_ref, gb_ref, o_ref, m_scr, l_scr, acc_scr, *, tq, tk, n_split):
    i = pl.program_id(2)
    rows = tq // n_split
    m_scr[...] = jnp.full(m_scr.shape, NEG_INF, F32)
    l_scr[...] = jnp.zeros(l_scr.shape, F32)
    acc_scr[...] = jnp.zeros(acc_scr.shape, F32)

    def scores(r, k_blk):
        return _scores(q_ref[pl.ds(r * rows, rows), :], k_blk)

    def update(r, s, v_blk, mask):
        rs = pl.ds(r * rows, rows)
        m_new, l_new, acc_new = _softmax_tile(s, v_blk, m_scr[rs, :], l_scr[rs, :], acc_scr[rs, :], mask)
        m_scr[rs, :] = m_new
        l_scr[rs, :] = l_new
        acc_scr[rs, :] = acc_new

    blocks_per_tile = tq // tk

    def body(j, carry):
        for d in range(blocks_per_tile):
            start = pl.multiple_of(j * tq + d * tk, tk)
            k_blk = k_ref[pl.ds(start, tk), :]
            v_blk = v_ref[pl.ds(start, tk), :]
            ss = [scores(r, k_blk) for r in range(min(ATT_AHEAD, n_split))]
            for r in range(n_split):
                if r + ATT_AHEAD < n_split:
                    ss.append(scores(r + ATT_AHEAD, k_blk))
                update(r, ss[r], v_blk, None)
        return carry

    lax.fori_loop(0, i, body, 0)

    tri = (lax.broadcasted_iota(jnp.int32, (rows, rows), 1) // CHUNK
           <= lax.broadcasted_iota(jnp.int32, (rows, rows), 0) // CHUNK)
    for d in range(blocks_per_tile):
        k0 = d * tk
        todo = []
        for r in range(n_split):
            r0, r1 = r * rows, (r + 1) * rows
            width = min(k0 + tk, r1) - k0
            if width <= 0:
                continue
            on_diagonal = k0 + width > r0
            assert not on_diagonal or (k0 + width == r1 and width >= rows)
            todo.append((r, width, tri if on_diagonal else None))
        start = pl.multiple_of(i * tq + k0, tk)
        diag_scores = lambda t: scores(todo[t][0], k_ref[pl.ds(start, todo[t][1]), :])
        ss = [diag_scores(t) for t in range(min(ATT_AHEAD, len(todo)))]
        for t, (r, width, mask) in enumerate(todo):
            if t + ATT_AHEAD < len(todo):
                ss.append(diag_scores(t + ATT_AHEAD))
            update(r, ss[t], v_ref[pl.ds(start, width), :], mask)

    l_row = jnp.sum(l_scr[...], axis=-1, keepdims=True)
    o_ref[...] = (_sigmoid(gb_ref[...].astype(F32)) * (acc_scr[...] / l_row)).astype(BF16)


def _attn_prompt_call(q_cat, k_cat, v, proj, batch, seq, tq):
    nq = seq // tq
    gb_blk = COL_GB // MLA_V
    est = (2 * (tq * HEAD_QK * 2 + seq * HEAD_QK * 2 + seq * MLA_V * 2 + 2 * tq * MLA_V * 4)
           + 3 * tq * V7X_LANES * 4 + 6 * tq * ATT_TK * 4 + VMEM_TEMP_ALLOWANCE)
    return pl.pallas_call(
        functools.partial(_attn_prompt_kernel, tq=tq, tk=ATT_TK, n_split=ATT_SPLIT),
        grid=(batch, MLA_HEADS, nq),
        in_specs=[pl.BlockSpec((tq, HEAD_QK), lambda b, h, i: (b * nq + i, h)),
                  pl.BlockSpec((seq, HEAD_QK), lambda b, h, i: (b, h)),
                  pl.BlockSpec((seq, MLA_V), lambda b, h, i: (b, h)),
                  pl.BlockSpec((tq, MLA_V), lambda b, h, i: (b * nq + i, gb_blk + h))],
        out_specs=pl.BlockSpec((tq, MLA_V), lambda b, h, i: (b * nq + i, h)),
        out_shape=jax.ShapeDtypeStruct((batch * seq, MLA_HEADS * MLA_V), BF16),
        scratch_shapes=[pltpu.VMEM((tq, V7X_LANES), F32), pltpu.VMEM((tq, V7X_LANES), F32),
                        pltpu.VMEM((tq, MLA_V), F32)],
        compiler_params=_cparams(("parallel", "parallel", "arbitrary"), est),
        name="mla_attention_prompt",
    )(q_cat, k_cat, v, proj)


def _attn_sample_kernel(qlat_ref, qr_ref, cc_ref, ckr_ref, nc_ref, nkr_ref, gb0_ref, gb1_ref, wukv_ref,
                        o_ref, *, sq):
    nt = (((1,), (1,)), ((), ()))
    q_lat = jnp.concatenate([qlat_ref[:, h * MLA_KV_LORA:(h + 1) * MLA_KV_LORA] for h in range(MLA_HEADS)],
                            axis=0)
    q_r = jnp.concatenate([qr_ref[:, h * MLA_ROPE:(h + 1) * MLA_ROPE] for h in range(MLA_HEADS)], axis=0)
    kc = cc_ref[...].astype(BF16)
    kn = nc_ref[...].astype(BF16)
    s_c = (lax.dot_general(q_lat, kc, nt, preferred_element_type=F32)
           + jnp.dot(q_r, ckr_ref[...].astype(BF16), preferred_element_type=F32))
    s_n = (lax.dot_general(q_lat, kn, nt, preferred_element_type=F32)
           + lax.dot_general(q_r, nkr_ref[...].astype(BF16), nt, preferred_element_type=F32))
    m = jnp.maximum(jnp.max(s_c, axis=-1, keepdims=True), jnp.max(s_n, axis=-1, keepdims=True))
    p_c = jnp.exp2(s_c - m)
    p_n = jnp.exp2(s_n - m)
    l_row = jnp.sum(p_c, axis=-1, keepdims=True) + jnp.sum(p_n, axis=-1, keepdims=True)
    o_lat = (jnp.dot(p_c.astype(BF16), kc, preferred_element_type=F32)
             + jnp.dot(p_n.astype(BF16), kn, preferred_element_type=F32)) / l_row
    o_lat = o_lat.astype(BF16)
    hw = MLA_NOPE + MLA_V
    half = MLA_HEADS // 2
    for h in range(MLA_HEADS):
        w_uv = wukv_ref[:, h * hw + MLA_NOPE:(h + 1) * hw]
        o_h = jnp.dot(o_lat[h * sq:(h + 1) * sq, :], w_uv, preferred_element_type=F32)
        gb_ref = gb0_ref if h < half else gb1_ref
        gb = gb_ref[:, (h % half) * MLA_V:(h % half + 1) * MLA_V].astype(F32)
        o_ref[:, h * MLA_V:(h + 1) * MLA_V] = (_sigmoid(gb) * o_h).astype(BF16)


def _attn_sample_call(q_lat, q_rope, ckv_new, kr_new, cache_ckv, cache_kr_t, proj, w_ukv_r, l, batch, sq):
    assert (PAST_LEN + sq - 1) // CHUNK <= PAST_LEN // CHUNK
    past = cache_ckv.shape[2]
    lat_w = MLA_HEADS * MLA_KV_LORA
    rope_w = MLA_HEADS * MLA_ROPE
    ww = MLA_HEADS * (MLA_NOPE + MLA_V)
    gw = D_MODEL // 2
    gb_blk = COL_GB // gw
    rows = MLA_HEADS * sq
    est = (2 * (sq * (lat_w + rope_w) * 2 + past * (MLA_KV_LORA + V7X_LANES) * 4 + MLA_KV_LORA * ww * 2
                + 4 * sq * D_MODEL * 4) + past * (MLA_KV_LORA + V7X_LANES) * 2 + 6 * rows * past * 4
           + 4 * rows * MLA_KV_LORA * 4 + VMEM_TEMP_ALLOWANCE)
    return pl.pallas_call(
        functools.partial(_attn_sample_kernel, sq=sq),
        grid=(batch,),
        in_specs=[pl.BlockSpec((sq, lat_w), lambda b: (b, 0)),
                  pl.BlockSpec((sq, rope_w), lambda b: (b, 0)),
                  pl.BlockSpec((None, None, past, MLA_KV_LORA), lambda b: (l, b, 0, 0)),
                  pl.BlockSpec((None, None, MLA_ROPE, past), lambda b: (l, b, 0, 0)),
                  pl.BlockSpec((sq, MLA_KV_LORA), lambda b: (b, 0)),
                  pl.BlockSpec((sq, MLA_ROPE), lambda b: (b, 0)),
                  pl.BlockSpec((sq, gw), lambda b: (b, gb_blk)),
                  pl.BlockSpec((sq, gw), lambda b: (b, gb_blk + 1)),
                  pl.BlockSpec((None, MLA_KV_LORA, ww), lambda b: (l, 0, 0))],
        out_specs=pl.BlockSpec((sq, D_MODEL), lambda b: (b, 0)),
        out_shape=jax.ShapeDtypeStruct((batch * sq, D_MODEL), BF16),
        compiler_params=_cparams(("parallel",), est),
        name="mla_attention_sample",
    )(q_lat, q_rope, cache_ckv, cache_kr_t, ckv_new, kr_new, proj, proj, w_ukv_r)


def _merge_kernel(a_ref, b_ref, x_ref, g1_ref, w_ref, o_ref):
    m = (a_ref[...].astype(F32) + b_ref[...].astype(F32)).astype(BF16)
    mix = jnp.dot(m, w_ref[...], preferred_element_type=F32)
    o_ref[...] = _gate_res(x_ref[...], g1_ref[...], mix)


def _merge_call(a_part, b_part, x, mod, w_o_b, l, tm, groups):
    m = x.shape[0]
    est = (2 * 2 * tm * D_MODEL * 2 + 4 * tm * D_MODEL * 4 + D_MODEL * D_MODEL * 2 + 3 * tm * D_MODEL * 4
           + VMEM_TEMP_ALLOWANCE)
    return pl.pallas_call(
        _merge_kernel,
        grid=(m // tm,),
        in_specs=[pl.BlockSpec((tm, D_MODEL), lambda i: (i, 0)),
                  pl.BlockSpec((tm, D_MODEL), lambda i: (i, 0)),
                  pl.BlockSpec((tm, D_MODEL), lambda i: (i, 0)),
                  pl.BlockSpec((None, None, None, groups, D_MODEL), lambda i: (l, 2, i, 0, 0)),
                  pl.BlockSpec((None, D_MODEL, D_MODEL), lambda i: (l, 0, 0), pipeline_mode=pl.Buffered(1))],
        out_specs=pl.BlockSpec((tm, D_MODEL), lambda i: (i, 0)),
        out_shape=jax.ShapeDtypeStruct((m, D_MODEL), F32),
        compiler_params=_cparams(("parallel",), est),
        name="merge_out_proj",
    )(a_part, b_part, x, mod, w_o_b)


def _ffn_kernel(x_ref, sc_ref, sh_ref, g2_ref, gn_ref, fg_ref, wg_ref, wu_ref, wo_ref, o_ref, *rest,
                nh, final, emit_bf16):
    h_scr, acc_scr = rest[-2:]
    j = pl.program_id(1)

    @pl.when(j == 0)
    def _():
        h_scr[...] = _norm_mod(x_ref[...], gn_ref[...], sc_ref[...], sh_ref[...]).astype(BF16)
        acc_scr[...] = jnp.zeros(acc_scr.shape, F32)

    wg = wg_ref[...].astype(BF16)
    wu = wu_ref[...].astype(BF16)
    wo = wo_ref[...].astype(BF16)
    if emit_bf16:
        wgb_ref, wub_ref, wob_ref = rest[:3]
        wgb_ref[...] = wg
        wub_ref[...] = wu
        wob_ref[...] = wo
    hb = h_scr[...]
    gate = jnp.dot(hb, wg, preferred_element_type=F32)
    up = jnp.dot(hb, wu, preferred_element_type=F32)
    act = ((gate * _sigmoid(gate)) * up).astype(BF16)
    acc_scr[...] += jnp.dot(act, wo, preferred_element_type=F32)

    @pl.when(j == nh - 1)
    def _():
        y = _gate_res(x_ref[...], g2_ref[...], acc_scr[...])
        o_ref[...] = _rms(y, fg_ref[...]) if final else y


def _ffn_call(x, mod, norm_g, norm_final, weights, l, tm, groups, th):
    m = x.shape[0]
    nh = FFN_HIDDEN // th
    emit_bf16 = len(weights) == 2
    assert not emit_bf16 or m == tm
    wbytes = 4 if emit_bf16 else 2
    est = (4 * tm * D_MODEL * 4 + tm * D_MODEL * 2 + tm * D_MODEL * 4 + 2 * 3 * D_MODEL * th * wbytes
           + (2 * 3 + 3) * D_MODEL * th * 2 * emit_bf16 + 4 * tm * th * 4 + tm * D_MODEL * 4 + VMEM_TEMP_ALLOWANCE)
    mod_spec = lambda k: pl.BlockSpec((None, None, None, groups, D_MODEL), lambda i, j: (l, k, i, 0, 0))
    out_specs = [pl.BlockSpec((tm, D_MODEL), lambda i, j: (i, 0))]
    out_shape = [jax.ShapeDtypeStruct((m, D_MODEL), F32)]
    if emit_bf16:
        w_in, w_out = weights
        w_specs = [pl.BlockSpec((None, D_MODEL, th), lambda i, j: (l, 0, j)),
                   pl.BlockSpec((None, D_MODEL, th), lambda i, j: (l, 0, nh + j)),
                   pl.BlockSpec((None, th, D_MODEL), lambda i, j: (l, j, 0))]
        w_args = [w_in, w_in, w_out]
        out_specs += [pl.BlockSpec((D_MODEL, th), lambda i, j: (0, j)),
                      pl.BlockSpec((D_MODEL, th), lambda i, j: (0, j)),
                      pl.BlockSpec((th, D_MODEL), lambda i, j: (j, 0))]
        out_shape += [jax.ShapeDtypeStruct((D_MODEL, FFN_HIDDEN), BF16),
                      jax.ShapeDtypeStruct((D_MODEL, FFN_HIDDEN), BF16),
                      jax.ShapeDtypeStruct((FFN_HIDDEN, D_MODEL), BF16)]
    else:
        w_specs = [pl.BlockSpec((D_MODEL, th), lambda i, j: (0, j)),
                   pl.BlockSpec((D_MODEL, th), lambda i, j: (0, j)),
                   pl.BlockSpec((th, D_MODEL), lambda i, j: (j, 0))]
        w_args = list(weights)
    return pl.pallas_call(
        functools.partial(_ffn_kernel, nh=nh, final=(l == DEPTH - 1), emit_bf16=emit_bf16),
        grid=(m // tm, nh),
        in_specs=[pl.BlockSpec((tm, D_MODEL), lambda i, j: (i, 0)),
                  mod_spec(4), mod_spec(3), mod_spec(5),
                  pl.BlockSpec((None, 1, D_MODEL), lambda i, j: (l, 0, 0)),
                  pl.BlockSpec((1, D_MODEL), lambda i, j: (0, 0))] + w_specs,
        out_specs=out_specs,
        out_shape=out_shape,
        scratch_shapes=[pltpu.VMEM((tm, D_MODEL), BF16), pltpu.VMEM((tm, D_MODEL), F32)],
        compiler_params=_cparams(("parallel", "arbitrary"), est),
        name="ffn_swiglu",
    )(x, mod, mod, mod, norm_g, norm_final, *w_args)


def _rope_tables(pos, dim):
    inv = jnp.exp(-math.log(ROPE_BASE) * jnp.arange(0, dim, 2, dtype=F32) / dim)
    ang = pos.astype(F32)[:, None] * inv[None, :]
    return jnp.cos(ang), jnp.sin(ang)


def _mla_rope_tables(pos, reps):
    cos, sin = _rope_tables(pos, MLA_ROPE)
    z = jnp.zeros_like(cos)
    c = jnp.concatenate([cos, z, cos, z], axis=-1)
    s = jnp.concatenate([-sin, z, sin, z], axis=-1)
    return jnp.tile(c, (reps, 1)), jnp.tile(s, (reps, 1))


def _prep_weights(w_uq, w_ukv):
    uq = w_uq.reshape(DEPTH, MLA_Q_LORA, MLA_HEADS, MLA_NOPE + MLA_ROPE)
    uq = jnp.concatenate([uq[..., :MLA_NOPE], _pack_rope_lanes(uq[..., MLA_NOPE:])], axis=-1)
    w_uq_r = uq.reshape(DEPTH, MLA_Q_LORA, MLA_HEADS * HEAD_QK).astype(BF16)
    return w_uq_r, w_ukv.astype(BF16)


def _layer_prompt(l, x, mod, tm, batch, seq, ret_tabs, mla_tabs, zero_state, log_g, W, stacks, ffn_w):
    proj, kr_raw = _inproj_call(x, mod, W["norm_mix"], W["w_in_t"], l, IN_TM_PROMPT, 1, IN_TM_PROMPT // tm)
    ckv_prev, kr_prev, st_prev = stacks
    a_part, st_stack = _ret_call(proj, log_g, ret_tabs[0], ret_tabs[1], zero_state, 0, batch, seq,
                                 RET_L_PROMPT, RET_HEADS_PROMPT, l, st_prev)
    q_cat, ckv_stack, kr_stack, k_cat, v = _mla_q_call(proj, kr_raw, W["q_norm"], W["kv_norm"], mla_tabs[0],
                                                       mla_tabs[1], W["w_uq_r"], W["w_ukv_r"], l, tm,
                                                       ckv_prev, kr_prev)
    b_part = _attn_prompt_call(q_cat, k_cat, v, proj, batch, seq, ATT_TQ)
    x = _merge_call(a_part, b_part, x, mod, W["w_o"], l, tm, 1)
    x, = _ffn_call(x, mod, W["norm_ffn"], W["norm_final"], ffn_w, l, tm, 1, FFN_TH)
    return x, (ckv_stack, kr_stack, st_stack)


def _layer_sample(l, x, mod, groups, tm, batch, seq, ret_tabs, mla_tabs, state_ret, cache_ckv, cache_kr,
                  log_g, W, st_prev):
    proj, kr_raw = _inproj_call(x, mod, W["norm_mix"], W["w_in_t"], l, tm, groups, 1)
    a_part, st_stack = _ret_call(proj, log_g, ret_tabs[0], ret_tabs[1], state_ret, l, batch, seq, seq,
                                 RET_HEADS_SAMPLE, l, st_prev)
    q_lat, q_rope, ckv, kr = _mla_q_sample_call(proj, kr_raw, W["q_norm"], W["kv_norm"], mla_tabs[0],
                                                mla_tabs[1], W["w_uq_r"], W["w_ukv_r"], l)
    b_part = _attn_sample_call(q_lat, q_rope, ckv, kr, cache_ckv, cache_kr, proj, W["w_ukv_r"], l, batch, seq)
    x = _merge_call(a_part, b_part, x, mod, W["w_o"], l, tm, groups)
    x, *ffn_w = _ffn_call(x, mod, W["norm_ffn"], W["norm_final"], (W["w_ffn_in"], W["w_ffn_out"]), l, tm,
                          groups, FFN_TH_SAMPLE)
    return x, ckv, kr, st_stack, tuple(ffn_w)


def kernel(x_prompt, x_sample, c_prompt, c_sample, cache_mla_ckv, cache_mla_krope, state_ret, w_ada, b_ada,
           norm_mix, norm_ffn, w_in, mla_q_norm, w_uq, mla_kv_norm, w_ukv, w_o, w_ffn_in, w_ffn_out,
           norm_final):
    bp, sp, _ = x_prompt.shape
    bs, ss, _ = x_sample.shape
    tm = TOKEN_TILE
    assert sp % IN_TM_PROMPT == 0 and (bs * ss) % tm == 0 and tm % ss == 0

    w_uq_r, w_ukv_r = _prep_weights(w_uq, w_ukv)
    W = dict(w_in_t=jnp.swapaxes(w_in, 1, 2), w_uq_r=w_uq_r, w_ukv_r=w_ukv_r,
             w_o=w_o.astype(BF16), w_ffn_in=w_ffn_in, w_ffn_out=w_ffn_out,
             norm_mix=norm_mix.reshape(DEPTH, 1, D_MODEL), norm_ffn=norm_ffn.reshape(DEPTH, 1, D_MODEL),
             norm_final=norm_final.reshape(1, D_MODEL),
             q_norm=mla_q_norm.reshape(DEPTH, 1, MLA_Q_LORA), kv_norm=mla_kv_norm.reshape(DEPTH, 1, MLA_KV_LORA))

    c_rows = -(-(bp + bs) // ADA_ROW_ALIGN) * ADA_ROW_ALIGN
    c_all = jnp.concatenate([c_prompt, c_sample, jnp.zeros((c_rows - bp - bs, D_MODEL), F32)], axis=0)
    mod_all = _ada_call(c_all, w_ada, b_ada)[:, :bp + bs]
    mod_all = mod_all.reshape(DEPTH, bp + bs, 6, D_MODEL).transpose(0, 2, 1, 3)
    tiles_per_batch = sp // tm
    mod_p = jnp.repeat(mod_all[:, :, :bp], tiles_per_batch, axis=2)[:, :, :, None, :]
    groups_s = tm // ss
    mod_s = mod_all[:, :, bp:].reshape(DEPTH, 6, (bs * ss) // tm, groups_s, D_MODEL)

    log_g = jnp.log1p(-jnp.exp2(-RET_GAMMA_EXP0 - jnp.arange(RET_HEADS, dtype=F32)))
    pos_p = jnp.arange(sp)
    pos_s = PAST_LEN + jnp.arange(ss)
    ret_tabs_p = _rope_tables(pos_p, RET_DK)
    ret_tabs_s = _rope_tables(pos_s, RET_DK)
    mla_tabs_p = _mla_rope_tables(pos_p, bp)
    mla_tabs_s = _mla_rope_tables(pos_s, bs)
    zero_state = jnp.zeros((1, bp, RET_HEADS, RET_DK, RET_DV), F32)
    cache_kr_t = jnp.swapaxes(cache_mla_krope, 2, 3)

    xp = x_prompt.reshape(bp * sp, D_MODEL)
    xs = x_sample.reshape(bs * ss, D_MODEL)
    stacks_p = (jnp.zeros((DEPTH, bp * sp, MLA_KV_LORA), F32), jnp.zeros((DEPTH, bp * sp, MLA_ROPE), F32),
                jnp.zeros((DEPTH, bp, RET_HEADS, RET_DK, RET_DV), F32))
    st_s = jnp.zeros((DEPTH, bs, RET_HEADS, RET_DK, RET_DV), F32)
    ckv_s, kr_s = [], []
    for l in range(DEPTH):
        xs, ckv, kr, st_s, ffn_w = _layer_sample(l, xs, mod_s, groups_s, tm, bs, ss, ret_tabs_s, mla_tabs_s,
                                                 state_ret, cache_mla_ckv, cache_kr_t, log_g, W, st_s)
        xp, stacks_p = _layer_prompt(l, xp, mod_p, tm, bp, sp, ret_tabs_p, mla_tabs_p, zero_state, log_g, W,
                                     stacks_p, ffn_w)
        ckv_s.append(ckv.reshape(bs, ss, MLA_KV_LORA))
        kr_s.append(kr.reshape(bs, ss, MLA_ROPE))

    y_prompt = xp.reshape(bp, sp, D_MODEL)
    y_sample = xs.reshape(bs, ss, D_MODEL)
    ckv_p, kr_p, st_p = stacks_p
    return (y_prompt, y_sample, ckv_p.reshape(DEPTH, bp, sp, MLA_KV_LORA), kr_p.reshape(DEPTH, bp, sp, MLA_ROPE),
            st_p, jnp.stack(ckv_s), jnp.stack(kr_s), st_s)
```
